```python
import numpy as np
import jax
import jax.numpy as jnp
from jax import lax

D_MODEL = 1024
BATCH = 16
SEQ = 2048
DEPTH = 2
DEC_BATCH = 32
DEC_SEQ = 4
PAST_LEN = 16384
PAGE_SIZE = 128

H_A = 4
DK_A = 128
DV_A = 128
CHUNK_A = 16
H_B = 8
G_B = 2
HG_B = H_B // G_B
HD_B = 64
L_CMP = 32
D_CMP = 16
L_SLC = 64
N_SEL = 16
WINDOW = 512
Q_BLK = 32
C_CONV = 512
CONV_K = 31
N_MEM = 256
NX_H = 4
XHD = D_MODEL // NX_H
D_FF = 4 * D_MODEL
ALPHA = (2 * DEPTH) ** 0.25
BETA = (8 * DEPTH) ** -0.25
LN_EPS = 1e-5
RMS_EPS = 1e-6

SPLIT_SIZES = (H_A * DK_A, H_A * DK_A, H_A * DV_A, H_A * DV_A,
               H_B * HD_B, 2 * G_B * HD_B, 2 * G_B * HD_B, 2 * G_B * HD_B, 3 * H_B,
               2 * C_CONV, 3 * D_MODEL)
SPLIT_OFFSETS = tuple(int(v) for v in np.cumsum(SPLIT_SIZES)[:-1])
N_IN = int(sum(SPLIT_SIZES))

kernel_name = 'hybrid_hgrn2_nsa_conformer_decode_step'


def _layernorm(x, g, b):
    x32 = x.astype(jnp.float32)
    mu = jnp.mean(x32, axis=-1, keepdims=True)
    var = jnp.mean(jnp.square(x32 - mu), axis=-1, keepdims=True)
    return ((x32 - mu) * lax.rsqrt(var + LN_EPS) * g + b).astype(x.dtype)


def _masked_softmax(s, mask):
    s = s.astype(jnp.float32)
    m = jnp.max(jnp.where(mask, s, -jnp.inf), axis=-1, keepdims=True)
    m = jnp.where(jnp.isfinite(m), m, 0.0)
    e = jnp.where(mask, jnp.exp(s - m), 0.0)
    d = jnp.sum(e, axis=-1, keepdims=True)
    return e / jnp.where(d > 0, d, 1.0)


def _hgrn2_chunk(S, blk):
    q, k, v, lf = blk
    C = q.shape[1]
    b = jnp.cumsum(lf, axis=1)
    causal = jnp.arange(C)[:, None] >= jnp.arange(C)[None, :]
    expo = b[:, :, None] - b[:, None, :]
    decay = jnp.exp(jnp.where(causal[None, :, :, None, None], expo, -jnp.inf))
    att = jnp.einsum('bthc,btshc,bshc->bhts', q, decay, k)
    o = (jnp.einsum('bhts,bshv->bthv', att, v)
         + jnp.einsum('bthc,bhcv->bthv', q * jnp.exp(b), S))
    b_last = b[:, -1]
    S_new = (jnp.exp(b_last)[..., None] * S
             + jnp.einsum('bshc,bshv->bhcv', k * jnp.exp(b_last[:, None] - b), v))
    return S_new, o


def _hgrn2(q_raw, f_raw, i_raw, g_raw, lb, gnorm, S0):
    f32 = jnp.float32
    B, T, _ = q_raw.shape
    q = jax.nn.silu(q_raw.astype(f32)).reshape(B, T, H_A, DK_A)
    z = f_raw.astype(f32).reshape(B, T, H_A, DK_A)
    lb = lb.reshape(H_A, DK_A)
    logf = jnp.logaddexp(jnp.log(lb), jnp.log1p(-lb) + jax.nn.log_sigmoid(z))
    k = -jnp.expm1(logf)
    v = i_raw.astype(f32).reshape(B, T, H_A, DV_A)
    c = CHUNK_A if T % CHUNK_A == 0 else T
    n = T // c

    def to_chunks(a):
        return a.reshape(B, n, c, *a.shape[2:]).swapaxes(0, 1)

    S_fin, o = lax.scan(_hgrn2_chunk, S0.astype(f32),
                        (to_chunks(q), to_chunks(k), to_chunks(v), to_chunks(logf)))
    o = o.swapaxes(0, 1).reshape(B, T, H_A, DV_A)
    o = o * lax.rsqrt(jnp.mean(o * o, axis=-1, keepdims=True) + RMS_EPS) * gnorm
    o = o.reshape(B, T, H_A * DV_A) * jax.nn.silu(g_raw.astype(f32))
    return o.astype(q_raw.dtype), S_fin.astype(S0.dtype)


def _compress(kv, w1, b1, w2, b2):
    B, T = kv.shape[:2]
    m = L_CMP // D_CMP
    n_ch = T // D_CMP
    n_cmp = n_ch - m + 1
    x = kv[:, :n_ch * D_CMP].reshape(B, n_ch, D_CMP, 2, G_B, HD_B)
    part = jnp.einsum('bcjegd,ehjdf->bhcegf', x, w1.reshape(2, m, D_CMP, HD_B, HD_B))
    h = sum(part[:, i, i:i + n_cmp] for i in range(m)) + b1[:, None, :]
    out = jnp.einsum('bnegf,efo->bnego', jax.nn.silu(h), w2) + b2[:, None, :]
    c_end = jnp.arange(n_cmp) * D_CMP + (L_CMP - 1)
    return out.astype(kv.dtype), c_end


def _slc_table(kv, n_sb):
    B, T = kv.shape[:2]
    kv = jnp.pad(kv, ((0, 0), (0, n_sb * L_SLC - T), (0, 0), (0, 0), (0, 0)))
    return kv.reshape(B, n_sb, L_SLC, 2, G_B, HD_B).transpose(0, 4, 1, 2, 3, 5)


def _nsa_block(q, gates, q_pos, kv_c, c_end, slc_tab, kv_w, w_pos):
    scale = HD_B ** -0.5
    B, Sq = q.shape[:2]
    s = jnp.einsum('bqghd,bngd->bghqn', q, kv_c[:, :, 0]) * scale
    p_c = _masked_softmax(s, c_end[None, :] <= q_pos[:, None])
    o_c = jnp.einsum('bghqn,bngd->bqghd', p_c.astype(q.dtype), kv_c[:, :, 1])
    m, r = L_CMP // D_CMP, L_SLC // D_CMP
    n_cmp, n_sb = kv_c.shape[1], slc_tab.shape[2]
    imp = jnp.pad(p_c.sum(axis=2), ((0, 0), (0, 0), (0, 0), (m - 1, m - 1)))
    chunk = sum(imp[..., m - 1 - n: m - 1 - n + n_cmp + m - 1] for n in range(m))
    chunk = jnp.pad(chunk, ((0, 0), (0, 0), (0, 0), (0, n_sb * r - chunk.shape[-1])))
    blk_score = chunk.reshape(B, G_B, Sq, n_sb, r).sum(-1)
    blk = jnp.arange(n_sb)[None, :]
    cur = (q_pos // L_SLC)[:, None]
    forced = (blk == 0) | (blk == cur) | (blk == cur - 1)
    score = jnp.where(blk <= cur, jnp.where(forced, jnp.inf, blk_score), -jnp.inf)
    k_sel = min(N_SEL, n_sb)
    _, idx = lax.top_k(score, k_sel)
    sel = jax.vmap(jax.vmap(lambda tab, ix: tab[ix]))(slc_tab, idx)
    k_pos = idx[..., None] * L_SLC + jnp.arange(L_SLC)
    s = jnp.einsum('bqghd,bgqksd->bghqks', q, sel[..., 0, :]) * scale
    mask = (k_pos <= q_pos[:, None, None])[:, :, None].reshape(B, G_B, 1, Sq, k_sel * L_SLC)
    p_s = _masked_softmax(s.reshape(B, G_B, HG_B, Sq, k_sel * L_SLC), mask).reshape(s.shape)
    o_s = jnp.einsum('bghqks,bgqksd->bqghd', p_s.astype(q.dtype), sel[..., 1, :])
    s = jnp.einsum('bqghd,bngd->bghqn', q, kv_w[:, :, 0]) * scale
    wp, qp = w_pos[None, :], q_pos[:, None]
    p_w = _masked_softmax(s, (wp <= qp) & (wp > qp - WINDOW) & (wp >= 0))
    o_w = jnp.einsum('bghqn,bngd->bqghd', p_w.astype(q.dtype), kv_w[:, :, 1])
    g = jax.nn.sigmoid(gates.astype(jnp.float32)).astype(q.dtype)
    return g[..., 0:1] * o_c + g[..., 1:2] * o_s + g[..., 2:3] * o_w


def _nsa_prompt(q, gates, kv_c_rows, kv_s_rows, kv_w_rows, cmp_w):
    B, T = q.shape[:2]
    kv_c, c_end = _compress(kv_c_rows, *cmp_w)
    tab = _slc_table(kv_s_rows, -(-T // L_SLC))
    kw_pad = jnp.pad(kv_w_rows, ((0, 0), (WINDOW, 0), (0, 0), (0, 0), (0, 0)))

    def body(i):
        start = i * Q_BLK
        qs = lax.dynamic_slice_in_dim(q, start, Q_BLK, axis=1)
        gs = lax.dynamic_slice_in_dim(gates, start, Q_BLK, axis=1)
        kw = lax.dynamic_slice_in_dim(kw_pad, start, WINDOW + Q_BLK, axis=1)
        pos = start + jnp.arange(Q_BLK)
        wpos = start - WINDOW + jnp.arange(WINDOW + Q_BLK)
        return _nsa_block(qs, gs, pos, kv_c, c_end, tab, kw, wpos)

    o = lax.map(body, jnp.arange(T // Q_BLK))
    return o.swapaxes(0, 1).reshape(B, T, G_B, HG_B, HD_B)


def _conv(glu_raw, prefix, w, b, g, beta_):
    a, gate = jnp.split(glu_raw, 2, axis=-1)
    u = a * jax.nn.sigmoid(gate)
    ue = jnp.concatenate([prefix.astype(u.dtype), u], axis=1)
    y = lax.conv_general_dilated(ue, w[:, None, :].astype(u.dtype), window_strides=(1,),
                                 padding='VALID', dimension_numbers=('NWC', 'WIO', 'NWC'),
                                 feature_group_count=C_CONV) + b
    y = jax.nn.silu(_layernorm(y, g, beta_))
    return y, ue[:, -(CONV_K - 1):]


def _xattn(x, kv, wq, wo):
    B, T, _ = x.shape
    q = (x @ wq).reshape(B, T, NX_H, XHD)
    s = jnp.einsum('bthd,bnhd->bhtn', q, kv[:, :, 0]).astype(jnp.float32) * (XHD ** -0.5)
    p = jax.nn.softmax(s, axis=-1).astype(x.dtype)
    o = jnp.einsum('bhtn,bnhd->bthd', p, kv[:, :, 1]).reshape(B, T, NX_H * XHD)
    return o @ wo


def _layer(x, p, S0, conv_prefix, mem_kv, nsa_fn):
    B, T, _ = x.shape
    qa, fa, ia, ga, qb, kvc, kvs, kvw, gb, glu, mg = jnp.split(x @ p['w_in'], SPLIT_OFFSETS, axis=-1)
    ya, S_new = _hgrn2(qa, fa, ia, ga, p['lb'], p['hg_norm'], S0)
    kv_shape = (B, T, 2, G_B, HD_B)
    kvc, kvs, kvw = kvc.reshape(kv_shape), kvs.reshape(kv_shape), kvw.reshape(kv_shape)
    yb, win_state = nsa_fn(qb.reshape(B, T, G_B, HG_B, HD_B), gb.reshape(B, T, G_B, HG_B, 3), kvc, kvs, kvw)
    yc, conv_state = _conv(glu, conv_prefix, p['conv_w'], p['conv_b'], p['conv_ln_g'], p['conv_ln_b'])
    m_a, m_b, m_c = jnp.split(jax.nn.sigmoid(mg), 3, axis=-1)
    merged = (m_a * (ya @ p['w_pa']) + m_b * (yb.reshape(B, T, H_B * HD_B) @ p['w_pb'])
              + m_c * (yc @ p['w_pc']))
    x = _layernorm(ALPHA * x + merged @ p['w_out'], p['ln_g'][0], p['ln_b'][0])
    x = _layernorm(ALPHA * x + _xattn(x, mem_kv, p['w_xq'], p['w_xo']), p['ln_g'][1], p['ln_b'][1])
    h = jnp.square(jax.nn.relu(x @ p['w_up']))
    x = _layernorm(ALPHA * x + h @ p['w_down'], p['ln_g'][2], p['ln_b'][2])
    return x, kvc, kvs, win_state, S_new, conv_state


def setup_inputs(seed: int = 0) -> dict:
    key = jax.random.key(seed)
    ks = jax.random.split(key, 32)
    f32 = jnp.float32

    def nrm(i, shape, scale):
        return jax.random.normal(ks[i], shape, f32) * scale

    n_pages = PAST_LEN // PAGE_SIZE
    n_used = DEC_BATCH * n_pages
    n_phys = n_used + max(1, n_used // 4)
    win_buf = min(WINDOW, PAST_LEN)
    mix_a, mix_b = H_A * DV_A, H_B * HD_B
    page_table = jax.random.permutation(ks[0], n_phys)[:n_used].reshape(DEC_BATCH, n_pages).astype(jnp.int32)
    return {
        'x_prompt': nrm(1, (BATCH, SEQ, D_MODEL), 1.0),
        'x_sample': nrm(2, (DEC_BATCH, DEC_SEQ, D_MODEL), 1.0),
        'cache_cmp': nrm(3, (n_phys, DEPTH, PAGE_SIZE, 2, G_B, HD_B), 1.0),
        'cache_slc': nrm(4, (n_phys, DEPTH, PAGE_SIZE, 2, G_B, HD_B), 1.0),
        'cache_win': nrm(5, (DEPTH, DEC_BATCH, win_buf, 2, G_B, HD_B), 1.0),
        'state_hgrn': nrm(6, (DEPTH, DEC_BATCH, H_A, DK_A, DV_A), 0.5),
        'state_conv': nrm(7, (DEPTH, DEC_BATCH, CONV_K - 1, C_CONV), 0.5),
        'cache_mem': nrm(8, (DEPTH, DEC_BATCH, N_MEM, 2, NX_H, XHD), 1.0),
        'page_table': page_table,
        'mem_prompt': nrm(9, (BATCH, N_MEM, D_MODEL), 1.0),
        'w_in': nrm(10, (DEPTH, D_MODEL, N_IN), D_MODEL ** -0.5),
        'lb_raw': nrm(11, (DEPTH, H_A * DK_A), 1.0),
        'hg_norm': 1.0 + nrm(12, (DEPTH, DV_A), 0.02),
        'w_cmp1': nrm(13, (DEPTH, 2, L_CMP, HD_B, HD_B), (L_CMP * HD_B) ** -0.5),
        'b_cmp1': nrm(14, (DEPTH, 2, HD_B), 0.02),
        'w_cmp2': nrm(15, (DEPTH, 2, HD_B, HD_B), HD_B ** -0.5),
        'b_cmp2': nrm(16, (DEPTH, 2, HD_B), 0.02),
        'conv_w': nrm(17, (DEPTH, CONV_K, C_CONV), CONV_K ** -0.5),
        'conv_b': nrm(18, (DEPTH, C_CONV), 0.02),
        'conv_ln_g': 1.0 + nrm(19, (DEPTH, C_CONV), 0.02),
        'conv_ln_b': nrm(20, (DEPTH, C_CONV), 0.02),
        'w_pa': nrm(21, (DEPTH, mix_a, D_MODEL), mix_a ** -0.5 * BETA),
        'w_pb': nrm(22, (DEPTH, mix_b, D_MODEL), mix_b ** -0.5 * BETA),
        'w_pc': nrm(23, (DEPTH, C_CONV, D_MODEL), C_CONV ** -0.5 * BETA),
        'w_out': nrm(24, (DEPTH, D_MODEL, D_MODEL), D_MODEL ** -0.5 * BETA),
        'ln_g': 1.0 + nrm(25, (DEPTH, 3, D_MODEL), 0.02),
        'ln_b': nrm(26, (DEPTH, 3, D_MODEL), 0.02),
        'w_xq': nrm(27, (DEPTH, D_MODEL, NX_H * XHD), D_MODEL ** -0.5),
        'w_xkv': nrm(28, (DEPTH, D_MODEL, 2 * NX_H * XHD), D_MODEL ** -0.5),
        'w_xo': nrm(29, (DEPTH, NX_H * XHD, D_MODEL), (NX_H * XHD) ** -0.5 * BETA),
        'w_up': nrm(30, (DEPTH, D_MODEL, D_FF), D_MODEL ** -0.5),
        'w_down': nrm(31, (DEPTH, D_FF, D_MODEL), D_FF ** -0.5 * BETA),
    }


def reference(x_prompt, x_sample, cache_cmp, cache_slc, cache_win, state_hgrn, state_conv, cache_mem,
              page_table, mem_prompt, w_in, lb_raw, hg_norm, w_cmp1, b_cmp1, w_cmp2, b_cmp2,
              conv_w, conv_b, conv_ln_g, conv_ln_b, w_pa, w_pb, w_pc, w_out, ln_g, ln_b,
              w_xq, w_xkv, w_xo, w_up, w_down):
    Bp, T = x_prompt.shape[:2]
    Bd, Sd = x_sample.shape[:2]
    past = page_table.shape[1] * cache_cmp.shape[2]
    win_buf = cache_win.shape[2]
    lb_cum = jnp.cumsum(jax.nn.softmax(lb_raw.astype(jnp.float32), axis=0), axis=0)
    lb_all = lb_cum - lb_cum[0]
    xp, xs = x_prompt, x_sample
    cmp_p, cmp_s, slc_p, slc_s, win_p, win_s = [], [], [], [], [], []
    hg_p, hg_s, cv_p, cv_s, mem_p = [], [], [], [], []
    for l in range(DEPTH):
        p = {'w_in': w_in[l], 'lb': lb_all[l], 'hg_norm': hg_norm[l], 'conv_w': conv_w[l],
             'conv_b': conv_b[l], 'conv_ln_g': conv_ln_g[l], 'conv_ln_b': conv_ln_b[l],
             'w_pa': w_pa[l], 'w_pb': w_pb[l], 'w_pc': w_pc[l], 'w_out': w_out[l],
             'ln_g': ln_g[l], 'ln_b': ln_b[l], 'w_xq': w_xq[l], 'w_xo': w_xo[l],
             'w_up': w_up[l], 'w_down': w_down[l]}
        cmp_w = (w_cmp1[l], b_cmp1[l], w_cmp2[l], b_cmp2[l])

        def nsa_prompt(q, g, kvc, kvs, kvw):
            o = _nsa_prompt(q, g, kvc, kvs, kvw, cmp_w)
            win = jnp.pad(kvw, ((0, 0), (max(win_buf - T, 0), 0), (0, 0), (0, 0), (0, 0)))[:, -win_buf:]
            return o, win

        def nsa_sample(q, g, kvc, kvs, kvw):
            tot = past + Sd
            full_c = jnp.concatenate(
                [cache_cmp[page_table, l].reshape(Bd, past, 2, G_B, HD_B).astype(kvc.dtype), kvc], axis=1)
            kv_c, c_end = _compress(full_c, *cmp_w)
            full_s = jnp.concatenate(
                [cache_slc[page_table, l].reshape(Bd, past, 2, G_B, HD_B).astype(kvs.dtype), kvs], axis=1)
            tab = _slc_table(full_s, -(-tot // L_SLC))
            kw = jnp.concatenate([cache_win[l].astype(kvw.dtype), kvw], axis=1)
            pos = past + jnp.arange(Sd)
            wpos = past - win_buf + jnp.arange(win_buf + Sd)
            o = _nsa_block(q, g, pos, kv_c, c_end, tab, kw, wpos)
            return o, kw[:, -win_buf:]

        mem_kv = (mem_prompt @ w_xkv[l]).reshape(Bp, N_MEM, 2, NX_H, XHD)
        xp, kc, ks_, wn, sh, cv = _layer(
            xp, p, jnp.zeros((Bp, H_A, DK_A, DV_A), xp.dtype),
            jnp.zeros((Bp, CONV_K - 1, C_CONV), xp.dtype), mem_kv, nsa_prompt)
        cmp_p.append(kc); slc_p.append(ks_); win_p.append(wn); hg_p.append(sh); cv_p.append(cv)
        mem_p.append(mem_kv)
        xs, kc, ks_, wn, sh, cv = _layer(xs, p, state_hgrn[l], state_conv[l], cache_mem[l], nsa_sample)
        cmp_s.append(kc); slc_s.append(ks_); win_s.append(wn); hg_s.append(sh); cv_s.append(cv)
    return (xp, xs,
            jnp.stack(cmp_p, axis=1), jnp.stack(cmp_s, axis=1),
            jnp.stack(slc_p, axis=1), jnp.stack(slc_s, axis=1),
            jnp.stack(win_p, axis=0), jnp.stack(win_s, axis=0),
            jnp.stack(hg_p, axis=0), jnp.stack(hg_s, axis=0),
            jnp.stack(cv_p, axis=0), jnp.stack(cv_s, axis=0),
            jnp.stack(mem_p, axis=0))
```

```python
import functools

import numpy as np
import jax
import jax.numpy as jnp
from jax import lax
from jax.experimental import pallas as pl
from jax.experimental.pallas import tpu as pltpu

F32 = jnp.float32
BF16 = jnp.bfloat16
HIGHEST = lax.Precision.HIGHEST

H_A, DK_A, DV_A, CHUNK_A = 4, 128, 128, 16
H_B, G_B, HG_B, HD_B = 8, 2, 4, 64
L_CMP, D_CMP, L_SLC, N_SEL, WINDOW = 32, 16, 64, 16, 512
C_CONV, CONV_K = 512, 31
NX_H = 4
LN_EPS, RMS_EPS = 1e-5, 1e-6
PAD_ROWS = 16

MIX_A = H_A * DK_A
MIX_B = H_B * HD_B
KV_B = 2 * G_B * HD_B
COL_QA, COL_FA, COL_IA, COL_GA = 0, 512, 1024, 1536
COL_MG = 2048
V7X_VMEM_LIMIT = 56 * 2**20


def _cparams(sem, vmem_mb=None):
    return pltpu.CompilerParams(dimension_semantics=sem,
                                vmem_limit_bytes=None if vmem_mb is None else vmem_mb * 2**20)


def _ln(y, g, b):
    mu = jnp.mean(y, axis=-1, keepdims=True)
    d = y - mu
    var = jnp.mean(d * d, axis=-1, keepdims=True)
    return d * lax.rsqrt(var + LN_EPS) * g + b


def _sigmoid(x):
    return 1.0 / (1.0 + jnp.exp(-x))


def _mm_kernel(x_ref, w_ref, o_ref, xb_ref):
    @pl.when(pl.program_id(1) == 0)
    def _():
        xb_ref[...] = x_ref[...].astype(BF16)

    o_ref[...] = jnp.dot(xb_ref[...], w_ref[...], preferred_element_type=F32)


def _matmul(x, w, tm, tn):
    m, k = x.shape
    n = w.shape[1]
    tm, tn = min(tm, m), min(tn, n)
    return pl.pallas_call(
        _mm_kernel,
        grid=(m // tm, n // tn),
        in_specs=[pl.BlockSpec((tm, k), lambda i, j: (i, 0)),
                  pl.BlockSpec((k, tn), lambda i, j: (0, j))],
        out_specs=pl.BlockSpec((tm, tn), lambda i, j: (i, j)),
        out_shape=jax.ShapeDtypeStruct((m, n), F32),
        scratch_shapes=[pltpu.VMEM((tm, k), BF16)],
        compiler_params=_cparams(("parallel", "arbitrary"), 40),
        name="proj_matmul",
    )(x, w)


def _mlp_kernel(alpha, x_ref, wu_ref, wd_ref, g_ref, b_ref, o_ref, xb_ref, acc_ref):
    j = pl.program_id(1)

    @pl.when(j == 0)
    def _():
        xb_ref[...] = x_ref[...].astype(BF16)
        acc_ref[...] = jnp.zeros_like(acc_ref)

    h = jnp.dot(xb_ref[...], wu_ref[...], preferred_element_type=F32)
    h = jnp.square(jnp.maximum(h, 0.0)).astype(BF16)
    acc_ref[...] += jnp.dot(h, wd_ref[...], preferred_element_type=F32)

    @pl.when(j == pl.num_programs(1) - 1)
    def _():
        o_ref[...] = _ln(alpha * x_ref[...] + acc_ref[...], g_ref[...], b_ref[...])


def _mlp(x, w_up, w_down, g, b, alpha, tm=512, tf=1024):
    m, d = x.shape
    ff = w_up.shape[1]
    tm = min(tm, m)
    return pl.pallas_call(
        functools.partial(_mlp_kernel, alpha),
        grid=(m // tm, ff // tf),
        in_specs=[pl.BlockSpec((tm, d), lambda i, j: (i, 0)),
                  pl.BlockSpec((d, tf), lambda i, j: (0, j)),
                  pl.BlockSpec((tf, d), lambda i, j: (j, 0)),
                  pl.BlockSpec((1, d), lambda i, j: (0, 0)),
                  pl.BlockSpec((1, d), lambda i, j: (0, 0))],
        out_specs=pl.BlockSpec((tm, d), lambda i, j: (i, 0)),
        out_shape=jax.ShapeDtypeStruct((m, d), F32),
        scratch_shapes=[pltpu.VMEM((tm, d), BF16), pltpu.VMEM((tm, d), F32)],
        compiler_params=_cparams(("parallel", "arbitrary"), 48),
        name="mlp",
    )(x, w_up, w_down, g, b)


def _merge_kernel(alpha, ya_ref, yb_ref, yc_ref, ma_ref, mb_ref, mc_ref, x_ref,
                  wpa_ref, wpb_ref, wpc_ref, wout_ref, g_ref, b_ref, o_ref):
    def branch(y_ref, m_ref, w_ref):
        return _sigmoid(m_ref[...]) * jnp.dot(y_ref[...].astype(BF16), w_ref[...], preferred_element_type=F32)

    merged = branch(ya_ref, ma_ref, wpa_ref) + branch(yb_ref, mb_ref, wpb_ref) + branch(yc_ref, mc_ref, wpc_ref)
    y = jnp.dot(merged.astype(BF16), wout_ref[...], preferred_element_type=F32)
    o_ref[...] = _ln(alpha * x_ref[...] + y, g_ref[...], b_ref[...])


def _merge(ya, yb, yc, proj, x, wpa, wpb, wpc, wout, g, b, alpha, tm=256):
    m, d = x.shape
    tm = min(tm, m)
    mg0 = COL_MG // d
    row = lambda i: (i, 0)
    const = lambda i: (0, 0)
    return pl.pallas_call(
        functools.partial(_merge_kernel, alpha),
        grid=(m // tm,),
        in_specs=[pl.BlockSpec((tm, MIX_A), row), pl.BlockSpec((tm, MIX_B), row), pl.BlockSpec((tm, C_CONV), row),
                  pl.BlockSpec((tm, d), lambda i: (i, mg0)), pl.BlockSpec((tm, d), lambda i: (i, mg0 + 1)),
                  pl.BlockSpec((tm, d), lambda i: (i, mg0 + 2)),
                  pl.BlockSpec((tm, d), row),
                  pl.BlockSpec((MIX_A, d), const), pl.BlockSpec((MIX_B, d), const), pl.BlockSpec((C_CONV, d), const),
                  pl.BlockSpec((d, d), const), pl.BlockSpec((1, d), const), pl.BlockSpec((1, d), const)],
        out_specs=pl.BlockSpec((tm, d), row),
        out_shape=jax.ShapeDtypeStruct((m, d), F32),
        compiler_params=_cparams(("parallel",), 48),
        name="merge_out",
    )(ya, yb, yc, proj, proj, proj, x, wpa, wpb, wpc, wout, g, b)


def _xattn_kernel(alpha, x_ref, kv_ref, wq_ref, wo_ref, g_ref, b_ref, o_ref):
    x = x_ref[...]
    d = x.shape[-1]
    hd = d // NX_H
    q = jnp.dot(x.astype(BF16), wq_ref[...], preferred_element_type=F32)
    outs = []
    for h in range(NX_H):
        qh = q[:, h * hd:(h + 1) * hd].astype(BF16)
        kh = kv_ref[:, h * hd:(h + 1) * hd].astype(BF16)
        vh = kv_ref[:, d + h * hd:d + (h + 1) * hd].astype(BF16)
        s = lax.dot_general(qh, kh, (((1,), (1,)), ((), ())), preferred_element_type=F32) * (hd ** -0.5)
        e = jnp.exp(s - jnp.max(s, axis=-1, keepdims=True))
        p = e / jnp.sum(e, axis=-1, keepdims=True)
        outs.append(jnp.dot(p.astype(BF16), vh, preferred_element_type=F32))
    o = jnp.concatenate(outs, axis=-1)
    y = jnp.dot(o.astype(BF16), wo_ref[...], preferred_element_type=F32)
    o_ref[...] = _ln(alpha * x + y, g_ref[...], b_ref[...])


def _xattn(x, kv, wq, wo, g, b, alpha, tm=256):
    bsz, t, d = x.shape
    n_mem = kv.shape[1]
    tm = min(tm, t)
    const = lambda i, j: (0, 0)
    return pl.pallas_call(
        functools.partial(_xattn_kernel, alpha),
        grid=(bsz, t // tm),
        in_specs=[pl.BlockSpec((None, tm, d), lambda i, j: (i, j, 0)),
                  pl.BlockSpec((None, n_mem, 2 * d), lambda i, j: (i, 0, 0)),
                  pl.BlockSpec((d, d), const), pl.BlockSpec((d, d), const),
                  pl.BlockSpec((1, d), const), pl.BlockSpec((1, d), const)],
        out_specs=pl.BlockSpec((None, tm, d), lambda i, j: (i, j, 0)),
        out_shape=jax.ShapeDtypeStruct((bsz, t, d), F32),
        compiler_params=_cparams(("parallel", "parallel"), 48),
        name="xattn",
    )(x, kv, wq, wo, g, b)


CONV_HALO = 32
CONV_SUB = 32


def _conv_kernel(rt, n_valid_last, a_ref, gt_ref, pre_ref, w_ref, cb_ref, g_ref, b_ref, y_ref, st_ref, ue_ref):
    t = pl.program_id(1)
    off = CONV_HALO - (CONV_K - 1)

    @pl.when(t == 0)
    def _():
        ue_ref[0:off, :] = jnp.zeros((off, C_CONV), F32)
        ue_ref[off:CONV_HALO, :] = pre_ref[...]

    ue_ref[CONV_HALO:CONV_HALO + rt, :] = a_ref[...] * _sigmoid(gt_ref[...])
    sub = min(CONV_SUB, rt)
    for r0 in range(0, rt, sub):
        acc = cb_ref[...] + w_ref[0:1, :] * ue_ref[off + r0:off + r0 + sub, :]
        for j in range(1, CONV_K):
            acc = acc + w_ref[j:j + 1, :] * ue_ref[off + r0 + j:off + r0 + j + sub, :]
        y = _ln(acc, g_ref[...], b_ref[...])
        y_ref[r0:r0 + sub, :] = y * _sigmoid(y)

    @pl.when(t == pl.num_programs(1) - 1)
    def _():
        st_ref[...] = ue_ref[off + n_valid_last:off + n_valid_last + CONV_K - 1, :]

    ue_ref[0:CONV_HALO, :] = ue_ref[rt:rt + CONV_HALO, :]


def _conv(proj3, col_a, prefix, w, cb, g, b, n_valid, rt=256):
    bsz, t, _ = proj3.shape
    rt = min(rt, t)
    n_valid_last = n_valid - (t - rt)
    vec = lambda i, j: (0, 0)
    return pl.pallas_call(
        functools.partial(_conv_kernel, rt, n_valid_last),
        grid=(bsz, t // rt),
        in_specs=[pl.BlockSpec((None, rt, C_CONV), lambda i, j: (i, j, col_a)),
                  pl.BlockSpec((None, rt, C_CONV), lambda i, j: (i, j, col_a + 1)),
                  pl.BlockSpec((None, CONV_K - 1, C_CONV), lambda i, j: (i, 0, 0)),
                  pl.BlockSpec((CONV_K, C_CONV), vec), pl.BlockSpec((1, C_CONV), vec),
                  pl.BlockSpec((1, C_CONV), vec), pl.BlockSpec((1, C_CONV), vec)],
        out_specs=[pl.BlockSpec((None, rt, C_CONV), lambda i, j: (i, j, 0)),
                   pl.BlockSpec((None, CONV_K - 1, C_CONV), lambda i, j: (i, 0, 0))],
        out_shape=[jax.ShapeDtypeStruct((bsz, t, C_CONV), F32),
                   jax.ShapeDtypeStruct((bsz, CONV_K - 1, C_CONV), F32)],
        scratch_shapes=[pltpu.VMEM((rt + CONV_HALO, C_CONV), F32)],
        compiler_params=_cparams(("parallel", "arbitrary"), 32),
        name="conformer_conv",
    )(proj3, proj3, prefix, w, cb, g, b)


def _hgrn_kernel(tt, n_valid, q_ref, f_ref, i_ref, g_ref, la_ref, l1_ref, oml_ref, gn_ref, s0_ref,
                 y_ref, s_ref, st_ref, qs_ref, kk_ref, bb_ref):
    c = CHUNK_A
    t = pl.program_id(1)

    @pl.when(t == 0)
    def _():
        for h in range(H_A):
            st_ref[h] = s0_ref[h].T

    z = f_ref[...]
    cc = l1_ref[...] + (jnp.minimum(z, 0.0) - jnp.log1p(jnp.exp(-jnp.abs(z))))
    a = la_ref[...]
    logf = jnp.maximum(a, cc) + jnp.log1p(jnp.exp(-jnp.abs(a - cc)))
    k = oml_ref[...] * _sigmoid(-z)
    row = lax.broadcasted_iota(jnp.int32, (tt, 1), 0)
    if n_valid < tt:
        logf = jnp.where(row < n_valid, logf, 0.0)
        k = jnp.where(row < n_valid, k, 0.0)
    b = logf
    rc = row & (c - 1)
    sh = 1
    while sh < c:
        b = b + jnp.where(rc >= sh, pltpu.roll(b, sh, 0), 0.0)
        sh *= 2
    q = q_ref[...]
    qs_ref[...] = q * _sigmoid(q)
    kk_ref[...] = k
    bb_ref[...] = b

    rowc = lax.broadcasted_iota(jnp.int32, (c, 1), 0)

    def chunk(ci, carry):
        r0 = pl.multiple_of(ci * c, c)
        for h in range(H_A):
            hs = slice(h * DK_A, (h + 1) * DK_A)
            qc = qs_ref[pl.ds(r0, c), hs]
            kc = kk_ref[pl.ds(r0, c), hs]
            bc = bb_ref[pl.ds(r0, c), hs]
            vc = i_ref[pl.ds(r0, c), hs]
            bl = bc[c - 1:c, :]
            st = st_ref[h]
            qe = (qc * jnp.exp(bc)).astype(BF16)
            o = lax.dot_general(qe, st.astype(BF16), (((1,), (1,)), ((), ())), preferred_element_type=F32)
            for s in range(c):
                e = jnp.exp(jnp.where(rowc >= s, bc - bc[s:s + 1, :], -jnp.inf))
                att = jnp.sum(qc * e * kc[s:s + 1, :], axis=-1, keepdims=True)
                o = o + att * vc[s:s + 1, :]
            y_ref[pl.ds(r0, c), hs] = o
            kd = (kc * jnp.exp(bl - bc)).astype(BF16)
            u = lax.dot_general(vc.astype(BF16), kd, (((0,), (0,)), ((), ())), preferred_element_type=F32)
            st_ref[h] = st * jnp.exp(bl) + u
        return carry

    lax.fori_loop(0, tt // c, chunk, 0)

    g = g_ref[...]
    gate = g * _sigmoid(g)
    for h in range(H_A):
        hs = slice(h * DV_A, (h + 1) * DV_A)
        o = y_ref[:, hs]
        o = o * lax.rsqrt(jnp.mean(o * o, axis=-1, keepdims=True) + RMS_EPS) * gn_ref[...]
        y_ref[:, hs] = o * gate[:, hs]

    @pl.when(t == pl.num_programs(1) - 1)
    def _():
        for h in range(H_A):
            s_ref[h] = st_ref[h].T


def _hgrn(proj3, lb, gnorm, s0, n_valid, tt=256):
    bsz, t, _ = proj3.shape
    tt = min(tt, t)
    lb = lb.reshape(1, MIX_A).astype(F32)
    la, l1, oml = jnp.log(lb), jnp.log1p(-lb), 1.0 - lb
    vec = lambda i, j: (0, 0)
    col = lambda cb: pl.BlockSpec((None, tt, MIX_A), lambda i, j: (i, j, cb))
    st = pl.BlockSpec((None, H_A, DK_A, DV_A), lambda i, j: (i, 0, 0, 0))
    return pl.pallas_call(
        functools.partial(_hgrn_kernel, tt, n_valid if t == tt else tt),
        grid=(bsz, t // tt),
        in_specs=[col(0), col(1), col(2), col(3),
                  pl.BlockSpec((1, MIX_A), vec), pl.BlockSpec((1, MIX_A), vec), pl.BlockSpec((1, MIX_A), vec),
                  pl.BlockSpec((1, DV_A), vec), st],
        out_specs=[pl.BlockSpec((None, tt, MIX_A), lambda i, j: (i, j, 0)), st],
        out_shape=[jax.ShapeDtypeStruct((bsz, t, MIX_A), F32),
                   jax.ShapeDtypeStruct((bsz, H_A, DK_A, DV_A), F32)],
        scratch_shapes=[pltpu.VMEM((H_A, DV_A, DK_A), F32), pltpu.VMEM((tt, MIX_A), F32),
                        pltpu.VMEM((tt, MIX_A), F32), pltpu.VMEM((tt, MIX_A), F32)],
        compiler_params=_cparams(("parallel", "arbitrary"), 32),
        name="hgrn2",
    )(proj3, proj3, proj3, proj3, la, l1, oml, gnorm.reshape(1, DV_A), s0)


def _masked_softmax(s, mask):
    m = jnp.max(jnp.where(mask, s, -jnp.inf), axis=-1, keepdims=True)
    m = jnp.where(jnp.isfinite(m), m, 0.0)
    e = jnp.where(mask, jnp.exp(s - m), 0.0)
    d = jnp.sum(e, axis=-1, keepdims=True)
    return e / jnp.where(d > 0, d, 1.0)


def _compress_jnp(kv, w1, b1, w2, b2):
    bsz, t = kv.shape[:2]
    m = L_CMP // D_CMP
    n_ch = t // D_CMP
    n_cmp = n_ch - m + 1
    x = kv[:, :n_ch * D_CMP].reshape(bsz, n_ch, D_CMP, 2, G_B, HD_B)
    part = jnp.einsum('bcjegd,ehjdf->bhcegf', x, w1.reshape(2, m, D_CMP, HD_B, HD_B))
    h = sum(part[:, i, i:i + n_cmp] for i in range(m)) + b1[:, None, :]
    out = jnp.einsum('bnegf,efo->bnego', jax.nn.silu(h), w2) + b2[:, None, :]
    return out, jnp.arange(n_cmp) * D_CMP + (L_CMP - 1)


def _select_blocks(p_c, q_pos, n_sb):
    m, r = L_CMP // D_CMP, L_SLC // D_CMP
    n_cmp = p_c.shape[-1]
    imp = p_c.sum(axis=-3)
    pad = [(0, 0)] * (imp.ndim - 1)
    imp = jnp.pad(imp, pad + [(m - 1, m - 1)])
    chunk = sum(imp[..., m - 1 - n: m - 1 - n + n_cmp + m - 1] for n in range(m))
    chunk = jnp.pad(chunk, pad + [(0, n_sb * r - chunk.shape[-1])])
    blk_score = chunk.reshape(chunk.shape[:-1] + (n_sb, r)).sum(-1)
    blk = jnp.arange(n_sb)[None, :]
    cur = (q_pos // L_SLC)[:, None]
    forced = (blk == 0) | (blk == cur) | (blk == cur - 1)
    score = jnp.where(blk <= cur, jnp.where(forced, jnp.inf, blk_score), -jnp.inf)
    si, sj = score[..., :, None], score[..., None, :]
    idx = jnp.arange(n_sb)
    ahead = (si > sj) | ((si == sj) & (idx[:, None] < idx[None, :]))
    rank = ahead.sum(axis=-2)
    return (rank < min(N_SEL, n_sb)) & (blk <= cur)


def _nsa_prompt_jnp(q, gates, kvc, kvs, kvw, cmp_w):
    bsz, t = q.shape[:2]
    scale = HD_B ** -0.5
    kv_c, c_end = _compress_jnp(kvc, *cmp_w)
    n_sb = -(-t // L_SLC)
    pos = jnp.arange(t)

    def one(args):
        qb, gb, kc, ks, kw = args
        s = jnp.einsum('qghd,ngd->ghqn', qb, kc[:, 0]) * scale
        p_c = _masked_softmax(s, c_end[None, :] <= pos[:, None])
        o_c = jnp.einsum('ghqn,ngd->qghd', p_c, kc[:, 1])
        sel = _select_blocks(p_c, pos, n_sb)
        kmask = jnp.repeat(sel, L_SLC, axis=-1)[..., :t] & (pos[None, :] <= pos[:, None])
        s = jnp.einsum('qghd,kgd->ghqk', qb, ks[:, 0]) * scale
        p_s = _masked_softmax(s, kmask[:, None])
        o_s = jnp.einsum('ghqk,kgd->qghd', p_s, ks[:, 1])
        s = jnp.einsum('qghd,kgd->ghqk', qb, kw[:, 0]) * scale
        wmask = (pos[None, :] <= pos[:, None]) & (pos[None, :] > pos[:, None] - WINDOW)
        p_w = _masked_softmax(s, wmask)
        o_w = jnp.einsum('ghqk,kgd->qghd', p_w, kw[:, 1])
        g = jax.nn.sigmoid(gb)
        return g[..., 0:1] * o_c + g[..., 1:2] * o_s + g[..., 2:3] * o_w

    return lax.map(one, (q, gates, kv_c, kvs, kvw))


def _nsa_sample_jnp(q, gates, kvc, kvs, kvw, cmp_w, cache_cmp, cache_slc, l, cache_win_l, page_table):
    bd, sd = q.shape[:2]
    past = page_table.shape[1] * cache_cmp.shape[2]
    win_buf = cache_win_l.shape[1]
    scale = HD_B ** -0.5
    tot = past + sd
    full_c = jnp.concatenate([cache_cmp[page_table, l].reshape(bd, past, 2, G_B, HD_B), kvc], axis=1)
    kv_c, c_end = _compress_jnp(full_c, *cmp_w)
    full_s = jnp.concatenate([cache_slc[page_table, l].reshape(bd, past, 2, G_B, HD_B), kvs], axis=1)
    n_sb = -(-tot // L_SLC)
    full_s = jnp.pad(full_s, ((0, 0), (0, n_sb * L_SLC - tot), (0, 0), (0, 0), (0, 0)))
    kw = jnp.concatenate([cache_win_l, kvw], axis=1)
    pos = past + jnp.arange(sd)
    wpos = past - win_buf + jnp.arange(win_buf + sd)
    s = jnp.einsum('bqghd,bngd->bghqn', q, kv_c[:, :, 0]) * scale
    p_c = _masked_softmax(s, c_end[None, :] <= pos[:, None])
    o_c = jnp.einsum('bghqn,bngd->bqghd', p_c, kv_c[:, :, 1])
    sel = _select_blocks(p_c, pos, n_sb)
    k_pos = jnp.arange(n_sb * L_SLC)
    kmask = jnp.repeat(sel, L_SLC, axis=-1) & (k_pos[None, :] <= pos[:, None])
    s = jnp.einsum('bqghd,bkgd->bghqk', q, full_s[:, :, 0]) * scale
    p_s = _masked_softmax(s, kmask[:, :, None])
    o_s = jnp.einsum('bghqk,bkgd->bqghd', p_s, full_s[:, :, 1])
    s = jnp.einsum('bqghd,bngd->bghqn', q, kw[:, :, 0]) * scale
    wp, qp = wpos[None, :], pos[:, None]
    p_w = _masked_softmax(s, (wp <= qp) & (wp > qp - WINDOW) & (wp >= 0))
    o_w = jnp.einsum('bghqn,bngd->bqghd', p_w, kw[:, :, 1])
    g = jax.nn.sigmoid(gates)
    return g[..., 0:1] * o_c + g[..., 1:2] * o_s + g[..., 2:3] * o_w, kw[:, -win_buf:]


def _prep_w_in(w, d):
    o = np.cumsum([0, MIX_A, MIX_A, MIX_A, MIX_A, MIX_B, KV_B, KV_B, KV_B, 3 * H_B, 2 * C_CONV, 3 * d])
    parts = [w[:, o[0]:o[4]], w[:, o[10]:o[11]], w[:, o[4]:o[5]], w[:, o[9]:o[10]], w[:, o[5]:o[8]], w[:, o[8]:o[9]]]
    n = sum(p.shape[1] for p in parts)
    n_pad = -(-n // 512) * 512
    parts.append(jnp.zeros((w.shape[0], n_pad - n), w.dtype))
    return jnp.concatenate(parts, axis=1).astype(BF16)


def _layer(x3, n_valid, p, s0, conv_prefix, mem_kv, nsa_fn, alpha):
    bsz, t, d = x3.shape
    m = bsz * t
    col_qb = COL_MG + 3 * d
    col_glu = col_qb + MIX_B
    col_kv = col_glu + 2 * C_CONV
    col_gb = col_kv + 3 * KV_B
    proj = _matmul(x3.reshape(m, d), p['w_in'], 1024, 512)
    proj3 = proj.reshape(bsz, t, -1)
    ya, s_new = _hgrn(proj3, p['lb'], p['hg_norm'], s0, n_valid)
    yc, conv_state = _conv(proj3, col_glu // C_CONV, conv_prefix, p['conv_w'], p['conv_b'],
                           p['conv_ln_g'], p['conv_ln_b'], n_valid)
    kv_shape = (bsz, n_valid, 2, G_B, HD_B)
    qb = proj3[:, :n_valid, col_qb:col_qb + MIX_B].reshape(bsz, n_valid, G_B, HG_B, HD_B)
    gb = proj3[:, :n_valid, col_gb:col_gb + 3 * H_B].reshape(bsz, n_valid, G_B, HG_B, 3)
    kvc = proj3[:, :n_valid, col_kv:col_kv + KV_B].reshape(kv_shape)
    kvs = proj3[:, :n_valid, col_kv + KV_B:col_kv + 2 * KV_B].reshape(kv_shape)
    kvw = proj3[:, :n_valid, col_kv + 2 * KV_B:col_kv + 3 * KV_B].reshape(kv_shape)
    yb, nsa_extra = nsa_fn(qb, gb, kvc, kvs, kvw)
    yb = jnp.pad(yb.reshape(bsz, n_valid, MIX_B), ((0, 0), (0, t - n_valid), (0, 0)))
    x1 = _merge(ya.reshape(m, MIX_A), yb.reshape(m, MIX_B), yc.reshape(m, C_CONV), proj, x3.reshape(m, d),
                p['w_pa'], p['w_pb'], p['w_pc'], p['w_out'], p['ln_g'][0:1], p['ln_b'][0:1], alpha)
    x2 = _xattn(x1.reshape(bsz, t, d), mem_kv, p['w_xq'], p['w_xo'], p['ln_g'][1:2], p['ln_b'][1:2], alpha)
    x3n = _mlp(x2.reshape(m, d), p['w_up'], p['w_down'], p['ln_g'][2:3], p['ln_b'][2:3], alpha)
    return x3n.reshape(bsz, t, d), kvc, kvs, nsa_extra, s_new, conv_state


def kernel(x_prompt, x_sample, cache_cmp, cache_slc, cache_win, state_hgrn, state_conv, cache_mem, page_table, mem_prompt, w_in, lb_raw, hg_norm, w_cmp1, b_cmp1, w_cmp2, b_cmp2, conv_w, conv_b, conv_ln_g, conv_ln_b, w_pa, w_pb, w_pc, w_out, ln_g, ln_b, w_xq, w_xkv, w_xo, w_up, w_down):
    bp, t, d = x_prompt.shape
    bd, sd = x_sample.shape[:2]
    depth = w_in.shape[0]
    n_mem = mem_prompt.shape[1]
    win_buf = cache_win.shape[2]
    alpha = (2 * depth) ** 0.25
    lb_cum = jnp.cumsum(jax.nn.softmax(lb_raw.astype(F32), axis=0), axis=0)
    lb_all = lb_cum - lb_cum[0]
    xp = x_prompt
    xs = jnp.pad(x_sample, ((0, 0), (0, PAD_ROWS - sd), (0, 0)))
    outs = {k: [] for k in ('cmp_p', 'cmp_s', 'slc_p', 'slc_s', 'win_p', 'win_s', 'hg_p', 'hg_s', 'cv_p', 'cv_s', 'mem_p')}
    for l in range(depth):
        p = {'w_in': _prep_w_in(w_in[l], d), 'lb': lb_all[l], 'hg_norm': hg_norm[l], 'conv_w': conv_w[l],
             'conv_b': conv_b[l].reshape(1, -1), 'conv_ln_g': conv_ln_g[l].reshape(1, -1),
             'conv_ln_b': conv_ln_b[l].reshape(1, -1),
             'w_pa': w_pa[l].astype(BF16), 'w_pb': w_pb[l].astype(BF16), 'w_pc': w_pc[l].astype(BF16),
             'w_out': w_out[l].astype(BF16), 'ln_g': ln_g[l], 'ln_b': ln_b[l],
             'w_xq': w_xq[l].astype(BF16), 'w_xo': w_xo[l].astype(BF16),
             'w_up': w_up[l].astype(BF16), 'w_down': w_down[l].astype(BF16)}
        cmp_w = (w_cmp1[l], b_cmp1[l], w_cmp2[l], b_cmp2[l])

        def nsa_prompt(q, g, kvc, kvs, kvw):
            o = _nsa_prompt_jnp(q, g, kvc, kvs, kvw, cmp_w)
            win = jnp.pad(kvw, ((0, 0), (max(win_buf - t, 0), 0), (0, 0), (0, 0), (0, 0)))[:, -win_buf:]
            return o, win

        def nsa_sample(q, g, kvc, kvs, kvw):
            return _nsa_sample_jnp(q, g, kvc, kvs, kvw, cmp_w, cache_cmp, cache_slc, l,
                                   cache_win[l], page_table)

        mem_kv = _matmul(mem_prompt.reshape(bp * n_mem, d), w_xkv[l].astype(BF16), 1024, 512)
        mem_kv = mem_kv.reshape(bp, n_mem, 2 * d)
        xp, kc, ks_, wn, sh, cv = _layer(xp, t, p, jnp.zeros((bp, H_A, DK_A, DV_A), F32),
                                         jnp.zeros((bp, CONV_K - 1, C_CONV), F32), mem_kv, nsa_prompt, alpha)
        outs['cmp_p'].append(kc); outs['slc_p'].append(ks_); outs['win_p'].append(wn)
        outs['hg_p'].append(sh); outs['cv_p'].append(cv)
        outs['mem_p'].append(mem_kv.reshape(bp, n_mem, 2, NX_H, d // NX_H))
        xs, kc, ks_, wn, sh, cv = _layer(xs, sd, p, state_hgrn[l], state_conv[l],
                                         cache_mem[l].reshape(bd, n_mem, 2 * d), nsa_sample, alpha)
        outs['cmp_s'].append(kc); outs['slc_s'].append(ks_); outs['win_s'].append(wn)
        outs['hg_s'].append(sh); outs['cv_s'].append(cv)
    st = lambda k, ax: jnp.stack(outs[k], axis=ax)
    return (xp, xs[:, :sd],
            st('cmp_p', 1), st('cmp_s', 1), st('slc_p', 1), st('slc_s', 1),
            st('win_p', 0), st('win_s', 0), st('hg_p', 0), st('hg_s', 0),
            st('cv_p', 0), st('cv_s', 0), st('mem_p', 0))
```

```python
import functools

import numpy as np
import jax
import jax.numpy as jnp
from jax import lax
from jax.experimental import pallas as pl
from jax.experimental.pallas import tpu as pltpu

F32 = jnp.float32
BF16 = jnp.bfloat16
HIGHEST = lax.Precision.HIGHEST

H_A, DK_A, DV_A, CHUNK_A = 4, 128, 128, 16
H_B, G_B, HG_B, HD_B = 8, 2, 4, 64
L_CMP, D_CMP, L_SLC, N_SEL, WINDOW = 32, 16, 64, 16, 512
C_CONV, CONV_K = 512, 31
NX_H = 4
LN_EPS, RMS_EPS = 1e-5, 1e-6
PAD_ROWS = 16

MIX_A = H_A * DK_A
MIX_B = H_B * HD_B
KV_B = 2 * G_B * HD_B
COL_QA, COL_FA, COL_IA, COL_GA = 0, 512, 1024, 1536
COL_MG = 2048
V7X_VMEM_LIMIT = 56 * 2**20


def _cparams(sem, vmem_mb=None):
    return pltpu.CompilerParams(dimension_semantics=sem,
                                vmem_limit_bytes=None if vmem_mb is None else vmem_mb * 2**20)


def _ln(y, g, b):
    mu = jnp.mean(y, axis=-1, keepdims=True)
    d = y - mu
    var = jnp.mean(d * d, axis=-1, keepdims=True)
    return d * lax.rsqrt(var + LN_EPS) * g + b


def _sigmoid(x):
    return 1.0 / (1.0 + jnp.exp(-x))


def _mm_kernel(x_ref, w_ref, o_ref, xb_ref):
    @pl.when(pl.program_id(1) == 0)
    def _():
        xb_ref[...] = x_ref[...].astype(BF16)

    o_ref[...] = jnp.dot(xb_ref[...], w_ref[...], preferred_element_type=F32)


def _matmul(x, w, tm, tn):
    m, k = x.shape
    n = w.shape[1]
    tm, tn = min(tm, m), min(tn, n)
    return pl.pallas_call(
        _mm_kernel,
        grid=(m // tm, n // tn),
        in_specs=[pl.BlockSpec((tm, k), lambda i, j: (i, 0)),
                  pl.BlockSpec((k, tn), lambda i, j: (0, j))],
        out_specs=pl.BlockSpec((tm, tn), lambda i, j: (i, j)),
        out_shape=jax.ShapeDtypeStruct((m, n), F32),
        scratch_shapes=[pltpu.VMEM((tm, k), BF16)],
        compiler_params=_cparams(("parallel", "arbitrary"), 40),
        name="proj_matmul",
    )(x, w)


def _mlp_kernel(alpha, x_ref, wu_ref, wd_ref, g_ref, b_ref, o_ref, xb_ref, acc_ref):
    j = pl.program_id(1)

    @pl.when(j == 0)
    def _():
        xb_ref[...] = x_ref[...].astype(BF16)
        acc_ref[...] = jnp.zeros_like(acc_ref)

    h = jnp.dot(xb_ref[...], wu_ref[...], preferred_element_type=F32)
    h = jnp.square(jnp.maximum(h, 0.0)).astype(BF16)
    acc_ref[...] += jnp.dot(h, wd_ref[...], preferred_element_type=F32)

    @pl.when(j == pl.num_programs(1) - 1)
    def _():
        o_ref[...] = _ln(alpha * x_ref[...] + acc_ref[...], g_ref[...], b_ref[...])


def _mlp(x, w_up, w_down, g, b, alpha, tm=512, tf=1024):
    m, d = x.shape
    ff = w_up.shape[1]
    tm = min(tm, m)
    return pl.pallas_call(
        functools.partial(_mlp_kernel, alpha),
        grid=(m // tm, ff // tf),
        in_specs=[pl.BlockSpec((tm, d), lambda i, j: (i, 0)),
                  pl.BlockSpec((d, tf), lambda i, j: (0, j)),
                  pl.BlockSpec((tf, d), lambda i, j: (j, 0)),
                  pl.BlockSpec((1, d), lambda i, j: (0, 0)),
                  pl.BlockSpec((1, d), lambda i, j: (0, 0))],
        out_specs=pl.BlockSpec((tm, d), lambda i, j: (i, 0)),
        out_shape=jax.ShapeDtypeStruct((m, d), F32),
        scratch_shapes=[pltpu.VMEM((tm, d), BF16), pltpu.VMEM((tm, d), F32)],
        compiler_params=_cparams(("parallel", "arbitrary"), 48),
        name="mlp",
    )(x, w_up, w_down, g, b)


def _merge_kernel(alpha, ya_ref, yb_ref, yc_ref, ma_ref, mb_ref, mc_ref, x_ref,
                  wpa_ref, wpb_ref, wpc_ref, wout_ref, g_ref, b_ref, o_ref):
    def branch(y_ref, m_ref, w_ref):
        return _sigmoid(m_ref[...]) * jnp.dot(y_ref[...].astype(BF16), w_ref[...], preferred_element_type=F32)

    merged = branch(ya_ref, ma_ref, wpa_ref) + branch(yb_ref, mb_ref, wpb_ref) + branch(yc_ref, mc_ref, wpc_ref)
    y = jnp.dot(merged.astype(BF16), wout_ref[...], preferred_element_type=F32)
    o_ref[...] = _ln(alpha * x_ref[...] + y, g_ref[...], b_ref[...])


def _merge(ya, yb, yc, proj, x, wpa, wpb, wpc, wout, g, b, alpha, tm=256):
    m, d = x.shape
    tm = min(tm, m)
    mg0 = COL_MG // d
    row = lambda i: (i, 0)
    const = lambda i: (0, 0)
    return pl.pallas_call(
        functools.partial(_merge_kernel, alpha),
        grid=(m // tm,),
        in_specs=[pl.BlockSpec((tm, MIX_A), row), pl.BlockSpec((tm, MIX_B), row), pl.BlockSpec((tm, C_CONV), row),
                  pl.BlockSpec((tm, d), lambda i: (i, mg0)), pl.BlockSpec((tm, d), lambda i: (i, mg0 + 1)),
                  pl.BlockSpec((tm, d), lambda i: (i, mg0 + 2)),
                  pl.BlockSpec((tm, d), row),
                  pl.BlockSpec((MIX_A, d), const), pl.BlockSpec((MIX_B, d), const), pl.BlockSpec((C_CONV, d), const),
                  pl.BlockSpec((d, d), const), pl.BlockSpec((1, d), const), pl.BlockSpec((1, d), const)],
        out_specs=pl.BlockSpec((tm, d), row),
        out_shape=jax.ShapeDtypeStruct((m, d), F32),
        compiler_params=_cparams(("parallel",), 48),
        name="merge_out",
    )(ya, yb, yc, proj, proj, proj, x, wpa, wpb, wpc, wout, g, b)


def _xattn_kernel(alpha, x_ref, kv_ref, wq_ref, wo_ref, g_ref, b_ref, o_ref):
    x = x_ref[...]
    d = x.shape[-1]
    hd = d // NX_H
    q = jnp.dot(x.astype(BF16), wq_ref[...], preferred_element_type=F32)
    outs = []
    for h in range(NX_H):
        qh = q[:, h * hd:(h + 1) * hd].astype(BF16)
        kh = kv_ref[:, h * hd:(h + 1) * hd].astype(BF16)
        vh = kv_ref[:, d + h * hd:d + (h + 1) * hd].astype(BF16)
        s = lax.dot_general(qh, kh, (((1,), (1,)), ((), ())), preferred_element_type=F32) * (hd ** -0.5)
        e = jnp.exp(s - jnp.max(s, axis=-1, keepdims=True))
        p = e / jnp.sum(e, axis=-1, keepdims=True)
        outs.append(jnp.dot(p.astype(BF16), vh, preferred_element_type=F32))
    o = jnp.concatenate(outs, axis=-1)
    y = jnp.dot(o.astype(BF16), wo_ref[...], preferred_element_type=F32)
    o_ref[...] = _ln(alpha * x + y, g_ref[...], b_ref[...])


def _xattn(x, kv, wq, wo, g, b, alpha, tm=256):
    bsz, t, d = x.shape
    n_mem = kv.shape[1]
    tm = min(tm, t)
    const = lambda i, j: (0, 0)
    return pl.pallas_call(
        functools.partial(_xattn_kernel, alpha),
        grid=(bsz, t // tm),
        in_specs=[pl.BlockSpec((None, tm, d), lambda i, j: (i, j, 0)),
                  pl.BlockSpec((None, n_mem, 2 * d), lambda i, j: (i, 0, 0)),
                  pl.BlockSpec((d, d), const), pl.BlockSpec((d, d), const),
                  pl.BlockSpec((1, d), const), pl.BlockSpec((1, d), const)],
        out_specs=pl.BlockSpec((None, tm, d), lambda i, j: (i, j, 0)),
        out_shape=jax.ShapeDtypeStruct((bsz, t, d), F32),
        compiler_params=_cparams(("parallel", "parallel"), 48),
        name="xattn",
    )(x, kv, wq, wo, g, b)


CONV_HALO = 32
CONV_SUB = 32


def _conv_kernel(rt, n_valid_last, a_ref, gt_ref, pre_ref, w_ref, cb_ref, g_ref, b_ref, y_ref, st_ref, ue_ref):
    t = pl.program_id(1)
    off = CONV_HALO - (CONV_K - 1)

    @pl.when(t == 0)
    def _():
        ue_ref[0:off, :] = jnp.zeros((off, C_CONV), F32)
        ue_ref[off:CONV_HALO, :] = pre_ref[...]

    ue_ref[CONV_HALO:CONV_HALO + rt, :] = a_ref[...] * _sigmoid(gt_ref[...])
    sub = min(CONV_SUB, rt)
    for r0 in range(0, rt, sub):
        acc = cb_ref[...] + w_ref[0:1, :] * ue_ref[off + r0:off + r0 + sub, :]
        for j in range(1, CONV_K):
            acc = acc + w_ref[j:j + 1, :] * ue_ref[off + r0 + j:off + r0 + j + sub, :]
        y = _ln(acc, g_ref[...], b_ref[...])
        y_ref[r0:r0 + sub, :] = y * _sigmoid(y)

    @pl.when(t == pl.num_programs(1) - 1)
    def _():
        st_ref[...] = ue_ref[off + n_valid_last:off + n_valid_last + CONV_K - 1, :]

    ue_ref[0:CONV_HALO, :] = ue_ref[rt:rt + CONV_HALO, :]


def _conv(proj3, col_a, prefix, w, cb, g, b, n_valid, rt=256):
    bsz, t, _ = proj3.shape
    rt = min(rt, t)
    n_valid_last = n_valid - (t - rt)
    vec = lambda i, j: (0, 0)
    return pl.pallas_call(
        functools.partial(_conv_kernel, rt, n_valid_last),
        grid=(bsz, t // rt),
        in_specs=[pl.BlockSpec((None, rt, C_CONV), lambda i, j: (i, j, col_a)),
                  pl.BlockSpec((None, rt, C_CONV), lambda i, j: (i, j, col_a + 1)),
                  pl.BlockSpec((None, CONV_K - 1, C_CONV), lambda i, j: (i, 0, 0)),
                  pl.BlockSpec((CONV_K, C_CONV), vec), pl.BlockSpec((1, C_CONV), vec),
                  pl.BlockSpec((1, C_CONV), vec), pl.BlockSpec((1, C_CONV), vec)],
        out_specs=[pl.BlockSpec((None, rt, C_CONV), lambda i, j: (i, j, 0)),
                   pl.BlockSpec((None, CONV_K - 1, C_CONV), lambda i, j: (i, 0, 0))],
        out_shape=[jax.ShapeDtypeStruct((bsz, t, C_CONV), F32),
                   jax.ShapeDtypeStruct((bsz, CONV_K - 1, C_CONV), F32)],
        scratch_shapes=[pltpu.VMEM((rt + CONV_HALO, C_CONV), F32)],
        compiler_params=_cparams(("parallel", "arbitrary"), 32),
        name="conformer_conv",
    )(proj3, proj3, prefix, w, cb, g, b)


def _hgrn_kernel(tt, n_valid, q_ref, f_ref, i_ref, g_ref, la_ref, l1_ref, oml_ref, gn_ref, s0_ref,
                 y_ref, s_ref, st_ref, qs_ref, kk_ref, bb_ref):
    c = CHUNK_A
    t = pl.program_id(1)

    @pl.when(t == 0)
    def _():
        for h in range(H_A):
            st_ref[h] = s0_ref[h].T

    z = f_ref[...]
    cc = l1_ref[...] + (jnp.minimum(z, 0.0) - jnp.log1p(jnp.exp(-jnp.abs(z))))
    a = la_ref[...]
    logf = jnp.maximum(a, cc) + jnp.log1p(jnp.exp(-jnp.abs(a - cc)))
    k = oml_ref[...] * _sigmoid(-z)
    row = lax.broadcasted_iota(jnp.int32, (tt, 1), 0)
    if n_valid < tt:
        logf = jnp.where(row < n_valid, logf, 0.0)
        k = jnp.where(row < n_valid, k, 0.0)
    b = logf
    rc = row & (c - 1)
    sh = 1
    while sh < c:
        b = b + jnp.where(rc >= sh, pltpu.roll(b, sh, 0), 0.0)
        sh *= 2
    q = q_ref[...]
    qs_ref[...] = q * _sigmoid(q)
    kk_ref[...] = k
    bb_ref[...] = b

    rowc = lax.broadcasted_iota(jnp.int32, (c, 1), 0)

    def chunk(ci, carry):
        r0 = pl.multiple_of(ci * c, c)
        for h in range(H_A):
            hs = slice(h * DK_A, (h + 1) * DK_A)
            qc = qs_ref[pl.ds(r0, c), hs]
            kc = kk_ref[pl.ds(r0, c), hs]
            bc = bb_ref[pl.ds(r0, c), hs]
            vc = i_ref[pl.ds(r0, c), hs]
            bl = bc[c - 1:c, :]
            st = st_ref[h]
            qe = (qc * jnp.exp(bc)).astype(BF16)
            o = lax.dot_general(qe, st.astype(BF16), (((1,), (1,)), ((), ())), preferred_element_type=F32)
            for s in range(c):
                e = jnp.exp(jnp.where(rowc >= s, bc - bc[s:s + 1, :], -jnp.inf))
                att = jnp.sum(qc * e * kc[s:s + 1, :], axis=-1, keepdims=True)
                o = o + att * vc[s:s + 1, :]
            y_ref[pl.ds(r0, c), hs] = o
            kd = (kc * jnp.exp(bl - bc)).astype(BF16)
            u = lax.dot_general(vc.astype(BF16), kd, (((0,), (0,)), ((), ())), preferred_element_type=F32)
            st_ref[h] = st * jnp.exp(bl) + u
        return carry

    lax.fori_loop(0, tt // c, chunk, 0)

    g = g_ref[...]
    gate = g * _sigmoid(g)
    for h in range(H_A):
        hs = slice(h * DV_A, (h + 1) * DV_A)
        o = y_ref[:, hs]
        o = o * lax.rsqrt(jnp.mean(o * o, axis=-1, keepdims=True) + RMS_EPS) * gn_ref[...]
        y_ref[:, hs] = o * gate[:, hs]

    @pl.when(t == pl.num_programs(1) - 1)
    def _():
        for h in range(H_A):
            s_ref[h] = st_ref[h].T


def _hgrn(proj3, lb, gnorm, s0, n_valid, tt=256):
    bsz, t, _ = proj3.shape
    tt = min(tt, t)
    lb = lb.reshape(1, MIX_A).astype(F32)
    la, l1, oml = jnp.log(lb), jnp.log1p(-lb), 1.0 - lb
    vec = lambda i, j: (0, 0)
    col = lambda cb: pl.BlockSpec((None, tt, MIX_A), lambda i, j: (i, j, cb))
    st = pl.BlockSpec((None, H_A, DK_A, DV_A), lambda i, j: (i, 0, 0, 0))
    return pl.pallas_call(
        functools.partial(_hgrn_kernel, tt, n_valid if t == tt else tt),
        grid=(bsz, t // tt),
        in_specs=[col(0), col(1), col(2), col(3),
                  pl.BlockSpec((1, MIX_A), vec), pl.BlockSpec((1, MIX_A), vec), pl.BlockSpec((1, MIX_A), vec),
                  pl.BlockSpec((1, DV_A), vec), st],
        out_specs=[pl.BlockSpec((None, tt, MIX_A), lambda i, j: (i, j, 0)), st],
        out_shape=[jax.ShapeDtypeStruct((bsz, t, MIX_A), F32),
                   jax.ShapeDtypeStruct((bsz, H_A, DK_A, DV_A), F32)],
        scratch_shapes=[pltpu.VMEM((H_A, DV_A, DK_A), F32), pltpu.VMEM((tt, MIX_A), F32),
                        pltpu.VMEM((tt, MIX_A), F32), pltpu.VMEM((tt, MIX_A), F32)],
        compiler_params=_cparams(("parallel", "arbitrary"), 32),
        name="hgrn2",
    )(proj3, proj3, proj3, proj3, la, l1, oml, gnorm.reshape(1, DV_A), s0)


NEG_BIG = -1e30
SEL_PER_CMP = L_SLC // D_CMP


def _cmp_weights(w1, b1, w2, b2):
    m = L_CMP // D_CMP
    eye_e, eye_g = jnp.eye(2, dtype=F32), jnp.eye(G_B, dtype=F32)
    w1r = w1.reshape(2, m, D_CMP, HD_B, HD_B)
    w1big = jnp.einsum('ehjdf,ea,gb->jegdhabf', w1r, eye_e, eye_g).reshape(D_CMP * KV_B, m * KV_B)
    w2big = jnp.einsum('efo,ea,gb->egfabo', w2, eye_e, eye_g).reshape(KV_B, KV_B)
    b1big = jnp.broadcast_to(b1[:, None, :], (2, G_B, HD_B)).reshape(1, KV_B)
    b2big = jnp.broadcast_to(b2[:, None, :], (2, G_B, HD_B)).reshape(1, KV_B)
    return w1big.astype(BF16), b1big, w2big.astype(BF16), b2big


def _cmp_to_sel_map(n_cmp_pad, n_cmp, n_sb):
    mm = np.zeros((n_sb, n_cmp_pad), np.float32)
    for n in range(n_cmp):
        for i in (n, n + 1):
            if i // SEL_PER_CMP < n_sb:
                mm[i // SEL_PER_CMP, n] += 1.0
    return mm


def _compress_tail(pp, n_cmp, b1_ref, w2_ref, b2_ref):
    n_ch = pp.shape[0]
    hid = pp[:, :KV_B] + pltpu.roll(pp[:, KV_B:], n_ch - 1, 0) + b1_ref[...]
    hid = hid * _sigmoid(hid)
    kc = jnp.dot(hid.astype(BF16), w2_ref[...], preferred_element_type=F32) + b2_ref[...]
    row = lax.broadcasted_iota(jnp.int32, (n_ch, 1), 0)
    return jnp.where(row < n_cmp, kc, 0.0)


def _nsa_prep_kernel(n_cmp, xc_ref, ks_ref, kw_ref, w1_ref, b1_ref, w2_ref, b2_ref,
                     kck_ref, kcvt_ref, ksk_ref, ksvt_ref, kwk_ref, kwvt_ref):
    half = KV_B // 2
    pp = jnp.dot(xc_ref[...].astype(BF16), w1_ref[...], preferred_element_type=F32)
    kc = _compress_tail(pp, n_cmp, b1_ref, w2_ref, b2_ref)
    kck_ref[...] = kc[:, :half].astype(BF16)
    kcvt_ref[...] = kc[:, half:].T.astype(BF16)
    ksk_ref[...] = ks_ref[:, :half].astype(BF16)
    ksvt_ref[...] = ks_ref[:, half:].T.astype(BF16)
    kwk_ref[...] = kw_ref[:, :half].astype(BF16)
    kwvt_ref[...] = kw_ref[:, half:].T.astype(BF16)


def _nsa_prep(kvc, proj3, col_kv, cmp_big):
    bsz, t, _ = proj3.shape
    n_ch = t // D_CMP
    n_cmp = n_ch - L_CMP // D_CMP + 1
    half = KV_B // 2
    w1big, b1big, w2big, b2big = cmp_big
    xc = kvc.reshape(bsz, n_ch, D_CMP * KV_B)
    cb = col_kv // KV_B
    const = lambda i: (0, 0)
    kspec = lambda n: pl.BlockSpec((None, n, half), lambda i: (i, 0, 0))
    vspec = lambda n: pl.BlockSpec((None, half, n), lambda i: (i, 0, 0))
    return pl.pallas_call(
        functools.partial(_nsa_prep_kernel, n_cmp),
        grid=(bsz,),
        in_specs=[pl.BlockSpec((None, n_ch, D_CMP * KV_B), lambda i: (i, 0, 0)),
                  pl.BlockSpec((None, t, KV_B), lambda i: (i, 0, cb + 1)),
                  pl.BlockSpec((None, t, KV_B), lambda i: (i, 0, cb + 2)),
                  pl.BlockSpec(w1big.shape, const), pl.BlockSpec((1, KV_B), const),
                  pl.BlockSpec((KV_B, KV_B), const), pl.BlockSpec((1, KV_B), const)],
        out_specs=[kspec(n_ch), vspec(n_ch), kspec(t), vspec(t), kspec(t), vspec(t)],
        out_shape=[jax.ShapeDtypeStruct((bsz, n_ch, half), BF16), jax.ShapeDtypeStruct((bsz, half, n_ch), BF16),
                   jax.ShapeDtypeStruct((bsz, t, half), BF16), jax.ShapeDtypeStruct((bsz, half, t), BF16),
                   jax.ShapeDtypeStruct((bsz, t, half), BF16), jax.ShapeDtypeStruct((bsz, half, t), BF16)],
        compiler_params=_cparams(("parallel",), 40),
        name="nsa_prep",
    )(xc, proj3, proj3, w1big, b1big, w2big, b2big)


def _nsa_prompt_kernel(tq, n_sb, q_ref, gb_ref, kck_ref, kcvt_ref, ksk_ref, ksvt_ref, kwk_ref, kwvt_ref, mm_ref, o_ref):
    qi = pl.program_id(1)
    q0 = qi * tq
    scale = HD_B ** -0.5
    n_cp = kck_ref.shape[0]
    w4 = HG_B * tq
    q_t = q_ref[...].T
    g_t = _sigmoid(gb_ref[...]).T
    qpos = q0 + lax.broadcasted_iota(jnp.int32, (1, tq), 1)
    qpos4 = jnp.concatenate([qpos] * HG_B, axis=1)
    zpad = jnp.zeros((HD_B, tq), BF16)

    def flash(k_ref, vt_ref, g, qt, lo, hi, mask_fn):
        def body(kt, carry):
            m, l, acc = carry
            k0 = pl.multiple_of(kt * tq, tq)
            s = jnp.dot(k_ref[pl.ds(k0, tq), :], qt, preferred_element_type=F32) * scale
            kpos = k0 + lax.broadcasted_iota(jnp.int32, (tq, w4), 0)
            mask = mask_fn(k0, kpos)
            m_new = jnp.maximum(m, jnp.max(jnp.where(mask, s, NEG_BIG), axis=0, keepdims=True))
            e = jnp.where(mask, jnp.exp(s - m_new), 0.0)
            a = jnp.exp(m - m_new)
            l = a * l + jnp.sum(e, axis=0, keepdims=True)
            pv = jnp.dot(vt_ref[:, pl.ds(k0, tq)], e.astype(BF16), preferred_element_type=F32)
            return m_new, l, a * acc + pv[g * HD_B:(g + 1) * HD_B, :]

        init = (jnp.full((1, w4), NEG_BIG, F32), jnp.zeros((1, w4), F32), jnp.zeros((HD_B, w4), F32))
        _, l, acc = lax.fori_loop(lo, hi, body, init)
        return acc / jnp.where(l > 0, l, 1.0)

    outs = []
    for g in range(G_B):
        cols = []
        for hg in range(HG_B):
            h = g * HG_B + hg
            qh = q_t[h * HD_B:(h + 1) * HD_B, :].astype(BF16)
            cols.append(jnp.concatenate([qh, zpad] if g == 0 else [zpad, qh], axis=0))
        qt = jnp.concatenate(cols, axis=1)
        s = jnp.dot(kck_ref[...], qt, preferred_element_type=F32) * scale
        c_end = lax.broadcasted_iota(jnp.int32, (n_cp, w4), 0) * D_CMP + (L_CMP - 1)
        mask = c_end <= qpos4
        m = jnp.max(jnp.where(mask, s, NEG_BIG), axis=0, keepdims=True)
        e = jnp.where(mask, jnp.exp(s - m), 0.0)
        d = jnp.sum(e, axis=0, keepdims=True)
        p = e / jnp.where(d > 0, d, 1.0)
        o_c = jnp.dot(kcvt_ref[...], p.astype(BF16), preferred_element_type=F32)[g * HD_B:(g + 1) * HD_B, :]
        imp = p[:, 0:tq]
        for hg in range(1, HG_B):
            imp = imp + p[:, hg * tq:(hg + 1) * tq]
        bs = jnp.dot(mm_ref[...], imp, precision=HIGHEST, preferred_element_type=F32)
        blk = lax.broadcasted_iota(jnp.int32, (n_sb, tq), 0)
        cur = qpos >> 6
        forced = (blk == 0) | (blk == cur) | (blk == cur - 1)
        score = jnp.where(blk <= cur, jnp.where(forced, jnp.inf, bs), -jnp.inf)
        rank = jnp.zeros((n_sb, tq), jnp.int32)
        for i in range(n_sb):
            si = score[i:i + 1, :]
            rank = rank + jnp.where((si > score) | ((si == score) & (blk > i)), 1, 0)
        sel_t = jnp.where((rank < N_SEL) & (blk <= cur), 1.0, 0.0).astype(BF16)

        def sel_mask(k0, kpos):
            ej = lax.broadcasted_iota(jnp.int32, (tq, n_sb), 1)
            ek = lax.broadcasted_iota(jnp.int32, (tq, n_sb), 0)
            expand = jnp.where(ej == ((k0 + ek) >> 6), 1.0, 0.0).astype(BF16)
            mk = jnp.dot(expand, sel_t, preferred_element_type=F32)
            return (jnp.concatenate([mk] * HG_B, axis=1) > 0.5) & (kpos <= qpos4)

        o_s = flash(ksk_ref, ksvt_ref, g, qt, 0, qi + 1, sel_mask)
        o_w = flash(kwk_ref, kwvt_ref, g, qt, jnp.maximum(qi - WINDOW // tq, 0), qi + 1,
                    lambda k0, kpos: (kpos <= qpos4) & (kpos > qpos4 - WINDOW))
        for hg in range(HG_B):
            c = (g * HG_B + hg) * 3
            sl = slice(hg * tq, (hg + 1) * tq)
            outs.append(g_t[c:c + 1, :] * o_c[:, sl] + g_t[c + 1:c + 2, :] * o_s[:, sl]
                        + g_t[c + 2:c + 3, :] * o_w[:, sl])
    o_ref[...] = jnp.concatenate(outs, axis=0).T


def _nsa_prompt(proj3, col_qb, col_gb, prep, tq=128):
    bsz, t, _ = proj3.shape
    kck, kcvt, ksk, ksvt, kwk, kwvt = prep
    n_ch = kck.shape[1]
    n_cmp = n_ch - L_CMP // D_CMP + 1
    n_sb = -(-t // L_SLC)
    half = KV_B // 2
    mm = jnp.asarray(_cmp_to_sel_map(n_ch, n_cmp, n_sb))
    per_b = lambda shape: pl.BlockSpec((None,) + shape, lambda i, j: (i, 0, 0))
    return pl.pallas_call(
        functools.partial(_nsa_prompt_kernel, tq, n_sb),
        grid=(bsz, t // tq),
        in_specs=[pl.BlockSpec((None, tq, MIX_B), lambda i, j: (i, j, col_qb // MIX_B)),
                  pl.BlockSpec((None, tq, 128), lambda i, j: (i, j, col_gb // 128)),
                  per_b((n_ch, half)), per_b((half, n_ch)), per_b((t, half)), per_b((half, t)),
                  per_b((t, half)), per_b((half, t)),
                  pl.BlockSpec((n_sb, n_ch), lambda i, j: (0, 0))],
        out_specs=pl.BlockSpec((None, tq, MIX_B), lambda i, j: (i, j, 0)),
        out_shape=jax.ShapeDtypeStruct((bsz, t, MIX_B), F32),
        compiler_params=_cparams(("parallel", "parallel"), 40),
        name="nsa_prompt",
    )(proj3, proj3, kck, kcvt, ksk, ksvt, kwk, kwvt, mm)


def _cmp_pages_kernel(n_pg, *refs):
    pages, w1_ref, o_ref = refs[2:2 + n_pg], refs[2 + n_pg], refs[3 + n_pg]
    x = jnp.concatenate([r[...] for r in pages], axis=0)
    o_ref[...] = jnp.dot(x.astype(BF16), w1_ref[...], preferred_element_type=F32)


def _cmp_pages(cache, layer, page_table, w1big):
    n_phys, depth, page = cache.shape[:3]
    bsz, n_pages = page_table.shape
    ch_pg = page // D_CMP
    n_pg = next(c for c in (32, 16, 8, 4, 2, 1) if n_pages % c == 0)
    pages = cache.reshape(n_phys, depth, ch_pg, D_CMP * KV_B)

    def page_spec(i):
        return pl.BlockSpec((None, None, ch_pg, D_CMP * KV_B),
                            lambda b, s, pt, lyr: (pt[b * n_pages + s * n_pg + i], lyr[0], 0, 0))

    grid_spec = pltpu.PrefetchScalarGridSpec(
        num_scalar_prefetch=2,
        grid=(bsz, n_pages // n_pg),
        in_specs=[page_spec(i) for i in range(n_pg)] + [pl.BlockSpec(w1big.shape, lambda b, s, pt, lyr: (0, 0))],
        out_specs=pl.BlockSpec((None, n_pg * ch_pg, w1big.shape[1]), lambda b, s, pt, lyr: (b, s, 0)))
    return pl.pallas_call(
        functools.partial(_cmp_pages_kernel, n_pg),
        grid_spec=grid_spec,
        out_shape=jax.ShapeDtypeStruct((bsz, n_pages * ch_pg, w1big.shape[1]), F32),
        compiler_params=_cparams(("parallel", "parallel"), 40),
        name="nsa_cmp_pages",
    )(page_table.reshape(-1), jnp.full((1,), layer, jnp.int32), *([pages] * n_pg), w1big)


def _nsa_score_kernel(past, sd, n_cmp, n_sb, pp_ref, q_ref, cw_ref, nw_ref, b1_ref, w2_ref, b2_ref, mm_ref,
                      oc_ref, ow_ref, idx_ref):
    half = KV_B // 2
    rq = HG_B * sd
    scale = HD_B ** -0.5
    n_ch = pp_ref.shape[0]
    win = cw_ref.shape[0]
    kc = _compress_tail(pp_ref[...], n_cmp, b1_ref, w2_ref, b2_ref)
    kck, kcv = kc[:, :half].astype(BF16), kc[:, half:].astype(BF16)
    qpos = past + lax.broadcasted_iota(jnp.int32, (rq, 1), 0) % sd
    nt = (((1,), (1,)), ((), ()))

    def softmax(parts):
        m = functools.reduce(jnp.maximum, [jnp.max(jnp.where(mk, s, NEG_BIG), axis=-1, keepdims=True) for s, mk in parts])
        es = [jnp.where(mk, jnp.exp(s - m), 0.0) for s, mk in parts]
        d = functools.reduce(jnp.add, [jnp.sum(e, axis=-1, keepdims=True) for e in es])
        return [e / jnp.where(d > 0, d, 1.0) for e in es]

    imps = []
    for g in range(G_B):
        qg = q_ref[g * rq:(g + 1) * rq, :]
        s = lax.dot_general(qg, kck, nt, preferred_element_type=F32) * scale
        c_end = lax.broadcasted_iota(jnp.int32, (rq, n_ch), 1) * D_CMP + (L_CMP - 1)
        p, = softmax([(s, c_end <= qpos)])
        oc_ref[g * rq:(g + 1) * rq, :] = jnp.dot(p.astype(BF16), kcv, preferred_element_type=F32)
        imps.append(functools.reduce(jnp.add, [p[hg * sd:(hg + 1) * sd, :] for hg in range(HG_B)]))
        s1 = lax.dot_general(qg, cw_ref[:, :half].astype(BF16), nt, preferred_element_type=F32) * scale
        s2 = lax.dot_general(qg, nw_ref[:, :half].astype(BF16), nt, preferred_element_type=F32) * scale
        wp1 = past - win + lax.broadcasted_iota(jnp.int32, s1.shape, 1)
        j2 = lax.broadcasted_iota(jnp.int32, s2.shape, 1)
        wp2 = past + j2
        p1, p2 = softmax([(s1, (wp1 <= qpos) & (wp1 > qpos - WINDOW) & (wp1 >= 0)),
                          (s2, (wp2 <= qpos) & (wp2 > qpos - WINDOW) & (j2 < sd))])
        ow_ref[g * rq:(g + 1) * rq, :] = (
            jnp.dot(p1.astype(BF16), cw_ref[:, half:].astype(BF16), preferred_element_type=F32)
            + jnp.dot(p2.astype(BF16), nw_ref[:, half:].astype(BF16), preferred_element_type=F32))
    imp = jnp.concatenate(imps, axis=0)
    rows = G_B * sd
    bs = jnp.dot(imp, mm_ref[...], precision=HIGHEST, preferred_element_type=F32)
    n_lane = bs.shape[1]
    blk = lax.broadcasted_iota(jnp.int32, (rows, n_lane), 1)
    cur = (past + lax.broadcasted_iota(jnp.int32, (rows, 1), 0) % sd) >> 6
    forced = (blk == 0) | (blk == cur) | (blk == cur - 1)
    score = jnp.where(blk <= cur, jnp.where(forced, jnp.inf, bs), -jnp.inf)
    blk_f = blk.astype(F32)
    taken = blk >= n_sb
    lane = lax.broadcasted_iota(jnp.int32, (rows, 128), 1)
    picked = jnp.zeros((rows, 128), F32)
    for it in range(min(N_SEL, n_sb)):
        live = jnp.where(taken, -jnp.inf, score)
        mx = jnp.max(live, axis=-1, keepdims=True)
        ix = jnp.min(jnp.where(~taken & (score == mx), blk_f, float(n_lane)), axis=-1, keepdims=True)
        picked = jnp.where(lane == it, ix, picked)
        taken = taken | (blk_f == ix)
    idx_ref[...] = picked.astype(jnp.int32)


def _nsa_select_kernel(past, sd, n_sb, n_pages, k_sel, *refs):
    pt_ref, ix_ref, lyr_ref = refs[:3]
    blocks = refs[3:3 + k_sel]
    new_ref, q_ref, oc_ref, ow_ref, gt_ref, o_ref = refs[3 + k_sel:]
    half = KV_B // 2
    rq = HG_B * sd
    b, g, i = pl.program_id(0), pl.program_id(1), pl.program_id(2)
    base = ((b * G_B + g) * sd + i) * k_sel
    lane = lax.broadcasted_iota(jnp.int32, (1, k_sel * L_SLC), 1)
    kpos = lane % L_SLC
    ks, vs = [], []
    for j in range(k_sel):
        blk = ix_ref[base + j]
        is_new = blk == n_sb - 1
        kv = jnp.where(is_new, new_ref[...], blocks[j][...])
        ks.append(kv[:, :half])
        vs.append(kv[:, half:])
        kpos = kpos + jnp.where(lane // L_SLC == j, blk * L_SLC, 0)
    k_all = jnp.concatenate(ks, axis=0).astype(BF16)
    v_all = jnp.concatenate(vs, axis=0).astype(BF16)
    s = lax.dot_general(q_ref[...], k_all, (((1,), (1,)), ((), ())), preferred_element_type=F32) * (HD_B ** -0.5)
    row_q = lax.broadcasted_iota(jnp.int32, (rq, 1), 0) % sd
    mask = kpos <= past + row_q
    m = jnp.max(jnp.where(mask, s, NEG_BIG), axis=-1, keepdims=True)
    e = jnp.where(mask, jnp.exp(s - m), 0.0)
    d = jnp.sum(e, axis=-1, keepdims=True)
    p = e / jnp.where(d > 0, d, 1.0)
    o_s = jnp.dot(p.astype(BF16), v_all, preferred_element_type=F32)
    gates = _sigmoid(gt_ref[...])
    y = gates[0] * oc_ref[...] + gates[1] * o_s + gates[2] * ow_ref[...]

    @pl.when(i == 0)
    def _():
        o_ref[...] = jnp.zeros_like(o_ref)

    o_ref[...] += jnp.where(row_q == i, y, 0.0)


def _nsa_sample(proj3, cols, kvs, sd, cmp_big, cache_cmp, cache_slc, cache_win_l, layer, page_table):
    col_qb, col_kv, col_gb = cols
    bsz = proj3.shape[0]
    n_phys, depth, page = cache_cmp.shape[:3]
    n_pages = page_table.shape[1]
    past = n_pages * page
    win = cache_win_l.shape[1]
    half = KV_B // 2
    rq = HG_B * sd
    n_ch = past // D_CMP
    n_cmp = (past + sd) // D_CMP - L_CMP // D_CMP + 1
    n_sb = -(-(past + sd) // L_SLC)
    k_sel = min(N_SEL, n_sb)
    assert (past + sd) // D_CMP == n_ch and past % L_SLC == 0 and sd <= L_SLC and page % L_SLC == 0
    w1big, b1big, w2big, b2big = cmp_big
    pp = _cmp_pages(cache_cmp, layer, page_table, w1big)
    q = proj3[:, :sd, col_qb:col_qb + MIX_B].reshape(bsz, sd, G_B, HG_B, HD_B).transpose(0, 2, 3, 1, 4)
    q = jnp.stack([jnp.pad(q[:, g], ((0, 0), (0, 0), (0, 0), (g * HD_B, half - (g + 1) * HD_B))) for g in range(G_B)], 1)
    q = q.reshape(bsz, G_B * rq, half).astype(BF16)
    gt = proj3[:, :sd, col_gb:col_gb + 3 * H_B].reshape(bsz, sd, G_B, HG_B, 3).transpose(0, 4, 2, 3, 1)
    gt = jnp.broadcast_to(gt.reshape(bsz, 3, G_B * rq, 1), (bsz, 3, G_B * rq, half))
    n_lane = -(-n_sb // 128) * 128
    mm = jnp.asarray(np.pad(_cmp_to_sel_map(n_ch, n_cmp, n_sb), ((0, n_lane - n_sb), (0, 0))).T)
    per_b = lambda shape: pl.BlockSpec((None,) + shape, lambda i: (i,) + (0,) * len(shape))
    const = lambda shape: pl.BlockSpec(shape, lambda i: (0,) * len(shape))
    o_c, o_w, idx = pl.pallas_call(
        functools.partial(_nsa_score_kernel, past, sd, n_cmp, n_sb),
        grid=(bsz,),
        in_specs=[per_b((n_ch, 2 * KV_B)), per_b((G_B * rq, half)), per_b((win, KV_B)),
                  pl.BlockSpec((None, PAD_ROWS, KV_B), lambda i: (i, 0, col_kv // KV_B + 2)),
                  const((1, KV_B)), const((KV_B, KV_B)), const((1, KV_B)), const((n_ch, n_lane))],
        out_specs=[per_b((G_B * rq, half)), per_b((G_B * rq, half)), per_b((G_B * sd, 128))],
        out_shape=[jax.ShapeDtypeStruct((bsz, G_B * rq, half), F32), jax.ShapeDtypeStruct((bsz, G_B * rq, half), F32),
                   jax.ShapeDtypeStruct((bsz, G_B * sd, 128), jnp.int32)],
        compiler_params=_cparams(("parallel",), 40),
        name="nsa_score_sample",
    )(pp, q, cache_win_l.reshape(bsz, win, KV_B), proj3, b1big, w2big, b2big, mm)
    blk_pg = page // L_SLC
    n_cached = past // L_SLC
    slc = cache_slc.reshape(n_phys, depth, blk_pg, L_SLC, KV_B)
    new_blk = jnp.pad(kvs, ((0, 0), (0, L_SLC - sd), (0, 0)))

    def blk_spec(j):
        def index(b, g, i, pt, ix, lyr):
            blk = jnp.minimum(ix[((b * G_B + g) * sd + i) * k_sel + j], n_cached - 1)
            return pt[b * n_pages + blk // blk_pg], lyr[0], blk % blk_pg, 0, 0
        return pl.BlockSpec((None, None, None, L_SLC, KV_B), index)

    grp = lambda b, g, i, pt, ix, lyr: (b, g, 0)
    grid_spec = pltpu.PrefetchScalarGridSpec(
        num_scalar_prefetch=3,
        grid=(bsz, G_B, sd),
        in_specs=[blk_spec(j) for j in range(k_sel)] + [
            pl.BlockSpec((None, L_SLC, KV_B), lambda b, g, i, pt, ix, lyr: (b, 0, 0)),
            pl.BlockSpec((None, rq, half), grp), pl.BlockSpec((None, rq, half), grp), pl.BlockSpec((None, rq, half), grp),
            pl.BlockSpec((None, 3, rq, half), lambda b, g, i, pt, ix, lyr: (b, 0, g, 0))],
        out_specs=pl.BlockSpec((None, rq, half), grp))
    y = pl.pallas_call(
        functools.partial(_nsa_select_kernel, past, sd, n_sb, n_pages, k_sel),
        grid_spec=grid_spec,
        out_shape=jax.ShapeDtypeStruct((bsz, G_B * rq, half), F32),
        compiler_params=_cparams(("parallel", "parallel", "arbitrary"), 40),
        name="nsa_select_sample",
    )(page_table.reshape(-1), idx[:, :, :k_sel].reshape(-1), jnp.full((1,), layer, jnp.int32),
      *([slc] * k_sel), new_blk, q, o_c, o_w, gt)
    y = y.reshape(bsz, G_B, HG_B, sd, G_B, HD_B)
    y = jnp.stack([y[:, g, :, :, g] for g in range(G_B)], axis=1)
    return y.transpose(0, 3, 1, 2, 4).reshape(bsz, sd, MIX_B)


def _prep_w_in(w, d):
    o = np.cumsum([0, MIX_A, MIX_A, MIX_A, MIX_A, MIX_B, KV_B, KV_B, KV_B, 3 * H_B, 2 * C_CONV, 3 * d])
    parts = [w[:, o[0]:o[4]], w[:, o[10]:o[11]], w[:, o[4]:o[5]], w[:, o[9]:o[10]], w[:, o[5]:o[8]], w[:, o[8]:o[9]]]
    n = sum(p.shape[1] for p in parts)
    n_pad = -(-n // 512) * 512
    parts.append(jnp.zeros((w.shape[0], n_pad - n), w.dtype))
    return jnp.concatenate(parts, axis=1).astype(BF16)


def _layer(x3, n_valid, p, s0, conv_prefix, mem_kv, nsa_fn, alpha):
    bsz, t, d = x3.shape
    m = bsz * t
    col_qb = COL_MG + 3 * d
    col_glu = col_qb + MIX_B
    col_kv = col_glu + 2 * C_CONV
    col_gb = col_kv + 3 * KV_B
    proj = _matmul(x3.reshape(m, d), p['w_in'], 1024, 512)
    proj3 = proj.reshape(bsz, t, -1)
    ya, s_new = _hgrn(proj3, p['lb'], p['hg_norm'], s0, n_valid)
    yc, conv_state = _conv(proj3, col_glu // C_CONV, conv_prefix, p['conv_w'], p['conv_b'],
                           p['conv_ln_g'], p['conv_ln_b'], n_valid)
    kv_shape = (bsz, n_valid, 2, G_B, HD_B)
    kvc = proj3[:, :n_valid, col_kv:col_kv + KV_B]
    kvs = proj3[:, :n_valid, col_kv + KV_B:col_kv + 2 * KV_B]
    kvw = proj3[:, :n_valid, col_kv + 2 * KV_B:col_kv + 3 * KV_B]
    yb, nsa_extra = nsa_fn(proj3, (col_qb, col_kv, col_gb), kvc, kvs, kvw)
    kvc, kvs = kvc.reshape(kv_shape), kvs.reshape(kv_shape)
    x1 = _merge(ya.reshape(m, MIX_A), yb.reshape(m, MIX_B), yc.reshape(m, C_CONV), proj, x3.reshape(m, d),
                p['w_pa'], p['w_pb'], p['w_pc'], p['w_out'], p['ln_g'][0:1], p['ln_b'][0:1], alpha)
    x2 = _xattn(x1.reshape(bsz, t, d), mem_kv, p['w_xq'], p['w_xo'], p['ln_g'][1:2], p['ln_b'][1:2], alpha)
    x3n = _mlp(x2.reshape(m, d), p['w_up'], p['w_down'], p['ln_g'][2:3], p['ln_b'][2:3], alpha)
    return x3n.reshape(bsz, t, d), kvc, kvs, nsa_extra, s_new, conv_state


def kernel(x_prompt, x_sample, cache_cmp, cache_slc, cache_win, state_hgrn, state_conv, cache_mem, page_table, mem_prompt, w_in, lb_raw, hg_norm, w_cmp1, b_cmp1, w_cmp2, b_cmp2, conv_w, conv_b, conv_ln_g, conv_ln_b, w_pa, w_pb, w_pc, w_out, ln_g, ln_b, w_xq, w_xkv, w_xo, w_up, w_down):
    bp, t, d = x_prompt.shape
    bd, sd = x_sample.shape[:2]
    depth = w_in.shape[0]
    n_mem = mem_prompt.shape[1]
    win_buf = cache_win.shape[2]
    alpha = (2 * depth) ** 0.25
    lb_cum = jnp.cumsum(jax.nn.softmax(lb_raw.astype(F32), axis=0), axis=0)
    lb_all = lb_cum - lb_cum[0]
    xp = x_prompt
    xs = jnp.pad(x_sample, ((0, 0), (0, PAD_ROWS - sd), (0, 0)))
    outs = {k: [] for k in ('cmp_p', 'cmp_s', 'slc_p', 'slc_s', 'win_p', 'win_s', 'hg_p', 'hg_s', 'cv_p', 'cv_s', 'mem_p')}
    for l in range(depth):
        p = {'w_in': _prep_w_in(w_in[l], d), 'lb': lb_all[l], 'hg_norm': hg_norm[l], 'conv_w': conv_w[l],
             'conv_b': conv_b[l].reshape(1, -1), 'conv_ln_g': conv_ln_g[l].reshape(1, -1),
             'conv_ln_b': conv_ln_b[l].reshape(1, -1),
             'w_pa': w_pa[l].astype(BF16), 'w_pb': w_pb[l].astype(BF16), 'w_pc': w_pc[l].astype(BF16),
             'w_out': w_out[l].astype(BF16), 'ln_g': ln_g[l], 'ln_b': ln_b[l],
             'w_xq': w_xq[l].astype(BF16), 'w_xo': w_xo[l].astype(BF16),
             'w_up': w_up[l].astype(BF16), 'w_down': w_down[l].astype(BF16)}
        cmp_w = (w_cmp1[l], b_cmp1[l], w_cmp2[l], b_cmp2[l])

        cmp_big = _cmp_weights(*cmp_w)

        def nsa_prompt(proj3, cols, kvc, kvs, kvw):
            col_qb, col_kv, col_gb = cols
            o = _nsa_prompt(proj3, col_qb, col_gb, _nsa_prep(kvc, proj3, col_kv, cmp_big))
            win = jnp.pad(kvw, ((0, 0), (max(win_buf - t, 0), 0), (0, 0)))[:, -win_buf:]
            return o, win.reshape(bp, win_buf, 2, G_B, HD_B)

        def nsa_sample(proj3, cols, kvc, kvs, kvw):
            o = _nsa_sample(proj3, cols, kvs, sd, cmp_big, cache_cmp, cache_slc, cache_win[l], l, page_table)
            o = jnp.pad(o, ((0, 0), (0, PAD_ROWS - sd), (0, 0)))
            win = jnp.concatenate([cache_win[l], kvw.reshape(bd, sd, 2, G_B, HD_B)], axis=1)[:, -win_buf:]
            return o, win

        mem_kv = _matmul(mem_prompt.reshape(bp * n_mem, d), w_xkv[l].astype(BF16), 1024, 512)
        mem_kv = mem_kv.reshape(bp, n_mem, 2 * d)
        xp, kc, ks_, wn, sh, cv = _layer(xp, t, p, jnp.zeros((bp, H_A, DK_A, DV_A), F32),
                                         jnp.zeros((bp, CONV_K - 1, C_CONV), F32), mem_kv, nsa_prompt, alpha)
        outs['cmp_p'].append(kc); outs['slc_p'].append(ks_); outs['win_p'].append(wn)
        outs['hg_p'].append(sh); outs['cv_p'].append(cv)
        outs['mem_p'].append(mem_kv.reshape(bp, n_mem, 2, NX_H, d // NX_H))
        xs, kc, ks_, wn, sh, cv = _layer(xs, sd, p, state_hgrn[l], state_conv[l],
                                         cache_mem[l].reshape(bd, n_mem, 2 * d), nsa_sample, alpha)
        outs['cmp_s'].append(kc); outs['slc_s'].append(ks_); outs['win_s'].append(wn)
        outs['hg_s'].append(sh); outs['cv_s'].append(cv)
    st = lambda k, ax: jnp.stack(outs[k], axis=ax)
    return (xp, xs[:, :sd],
            st('cmp_p', 1), st('cmp_s', 1), st('slc_p', 1), st('slc_s', 1),
            st('win_p', 0), st('win_s', 0), st('hg_p', 0), st('hg_s', 0),
            st('cv_p', 0), st('cv_s', 0), st('mem_p', 0))
```

```python
import functools

import numpy as np
import jax
import jax.numpy as jnp
from jax import lax
from jax.experimental import pallas as pl
from jax.experimental.pallas import tpu as pltpu

F32 = jnp.float32
BF16 = jnp.bfloat16
HIGHEST = lax.Precision.HIGHEST

H_A, DK_A, DV_A, CHUNK_A = 4, 128, 128, 16
H_B, G_B, HG_B, HD_B = 8, 2, 4, 64
L_CMP, D_CMP, L_SLC, N_SEL, WINDOW = 32, 16, 64, 16, 512
C_CONV, CONV_K = 512, 31
NX_H = 4
LN_EPS, RMS_EPS = 1e-5, 1e-6
PAD_ROWS = 16

MIX_A = H_A * DK_A
MIX_B = H_B * HD_B
KV_B = 2 * G_B * HD_B
COL_QA, COL_FA, COL_IA, COL_GA = 0, 512, 1024, 1536
COL_MG = 2048
V7X_VMEM_LIMIT = 56 * 2**20


def _cparams(sem, vmem_mb=None):
    return pltpu.CompilerParams(dimension_semantics=sem,
                                vmem_limit_bytes=None if vmem_mb is None else vmem_mb * 2**20)


def _ln(y, g, b):
    mu = jnp.mean(y, axis=-1, keepdims=True)
    d = y - mu
    var = jnp.mean(d * d, axis=-1, keepdims=True)
    return d * lax.rsqrt(var + LN_EPS) * g + b


def _sigmoid(x):
    return 1.0 / (1.0 + jnp.exp(-x))


def _mm_kernel(x_ref, w_ref, o_ref, xb_ref):
    @pl.when(pl.program_id(1) == 0)
    def _():
        xb_ref[...] = x_ref[...].astype(BF16)

    o_ref[...] = jnp.dot(xb_ref[...], w_ref[...], preferred_element_type=F32)


def _matmul(x, w, tm, tn):
    m, k = x.shape
    n = w.shape[1]
    tm, tn = min(tm, m), min(tn, n)
    return pl.pallas_call(
        _mm_kernel,
        grid=(m // tm, n // tn),
        in_specs=[pl.BlockSpec((tm, k), lambda i, j: (i, 0)),
                  pl.BlockSpec((k, tn), lambda i, j: (0, j))],
        out_specs=pl.BlockSpec((tm, tn), lambda i, j: (i, j)),
        out_shape=jax.ShapeDtypeStruct((m, n), F32),
        scratch_shapes=[pltpu.VMEM((tm, k), BF16)],
        compiler_params=_cparams(("parallel", "arbitrary"), 40),
        name="proj_matmul",
    )(x, w)


def _mlp_kernel(alpha, x_ref, wu_ref, wd_ref, g_ref, b_ref, o_ref, xb_ref, acc_ref):
    j = pl.program_id(1)

    @pl.when(j == 0)
    def _():
        xb_ref[...] = x_ref[...].astype(BF16)
        acc_ref[...] = jnp.zeros_like(acc_ref)

    h = jnp.dot(xb_ref[...], wu_ref[...], preferred_element_type=F32)
    h = jnp.square(jnp.maximum(h, 0.0)).astype(BF16)
    acc_ref[...] += jnp.dot(h, wd_ref[...], preferred_element_type=F32)

    @pl.when(j == pl.num_programs(1) - 1)
    def _():
        o_ref[...] = _ln(alpha * x_ref[...] + acc_ref[...], g_ref[...], b_ref[...])


def _mlp(x, w_up, w_down, g, b, alpha, tm=512, tf=1024):
    m, d = x.shape
    ff = w_up.shape[1]
    tm = min(tm, m)
    return pl.pallas_call(
        functools.partial(_mlp_kernel, alpha),
        grid=(m // tm, ff // tf),
        in_specs=[pl.BlockSpec((tm, d), lambda i, j: (i, 0)),
                  pl.BlockSpec((d, tf), lambda i, j: (0, j)),
                  pl.BlockSpec((tf, d), lambda i, j: (j, 0)),
                  pl.BlockSpec((1, d), lambda i, j: (0, 0)),
                  pl.BlockSpec((1, d), lambda i, j: (0, 0))],
        out_specs=pl.BlockSpec((tm, d), lambda i, j: (i, 0)),
        out_shape=jax.ShapeDtypeStruct((m, d), F32),
        scratch_shapes=[pltpu.VMEM((tm, d), BF16), pltpu.VMEM((tm, d), F32)],
        compiler_params=_cparams(("parallel", "arbitrary"), 48),
        name="mlp",
    )(x, w_up, w_down, g, b)


def _merge_kernel(alpha, ya_ref, yb_ref, yc_ref, ma_ref, mb_ref, mc_ref, x_ref,
                  wpa_ref, wpb_ref, wpc_ref, wout_ref, g_ref, b_ref, o_ref):
    def branch(y_ref, m_ref, w_ref):
        return _sigmoid(m_ref[...]) * jnp.dot(y_ref[...].astype(BF16), w_ref[...], preferred_element_type=F32)

    merged = branch(ya_ref, ma_ref, wpa_ref) + branch(yb_ref, mb_ref, wpb_ref) + branch(yc_ref, mc_ref, wpc_ref)
    y = jnp.dot(merged.astype(BF16), wout_ref[...], preferred_element_type=F32)
    o_ref[...] = _ln(alpha * x_ref[...] + y, g_ref[...], b_ref[...])


def _merge(ya, yb, yc, proj, x, wpa, wpb, wpc, wout, g, b, alpha, tm=256):
    m, d = x.shape
    tm = min(tm, m)
    mg0 = COL_MG // d
    row = lambda i: (i, 0)
    const = lambda i: (0, 0)
    return pl.pallas_call(
        functools.partial(_merge_kernel, alpha),
        grid=(m // tm,),
        in_specs=[pl.BlockSpec((tm, MIX_A), row), pl.BlockSpec((tm, MIX_B), row), pl.BlockSpec((tm, C_CONV), row),
                  pl.BlockSpec((tm, d), lambda i: (i, mg0)), pl.BlockSpec((tm, d), lambda i: (i, mg0 + 1)),
                  pl.BlockSpec((tm, d), lambda i: (i, mg0 + 2)),
                  pl.BlockSpec((tm, d), row),
                  pl.BlockSpec((MIX_A, d), const), pl.BlockSpec((MIX_B, d), const), pl.BlockSpec((C_CONV, d), const),
                  pl.BlockSpec((d, d), const), pl.BlockSpec((1, d), const), pl.BlockSpec((1, d), const)],
        out_specs=pl.BlockSpec((tm, d), row),
        out_shape=jax.ShapeDtypeStruct((m, d), F32),
        compiler_params=_cparams(("parallel",), 48),
        name="merge_out",
    )(ya, yb, yc, proj, proj, proj, x, wpa, wpb, wpc, wout, g, b)


def _xattn_kernel(alpha, x_ref, kv_ref, wq_ref, wo_ref, g_ref, b_ref, o_ref):
    x = x_ref[...]
    d = x.shape[-1]
    hd = d // NX_H
    q = jnp.dot(x.astype(BF16), wq_ref[...], preferred_element_type=F32)
    outs = []
    for h in range(NX_H):
        qh = q[:, h * hd:(h + 1) * hd].astype(BF16)
        kh = kv_ref[:, h * hd:(h + 1) * hd].astype(BF16)
        vh = kv_ref[:, d + h * hd:d + (h + 1) * hd].astype(BF16)
        s = lax.dot_general(qh, kh, (((1,), (1,)), ((), ())), preferred_element_type=F32) * (hd ** -0.5)
        e = jnp.exp(s - jnp.max(s, axis=-1, keepdims=True))
        p = e / jnp.sum(e, axis=-1, keepdims=True)
        outs.append(jnp.dot(p.astype(BF16), vh, preferred_element_type=F32))
    o = jnp.concatenate(outs, axis=-1)
    y = jnp.dot(o.astype(BF16), wo_ref[...], preferred_element_type=F32)
    o_ref[...] = _ln(alpha * x + y, g_ref[...], b_ref[...])


def _xattn(x, kv, wq, wo, g, b, alpha, tm=256):
    bsz, t, d = x.shape
    n_mem = kv.shape[1]
    tm = min(tm, t)
    const = lambda i, j: (0, 0)
    return pl.pallas_call(
        functools.partial(_xattn_kernel, alpha),
        grid=(bsz, t // tm),
        in_specs=[pl.BlockSpec((None, tm, d), lambda i, j: (i, j, 0)),
                  pl.BlockSpec((None, n_mem, 2 * d), lambda i, j: (i, 0, 0)),
                  pl.BlockSpec((d, d), const), pl.BlockSpec((d, d), const),
                  pl.BlockSpec((1, d), const), pl.BlockSpec((1, d), const)],
        out_specs=pl.BlockSpec((None, tm, d), lambda i, j: (i, j, 0)),
        out_shape=jax.ShapeDtypeStruct((bsz, t, d), F32),
        compiler_params=_cparams(("parallel", "parallel"), 48),
        name="xattn",
    )(x, kv, wq, wo, g, b)


CONV_HALO = 32
CONV_SUB = 32


def _conv_kernel(rt, n_valid_last, a_ref, gt_ref, pre_ref, w_ref, cb_ref, g_ref, b_ref, y_ref, st_ref, ue_ref, sh_ref):
    t = pl.program_id(1)
    off = CONV_HALO - (CONV_K - 1)
    sl = 8

    @pl.when(t == 0)
    def _():
        ue_ref[0:off, :] = jnp.zeros((off, C_CONV), F32)
        ue_ref[off:CONV_HALO, :] = pre_ref[...]

    ue_ref[CONV_HALO:CONV_HALO + rt, :] = a_ref[...] * _sigmoid(gt_ref[...])
    n_sh = rt + CONV_HALO - sl
    for s in range(1, sl):
        sh_ref[s - 1, 0:n_sh, :] = ue_ref[s:s + n_sh, :]
    sub = min(CONV_SUB, rt)
    for r0 in range(0, rt, sub):
        acc = cb_ref[...]
        for j in range(CONV_K):
            a, s = divmod(off + j, sl)
            lo = r0 + a * sl
            win = ue_ref[lo:lo + sub, :] if s == 0 else sh_ref[s - 1, lo:lo + sub, :]
            acc = acc + w_ref[j:j + 1, :] * win
        y = _ln(acc, g_ref[...], b_ref[...])
        y_ref[r0:r0 + sub, :] = y * _sigmoid(y)

    @pl.when(t == pl.num_programs(1) - 1)
    def _():
        st_ref[...] = ue_ref[off + n_valid_last:off + n_valid_last + CONV_K - 1, :]

    ue_ref[0:CONV_HALO, :] = ue_ref[rt:rt + CONV_HALO, :]


def _conv(proj3, col_a, prefix, w, cb, g, b, n_valid, rt=256):
    bsz, t, _ = proj3.shape
    rt = min(rt, t)
    n_valid_last = n_valid - (t - rt)
    vec = lambda i, j: (0, 0)
    return pl.pallas_call(
        functools.partial(_conv_kernel, rt, n_valid_last),
        grid=(bsz, t // rt),
        in_specs=[pl.BlockSpec((None, rt, C_CONV), lambda i, j: (i, j, col_a)),
                  pl.BlockSpec((None, rt, C_CONV), lambda i, j: (i, j, col_a + 1)),
                  pl.BlockSpec((None, CONV_K - 1, C_CONV), lambda i, j: (i, 0, 0)),
                  pl.BlockSpec((CONV_K, C_CONV), vec), pl.BlockSpec((1, C_CONV), vec),
                  pl.BlockSpec((1, C_CONV), vec), pl.BlockSpec((1, C_CONV), vec)],
        out_specs=[pl.BlockSpec((None, rt, C_CONV), lambda i, j: (i, j, 0)),
                   pl.BlockSpec((None, CONV_K - 1, C_CONV), lambda i, j: (i, 0, 0))],
        out_shape=[jax.ShapeDtypeStruct((bsz, t, C_CONV), F32),
                   jax.ShapeDtypeStruct((bsz, CONV_K - 1, C_CONV), F32)],
        scratch_shapes=[pltpu.VMEM((rt + CONV_HALO, C_CONV), F32), pltpu.VMEM((7, rt + CONV_HALO - 8, C_CONV), F32)],
        compiler_params=_cparams(("parallel", "arbitrary"), 32),
        name="conformer_conv",
    )(proj3, proj3, prefix, w, cb, g, b)


def _hgrn_kernel(tt, n_valid, q_ref, f_ref, i_ref, g_ref, la_ref, l1_ref, oml_ref, gn_ref, s0_ref,
                 y_ref, s_ref, st_ref, qs_ref, kk_ref, bb_ref):
    c = CHUNK_A
    t = pl.program_id(1)

    @pl.when(t == 0)
    def _():
        for h in range(H_A):
            st_ref[h] = s0_ref[h].T

    z = f_ref[...]
    cc = l1_ref[...] + (jnp.minimum(z, 0.0) - jnp.log1p(jnp.exp(-jnp.abs(z))))
    a = la_ref[...]
    logf = jnp.maximum(a, cc) + jnp.log1p(jnp.exp(-jnp.abs(a - cc)))
    k = oml_ref[...] * _sigmoid(-z)
    row = lax.broadcasted_iota(jnp.int32, (tt, 1), 0)
    if n_valid < tt:
        logf = jnp.where(row < n_valid, logf, 0.0)
        k = jnp.where(row < n_valid, k, 0.0)
    b = logf
    rc = row & (c - 1)
    sh = 1
    while sh < c:
        b = b + jnp.where(rc >= sh, pltpu.roll(b, sh, 0), 0.0)
        sh *= 2
    q = q_ref[...]
    qs_ref[...] = q * _sigmoid(q)
    kk_ref[...] = k
    bb_ref[...] = b

    rowc = lax.broadcasted_iota(jnp.int32, (c, 1), 0)

    def chunk(ci, carry):
        r0 = pl.multiple_of(ci * c, c)
        for h in range(H_A):
            hs = slice(h * DK_A, (h + 1) * DK_A)
            qc = qs_ref[pl.ds(r0, c), hs]
            kc = kk_ref[pl.ds(r0, c), hs]
            bc = bb_ref[pl.ds(r0, c), hs]
            vc = i_ref[pl.ds(r0, c), hs]
            bl = bc[c - 1:c, :]
            st = st_ref[h]
            qe = (qc * jnp.exp(bc)).astype(BF16)
            o = lax.dot_general(qe, st.astype(BF16), (((1,), (1,)), ((), ())), preferred_element_type=F32)
            for s in range(c):
                e = jnp.exp(jnp.where(rowc >= s, bc - bc[s:s + 1, :], -jnp.inf))
                att = jnp.sum(qc * e * kc[s:s + 1, :], axis=-1, keepdims=True)
                o = o + att * vc[s:s + 1, :]
            y_ref[pl.ds(r0, c), hs] = o
            kd = (kc * jnp.exp(bl - bc)).astype(BF16)
            u = lax.dot_general(vc.astype(BF16), kd, (((0,), (0,)), ((), ())), preferred_element_type=F32)
            st_ref[h] = st * jnp.exp(bl) + u
        return carry

    lax.fori_loop(0, tt // c, chunk, 0)

    g = g_ref[...]
    gate = g * _sigmoid(g)
    for h in range(H_A):
        hs = slice(h * DV_A, (h + 1) * DV_A)
        o = y_ref[:, hs]
        o = o * lax.rsqrt(jnp.mean(o * o, axis=-1, keepdims=True) + RMS_EPS) * gn_ref[...]
        y_ref[:, hs] = o * gate[:, hs]

    @pl.when(t == pl.num_programs(1) - 1)
    def _():
        for h in range(H_A):
            s_ref[h] = st_ref[h].T


def _hgrn(proj3, lb, gnorm, s0, n_valid, tt=256):
    bsz, t, _ = proj3.shape
    tt = min(tt, t)
    lb = lb.reshape(1, MIX_A).astype(F32)
    la, l1, oml = jnp.log(lb), jnp.log1p(-lb), 1.0 - lb
    vec = lambda i, j: (0, 0)
    col = lambda cb: pl.BlockSpec((None, tt, MIX_A), lambda i, j: (i, j, cb))
    st = pl.BlockSpec((None, H_A, DK_A, DV_A), lambda i, j: (i, 0, 0, 0))
    return pl.pallas_call(
        functools.partial(_hgrn_kernel, tt, n_valid if t == tt else tt),
        grid=(bsz, t // tt),
        in_specs=[col(0), col(1), col(2), col(3),
                  pl.BlockSpec((1, MIX_A), vec), pl.BlockSpec((1, MIX_A), vec), pl.BlockSpec((1, MIX_A), vec),
                  pl.BlockSpec((1, DV_A), vec), st],
        out_specs=[pl.BlockSpec((None, tt, MIX_A), lambda i, j: (i, j, 0)), st],
        out_shape=[jax.ShapeDtypeStruct((bsz, t, MIX_A), F32),
                   jax.ShapeDtypeStruct((bsz, H_A, DK_A, DV_A), F32)],
        scratch_shapes=[pltpu.VMEM((H_A, DV_A, DK_A), F32), pltpu.VMEM((tt, MIX_A), F32),
                        pltpu.VMEM((tt, MIX_A), F32), pltpu.VMEM((tt, MIX_A), F32)],
        compiler_params=_cparams(("parallel", "arbitrary"), 32),
        name="hgrn2",
    )(proj3, proj3, proj3, proj3, la, l1, oml, gnorm.reshape(1, DV_A), s0)


NEG_BIG = -1e30
SEL_PER_CMP = L_SLC // D_CMP


def _cmp_weights(w1, b1, w2, b2):
    m = L_CMP // D_CMP
    eye_e, eye_g = jnp.eye(2, dtype=F32), jnp.eye(G_B, dtype=F32)
    w1r = w1.reshape(2, m, D_CMP, HD_B, HD_B)
    w1big = jnp.einsum('ehjdf,ea,gb->jegdhabf', w1r, eye_e, eye_g).reshape(D_CMP * KV_B, m * KV_B)
    w2big = jnp.einsum('efo,ea,gb->egfabo', w2, eye_e, eye_g).reshape(KV_B, KV_B)
    b1big = jnp.broadcast_to(b1[:, None, :], (2, G_B, HD_B)).reshape(1, KV_B)
    b2big = jnp.broadcast_to(b2[:, None, :], (2, G_B, HD_B)).reshape(1, KV_B)
    w1tok = jnp.einsum('ehjdf,gb->ejgdhbf', w1r, eye_g).reshape(2, D_CMP, KV_B // 2, m * KV_B // 2)
    return w1big.astype(BF16), b1big, w2big.astype(BF16), b2big, w1tok.astype(BF16)


def _cmp_to_sel_map(n_cmp_pad, n_cmp, n_sb):
    mm = np.zeros((n_sb, n_cmp_pad), np.float32)
    for n in range(n_cmp):
        for i in (n, n + 1):
            if i // SEL_PER_CMP < n_sb:
                mm[i // SEL_PER_CMP, n] += 1.0
    return mm


def _compress_tail(pp, n_cmp, b1_ref, w2_ref, b2_ref):
    n_ch = pp.shape[0]
    hid = pp[:, :KV_B] + pltpu.roll(pp[:, KV_B:], n_ch - 1, 0) + b1_ref[...]
    hid = hid * _sigmoid(hid)
    kc = jnp.dot(hid.astype(BF16), w2_ref[...], preferred_element_type=F32) + b2_ref[...]
    row = lax.broadcasted_iota(jnp.int32, (n_ch, 1), 0)
    return jnp.where(row < n_cmp, kc, 0.0)


def _nsa_prep_kernel(n_cmp, xc_ref, ks_ref, kw_ref, w1_ref, b1_ref, w2_ref, b2_ref,
                     kck_ref, kcvt_ref, ksk_ref, ksvt_ref, kwk_ref, kwvt_ref):
    half = KV_B // 2
    pp = jnp.dot(xc_ref[...].astype(BF16), w1_ref[...], preferred_element_type=F32)
    kc = _compress_tail(pp, n_cmp, b1_ref, w2_ref, b2_ref)
    kck_ref[...] = kc[:, :half].astype(BF16)
    kcvt_ref[...] = kc[:, half:].T.astype(BF16)
    ksk_ref[...] = ks_ref[:, :half].astype(BF16)
    ksvt_ref[...] = ks_ref[:, half:].T.astype(BF16)
    kwk_ref[...] = kw_ref[:, :half].astype(BF16)
    kwvt_ref[...] = kw_ref[:, half:].T.astype(BF16)


def _nsa_prep(kvc, proj3, col_kv, cmp_big):
    bsz, t, _ = proj3.shape
    n_ch = t // D_CMP
    n_cmp = n_ch - L_CMP // D_CMP + 1
    half = KV_B // 2
    w1big, b1big, w2big, b2big = cmp_big[:4]
    xc = kvc.reshape(bsz, n_ch, D_CMP * KV_B)
    cb = col_kv // KV_B
    const = lambda i: (0, 0)
    kspec = lambda n: pl.BlockSpec((None, n, half), lambda i: (i, 0, 0))
    vspec = lambda n: pl.BlockSpec((None, half, n), lambda i: (i, 0, 0))
    return pl.pallas_call(
        functools.partial(_nsa_prep_kernel, n_cmp),
        grid=(bsz,),
        in_specs=[pl.BlockSpec((None, n_ch, D_CMP * KV_B), lambda i: (i, 0, 0)),
                  pl.BlockSpec((None, t, KV_B), lambda i: (i, 0, cb + 1)),
                  pl.BlockSpec((None, t, KV_B), lambda i: (i, 0, cb + 2)),
                  pl.BlockSpec(w1big.shape, const), pl.BlockSpec((1, KV_B), const),
                  pl.BlockSpec((KV_B, KV_B), const), pl.BlockSpec((1, KV_B), const)],
        out_specs=[kspec(n_ch), vspec(n_ch), kspec(t), vspec(t), kspec(t), vspec(t)],
        out_shape=[jax.ShapeDtypeStruct((bsz, n_ch, half), BF16), jax.ShapeDtypeStruct((bsz, half, n_ch), BF16),
                   jax.ShapeDtypeStruct((bsz, t, half), BF16), jax.ShapeDtypeStruct((bsz, half, t), BF16),
                   jax.ShapeDtypeStruct((bsz, t, half), BF16), jax.ShapeDtypeStruct((bsz, half, t), BF16)],
        compiler_params=_cparams(("parallel",), 40),
        name="nsa_prep",
    )(xc, proj3, proj3, w1big, b1big, w2big, b2big)


def _nsa_prompt_kernel(tq, n_sb, q_ref, gb_ref, kck_ref, kcvt_ref, ksk_ref, ksvt_ref, kwk_ref, kwvt_ref, mm_ref, o_ref):
    qi = pl.program_id(1)
    q0 = qi * tq
    n_cp = kck_ref.shape[0]
    w4 = HG_B * tq
    q_t = (q_ref[...] * HD_B ** -0.5).T
    g_t = _sigmoid(gb_ref[...]).T
    qpos = q0 + lax.broadcasted_iota(jnp.int32, (1, tq), 1)
    qpos4 = jnp.concatenate([qpos] * HG_B, axis=1)
    zpad = jnp.zeros((HD_B, tq), BF16)

    def update(state, s, pen, vt, g):
        m, l, acc = state
        s = s + jnp.concatenate([pen] * HG_B, axis=1)
        m_new = jnp.maximum(m, jnp.max(s, axis=0, keepdims=True))
        e = jnp.exp(s - m_new)
        a = jnp.exp(m - m_new)
        l = a * l + jnp.sum(e, axis=0, keepdims=True)
        pv = jnp.dot(vt, e.astype(BF16), preferred_element_type=F32)
        return m_new, l, a * acc + pv[g * HD_B:(g + 1) * HD_B, :]

    qts, o_cs, sel_ts = [], [], []
    for g in range(G_B):
        cols = []
        for hg in range(HG_B):
            h = g * HG_B + hg
            qh = q_t[h * HD_B:(h + 1) * HD_B, :].astype(BF16)
            cols.append(jnp.concatenate([qh, zpad] if g == 0 else [zpad, qh], axis=0))
        qt = jnp.concatenate(cols, axis=1)
        qts.append(qt)
        s = jnp.dot(kck_ref[...], qt, preferred_element_type=F32)
        c_end = lax.broadcasted_iota(jnp.int32, (n_cp, w4), 0) * D_CMP + (L_CMP - 1)
        mask = c_end <= qpos4
        m = jnp.max(jnp.where(mask, s, NEG_BIG), axis=0, keepdims=True)
        e = jnp.where(mask, jnp.exp(s - m), 0.0)
        d = jnp.sum(e, axis=0, keepdims=True)
        p = e / jnp.where(d > 0, d, 1.0)
        o_cs.append(jnp.dot(kcvt_ref[...], p.astype(BF16), preferred_element_type=F32)[g * HD_B:(g + 1) * HD_B, :])
        imp = p[:, 0:tq]
        for hg in range(1, HG_B):
            imp = imp + p[:, hg * tq:(hg + 1) * tq]
        bs = jnp.dot(mm_ref[...], imp, precision=HIGHEST, preferred_element_type=F32)
        blk = lax.broadcasted_iota(jnp.int32, (n_sb, tq), 0)
        cur = qpos >> 6
        forced = (blk == 0) | (blk == cur) | (blk == cur - 1)
        score = jnp.where(blk <= cur, jnp.where(forced, jnp.inf, bs), -jnp.inf)
        rank = jnp.zeros((n_sb, tq), jnp.int32)
        for i in range(n_sb):
            si = score[i:i + 1, :]
            rank = rank + jnp.where((si > score) | ((si == score) & (blk > i)), 1, 0)
        sel_ts.append(jnp.where((rank < N_SEL) & (blk <= cur), 1.0, 0.0).astype(BF16))

    sel_both = jnp.concatenate(sel_ts, axis=1)
    ej = lax.broadcasted_iota(jnp.int32, (tq, n_sb), 1)
    ek = lax.broadcasted_iota(jnp.int32, (tq, n_sb), 0)
    krow = lax.broadcasted_iota(jnp.int32, (tq, tq), 0)

    def sel_penalties(k0):
        expand = jnp.where(ej == ((k0 + ek) >> 6), 1.0, 0.0).astype(BF16)
        pen = (jnp.dot(expand, sel_both, preferred_element_type=F32) - 1.0) * (-NEG_BIG)
        return [pen[:, g * tq:(g + 1) * tq] for g in range(G_B)]

    def far_tiles(kt, states):
        k0 = pl.multiple_of(kt * tq, tq)
        k_tile, vt_tile = ksk_ref[pl.ds(k0, tq), :], ksvt_ref[:, pl.ds(k0, tq)]
        pens = sel_penalties(k0)
        return tuple(update(states[g], jnp.dot(k_tile, qts[g], preferred_element_type=F32), pens[g], vt_tile, g)
                     for g in range(G_B))

    def near_tiles(kt, states):
        k0 = pl.multiple_of(kt * tq, tq)
        rel = qpos - k0
        causal = krow <= rel
        pen_causal = jnp.where(causal, 0.0, NEG_BIG)
        pen_win = jnp.where(causal & (krow > rel - WINDOW), 0.0, NEG_BIG)
        k_tile, vt_tile = ksk_ref[pl.ds(k0, tq), :], ksvt_ref[:, pl.ds(k0, tq)]
        kw_tile, vwt_tile = kwk_ref[pl.ds(k0, tq), :], kwvt_ref[:, pl.ds(k0, tq)]
        pens = sel_penalties(k0)
        new_sel = tuple(update(states[g], jnp.dot(k_tile, qts[g], preferred_element_type=F32),
                               pens[g] + pen_causal, vt_tile, g) for g in range(G_B))
        new_win = tuple(update(states[G_B + g], jnp.dot(kw_tile, qts[g], preferred_element_type=F32),
                               pen_win, vwt_tile, g) for g in range(G_B))
        return new_sel + new_win

    init = (jnp.full((1, w4), NEG_BIG, F32), jnp.zeros((1, w4), F32), jnp.zeros((HD_B, w4), F32))
    first_near = jnp.maximum(qi - WINDOW // tq, 0)
    states = lax.fori_loop(0, first_near, far_tiles, (init,) * G_B)
    states = lax.fori_loop(first_near, qi + 1, near_tiles, states + (init,) * G_B)
    finish = lambda st: st[2] / jnp.where(st[1] > 0, st[1], 1.0)
    outs = []
    for g in range(G_B):
        o_c, o_s, o_w = o_cs[g], finish(states[g]), finish(states[G_B + g])
        for hg in range(HG_B):
            c = (g * HG_B + hg) * 3
            sl = slice(hg * tq, (hg + 1) * tq)
            outs.append(g_t[c:c + 1, :] * o_c[:, sl] + g_t[c + 1:c + 2, :] * o_s[:, sl]
                        + g_t[c + 2:c + 3, :] * o_w[:, sl])
    o_ref[...] = jnp.concatenate(outs, axis=0).T


def _nsa_prompt(proj3, col_qb, col_gb, prep, tq=256):
    bsz, t, _ = proj3.shape
    tq = min(tq, t)
    kck, kcvt, ksk, ksvt, kwk, kwvt = prep
    n_ch = kck.shape[1]
    n_cmp = n_ch - L_CMP // D_CMP + 1
    n_sb = -(-t // L_SLC)
    half = KV_B // 2
    mm = jnp.asarray(_cmp_to_sel_map(n_ch, n_cmp, n_sb))
    per_b = lambda shape: pl.BlockSpec((None,) + shape, lambda i, j: (i, 0, 0))
    return pl.pallas_call(
        functools.partial(_nsa_prompt_kernel, tq, n_sb),
        grid=(bsz, t // tq),
        in_specs=[pl.BlockSpec((None, tq, MIX_B), lambda i, j: (i, j, col_qb // MIX_B)),
                  pl.BlockSpec((None, tq, 128), lambda i, j: (i, j, col_gb // 128)),
                  per_b((n_ch, half)), per_b((half, n_ch)), per_b((t, half)), per_b((half, t)),
                  per_b((t, half)), per_b((half, t)),
                  pl.BlockSpec((n_sb, n_ch), lambda i, j: (0, 0))],
        out_specs=pl.BlockSpec((None, tq, MIX_B), lambda i, j: (i, j, 0)),
        out_shape=jax.ShapeDtypeStruct((bsz, t, MIX_B), F32),
        compiler_params=_cparams(("parallel", "parallel"), 40),
        name="nsa_prompt",
    )(proj3, proj3, kck, kcvt, ksk, ksvt, kwk, kwvt, mm)


def _pages_view(cache):
    n_phys, depth, page = cache.shape[:3]
    return jnp.transpose(cache, (0, 1, 3, 4, 5, 2)).reshape(n_phys, depth, 2, KV_B // 2, page)


def _cmp_pages_kernel(n_pg, page, *refs):
    pages, w1_ref, o_ref, xs_ref = refs[2:2 + n_pg], refs[2 + n_pg], refs[3 + n_pg], refs[4 + n_pg]
    half = KV_B // 2
    n_ch = n_pg * page // D_CMP
    for i, pg in enumerate(pages):
        for e in range(2):
            xs_ref[e, i * page:(i + 1) * page, :] = pg[e].T
    for e in range(2):
        acc = jnp.zeros((n_ch, 2 * half), F32)
        for j in range(D_CMP):
            x = xs_ref[e, pl.ds(j, n_ch, stride=D_CMP), :].astype(BF16)
            acc = acc + jnp.dot(x, w1_ref[e, j], preferred_element_type=F32)
        o_ref[:, e * half:(e + 1) * half] = acc[:, :half]
        o_ref[:, KV_B + e * half:KV_B + (e + 1) * half] = acc[:, half:]


def _cmp_pages(pages, layer, page_table, w1e):
    n_phys, depth, _, _, page = pages.shape
    bsz, n_pages = page_table.shape
    ch_pg = page // D_CMP
    n_pg = next(c for c in (16, 8, 4, 2, 1) if n_pages % c == 0)

    def page_spec(i):
        return pl.BlockSpec((None, None, 2, KV_B // 2, page),
                            lambda b, s, pt, lyr: (pt[b * n_pages + s * n_pg + i], lyr[0], 0, 0, 0))

    grid_spec = pltpu.PrefetchScalarGridSpec(
        num_scalar_prefetch=2,
        grid=(bsz, n_pages // n_pg),
        in_specs=[page_spec(i) for i in range(n_pg)] + [pl.BlockSpec(w1e.shape, lambda b, s, pt, lyr: (0, 0, 0, 0))],
        out_specs=pl.BlockSpec((None, n_pg * ch_pg, 2 * KV_B), lambda b, s, pt, lyr: (b, s, 0)),
        scratch_shapes=[pltpu.VMEM((2, n_pg * page, KV_B // 2), F32)])
    return pl.pallas_call(
        functools.partial(_cmp_pages_kernel, n_pg, page),
        grid_spec=grid_spec,
        out_shape=jax.ShapeDtypeStruct((bsz, n_pages * ch_pg, 2 * KV_B), F32),
        compiler_params=_cparams(("parallel", "parallel"), 40),
        name="nsa_cmp_pages",
    )(page_table.reshape(-1), jnp.full((1,), layer, jnp.int32), *([pages] * n_pg), w1e)


def _nsa_score_kernel(past, sd, n_cmp, n_sb, pp_ref, q_ref, cw_ref, nw_ref, b1_ref, w2_ref, b2_ref, mm_ref,
                      oc_ref, ow_ref, idx_ref):
    half = KV_B // 2
    rq = HG_B * sd
    scale = HD_B ** -0.5
    n_ch = pp_ref.shape[0]
    win = cw_ref.shape[2]
    kc = _compress_tail(pp_ref[...], n_cmp, b1_ref, w2_ref, b2_ref)
    kck, kcv = kc[:, :half].astype(BF16), kc[:, half:].astype(BF16)
    qpos = past + lax.broadcasted_iota(jnp.int32, (rq, 1), 0) % sd
    nt = (((1,), (1,)), ((), ()))

    def softmax(parts):
        m = functools.reduce(jnp.maximum, [jnp.max(jnp.where(mk, s, NEG_BIG), axis=-1, keepdims=True) for s, mk in parts])
        es = [jnp.where(mk, jnp.exp(s - m), 0.0) for s, mk in parts]
        d = functools.reduce(jnp.add, [jnp.sum(e, axis=-1, keepdims=True) for e in es])
        return [e / jnp.where(d > 0, d, 1.0) for e in es]

    imps = []
    for g in range(G_B):
        qg = q_ref[g * rq:(g + 1) * rq, :]
        s = lax.dot_general(qg, kck, nt, preferred_element_type=F32) * scale
        c_end = lax.broadcasted_iota(jnp.int32, (rq, n_ch), 1) * D_CMP + (L_CMP - 1)
        p, = softmax([(s, c_end <= qpos)])
        oc_ref[g * rq:(g + 1) * rq, :] = jnp.dot(p.astype(BF16), kcv, preferred_element_type=F32)
        imps.append(functools.reduce(jnp.add, [p[hg * sd:(hg + 1) * sd, :] for hg in range(HG_B)]))
        s1 = jnp.dot(qg, cw_ref[0].astype(BF16), preferred_element_type=F32) * scale
        s2 = lax.dot_general(qg, nw_ref[:, :half].astype(BF16), nt, preferred_element_type=F32) * scale
        wp1 = past - win + lax.broadcasted_iota(jnp.int32, s1.shape, 1)
        j2 = lax.broadcasted_iota(jnp.int32, s2.shape, 1)
        wp2 = past + j2
        p1, p2 = softmax([(s1, (wp1 <= qpos) & (wp1 > qpos - WINDOW) & (wp1 >= 0)),
                          (s2, (wp2 <= qpos) & (wp2 > qpos - WINDOW) & (j2 < sd))])
        ow_ref[g * rq:(g + 1) * rq, :] = (
            lax.dot_general(p1.astype(BF16), cw_ref[1].astype(BF16), nt, preferred_element_type=F32)
            + jnp.dot(p2.astype(BF16), nw_ref[:, half:].astype(BF16), preferred_element_type=F32))
    imp = jnp.concatenate(imps, axis=0)
    rows = G_B * sd
    bs = jnp.dot(imp, mm_ref[...], precision=HIGHEST, preferred_element_type=F32)
    n_lane = bs.shape[1]
    blk = lax.broadcasted_iota(jnp.int32, (rows, n_lane), 1)
    cur = (past + lax.broadcasted_iota(jnp.int32, (rows, 1), 0) % sd) >> 6
    forced = (blk == 0) | (blk == cur) | (blk == cur - 1)
    score = jnp.where(blk <= cur, jnp.where(forced, jnp.inf, bs), -jnp.inf)
    blk_f = blk.astype(F32)
    taken = blk >= n_sb
    lane = lax.broadcasted_iota(jnp.int32, (rows, 128), 1)
    picked = jnp.zeros((rows, 128), F32)
    for it in range(min(N_SEL, n_sb)):
        live = jnp.where(taken, -jnp.inf, score)
        mx = jnp.max(live, axis=-1, keepdims=True)
        ix = jnp.min(jnp.where(~taken & (score == mx), blk_f, float(n_lane)), axis=-1, keepdims=True)
        picked = jnp.where(lane == it, ix, picked)
        taken = taken | (blk_f == ix)
    idx_ref[...] = picked.astype(jnp.int32)


def _nsa_select_kernel(past, sd, n_sb, n_pages, k_sel, *refs):
    pt_ref, ix_ref, lyr_ref = refs[:3]
    pages = refs[3:3 + k_sel]
    new_ref, q_ref, oc_ref, ow_ref, gt_ref, o_ref = refs[3 + k_sel:]
    rq = HG_B * sd
    page = new_ref.shape[-1]
    blk_pg = page // L_SLC
    b, g, i = pl.program_id(0), pl.program_id(1), pl.program_id(2)
    base = ((b * G_B + g) * sd + i) * k_sel
    lane = lax.broadcasted_iota(jnp.int32, (1, k_sel * page), 1)
    in_page = lane % page
    kpos = in_page
    picked = lane < 0
    kts, vts = [], []
    for j in range(k_sel):
        blk = ix_ref[base + j]
        is_new = blk == n_sb - 1
        kts.append(jnp.where(is_new, new_ref[0], pages[j][0]))
        vts.append(jnp.where(is_new, new_ref[1], pages[j][1]))
        mine = lane // page == j
        kpos = kpos + jnp.where(mine, (blk // blk_pg) * page, 0)
        picked = picked | (mine & (in_page // L_SLC == blk % blk_pg))
    kt_all = jnp.concatenate(kts, axis=1).astype(BF16)
    vt_all = jnp.concatenate(vts, axis=1).astype(BF16)
    s = jnp.dot(q_ref[...], kt_all, preferred_element_type=F32) * (HD_B ** -0.5)
    row_q = lax.broadcasted_iota(jnp.int32, (rq, 1), 0) % sd
    mask = picked & (kpos <= past + row_q)
    m = jnp.max(jnp.where(mask, s, NEG_BIG), axis=-1, keepdims=True)
    e = jnp.where(mask, jnp.exp(s - m), 0.0)
    d = jnp.sum(e, axis=-1, keepdims=True)
    p = e / jnp.where(d > 0, d, 1.0)
    o_s = lax.dot_general(p.astype(BF16), vt_all, (((1,), (1,)), ((), ())), preferred_element_type=F32)
    gates = _sigmoid(gt_ref[...])
    y = gates[0] * oc_ref[...] + gates[1] * o_s + gates[2] * ow_ref[...]

    @pl.when(i == 0)
    def _():
        o_ref[...] = jnp.zeros_like(o_ref)

    o_ref[...] += jnp.where(row_q == i, y, 0.0)


def _nsa_sample(proj3, cols, kvs, sd, cmp_big, pages_cmp, pages_slc, win_view, layer, page_table):
    col_qb, col_kv, col_gb = cols
    bsz = proj3.shape[0]
    n_phys, depth, _, _, page = pages_cmp.shape
    n_pages = page_table.shape[1]
    past = n_pages * page
    win = win_view.shape[-1]
    half = KV_B // 2
    rq = HG_B * sd
    n_ch = past // D_CMP
    n_cmp = (past + sd) // D_CMP - L_CMP // D_CMP + 1
    n_sb = -(-(past + sd) // L_SLC)
    k_sel = min(N_SEL, n_sb)
    assert (past + sd) // D_CMP == n_ch and past % L_SLC == 0 and sd <= L_SLC and page % L_SLC == 0
    _, b1big, w2big, b2big, w1tok = cmp_big
    pp = _cmp_pages(pages_cmp, layer, page_table, w1tok)
    q = proj3[:, :sd, col_qb:col_qb + MIX_B].reshape(bsz, sd, G_B, HG_B, HD_B).transpose(0, 2, 3, 1, 4)
    q = jnp.stack([jnp.pad(q[:, g], ((0, 0), (0, 0), (0, 0), (g * HD_B, half - (g + 1) * HD_B))) for g in range(G_B)], 1)
    q = q.reshape(bsz, G_B * rq, half).astype(BF16)
    gt = proj3[:, :sd, col_gb:col_gb + 3 * H_B].reshape(bsz, sd, G_B, HG_B, 3).transpose(0, 4, 2, 3, 1)
    gt = jnp.broadcast_to(gt.reshape(bsz, 3, G_B * rq, 1), (bsz, 3, G_B * rq, half))
    n_lane = -(-n_sb // 128) * 128
    mm = jnp.asarray(np.pad(_cmp_to_sel_map(n_ch, n_cmp, n_sb), ((0, n_lane - n_sb), (0, 0))).T)
    per_b = lambda shape: pl.BlockSpec((None,) + shape, lambda i: (i,) + (0,) * len(shape))
    const = lambda shape: pl.BlockSpec(shape, lambda i: (0,) * len(shape))
    o_c, o_w, idx = pl.pallas_call(
        functools.partial(_nsa_score_kernel, past, sd, n_cmp, n_sb),
        grid=(bsz,),
        in_specs=[per_b((n_ch, 2 * KV_B)), per_b((G_B * rq, half)),
                  pl.BlockSpec((None, None, 2, half, win), lambda i: (layer, i, 0, 0, 0)),
                  pl.BlockSpec((None, PAD_ROWS, KV_B), lambda i: (i, 0, col_kv // KV_B + 2)),
                  const((1, KV_B)), const((KV_B, KV_B)), const((1, KV_B)), const((n_ch, n_lane))],
        out_specs=[per_b((G_B * rq, half)), per_b((G_B * rq, half)), per_b((G_B * sd, 128))],
        out_shape=[jax.ShapeDtypeStruct((bsz, G_B * rq, half), F32), jax.ShapeDtypeStruct((bsz, G_B * rq, half), F32),
                   jax.ShapeDtypeStruct((bsz, G_B * sd, 128), jnp.int32)],
        compiler_params=_cparams(("parallel",), 40),
        name="nsa_score_sample",
    )(pp, q, win_view, proj3, b1big, w2big, b2big, mm)
    blk_pg = page // L_SLC
    n_cached = past // L_SLC
    new_blk = jnp.pad(kvs.reshape(bsz, sd, 2, half).transpose(0, 2, 3, 1), ((0, 0), (0, 0), (0, 0), (0, page - sd)))

    def blk_spec(j):
        def index(b, g, i, pt, ix, lyr):
            blk = jnp.minimum(ix[((b * G_B + g) * sd + i) * k_sel + j], n_cached - 1)
            return pt[b * n_pages + blk // blk_pg], lyr[0], 0, 0, 0
        return pl.BlockSpec((None, None, 2, half, page), index)

    grp = lambda b, g, i, pt, ix, lyr: (b, g, 0)
    grid_spec = pltpu.PrefetchScalarGridSpec(
        num_scalar_prefetch=3,
        grid=(bsz, G_B, sd),
        in_specs=[blk_spec(j) for j in range(k_sel)] + [
            pl.BlockSpec((None, 2, half, page), lambda b, g, i, pt, ix, lyr: (b, 0, 0, 0)),
            pl.BlockSpec((None, rq, half), grp), pl.BlockSpec((None, rq, half), grp), pl.BlockSpec((None, rq, half), grp),
            pl.BlockSpec((None, 3, rq, half), lambda b, g, i, pt, ix, lyr: (b, 0, g, 0))],
        out_specs=pl.BlockSpec((None, rq, half), grp))
    y = pl.pallas_call(
        functools.partial(_nsa_select_kernel, past, sd, n_sb, n_pages, k_sel),
        grid_spec=grid_spec,
        out_shape=jax.ShapeDtypeStruct((bsz, G_B * rq, half), F32),
        compiler_params=_cparams(("parallel", "parallel", "arbitrary"), 40),
        name="nsa_select_sample",
    )(page_table.reshape(-1), idx[:, :, :k_sel].reshape(-1), jnp.full((1,), layer, jnp.int32),
      *([pages_slc] * k_sel), new_blk, q, o_c, o_w, gt)
    y = y.reshape(bsz, G_B, HG_B, sd, G_B, HD_B)
    y = jnp.stack([y[:, g, :, :, g] for g in range(G_B)], axis=1)
    return y.transpose(0, 3, 1, 2, 4).reshape(bsz, sd, MIX_B)


def _prep_w_in(w, d):
    o = np.cumsum([0, MIX_A, MIX_A, MIX_A, MIX_A, MIX_B, KV_B, KV_B, KV_B, 3 * H_B, 2 * C_CONV, 3 * d])
    parts = [w[:, o[0]:o[4]], w[:, o[10]:o[11]], w[:, o[4]:o[5]], w[:, o[9]:o[10]], w[:, o[5]:o[8]], w[:, o[8]:o[9]]]
    n = sum(p.shape[1] for p in parts)
    n_pad = -(-n // 512) * 512
    parts.append(jnp.zeros((w.shape[0], n_pad - n), w.dtype))
    return jnp.concatenate(parts, axis=1).astype(BF16)


def _layer(x3, n_valid, p, s0, conv_prefix, mem_kv, nsa_fn, alpha):
    bsz, t, d = x3.shape
    m = bsz * t
    col_qb = COL_MG + 3 * d
    col_glu = col_qb + MIX_B
    col_kv = col_glu + 2 * C_CONV
    col_gb = col_kv + 3 * KV_B
    proj = _matmul(x3.reshape(m, d), p['w_in'], 1024, 512)
    proj3 = proj.reshape(bsz, t, -1)
    ya, s_new = _hgrn(proj3, p['lb'], p['hg_norm'], s0, n_valid)
    yc, conv_state = _conv(proj3, col_glu // C_CONV, conv_prefix, p['conv_w'], p['conv_b'],
                           p['conv_ln_g'], p['conv_ln_b'], n_valid)
    kv_shape = (bsz, n_valid, 2, G_B, HD_B)
    kvc = proj3[:, :n_valid, col_kv:col_kv + KV_B]
    kvs = proj3[:, :n_valid, col_kv + KV_B:col_kv + 2 * KV_B]
    kvw = proj3[:, :n_valid, col_kv + 2 * KV_B:col_kv + 3 * KV_B]
    yb, nsa_extra = nsa_fn(proj3, (col_qb, col_kv, col_gb), kvc, kvs, kvw)
    kvc, kvs = kvc.reshape(kv_shape), kvs.reshape(kv_shape)
    x1 = _merge(ya.reshape(m, MIX_A), yb.reshape(m, MIX_B), yc.reshape(m, C_CONV), proj, x3.reshape(m, d),
                p['w_pa'], p['w_pb'], p['w_pc'], p['w_out'], p['ln_g'][0:1], p['ln_b'][0:1], alpha)
    x2 = _xattn(x1.reshape(bsz, t, d), mem_kv, p['w_xq'], p['w_xo'], p['ln_g'][1:2], p['ln_b'][1:2], alpha)
    x3n = _mlp(x2.reshape(m, d), p['w_up'], p['w_down'], p['ln_g'][2:3], p['ln_b'][2:3], alpha)
    return x3n.reshape(bsz, t, d), kvc, kvs, nsa_extra, s_new, conv_state


def kernel(x_prompt, x_sample, cache_cmp, cache_slc, cache_win, state_hgrn, state_conv, cache_mem, page_table, mem_prompt, w_in, lb_raw, hg_norm, w_cmp1, b_cmp1, w_cmp2, b_cmp2, conv_w, conv_b, conv_ln_g, conv_ln_b, w_pa, w_pb, w_pc, w_out, ln_g, ln_b, w_xq, w_xkv, w_xo, w_up, w_down):
    bp, t, d = x_prompt.shape
    bd, sd = x_sample.shape[:2]
    depth = w_in.shape[0]
    n_mem = mem_prompt.shape[1]
    win_buf = cache_win.shape[2]
    alpha = (2 * depth) ** 0.25
    lb_cum = jnp.cumsum(jax.nn.softmax(lb_raw.astype(F32), axis=0), axis=0)
    lb_all = lb_cum - lb_cum[0]
    pages_cmp, pages_slc, win_view = _pages_view(cache_cmp), _pages_view(cache_slc), _pages_view(cache_win)
    xp = x_prompt
    xs = jnp.pad(x_sample, ((0, 0), (0, PAD_ROWS - sd), (0, 0)))
    outs = {k: [] for k in ('cmp_p', 'cmp_s', 'slc_p', 'slc_s', 'win_p', 'win_s', 'hg_p', 'hg_s', 'cv_p', 'cv_s', 'mem_p')}
    for l in range(depth):
        p = {'w_in': _prep_w_in(w_in[l], d), 'lb': lb_all[l], 'hg_norm': hg_norm[l], 'conv_w': conv_w[l],
             'conv_b': conv_b[l].reshape(1, -1), 'conv_ln_g': conv_ln_g[l].reshape(1, -1),
             'conv_ln_b': conv_ln_b[l].reshape(1, -1),
             'w_pa': w_pa[l].astype(BF16), 'w_pb': w_pb[l].astype(BF16), 'w_pc': w_pc[l].astype(BF16),
             'w_out': w_out[l].astype(BF16), 'ln_g': ln_g[l], 'ln_b': ln_b[l],
             'w_xq': w_xq[l].astype(BF16), 'w_xo': w_xo[l].astype(BF16),
             'w_up': w_up[l].astype(BF16), 'w_down': w_down[l].astype(BF16)}
        cmp_w = (w_cmp1[l], b_cmp1[l], w_cmp2[l], b_cmp2[l])

        cmp_big = _cmp_weights(*cmp_w)

        def nsa_prompt(proj3, cols, kvc, kvs, kvw):
            col_qb, col_kv, col_gb = cols
            o = _nsa_prompt(proj3, col_qb, col_gb, _nsa_prep(kvc, proj3, col_kv, cmp_big))
            win = jnp.pad(kvw, ((0, 0), (max(win_buf - t, 0), 0), (0, 0)))[:, -win_buf:]
            return o, win.reshape(bp, win_buf, 2, G_B, HD_B)

        def nsa_sample(proj3, cols, kvc, kvs, kvw):
            o = _nsa_sample(proj3, cols, kvs, sd, cmp_big, pages_cmp, pages_slc, win_view, l, page_table)
            o = jnp.pad(o, ((0, 0), (0, PAD_ROWS - sd), (0, 0)))
            win = jnp.concatenate([cache_win[l], kvw.reshape(bd, sd, 2, G_B, HD_B)], axis=1)[:, -win_buf:]
            return o, win

        mem_kv = _matmul(mem_prompt.reshape(bp * n_mem, d), w_xkv[l].astype(BF16), 1024, 512)
        mem_kv = mem_kv.reshape(bp, n_mem, 2 * d)
        xp, kc, ks_, wn, sh, cv = _layer(xp, t, p, jnp.zeros((bp, H_A, DK_A, DV_A), F32),
                                         jnp.zeros((bp, CONV_K - 1, C_CONV), F32), mem_kv, nsa_prompt, alpha)
        outs['cmp_p'].append(kc); outs['slc_p'].append(ks_); outs['win_p'].append(wn)
        outs['hg_p'].append(sh); outs['cv_p'].append(cv)
        outs['mem_p'].append(mem_kv.reshape(bp, n_mem, 2, NX_H, d // NX_H))
        xs, kc, ks_, wn, sh, cv = _layer(xs, sd, p, state_hgrn[l], state_conv[l],
                                         cache_mem[l].reshape(bd, n_mem, 2 * d), nsa_sample, alpha)
        outs['cmp_s'].append(kc); outs['slc_s'].append(ks_); outs['win_s'].append(wn)
        outs['hg_s'].append(sh); outs['cv_s'].append(cv)
    st = lambda k, ax: jnp.stack(outs[k], axis=ax)
    return (xp, xs[:, :sd],
            st('cmp_p', 1), st('cmp_s', 1), st('slc_p', 1), st('slc_s', 1),
            st('win_p', 0), st('win_s', 0), st('hg_p', 0), st('hg_s', 0),
            st('cv_p', 0), st('cv_s', 0), st('mem_p', 0))
```

```python
import functools

import numpy as np
import jax
import jax.numpy as jnp
from jax import lax
from jax.experimental import pallas as pl
from jax.experimental.pallas import tpu as pltpu

F32 = jnp.float32
BF16 = jnp.bfloat16
HIGHEST = lax.Precision.HIGHEST

H_A, DK_A, DV_A, CHUNK_A = 4, 128, 128, 16
H_B, G_B, HG_B, HD_B = 8, 2, 4, 64
L_CMP, D_CMP, L_SLC, N_SEL, WINDOW = 32, 16, 64, 16, 512
C_CONV, CONV_K = 512, 31
NX_H = 4
LN_EPS, RMS_EPS = 1e-5, 1e-6
PAD_ROWS = 16
LANES = 128

MIX_A = H_A * DK_A
MIX_B = H_B * HD_B
KV_B = 2 * G_B * HD_B
COL_QA, COL_FA, COL_IA, COL_GA = 0, 512, 1024, 1536
COL_MG = 2048
V7X_VMEM_LIMIT = 56 * 2**20


def _cparams(sem, vmem_mb=None):
    return pltpu.CompilerParams(dimension_semantics=sem,
                                vmem_limit_bytes=None if vmem_mb is None else vmem_mb * 2**20)


def _ln(y, g, b):
    mu = jnp.mean(y, axis=-1, keepdims=True)
    d = y - mu
    var = jnp.mean(d * d, axis=-1, keepdims=True)
    return d * lax.rsqrt(var + LN_EPS) * g + b


def _sigmoid(x):
    return 1.0 / (1.0 + jnp.exp(-x))


def _mm_kernel(x_ref, w_ref, o_ref, xb_ref):
    @pl.when(pl.program_id(1) == 0)
    def _():
        xb_ref[...] = x_ref[...].astype(BF16)

    o_ref[...] = jnp.dot(xb_ref[...], w_ref[...], preferred_element_type=F32)


def _matmul(x, w, tm, tn):
    m, k = x.shape
    n = w.shape[1]
    tm, tn = min(tm, m), min(tn, n)
    return pl.pallas_call(
        _mm_kernel,
        grid=(m // tm, n // tn),
        in_specs=[pl.BlockSpec((tm, k), lambda i, j: (i, 0)),
                  pl.BlockSpec((k, tn), lambda i, j: (0, j))],
        out_specs=pl.BlockSpec((tm, tn), lambda i, j: (i, j)),
        out_shape=jax.ShapeDtypeStruct((m, n), F32),
        scratch_shapes=[pltpu.VMEM((tm, k), BF16)],
        compiler_params=_cparams(("parallel", "arbitrary"), 40),
        name="proj_matmul",
    )(x, w)


def _mlp_kernel(alpha, x_ref, wu_ref, wd_ref, g_ref, b_ref, o_ref, xb_ref, acc_ref):
    j = pl.program_id(1)

    @pl.when(j == 0)
    def _():
        xb_ref[...] = x_ref[...].astype(BF16)
        acc_ref[...] = jnp.zeros_like(acc_ref)

    h = jnp.dot(xb_ref[...], wu_ref[...], preferred_element_type=F32)
    h = jnp.square(jnp.maximum(h, 0.0)).astype(BF16)
    acc_ref[...] += jnp.dot(h, wd_ref[...], preferred_element_type=F32)

    @pl.when(j == pl.num_programs(1) - 1)
    def _():
        o_ref[...] = _ln(alpha * x_ref[...] + acc_ref[...], g_ref[...], b_ref[...])


def _mlp(x, w_up, w_down, g, b, alpha, tm=512, tf=1024):
    m, d = x.shape
    ff = w_up.shape[1]
    tm = min(tm, m)
    return pl.pallas_call(
        functools.partial(_mlp_kernel, alpha),
        grid=(m // tm, ff // tf),
        in_specs=[pl.BlockSpec((tm, d), lambda i, j: (i, 0)),
                  pl.BlockSpec((d, tf), lambda i, j: (0, j)),
                  pl.BlockSpec((tf, d), lambda i, j: (j, 0)),
                  pl.BlockSpec((1, d), lambda i, j: (0, 0)),
                  pl.BlockSpec((1, d), lambda i, j: (0, 0))],
        out_specs=pl.BlockSpec((tm, d), lambda i, j: (i, 0)),
        out_shape=jax.ShapeDtypeStruct((m, d), F32),
        scratch_shapes=[pltpu.VMEM((tm, d), BF16), pltpu.VMEM((tm, d), F32)],
        compiler_params=_cparams(("parallel", "arbitrary"), 48),
        name="mlp",
    )(x, w_up, w_down, g, b)


def _merge_kernel(alpha, ya_ref, yb_ref, yc_ref, ma_ref, mb_ref, mc_ref, x_ref,
                  wpa_ref, wpb_ref, wpc_ref, wout_ref, g_ref, b_ref, o_ref):
    def branch(y_ref, m_ref, w_ref):
        return _sigmoid(m_ref[...]) * jnp.dot(y_ref[...].astype(BF16), w_ref[...], preferred_element_type=F32)

    merged = branch(ya_ref, ma_ref, wpa_ref) + branch(yb_ref, mb_ref, wpb_ref) + branch(yc_ref, mc_ref, wpc_ref)
    y = jnp.dot(merged.astype(BF16), wout_ref[...], preferred_element_type=F32)
    o_ref[...] = _ln(alpha * x_ref[...] + y, g_ref[...], b_ref[...])


def _merge(ya, yb, yc, proj, x, wpa, wpb, wpc, wout, g, b, alpha, tm=256):
    m, d = x.shape
    tm = min(tm, m)
    mg0 = COL_MG // d
    row = lambda i: (i, 0)
    const = lambda i: (0, 0)
    return pl.pallas_call(
        functools.partial(_merge_kernel, alpha),
        grid=(m // tm,),
        in_specs=[pl.BlockSpec((tm, MIX_A), row), pl.BlockSpec((tm, MIX_B), row), pl.BlockSpec((tm, C_CONV), row),
                  pl.BlockSpec((tm, d), lambda i: (i, mg0)), pl.BlockSpec((tm, d), lambda i: (i, mg0 + 1)),
                  pl.BlockSpec((tm, d), lambda i: (i, mg0 + 2)),
                  pl.BlockSpec((tm, d), row),
                  pl.BlockSpec((MIX_A, d), const), pl.BlockSpec((MIX_B, d), const), pl.BlockSpec((C_CONV, d), const),
                  pl.BlockSpec((d, d), const), pl.BlockSpec((1, d), const), pl.BlockSpec((1, d), const)],
        out_specs=pl.BlockSpec((tm, d), row),
        out_shape=jax.ShapeDtypeStruct((m, d), F32),
        compiler_params=_cparams(("parallel",), 48),
        name="merge_out",
    )(ya, yb, yc, proj, proj, proj, x, wpa, wpb, wpc, wout, g, b)


def _xattn_kernel(alpha, x_ref, kv_ref, wq_ref, wo_ref, g_ref, b_ref, o_ref):
    x = x_ref[...]
    d = x.shape[-1]
    hd = d // NX_H
    q = jnp.dot(x.astype(BF16), wq_ref[...], preferred_element_type=F32)
    n_dt = hd // LANES

    def head(base, h):
        parts = [kv_ref[:, base + (dt * NX_H + h) * LANES:base + (dt * NX_H + h + 1) * LANES] for dt in range(n_dt)]
        return jnp.concatenate(parts, axis=1).astype(BF16)

    outs = []
    for h in range(NX_H):
        qh = q[:, h * hd:(h + 1) * hd].astype(BF16)
        kh, vh = head(0, h), head(d, h)
        s = lax.dot_general(qh, kh, (((1,), (1,)), ((), ())), preferred_element_type=F32) * (hd ** -0.5)
        e = jnp.exp(s - jnp.max(s, axis=-1, keepdims=True))
        p = e / jnp.sum(e, axis=-1, keepdims=True)
        outs.append(jnp.dot(p.astype(BF16), vh, preferred_element_type=F32))
    o = jnp.concatenate(outs, axis=-1)
    y = jnp.dot(o.astype(BF16), wo_ref[...], preferred_element_type=F32)
    o_ref[...] = _ln(alpha * x + y, g_ref[...], b_ref[...])


def _mem_tile_order(a, d):
    lead = a.shape[:-3]
    a = a.reshape(lead + (2, NX_H, d // NX_H // LANES, LANES))
    return jnp.swapaxes(a, -3, -2).reshape(lead + (2 * d,))


def _mem_head_order(a, d):
    lead = a.shape[:-1]
    a = a.reshape(lead + (2, d // NX_H // LANES, NX_H, LANES))
    return jnp.swapaxes(a, -3, -2).reshape(lead + (2, NX_H, d // NX_H))


def _xattn(x, kv, layer, wq, wo, g, b, alpha, tm=256):
    bsz, t, d = x.shape
    n_mem = kv.shape[2]
    tm = min(tm, t)
    const = lambda i, j: (0, 0)
    return pl.pallas_call(
        functools.partial(_xattn_kernel, alpha),
        grid=(bsz, t // tm),
        in_specs=[pl.BlockSpec((None, tm, d), lambda i, j: (i, j, 0)),
                  pl.BlockSpec((None, None, n_mem, 2 * d), lambda i, j: (layer, i, 0, 0)),
                  pl.BlockSpec((d, d), const), pl.BlockSpec((d, d), const),
                  pl.BlockSpec((1, d), const), pl.BlockSpec((1, d), const)],
        out_specs=pl.BlockSpec((None, tm, d), lambda i, j: (i, j, 0)),
        out_shape=jax.ShapeDtypeStruct((bsz, t, d), F32),
        compiler_params=_cparams(("parallel", "parallel"), 48),
        name="xattn",
    )(x, kv, wq, wo, g, b)


CONV_HALO = 32
CONV_SUB = 32


def _conv_kernel(rt, n_valid_last, a_ref, gt_ref, pre_ref, w_ref, cb_ref, g_ref, b_ref, y_ref, st_ref, ue_ref, sh_ref):
    t = pl.program_id(1)
    off = CONV_HALO - (CONV_K - 1)
    sl = 8

    @pl.when(t == 0)
    def _():
        ue_ref[0:off, :] = jnp.zeros((off, C_CONV), F32)
        ue_ref[off:CONV_HALO, :] = pre_ref[...]

    ue_ref[CONV_HALO:CONV_HALO + rt, :] = a_ref[...] * _sigmoid(gt_ref[...])
    n_sh = rt + CONV_HALO - sl
    for s in range(1, sl):
        sh_ref[s - 1, 0:n_sh, :] = ue_ref[s:s + n_sh, :]
    sub = min(CONV_SUB, rt)
    for r0 in range(0, rt, sub):
        acc = cb_ref[...]
        for j in range(CONV_K):
            a, s = divmod(off + j, sl)
            lo = r0 + a * sl
            win = ue_ref[lo:lo + sub, :] if s == 0 else sh_ref[s - 1, lo:lo + sub, :]
            acc = acc + w_ref[j:j + 1, :] * win
        y = _ln(acc, g_ref[...], b_ref[...])
        y_ref[r0:r0 + sub, :] = y * _sigmoid(y)

    @pl.when(t == pl.num_programs(1) - 1)
    def _():
        st_ref[...] = ue_ref[off + n_valid_last:off + n_valid_last + CONV_K - 1, :]

    ue_ref[0:CONV_HALO, :] = ue_ref[rt:rt + CONV_HALO, :]


def _conv(proj3, col_a, prefix, w, cb, g, b, n_valid, rt=256):
    bsz, t, _ = proj3.shape
    rt = min(rt, t)
    n_valid_last = n_valid - (t - rt)
    vec = lambda i, j: (0, 0)
    return pl.pallas_call(
        functools.partial(_conv_kernel, rt, n_valid_last),
        grid=(bsz, t // rt),
        in_specs=[pl.BlockSpec((None, rt, C_CONV), lambda i, j: (i, j, col_a)),
                  pl.BlockSpec((None, rt, C_CONV), lambda i, j: (i, j, col_a + 1)),
                  pl.BlockSpec((None, CONV_K - 1, C_CONV), lambda i, j: (i, 0, 0)),
                  pl.BlockSpec((CONV_K, C_CONV), vec), pl.BlockSpec((1, C_CONV), vec),
                  pl.BlockSpec((1, C_CONV), vec), pl.BlockSpec((1, C_CONV), vec)],
        out_specs=[pl.BlockSpec((None, rt, C_CONV), lambda i, j: (i, j, 0)),
                   pl.BlockSpec((None, CONV_K - 1, C_CONV), lambda i, j: (i, 0, 0))],
        out_shape=[jax.ShapeDtypeStruct((bsz, t, C_CONV), F32),
                   jax.ShapeDtypeStruct((bsz, CONV_K - 1, C_CONV), F32)],
        scratch_shapes=[pltpu.VMEM((rt + CONV_HALO, C_CONV), F32), pltpu.VMEM((7, rt + CONV_HALO - 8, C_CONV), F32)],
        compiler_params=_cparams(("parallel", "arbitrary"), 32),
        name="conformer_conv",
    )(proj3, proj3, prefix, w, cb, g, b)


def _hgrn_kernel(tt, n_valid, q_ref, f_ref, i_ref, g_ref, la_ref, l1_ref, oml_ref, gn_ref, s0_ref,
                 y_ref, s_ref, st_ref, qs_ref, kk_ref, bb_ref):
    c = CHUNK_A
    t = pl.program_id(1)

    @pl.when(t == 0)
    def _():
        for h in range(H_A):
            st_ref[h] = s0_ref[h].T

    z = f_ref[...]
    cc = l1_ref[...] + (jnp.minimum(z, 0.0) - jnp.log1p(jnp.exp(-jnp.abs(z))))
    a = la_ref[...]
    logf = jnp.maximum(a, cc) + jnp.log1p(jnp.exp(-jnp.abs(a - cc)))
    k = oml_ref[...] * _sigmoid(-z)
    row = lax.broadcasted_iota(jnp.int32, (tt, 1), 0)
    if n_valid < tt:
        logf = jnp.where(row < n_valid, logf, 0.0)
        k = jnp.where(row < n_valid, k, 0.0)
    b = logf
    rc = row & (c - 1)
    sh = 1
    while sh < c:
        b = b + jnp.where(rc >= sh, pltpu.roll(b, sh, 0), 0.0)
        sh *= 2
    q = q_ref[...]
    qs_ref[...] = q * _sigmoid(q)
    kk_ref[...] = k
    bb_ref[...] = b

    rowc = lax.broadcasted_iota(jnp.int32, (c, 1), 0)

    def chunk(ci, carry):
        r0 = pl.multiple_of(ci * c, c)
        for h in range(H_A):
            hs = slice(h * DK_A, (h + 1) * DK_A)
            qc = qs_ref[pl.ds(r0, c), hs]
            kc = kk_ref[pl.ds(r0, c), hs]
            bc = bb_ref[pl.ds(r0, c), hs]
            vc = i_ref[pl.ds(r0, c), hs]
            bl = bc[c - 1:c, :]
            st = st_ref[h]
            qe = (qc * jnp.exp(bc)).astype(BF16)
            o = lax.dot_general(qe, st.astype(BF16), (((1,), (1,)), ((), ())), preferred_element_type=F32)
            for s in range(c):
                e = jnp.exp(jnp.where(rowc >= s, bc - bc[s:s + 1, :], -jnp.inf))
                att = jnp.sum(qc * e * kc[s:s + 1, :], axis=-1, keepdims=True)
                o = o + att * vc[s:s + 1, :]
            y_ref[pl.ds(r0, c), hs] = o
            kd = (kc * jnp.exp(bl - bc)).astype(BF16)
            u = lax.dot_general(vc.astype(BF16), kd, (((0,), (0,)), ((), ())), preferred_element_type=F32)
            st_ref[h] = st * jnp.exp(bl) + u
        return carry

    lax.fori_loop(0, tt // c, chunk, 0)

    g = g_ref[...]
    gate = g * _sigmoid(g)
    for h in range(H_A):
        hs = slice(h * DV_A, (h + 1) * DV_A)
        o = y_ref[:, hs]
        o = o * lax.rsqrt(jnp.mean(o * o, axis=-1, keepdims=True) + RMS_EPS) * gn_ref[...]
        y_ref[:, hs] = o * gate[:, hs]

    @pl.when(t == pl.num_programs(1) - 1)
    def _():
        for h in range(H_A):
            s_ref[h] = st_ref[h].T


def _hgrn(proj3, lb, gnorm, s0, n_valid, tt=256):
    bsz, t, _ = proj3.shape
    tt = min(tt, t)
    lb = lb.reshape(1, MIX_A).astype(F32)
    la, l1, oml = jnp.log(lb), jnp.log1p(-lb), 1.0 - lb
    vec = lambda i, j: (0, 0)
    col = lambda cb: pl.BlockSpec((None, tt, MIX_A), lambda i, j: (i, j, cb))
    st = pl.BlockSpec((None, H_A, DK_A, DV_A), lambda i, j: (i, 0, 0, 0))
    return pl.pallas_call(
        functools.partial(_hgrn_kernel, tt, n_valid if t == tt else tt),
        grid=(bsz, t // tt),
        in_specs=[col(0), col(1), col(2), col(3),
                  pl.BlockSpec((1, MIX_A), vec), pl.BlockSpec((1, MIX_A), vec), pl.BlockSpec((1, MIX_A), vec),
                  pl.BlockSpec((1, DV_A), vec), st],
        out_specs=[pl.BlockSpec((None, tt, MIX_A), lambda i, j: (i, j, 0)), st],
        out_shape=[jax.ShapeDtypeStruct((bsz, t, MIX_A), F32),
                   jax.ShapeDtypeStruct((bsz, H_A, DK_A, DV_A), F32)],
        scratch_shapes=[pltpu.VMEM((H_A, DV_A, DK_A), F32), pltpu.VMEM((tt, MIX_A), F32),
                        pltpu.VMEM((tt, MIX_A), F32), pltpu.VMEM((tt, MIX_A), F32)],
        compiler_params=_cparams(("parallel", "arbitrary"), 32),
        name="hgrn2",
    )(proj3, proj3, proj3, proj3, la, l1, oml, gnorm.reshape(1, DV_A), s0)


NEG_BIG = -1e30
SEL_PER_CMP = L_SLC // D_CMP


def _cmp_weights(w1, b1, w2, b2):
    m = L_CMP // D_CMP
    eye_e, eye_g = jnp.eye(2, dtype=F32), jnp.eye(G_B, dtype=F32)
    w1r = w1.reshape(2, m, D_CMP, HD_B, HD_B)
    w1big = jnp.einsum('ehjdf,ea,gb->jegdhabf', w1r, eye_e, eye_g).reshape(D_CMP * KV_B, m * KV_B)
    w2big = jnp.einsum('efo,ea,gb->egfabo', w2, eye_e, eye_g).reshape(KV_B, KV_B)
    b1big = jnp.broadcast_to(b1[:, None, :], (2, G_B, HD_B)).reshape(1, KV_B)
    b2big = jnp.broadcast_to(b2[:, None, :], (2, G_B, HD_B)).reshape(1, KV_B)
    w1tok = jnp.einsum('ehjdf,gb->ejgdhbf', w1r, eye_g).reshape(2, D_CMP, KV_B // 2, m * KV_B // 2)
    return w1big.astype(BF16), b1big, w2big.astype(BF16), b2big, w1tok.astype(BF16)


def _cmp_to_sel_map(n_cmp_pad, n_cmp, n_sb):
    mm = np.zeros((n_sb, n_cmp_pad), np.float32)
    for n in range(n_cmp):
        for i in (n, n + 1):
            if i // SEL_PER_CMP < n_sb:
                mm[i // SEL_PER_CMP, n] += 1.0
    return mm


def _compress_tail(pp, n_cmp, b1_ref, w2_ref, b2_ref):
    n_ch = pp.shape[0]
    hid = pp[:, :KV_B] + pltpu.roll(pp[:, KV_B:], n_ch - 1, 0) + b1_ref[...]
    hid = hid * _sigmoid(hid)
    kc = jnp.dot(hid.astype(BF16), w2_ref[...], preferred_element_type=F32) + b2_ref[...]
    row = lax.broadcasted_iota(jnp.int32, (n_ch, 1), 0)
    return jnp.where(row < n_cmp, kc, 0.0)


def _nsa_prep_kernel(n_cmp, win, kc_ref, ks_ref, kw_ref, w1_ref, b1_ref, w2_ref, b2_ref,
                     kck_ref, kcvt_ref, ksk_ref, ksvt_ref, kwk_ref, kwvt_ref, ct_ref, st_ref, wt_ref, xk_ref):
    half = KV_B // 2
    t = kc_ref.shape[0]
    n_ch = t // D_CMP
    for e in range(2):
        xk_ref[e] = kc_ref[:, e * half:(e + 1) * half]
    pp = jnp.zeros((n_ch, w1_ref.shape[2]), F32)
    for j in range(D_CMP):
        for e in range(2):
            x = xk_ref[e, pl.ds(j, n_ch, stride=D_CMP), :].astype(BF16)
            pp = pp + jnp.dot(x, w1_ref[j, e * half:(e + 1) * half, :], preferred_element_type=F32)
    kc = _compress_tail(pp, n_cmp, b1_ref, w2_ref, b2_ref)
    kck_ref[...] = kc[:, :half].astype(BF16)
    kcvt_ref[...] = kc[:, half:].T.astype(BF16)
    ct_ref[...] = kc_ref[...].T
    ks_t = ks_ref[...].T
    st_ref[...] = ks_t
    ksk_ref[...] = ks_ref[:, :half].astype(BF16)
    ksvt_ref[...] = ks_t[half:, :].astype(BF16)
    kw_t = kw_ref[...].T
    wt_ref[...] = kw_t[:, t - win:]
    kwk_ref[...] = kw_ref[:, :half].astype(BF16)
    kwvt_ref[...] = kw_t[half:, :].astype(BF16)


def _nsa_prep(proj3, col_kv, cmp_big, win):
    bsz, t, _ = proj3.shape
    n_ch = t // D_CMP
    n_cmp = n_ch - L_CMP // D_CMP + 1
    half = KV_B // 2
    w1big, b1big, w2big, b2big = cmp_big[:4]
    w1pos = w1big.reshape(D_CMP, KV_B, w1big.shape[1])
    cb = col_kv // KV_B
    const = lambda i: (0, 0)
    kspec = lambda n: pl.BlockSpec((None, n, half), lambda i: (i, 0, 0))
    vspec = lambda n: pl.BlockSpec((None, half, n), lambda i: (i, 0, 0))
    fspec = lambda n: pl.BlockSpec((None, KV_B, n), lambda i: (i, 0, 0))
    col = lambda c: pl.BlockSpec((None, t, KV_B), lambda i: (i, 0, cb + c))
    return pl.pallas_call(
        functools.partial(_nsa_prep_kernel, n_cmp, win),
        grid=(bsz,),
        in_specs=[col(0), col(1), col(2),
                  pl.BlockSpec(w1pos.shape, lambda i: (0, 0, 0)), pl.BlockSpec((1, KV_B), const),
                  pl.BlockSpec((KV_B, KV_B), const), pl.BlockSpec((1, KV_B), const)],
        out_specs=[kspec(n_ch), vspec(n_ch), kspec(t), vspec(t), kspec(t), vspec(t), fspec(t), fspec(t), fspec(win)],
        out_shape=[jax.ShapeDtypeStruct((bsz, n_ch, half), BF16), jax.ShapeDtypeStruct((bsz, half, n_ch), BF16),
                   jax.ShapeDtypeStruct((bsz, t, half), BF16), jax.ShapeDtypeStruct((bsz, half, t), BF16),
                   jax.ShapeDtypeStruct((bsz, t, half), BF16), jax.ShapeDtypeStruct((bsz, half, t), BF16),
                   jax.ShapeDtypeStruct((bsz, KV_B, t), F32), jax.ShapeDtypeStruct((bsz, KV_B, t), F32),
                   jax.ShapeDtypeStruct((bsz, KV_B, win), F32)],
        scratch_shapes=[pltpu.VMEM((2, t, half), F32)],
        compiler_params=_cparams(("parallel",), 48),
        name="nsa_prep",
    )(proj3, proj3, proj3, w1pos, b1big, w2big, b2big)


def _nsa_prompt_kernel(tq, tk, n_sb, q_ref, gb_ref, kck_ref, kcvt_ref, ksk_ref, ksvt_ref, kwk_ref, kwvt_ref, mm_ref, o_ref):
    qi = pl.program_id(1)
    q0 = qi * tq
    n_cp = kck_ref.shape[0]
    w4 = HG_B * tq
    q_t = (q_ref[...] * HD_B ** -0.5).T
    g_t = _sigmoid(gb_ref[...]).T
    qpos = q0 + lax.broadcasted_iota(jnp.int32, (1, tq), 1)
    qpos4 = jnp.concatenate([qpos] * HG_B, axis=1)
    zpad = jnp.zeros((HD_B, tq), BF16)

    def update(state, s, pen, vt, g):
        m, l, acc = state
        s = s + jnp.concatenate([pen] * HG_B, axis=1)
        m_new = jnp.maximum(m, jnp.max(s, axis=0, keepdims=True))
        e = jnp.exp(s - m_new)
        a = jnp.exp(m - m_new)
        l = a * l + jnp.sum(e, axis=0, keepdims=True)
        pv = jnp.dot(vt[g * HD_B:(g + 1) * HD_B, :], e.astype(BF16), preferred_element_type=F32)
        return m_new, l, a * acc + pv

    qts, o_cs, sel_ts = [], [], []
    for g in range(G_B):
        cols = []
        for hg in range(HG_B):
            h = g * HG_B + hg
            qh = q_t[h * HD_B:(h + 1) * HD_B, :].astype(BF16)
            cols.append(jnp.concatenate([qh, zpad] if g == 0 else [zpad, qh], axis=0))
        qt = jnp.concatenate(cols, axis=1)
        qts.append(qt)
        s = jnp.dot(kck_ref[...], qt, preferred_element_type=F32)
        c_end = lax.broadcasted_iota(jnp.int32, (n_cp, w4), 0) * D_CMP + (L_CMP - 1)
        mask = c_end <= qpos4
        m = jnp.max(jnp.where(mask, s, NEG_BIG), axis=0, keepdims=True)
        e = jnp.where(mask, jnp.exp(s - m), 0.0)
        d = jnp.sum(e, axis=0, keepdims=True)
        p = e / jnp.where(d > 0, d, 1.0)
        o_cs.append(jnp.dot(kcvt_ref[...], p.astype(BF16), preferred_element_type=F32)[g * HD_B:(g + 1) * HD_B, :])
        imp = p[:, 0:tq]
        for hg in range(1, HG_B):
            imp = imp + p[:, hg * tq:(hg + 1) * tq]
        bs = jnp.dot(mm_ref[...], imp, precision=HIGHEST, preferred_element_type=F32)
        blk = lax.broadcasted_iota(jnp.int32, (n_sb, tq), 0)
        cur = qpos >> 6
        forced = (blk == 0) | (blk == cur) | (blk == cur - 1)
        score = jnp.where(blk <= cur, jnp.where(forced, jnp.inf, bs), -jnp.inf)
        rank = jnp.zeros((n_sb, tq), jnp.int32)
        for i in range(n_sb):
            si = score[i:i + 1, :]
            rank = rank + jnp.where((si > score) | ((si == score) & (blk > i)), 1, 0)
        sel_ts.append(jnp.where((rank < N_SEL) & (blk <= cur), 1.0, 0.0).astype(BF16))

    sel_both = jnp.concatenate(sel_ts, axis=1)
    ej = lax.broadcasted_iota(jnp.int32, (tk, n_sb), 1)
    ek = lax.broadcasted_iota(jnp.int32, (tk, n_sb), 0)
    krow = lax.broadcasted_iota(jnp.int32, (tk, tq), 0)

    def sel_penalties(k0):
        expand = jnp.where(ej == ((k0 + ek) >> 6), 1.0, 0.0).astype(BF16)
        pen = (jnp.dot(expand, sel_both, preferred_element_type=F32) - 1.0) * (-NEG_BIG)
        return [pen[:, g * tq:(g + 1) * tq] for g in range(G_B)]

    def far_tiles(kt, states):
        k0 = pl.multiple_of(kt * tk, tk)
        k_tile, vt_tile = ksk_ref[pl.ds(k0, tk), :], ksvt_ref[:, pl.ds(k0, tk)]
        pens = sel_penalties(k0)
        return tuple(update(states[g], jnp.dot(k_tile, qts[g], preferred_element_type=F32), pens[g], vt_tile, g)
                     for g in range(G_B))

    def near_tiles(kt, states):
        k0 = pl.multiple_of(kt * tk, tk)
        rel = qpos - k0
        causal = krow <= rel
        pen_causal = jnp.where(causal, 0.0, NEG_BIG)
        pen_win = jnp.where(causal & (krow > rel - WINDOW), 0.0, NEG_BIG)
        k_tile, vt_tile = ksk_ref[pl.ds(k0, tk), :], ksvt_ref[:, pl.ds(k0, tk)]
        kw_tile, vwt_tile = kwk_ref[pl.ds(k0, tk), :], kwvt_ref[:, pl.ds(k0, tk)]
        pens = sel_penalties(k0)
        new_sel = tuple(update(states[g], jnp.dot(k_tile, qts[g], preferred_element_type=F32),
                               pens[g] + pen_causal, vt_tile, g) for g in range(G_B))
        new_win = tuple(update(states[G_B + g], jnp.dot(kw_tile, qts[g], preferred_element_type=F32),
                               pen_win, vwt_tile, g) for g in range(G_B))
        return new_sel + new_win

    init = (jnp.full((1, w4), NEG_BIG, F32), jnp.zeros((1, w4), F32), jnp.zeros((HD_B, w4), F32))
    first_near = jnp.maximum((q0 - WINDOW) // tk, 0)
    states = lax.fori_loop(0, first_near, far_tiles, (init,) * G_B)
    states = lax.fori_loop(first_near, (q0 + tq) // tk, near_tiles, states + (init,) * G_B)
    finish = lambda st: st[2] / jnp.where(st[1] > 0, st[1], 1.0)
    outs = []
    for g in range(G_B):
        o_c, o_s, o_w = o_cs[g], finish(states[g]), finish(states[G_B + g])
        for hg in range(HG_B):
            c = (g * HG_B + hg) * 3
            sl = slice(hg * tq, (hg + 1) * tq)
            outs.append(g_t[c:c + 1, :] * o_c[:, sl] + g_t[c + 1:c + 2, :] * o_s[:, sl]
                        + g_t[c + 2:c + 3, :] * o_w[:, sl])
    o_ref[...] = jnp.concatenate(outs, axis=0).T


def _nsa_prompt(proj3, col_qb, col_gb, prep, tq=256, tk=256):
    bsz, t, _ = proj3.shape
    tq, tk = min(tq, t), min(tk, t)
    kck, kcvt, ksk, ksvt, kwk, kwvt = prep
    n_ch = kck.shape[1]
    n_cmp = n_ch - L_CMP // D_CMP + 1
    n_sb = -(-t // L_SLC)
    half = KV_B // 2
    mm = jnp.asarray(_cmp_to_sel_map(n_ch, n_cmp, n_sb))
    per_b = lambda shape: pl.BlockSpec((None,) + shape, lambda i, j: (i, 0, 0))
    return pl.pallas_call(
        functools.partial(_nsa_prompt_kernel, tq, tk, n_sb),
        grid=(bsz, t // tq),
        in_specs=[pl.BlockSpec((None, tq, MIX_B), lambda i, j: (i, j, col_qb // MIX_B)),
                  pl.BlockSpec((None, tq, 128), lambda i, j: (i, j, col_gb // 128)),
                  per_b((n_ch, half)), per_b((half, n_ch)), per_b((t, half)), per_b((half, t)),
                  per_b((t, half)), per_b((half, t)),
                  pl.BlockSpec((n_sb, n_ch), lambda i, j: (0, 0))],
        out_specs=pl.BlockSpec((None, tq, MIX_B), lambda i, j: (i, j, 0)),
        out_shape=jax.ShapeDtypeStruct((bsz, t, MIX_B), F32),
        compiler_params=_cparams(("parallel", "parallel"), 40),
        name="nsa_prompt",
    )(proj3, proj3, kck, kcvt, ksk, ksvt, kwk, kwvt, mm)


def _pages_view(cache):
    n_phys, depth, page = cache.shape[:3]
    return jnp.transpose(cache, (0, 1, 3, 4, 5, 2)).reshape(n_phys, depth, 2, KV_B // 2, page)


def _cmp_pages_kernel(n_pg, page, *refs):
    pages, w1_ref, o_ref, xs_ref = refs[2:2 + n_pg], refs[2 + n_pg], refs[3 + n_pg], refs[4 + n_pg]
    half = KV_B // 2
    ch_pg = page // D_CMP
    n_ch = n_pg * ch_pg
    r = lax.broadcasted_iota(jnp.int32, (page, page), 0)
    pos = lax.broadcasted_iota(jnp.int32, (page, page), 1)
    pick = jnp.where(pos == (r % ch_pg) * D_CMP + r // ch_pg, 1.0, 0.0).astype(BF16)
    for i, pg in enumerate(pages):
        for e in range(2):
            y = lax.dot_general(pick, pg[e].astype(BF16), (((1,), (1,)), ((), ())), preferred_element_type=F32)
            for j in range(D_CMP):
                xs_ref[e, j, i * ch_pg:(i + 1) * ch_pg, :] = y[j * ch_pg:(j + 1) * ch_pg, :]
    for e in range(2):
        acc = jnp.zeros((n_ch, 2 * half), F32)
        for j in range(D_CMP):
            acc = acc + jnp.dot(xs_ref[e, j].astype(BF16), w1_ref[e, j], preferred_element_type=F32)
        o_ref[:, e * half:(e + 1) * half] = acc[:, :half]
        o_ref[:, KV_B + e * half:KV_B + (e + 1) * half] = acc[:, half:]


def _cmp_pages(pages, layer, page_table, w1e):
    n_phys, depth, _, _, page = pages.shape
    bsz, n_pages = page_table.shape
    ch_pg = page // D_CMP
    n_pg = next(c for c in (32, 16, 8, 4, 2, 1) if n_pages % c == 0)

    def page_spec(i):
        return pl.BlockSpec((None, None, 2, KV_B // 2, page),
                            lambda b, s, pt, lyr: (pt[b * n_pages + s * n_pg + i], lyr[0], 0, 0, 0))

    grid_spec = pltpu.PrefetchScalarGridSpec(
        num_scalar_prefetch=2,
        grid=(bsz, n_pages // n_pg),
        in_specs=[page_spec(i) for i in range(n_pg)] + [pl.BlockSpec(w1e.shape, lambda b, s, pt, lyr: (0, 0, 0, 0))],
        out_specs=pl.BlockSpec((None, n_pg * ch_pg, 2 * KV_B), lambda b, s, pt, lyr: (b, s, 0)),
        scratch_shapes=[pltpu.VMEM((2, D_CMP, n_pg * ch_pg, KV_B // 2), F32)])
    return pl.pallas_call(
        functools.partial(_cmp_pages_kernel, n_pg, page),
        grid_spec=grid_spec,
        out_shape=jax.ShapeDtypeStruct((bsz, n_pages * ch_pg, 2 * KV_B), F32),
        compiler_params=_cparams(("parallel", "parallel"), 48),
        name="nsa_cmp_pages",
    )(page_table.reshape(-1), jnp.full((1,), layer, jnp.int32), *([pages] * n_pg), w1e)


def _nsa_score_kernel(past, sd, n_cmp, n_sb, pp_ref, q_ref, cw_ref, nw_ref, b1_ref, w2_ref, b2_ref, mm_ref,
                      oc_ref, ow_ref, idx_ref):
    half = KV_B // 2
    rq = HG_B * sd
    scale = HD_B ** -0.5
    n_ch = pp_ref.shape[0]
    win = cw_ref.shape[2]
    kc = _compress_tail(pp_ref[...], n_cmp, b1_ref, w2_ref, b2_ref)
    kck, kcv = kc[:, :half].astype(BF16), kc[:, half:].astype(BF16)
    qpos = past + lax.broadcasted_iota(jnp.int32, (rq, 1), 0) % sd
    nt = (((1,), (1,)), ((), ()))

    def softmax(parts):
        m = functools.reduce(jnp.maximum, [jnp.max(jnp.where(mk, s, NEG_BIG), axis=-1, keepdims=True) for s, mk in parts])
        es = [jnp.where(mk, jnp.exp(s - m), 0.0) for s, mk in parts]
        d = functools.reduce(jnp.add, [jnp.sum(e, axis=-1, keepdims=True) for e in es])
        return [e / jnp.where(d > 0, d, 1.0) for e in es]

    imps = []
    for g in range(G_B):
        qg = q_ref[g * rq:(g + 1) * rq, :]
        s = lax.dot_general(qg, kck, nt, preferred_element_type=F32) * scale
        c_end = lax.broadcasted_iota(jnp.int32, (rq, n_ch), 1) * D_CMP + (L_CMP - 1)
        p, = softmax([(s, c_end <= qpos)])
        oc_ref[g * rq:(g + 1) * rq, :] = jnp.dot(p.astype(BF16), kcv, preferred_element_type=F32)
        imps.append(functools.reduce(jnp.add, [p[hg * sd:(hg + 1) * sd, :] for hg in range(HG_B)]))
        s1 = jnp.dot(qg, cw_ref[0].astype(BF16), preferred_element_type=F32) * scale
        s2 = lax.dot_general(qg, nw_ref[:, :half].astype(BF16), nt, preferred_element_type=F32) * scale
        wp1 = past - win + lax.broadcasted_iota(jnp.int32, s1.shape, 1)
        j2 = lax.broadcasted_iota(jnp.int32, s2.shape, 1)
        wp2 = past + j2
        p1, p2 = softmax([(s1, (wp1 <= qpos) & (wp1 > qpos - WINDOW) & (wp1 >= 0)),
                          (s2, (wp2 <= qpos) & (wp2 > qpos - WINDOW) & (j2 < sd))])
        ow_ref[g * rq:(g + 1) * rq, :] = (
            lax.dot_general(p1.astype(BF16), cw_ref[1].astype(BF16), nt, preferred_element_type=F32)
            + jnp.dot(p2.astype(BF16), nw_ref[:, half:].astype(BF16), preferred_element_type=F32))
    imp = jnp.concatenate(imps, axis=0)
    rows = G_B * sd
    bs = jnp.dot(imp, mm_ref[...], precision=HIGHEST, preferred_element_type=F32)
    n_lane = bs.shape[1]
    blk = lax.broadcasted_iota(jnp.int32, (rows, n_lane), 1)
    cur = (past + lax.broadcasted_iota(jnp.int32, (rows, 1), 0) % sd) >> 6
    forced = (blk == 0) | (blk == cur) | (blk == cur - 1)
    score = jnp.where(blk <= cur, jnp.where(forced, jnp.inf, bs), -jnp.inf)
    blk_f = blk.astype(F32)
    taken = blk >= n_sb
    lane = lax.broadcasted_iota(jnp.int32, (rows, 128), 1)
    picked = jnp.zeros((rows, 128), F32)
    for it in range(min(N_SEL, n_sb)):
        live = jnp.where(taken, -jnp.inf, score)
        mx = jnp.max(live, axis=-1, keepdims=True)
        ix = jnp.min(jnp.where(~taken & (score == mx), blk_f, float(n_lane)), axis=-1, keepdims=True)
        picked = jnp.where(lane == it, ix, picked)
        taken = taken | (blk_f == ix)
    idx_ref[...] = picked.astype(jnp.int32)


def _nsa_select_kernel(past, sd, n_sb, n_pages, k_sel, *refs):
    pt_ref, ix_ref, lyr_ref = refs[:3]
    pages = refs[3:3 + k_sel]
    new_ref, q_ref, oc_ref, ow_ref, gt_ref, o_ref = refs[3 + k_sel:]
    rq = HG_B * sd
    page = new_ref.shape[-1]
    blk_pg = page // L_SLC
    b, g, i = pl.program_id(0), pl.program_id(1), pl.program_id(2)
    base = ((b * G_B + g) * sd + i) * k_sel
    lane = lax.broadcasted_iota(jnp.int32, (1, k_sel * page), 1)
    in_page = lane % page
    kpos = in_page
    picked = lane < 0
    kts, vts = [], []
    for j in range(k_sel):
        blk = ix_ref[base + j]
        is_new = blk == n_sb - 1
        kts.append(jnp.where(is_new, new_ref[0], pages[j][0]))
        vts.append(jnp.where(is_new, new_ref[1], pages[j][1]))
        mine = lane // page == j
        kpos = kpos + jnp.where(mine, (blk // blk_pg) * page, 0)
        picked = picked | (mine & (in_page // L_SLC == blk % blk_pg))
    kt_all = jnp.concatenate(kts, axis=1).astype(BF16)
    vt_all = jnp.concatenate(vts, axis=1).astype(BF16)
    s = jnp.dot(q_ref[...], kt_all, preferred_element_type=F32) * (HD_B ** -0.5)
    row_q = lax.broadcasted_iota(jnp.int32, (rq, 1), 0) % sd
    mask = picked & (kpos <= past + row_q)
    m = jnp.max(jnp.where(mask, s, NEG_BIG), axis=-1, keepdims=True)
    e = jnp.where(mask, jnp.exp(s - m), 0.0)
    d = jnp.sum(e, axis=-1, keepdims=True)
    p = e / jnp.where(d > 0, d, 1.0)
    o_s = lax.dot_general(p.astype(BF16), vt_all, (((1,), (1,)), ((), ())), preferred_element_type=F32)
    gates = _sigmoid(gt_ref[...])
    y = gates[0] * oc_ref[...] + gates[1] * o_s + gates[2] * ow_ref[...]

    @pl.when(i == 0)
    def _():
        o_ref[...] = jnp.zeros_like(o_ref)

    o_ref[...] += jnp.where(row_q == i, y, 0.0)


def _nsa_sample(proj3, cols, kvs, sd, cmp_big, pages_cmp, pages_slc, win_view, layer, page_table):
    col_qb, col_kv, col_gb = cols
    bsz = proj3.shape[0]
    n_phys, depth, _, _, page = pages_cmp.shape
    n_pages = page_table.shape[1]
    past = n_pages * page
    win = win_view.shape[-1]
    half = KV_B // 2
    rq = HG_B * sd
    n_ch = past // D_CMP
    n_cmp = (past + sd) // D_CMP - L_CMP // D_CMP + 1
    n_sb = -(-(past + sd) // L_SLC)
    k_sel = min(N_SEL, n_sb)
    assert (past + sd) // D_CMP == n_ch and past % L_SLC == 0 and sd <= L_SLC and page % L_SLC == 0
    _, b1big, w2big, b2big, w1tok = cmp_big
    pp = _cmp_pages(pages_cmp, layer, page_table, w1tok)
    q = proj3[:, :sd, col_qb:col_qb + MIX_B].reshape(bsz, sd, G_B, HG_B, HD_B).transpose(0, 2, 3, 1, 4)
    q = jnp.stack([jnp.pad(q[:, g], ((0, 0), (0, 0), (0, 0), (g * HD_B, half - (g + 1) * HD_B))) for g in range(G_B)], 1)
    q = q.reshape(bsz, G_B * rq, half).astype(BF16)
    gt = proj3[:, :sd, col_gb:col_gb + 3 * H_B].reshape(bsz, sd, G_B, HG_B, 3).transpose(0, 4, 2, 3, 1)
    gt = jnp.broadcast_to(gt.reshape(bsz, 3, G_B * rq, 1), (bsz, 3, G_B * rq, half))
    n_lane = -(-n_sb // 128) * 128
    mm = jnp.asarray(np.pad(_cmp_to_sel_map(n_ch, n_cmp, n_sb), ((0, n_lane - n_sb), (0, 0))).T)
    per_b = lambda shape: pl.BlockSpec((None,) + shape, lambda i: (i,) + (0,) * len(shape))
    const = lambda shape: pl.BlockSpec(shape, lambda i: (0,) * len(shape))
    o_c, o_w, idx = pl.pallas_call(
        functools.partial(_nsa_score_kernel, past, sd, n_cmp, n_sb),
        grid=(bsz,),
        in_specs=[per_b((n_ch, 2 * KV_B)), per_b((G_B * rq, half)),
                  pl.BlockSpec((None, None, 2, half, win), lambda i: (layer, i, 0, 0, 0)),
                  pl.BlockSpec((None, PAD_ROWS, KV_B), lambda i: (i, 0, col_kv // KV_B + 2)),
                  const((1, KV_B)), const((KV_B, KV_B)), const((1, KV_B)), const((n_ch, n_lane))],
        out_specs=[per_b((G_B * rq, half)), per_b((G_B * rq, half)), per_b((G_B * sd, 128))],
        out_shape=[jax.ShapeDtypeStruct((bsz, G_B * rq, half), F32), jax.ShapeDtypeStruct((bsz, G_B * rq, half), F32),
                   jax.ShapeDtypeStruct((bsz, G_B * sd, 128), jnp.int32)],
        compiler_params=_cparams(("parallel",), 40),
        name="nsa_score_sample",
    )(pp, q, win_view, proj3, b1big, w2big, b2big, mm)
    blk_pg = page // L_SLC
    n_cached = past // L_SLC
    new_blk = jnp.pad(kvs.reshape(bsz, sd, 2, half).transpose(0, 2, 3, 1), ((0, 0), (0, 0), (0, 0), (0, page - sd)))

    def blk_spec(j):
        def index(b, g, i, pt, ix, lyr):
            blk = jnp.minimum(ix[((b * G_B + g) * sd + i) * k_sel + j], n_cached - 1)
            return pt[b * n_pages + blk // blk_pg], lyr[0], 0, 0, 0
        return pl.BlockSpec((None, None, 2, half, page), index)

    grp = lambda b, g, i, pt, ix, lyr: (b, g, 0)
    grid_spec = pltpu.PrefetchScalarGridSpec(
        num_scalar_prefetch=3,
        grid=(bsz, G_B, sd),
        in_specs=[blk_spec(j) for j in range(k_sel)] + [
            pl.BlockSpec((None, 2, half, page), lambda b, g, i, pt, ix, lyr: (b, 0, 0, 0)),
            pl.BlockSpec((None, rq, half), grp), pl.BlockSpec((None, rq, half), grp), pl.BlockSpec((None, rq, half), grp),
            pl.BlockSpec((None, 3, rq, half), lambda b, g, i, pt, ix, lyr: (b, 0, g, 0))],
        out_specs=pl.BlockSpec((None, rq, half), grp))
    y = pl.pallas_call(
        functools.partial(_nsa_select_kernel, past, sd, n_sb, n_pages, k_sel),
        grid_spec=grid_spec,
        out_shape=jax.ShapeDtypeStruct((bsz, G_B * rq, half), F32),
        compiler_params=_cparams(("parallel", "parallel", "arbitrary"), 40),
        name="nsa_select_sample",
    )(page_table.reshape(-1), idx[:, :, :k_sel].reshape(-1), jnp.full((1,), layer, jnp.int32),
      *([pages_slc] * k_sel), new_blk, q, o_c, o_w, gt)
    y = y.reshape(bsz, G_B, HG_B, sd, G_B, HD_B)
    y = jnp.stack([y[:, g, :, :, g] for g in range(G_B)], axis=1)
    return y.transpose(0, 3, 1, 2, 4).reshape(bsz, sd, MIX_B)


def _prep_w_in(w, d):
    o = np.cumsum([0, MIX_A, MIX_A, MIX_A, MIX_A, MIX_B, KV_B, KV_B, KV_B, 3 * H_B, 2 * C_CONV, 3 * d])
    parts = [w[:, o[0]:o[4]], w[:, o[10]:o[11]], w[:, o[4]:o[5]], w[:, o[9]:o[10]], w[:, o[5]:o[8]], w[:, o[8]:o[9]]]
    n = sum(p.shape[1] for p in parts)
    n_pad = -(-n // 512) * 512
    parts.append(jnp.zeros((w.shape[0], n_pad - n), w.dtype))
    return jnp.concatenate(parts, axis=1).astype(BF16)


def _layer(x3, n_valid, p, s0, conv_prefix, mem_kv, nsa_fn, alpha):
    bsz, t, d = x3.shape
    m = bsz * t
    col_qb = COL_MG + 3 * d
    col_glu = col_qb + MIX_B
    col_kv = col_glu + 2 * C_CONV
    col_gb = col_kv + 3 * KV_B
    proj = _matmul(x3.reshape(m, d), p['w_in'], 1024, 1536)
    proj3 = proj.reshape(bsz, t, -1)
    ya, s_new = _hgrn(proj3, p['lb'], p['hg_norm'], s0, n_valid)
    yc, conv_state = _conv(proj3, col_glu // C_CONV, conv_prefix, p['conv_w'], p['conv_b'],
                           p['conv_ln_g'], p['conv_ln_b'], n_valid)
    kv_shape = (bsz, n_valid, 2, G_B, HD_B)
    kvc = proj3[:, :n_valid, col_kv:col_kv + KV_B]
    kvs = proj3[:, :n_valid, col_kv + KV_B:col_kv + 2 * KV_B]
    kvw = proj3[:, :n_valid, col_kv + 2 * KV_B:col_kv + 3 * KV_B]
    yb, nsa_extra = nsa_fn(proj3, (col_qb, col_kv, col_gb), kvc, kvs, kvw)
    kvc, kvs = kvc.reshape(kv_shape), kvs.reshape(kv_shape)
    x1 = _merge(ya.reshape(m, MIX_A), yb.reshape(m, MIX_B), yc.reshape(m, C_CONV), proj, x3.reshape(m, d),
                p['w_pa'], p['w_pb'], p['w_pc'], p['w_out'], p['ln_g'][0:1], p['ln_b'][0:1], alpha)
    x2 = _xattn(x1.reshape(bsz, t, d), mem_kv[0], mem_kv[1], p['w_xq'], p['w_xo'], p['ln_g'][1:2], p['ln_b'][1:2], alpha)
    x3n = _mlp(x2.reshape(m, d), p['w_up'], p['w_down'], p['ln_g'][2:3], p['ln_b'][2:3], alpha)
    return x3n.reshape(bsz, t, d), kvc, kvs, nsa_extra, s_new, conv_state


def kernel(x_prompt, x_sample, cache_cmp, cache_slc, cache_win, state_hgrn, state_conv, cache_mem, page_table, mem_prompt, w_in, lb_raw, hg_norm, w_cmp1, b_cmp1, w_cmp2, b_cmp2, conv_w, conv_b, conv_ln_g, conv_ln_b, w_pa, w_pb, w_pc, w_out, ln_g, ln_b, w_xq, w_xkv, w_xo, w_up, w_down):
    bp, t, d = x_prompt.shape
    bd, sd = x_sample.shape[:2]
    depth = w_in.shape[0]
    n_mem = mem_prompt.shape[1]
    win_buf = cache_win.shape[2]
    alpha = (2 * depth) ** 0.25
    lb_cum = jnp.cumsum(jax.nn.softmax(lb_raw.astype(F32), axis=0), axis=0)
    lb_all = lb_cum - lb_cum[0]
    pages_cmp, pages_slc, win_view = _pages_view(cache_cmp), _pages_view(cache_slc), _pages_view(cache_win)
    mem_cache = _mem_tile_order(cache_mem, d)
    xp = x_prompt
    xs = jnp.pad(x_sample, ((0, 0), (0, PAD_ROWS - sd), (0, 0)))
    outs = {k: [] for k in ('cmp_p', 'cmp_s', 'slc_p', 'slc_s', 'win_p', 'win_s', 'hg_p', 'hg_s', 'cv_p', 'cv_s', 'mem_p')}
    for l in range(depth):
        p = {'w_in': _prep_w_in(w_in[l], d), 'lb': lb_all[l], 'hg_norm': hg_norm[l], 'conv_w': conv_w[l],
             'conv_b': conv_b[l].reshape(1, -1), 'conv_ln_g': conv_ln_g[l].reshape(1, -1),
             'conv_ln_b': conv_ln_b[l].reshape(1, -1),
             'w_pa': w_pa[l].astype(BF16), 'w_pb': w_pb[l].astype(BF16), 'w_pc': w_pc[l].astype(BF16),
             'w_out': w_out[l].astype(BF16), 'ln_g': ln_g[l], 'ln_b': ln_b[l],
             'w_xq': w_xq[l].astype(BF16), 'w_xo': w_xo[l].astype(BF16),
             'w_up': w_up[l].astype(BF16), 'w_down': w_down[l].astype(BF16)}
        cmp_w = (w_cmp1[l], b_cmp1[l], w_cmp2[l], b_cmp2[l])

        cmp_big = _cmp_weights(*cmp_w)

        def nsa_prompt(proj3, cols, kvc, kvs, kvw):
            col_qb, col_kv, col_gb = cols
            prep = _nsa_prep(proj3, col_kv, cmp_big, min(win_buf, t))
            o = _nsa_prompt(proj3, col_qb, col_gb, prep[:6])
            ct, st_, wt = prep[6:]
            return o, (ct, st_, jnp.pad(wt, ((0, 0), (0, 0), (max(win_buf - t, 0), 0))))

        def nsa_sample(proj3, cols, kvc, kvs, kvw):
            o = _nsa_sample(proj3, cols, kvs, sd, cmp_big, pages_cmp, pages_slc, win_view, l, page_table)
            o = jnp.pad(o, ((0, 0), (0, PAD_ROWS - sd), (0, 0)))
            win = jnp.concatenate([cache_win[l], kvw.reshape(bd, sd, 2, G_B, HD_B)], axis=1)[:, -win_buf:]
            return o, win

        w_kv = _mem_tile_order(w_xkv[l].reshape(d, 2, NX_H, d // NX_H), d).astype(BF16)
        mem_kv = _matmul(mem_prompt.reshape(bp * n_mem, d), w_kv, 1024, 512).reshape(1, bp, n_mem, 2 * d)
        xp, kc, ks_, wn, sh, cv = _layer(xp, t, p, jnp.zeros((bp, H_A, DK_A, DV_A), F32),
                                         jnp.zeros((bp, CONV_K - 1, C_CONV), F32), (mem_kv, 0), nsa_prompt, alpha)
        outs['cmp_p'].append(wn[0]); outs['slc_p'].append(wn[1]); outs['win_p'].append(wn[2])
        outs['hg_p'].append(sh); outs['cv_p'].append(cv)
        outs['mem_p'].append(_mem_head_order(mem_kv[0], d))
        xs, kc, ks_, wn, sh, cv = _layer(xs, sd, p, state_hgrn[l], state_conv[l], (mem_cache, l), nsa_sample, alpha)
        outs['cmp_s'].append(kc); outs['slc_s'].append(ks_); outs['win_s'].append(wn)
        outs['hg_s'].append(sh); outs['cv_s'].append(cv)
    st = lambda k, ax: jnp.stack(outs[k], axis=ax)

    def rows(k, ax):
        a = st(k, ax)
        a = a.reshape(a.shape[:2] + (2, G_B, HD_B, a.shape[-1]))
        return jnp.transpose(a, (0, 1, 5, 2, 3, 4))

    return (xp, xs[:, :sd],
            rows('cmp_p', 1), st('cmp_s', 1), rows('slc_p', 1), st('slc_s', 1),
            rows('win_p', 0), st('win_s', 0), st('hg_p', 0), st('hg_s', 0),
            st('cv_p', 0), st('cv_s', 0), st('mem_p', 0))
```

```python
import functools

import numpy as np
import jax
import jax.numpy as jnp
from jax import lax
from jax.experimental import pallas as pl
from jax.experimental.pallas import tpu as pltpu

F32 = jnp.float32
BF16 = jnp.bfloat16
HIGHEST = lax.Precision.HIGHEST

H_A, DK_A, DV_A, CHUNK_A = 4, 128, 128, 16
H_B, G_B, HG_B, HD_B = 8, 2, 4, 64
L_CMP, D_CMP, L_SLC, N_SEL, WINDOW = 32, 16, 64, 16, 512
C_CONV, CONV_K = 512, 31
NX_H = 4
LN_EPS, RMS_EPS = 1e-5, 1e-6
PAD_ROWS = 16
LANES = 128

MIX_A = H_A * DK_A
MIX_B = H_B * HD_B
KV_B = 2 * G_B * HD_B
COL_QA, COL_FA, COL_IA, COL_GA = 0, 512, 1024, 1536
COL_MG = 2048
V7X_VMEM_LIMIT = 56 * 2**20


def _cparams(sem, vmem_mb=None):
    return pltpu.CompilerParams(dimension_semantics=sem,
                                vmem_limit_bytes=None if vmem_mb is None else vmem_mb * 2**20)


def _ln(y, g, b):
    mu = jnp.mean(y, axis=-1, keepdims=True)
    d = y - mu
    var = jnp.mean(d * d, axis=-1, keepdims=True)
    return d * lax.rsqrt(var + LN_EPS) * g + b


def _sigmoid(x):
    return 1.0 / (1.0 + jnp.exp(-x))


def _mm_kernel(x_ref, w_ref, o_ref, xb_ref):
    @pl.when(pl.program_id(1) == 0)
    def _():
        xb_ref[...] = x_ref[...].astype(BF16)

    o_ref[...] = jnp.dot(xb_ref[...], w_ref[...], preferred_element_type=F32).astype(o_ref.dtype)


def _matmul(x, w, tm, tn, out_dtype=F32):
    m, k = x.shape
    n = w.shape[1]
    tm, tn = min(tm, m), min(tn, n)
    return pl.pallas_call(
        _mm_kernel,
        grid=(m // tm, n // tn),
        in_specs=[pl.BlockSpec((tm, k), lambda i, j: (i, 0)),
                  pl.BlockSpec((k, tn), lambda i, j: (0, j))],
        out_specs=pl.BlockSpec((tm, tn), lambda i, j: (i, j)),
        out_shape=jax.ShapeDtypeStruct((m, n), out_dtype),
        scratch_shapes=[pltpu.VMEM((tm, k), BF16)],
        compiler_params=_cparams(("parallel", "arbitrary"), 40),
        name="proj_matmul",
    )(x, w)


def _mlp_kernel(alpha, x_ref, wu_ref, wd_ref, g_ref, b_ref, o_ref, xb_ref, acc_ref):
    j = pl.program_id(1)

    @pl.when(j == 0)
    def _():
        xb_ref[...] = x_ref[...].astype(BF16)
        acc_ref[...] = jnp.zeros_like(acc_ref)

    h = jnp.dot(xb_ref[...], wu_ref[...], preferred_element_type=F32)
    h = jnp.square(jnp.maximum(h, 0.0)).astype(BF16)
    acc_ref[...] += jnp.dot(h, wd_ref[...], preferred_element_type=F32)

    @pl.when(j == pl.num_programs(1) - 1)
    def _():
        o_ref[...] = _ln(alpha * x_ref[...] + acc_ref[...], g_ref[...], b_ref[...])


def _mlp(x, w_up, w_down, g, b, alpha, tm=512, tf=1024):
    m, d = x.shape
    ff = w_up.shape[1]
    tm = min(tm, m)
    return pl.pallas_call(
        functools.partial(_mlp_kernel, alpha),
        grid=(m // tm, ff // tf),
        in_specs=[pl.BlockSpec((tm, d), lambda i, j: (i, 0)),
                  pl.BlockSpec((d, tf), lambda i, j: (0, j)),
                  pl.BlockSpec((tf, d), lambda i, j: (j, 0)),
                  pl.BlockSpec((1, d), lambda i, j: (0, 0)),
                  pl.BlockSpec((1, d), lambda i, j: (0, 0))],
        out_specs=pl.BlockSpec((tm, d), lambda i, j: (i, 0)),
        out_shape=jax.ShapeDtypeStruct((m, d), F32),
        scratch_shapes=[pltpu.VMEM((tm, d), BF16), pltpu.VMEM((tm, d), F32)],
        compiler_params=_cparams(("parallel", "arbitrary"), 48),
        name="mlp",
    )(x, w_up, w_down, g, b)


def _merge_kernel(alpha, ya_ref, yb_ref, yc_ref, ma_ref, mb_ref, mc_ref, x_ref,
                  wpa_ref, wpb_ref, wpc_ref, wout_ref, g_ref, b_ref, o_ref):
    def branch(y_ref, m_ref, w_ref):
        return _sigmoid(m_ref[...].astype(F32)) * jnp.dot(y_ref[...].astype(BF16), w_ref[...], preferred_element_type=F32)

    merged = branch(ya_ref, ma_ref, wpa_ref) + branch(yb_ref, mb_ref, wpb_ref) + branch(yc_ref, mc_ref, wpc_ref)
    y = jnp.dot(merged.astype(BF16), wout_ref[...], preferred_element_type=F32)
    o_ref[...] = _ln(alpha * x_ref[...] + y, g_ref[...], b_ref[...])


def _merge(ya, yb, yc, proj, x, wpa, wpb, wpc, wout, g, b, alpha, tm=512):
    m, d = x.shape
    tm = min(tm, m)
    mg0 = COL_MG // d
    row = lambda i: (i, 0)
    const = lambda i: (0, 0)
    return pl.pallas_call(
        functools.partial(_merge_kernel, alpha),
        grid=(m // tm,),
        in_specs=[pl.BlockSpec((tm, MIX_A), row), pl.BlockSpec((tm, MIX_B), row), pl.BlockSpec((tm, C_CONV), row),
                  pl.BlockSpec((tm, d), lambda i: (i, mg0)), pl.BlockSpec((tm, d), lambda i: (i, mg0 + 1)),
                  pl.BlockSpec((tm, d), lambda i: (i, mg0 + 2)),
                  pl.BlockSpec((tm, d), row),
                  pl.BlockSpec((MIX_A, d), const), pl.BlockSpec((MIX_B, d), const), pl.BlockSpec((C_CONV, d), const),
                  pl.BlockSpec((d, d), const), pl.BlockSpec((1, d), const), pl.BlockSpec((1, d), const)],
        out_specs=pl.BlockSpec((tm, d), row),
        out_shape=jax.ShapeDtypeStruct((m, d), F32),
        compiler_params=_cparams(("parallel",), 48),
        name="merge_out",
    )(ya, yb, yc, proj, proj, proj, x, wpa, wpb, wpc, wout, g, b)


def _xattn_kernel(alpha, x_ref, kv_ref, wq_ref, wo_ref, g_ref, b_ref, o_ref):
    x = x_ref[...]
    d = x.shape[-1]
    hd = d // NX_H
    q = jnp.dot(x.astype(BF16), wq_ref[...], preferred_element_type=F32)
    n_dt = hd // LANES

    def head(base, h):
        parts = [kv_ref[:, base + (dt * NX_H + h) * LANES:base + (dt * NX_H + h + 1) * LANES] for dt in range(n_dt)]
        return jnp.concatenate(parts, axis=1).astype(BF16)

    outs = []
    for h in range(NX_H):
        qh = q[:, h * hd:(h + 1) * hd].astype(BF16)
        kh, vh = head(0, h), head(d, h)
        s = lax.dot_general(qh, kh, (((1,), (1,)), ((), ())), preferred_element_type=F32) * (hd ** -0.5)
        e = jnp.exp(s - jnp.max(s, axis=-1, keepdims=True))
        p = e / jnp.sum(e, axis=-1, keepdims=True)
        outs.append(jnp.dot(p.astype(BF16), vh, preferred_element_type=F32))
    o = jnp.concatenate(outs, axis=-1)
    y = jnp.dot(o.astype(BF16), wo_ref[...], preferred_element_type=F32)
    o_ref[...] = _ln(alpha * x + y, g_ref[...], b_ref[...])


def _mem_tile_order(a, d):
    lead = a.shape[:-3]
    a = a.reshape(lead + (2, NX_H, d // NX_H // LANES, LANES))
    return jnp.swapaxes(a, -3, -2).reshape(lead + (2 * d,))


def _mem_head_order(a, d):
    lead = a.shape[:-1]
    a = a.reshape(lead + (2, d // NX_H // LANES, NX_H, LANES))
    return jnp.swapaxes(a, -3, -2).reshape(lead + (2, NX_H, d // NX_H))


def _xattn(x, kv, layer, wq, wo, g, b, alpha, tm=512):
    bsz, t, d = x.shape
    n_mem = kv.shape[2]
    tm = min(tm, t)
    const = lambda i, j: (0, 0)
    return pl.pallas_call(
        functools.partial(_xattn_kernel, alpha),
        grid=(bsz, t // tm),
        in_specs=[pl.BlockSpec((None, tm, d), lambda i, j: (i, j, 0)),
                  pl.BlockSpec((None, None, n_mem, 2 * d), lambda i, j: (layer, i, 0, 0)),
                  pl.BlockSpec((d, d), const), pl.BlockSpec((d, d), const),
                  pl.BlockSpec((1, d), const), pl.BlockSpec((1, d), const)],
        out_specs=pl.BlockSpec((None, tm, d), lambda i, j: (i, j, 0)),
        out_shape=jax.ShapeDtypeStruct((bsz, t, d), F32),
        compiler_params=_cparams(("parallel", "parallel"), 48),
        name="xattn",
    )(x, kv, wq, wo, g, b)


CONV_HALO = 32
CONV_SUB = 32


def _conv_kernel(rt, n_valid_last, a_ref, gt_ref, pre_ref, w_ref, cb_ref, g_ref, b_ref, y_ref, st_ref, ue_ref, sh_ref):
    t = pl.program_id(1)
    off = CONV_HALO - (CONV_K - 1)
    sl = 8

    @pl.when(t == 0)
    def _():
        ue_ref[0:off, :] = jnp.zeros((off, C_CONV), F32)
        ue_ref[off:CONV_HALO, :] = pre_ref[...]

    ue_ref[CONV_HALO:CONV_HALO + rt, :] = a_ref[...].astype(F32) * _sigmoid(gt_ref[...].astype(F32))
    n_sh = rt + CONV_HALO - sl
    for s in range(1, sl):
        sh_ref[s - 1, 0:n_sh, :] = ue_ref[s:s + n_sh, :]
    sub = min(CONV_SUB, rt)
    for r0 in range(0, rt, sub):
        acc = cb_ref[...]
        for j in range(CONV_K):
            a, s = divmod(off + j, sl)
            lo = r0 + a * sl
            win = ue_ref[lo:lo + sub, :] if s == 0 else sh_ref[s - 1, lo:lo + sub, :]
            acc = acc + w_ref[j:j + 1, :] * win
        y = _ln(acc, g_ref[...], b_ref[...])
        y_ref[r0:r0 + sub, :] = y * _sigmoid(y)

    @pl.when(t == pl.num_programs(1) - 1)
    def _():
        st_ref[...] = ue_ref[off + n_valid_last:off + n_valid_last + CONV_K - 1, :]

    ue_ref[0:CONV_HALO, :] = ue_ref[rt:rt + CONV_HALO, :]


def _conv(proj3, col_a, prefix, w, cb, g, b, n_valid, rt=256):
    bsz, t, _ = proj3.shape
    rt = min(rt, t)
    n_valid_last = n_valid - (t - rt)
    vec = lambda i, j: (0, 0)
    return pl.pallas_call(
        functools.partial(_conv_kernel, rt, n_valid_last),
        grid=(bsz, t // rt),
        in_specs=[pl.BlockSpec((None, rt, C_CONV), lambda i, j: (i, j, col_a)),
                  pl.BlockSpec((None, rt, C_CONV), lambda i, j: (i, j, col_a + 1)),
                  pl.BlockSpec((None, CONV_K - 1, C_CONV), lambda i, j: (i, 0, 0)),
                  pl.BlockSpec((CONV_K, C_CONV), vec), pl.BlockSpec((1, C_CONV), vec),
                  pl.BlockSpec((1, C_CONV), vec), pl.BlockSpec((1, C_CONV), vec)],
        out_specs=[pl.BlockSpec((None, rt, C_CONV), lambda i, j: (i, j, 0)),
                   pl.BlockSpec((None, CONV_K - 1, C_CONV), lambda i, j: (i, 0, 0))],
        out_shape=[jax.ShapeDtypeStruct((bsz, t, C_CONV), F32),
                   jax.ShapeDtypeStruct((bsz, CONV_K - 1, C_CONV), F32)],
        scratch_shapes=[pltpu.VMEM((rt + CONV_HALO, C_CONV), F32), pltpu.VMEM((7, rt + CONV_HALO - 8, C_CONV), F32)],
        compiler_params=_cparams(("parallel", "arbitrary"), 32),
        name="conformer_conv",
    )(proj3, proj3, prefix, w, cb, g, b)


def _hgrn_kernel(tt, n_valid, q_ref, f_ref, i_ref, g_ref, la_ref, l1_ref, oml_ref, gn_ref, s0_ref,
                 y_ref, s_ref, st_ref, qs_ref, kk_ref, bb_ref):
    c = CHUNK_A
    t = pl.program_id(1)

    @pl.when(t == 0)
    def _():
        for h in range(H_A):
            st_ref[h] = s0_ref[h].T

    z = f_ref[...].astype(F32)
    cc = l1_ref[...] + (jnp.minimum(z, 0.0) - jnp.log1p(jnp.exp(-jnp.abs(z))))
    a = la_ref[...]
    logf = jnp.maximum(a, cc) + jnp.log1p(jnp.exp(-jnp.abs(a - cc)))
    k = oml_ref[...] * _sigmoid(-z)
    row = lax.broadcasted_iota(jnp.int32, (tt, 1), 0)
    if n_valid < tt:
        logf = jnp.where(row < n_valid, logf, 0.0)
        k = jnp.where(row < n_valid, k, 0.0)
    b = logf
    rc = row & (c - 1)
    sh = 1
    while sh < c:
        b = b + jnp.where(rc >= sh, pltpu.roll(b, sh, 0), 0.0)
        sh *= 2
    q = q_ref[...].astype(F32)
    qs_ref[...] = q * _sigmoid(q)
    kk_ref[...] = k
    bb_ref[...] = b

    rowc = lax.broadcasted_iota(jnp.int32, (c, 1), 0)

    def chunk(ci, carry):
        r0 = pl.multiple_of(ci * c, c)
        for h in range(H_A):
            hs = slice(h * DK_A, (h + 1) * DK_A)
            qc = qs_ref[pl.ds(r0, c), hs]
            kc = kk_ref[pl.ds(r0, c), hs]
            bc = bb_ref[pl.ds(r0, c), hs]
            vc = i_ref[pl.ds(r0, c), hs].astype(F32)
            bl = bc[c - 1:c, :]
            st = st_ref[h]
            qe = (qc * jnp.exp(bc)).astype(BF16)
            o = lax.dot_general(qe, st.astype(BF16), (((1,), (1,)), ((), ())), preferred_element_type=F32)
            for s in range(c):
                e = jnp.exp(jnp.where(rowc >= s, bc - bc[s:s + 1, :], -jnp.inf))
                att = jnp.sum(qc * e * kc[s:s + 1, :], axis=-1, keepdims=True)
                o = o + att * vc[s:s + 1, :]
            y_ref[pl.ds(r0, c), hs] = o
            kd = (kc * jnp.exp(bl - bc)).astype(BF16)
            u = lax.dot_general(vc.astype(BF16), kd, (((0,), (0,)), ((), ())), preferred_element_type=F32)
            st_ref[h] = st * jnp.exp(bl) + u
        return carry

    lax.fori_loop(0, tt // c, chunk, 0)

    g = g_ref[...].astype(F32)
    gate = g * _sigmoid(g)
    for h in range(H_A):
        hs = slice(h * DV_A, (h + 1) * DV_A)
        o = y_ref[:, hs]
        o = o * lax.rsqrt(jnp.mean(o * o, axis=-1, keepdims=True) + RMS_EPS) * gn_ref[...]
        y_ref[:, hs] = o * gate[:, hs]

    @pl.when(t == pl.num_programs(1) - 1)
    def _():
        for h in range(H_A):
            s_ref[h] = st_ref[h].T


def _hgrn(proj3, lb, gnorm, s0, n_valid, tt=256):
    bsz, t, _ = proj3.shape
    tt = min(tt, t)
    lb = lb.reshape(1, MIX_A).astype(F32)
    la, l1, oml = jnp.log(lb), jnp.log1p(-lb), 1.0 - lb
    vec = lambda i, j: (0, 0)
    col = lambda cb: pl.BlockSpec((None, tt, MIX_A), lambda i, j: (i, j, cb))
    st = pl.BlockSpec((None, H_A, DK_A, DV_A), lambda i, j: (i, 0, 0, 0))
    return pl.pallas_call(
        functools.partial(_hgrn_kernel, tt, n_valid if t == tt else tt),
        grid=(bsz, t // tt),
        in_specs=[col(0), col(1), col(2), col(3),
                  pl.BlockSpec((1, MIX_A), vec), pl.BlockSpec((1, MIX_A), vec), pl.BlockSpec((1, MIX_A), vec),
                  pl.BlockSpec((1, DV_A), vec), st],
        out_specs=[pl.BlockSpec((None, tt, MIX_A), lambda i, j: (i, j, 0)), st],
        out_shape=[jax.ShapeDtypeStruct((bsz, t, MIX_A), F32),
                   jax.ShapeDtypeStruct((bsz, H_A, DK_A, DV_A), F32)],
        scratch_shapes=[pltpu.VMEM((H_A, DV_A, DK_A), F32), pltpu.VMEM((tt, MIX_A), F32),
                        pltpu.VMEM((tt, MIX_A), F32), pltpu.VMEM((tt, MIX_A), F32)],
        compiler_params=_cparams(("parallel", "arbitrary"), 32),
        name="hgrn2",
    )(proj3, proj3, proj3, proj3, la, l1, oml, gnorm.reshape(1, DV_A), s0)


NEG_BIG = -1e30
SEL_PER_CMP = L_SLC // D_CMP


def _cmp_weights(w1, b1, w2, b2):
    m = L_CMP // D_CMP
    eye_e, eye_g = jnp.eye(2, dtype=F32), jnp.eye(G_B, dtype=F32)
    w1r = w1.reshape(2, m, D_CMP, HD_B, HD_B)
    w1big = jnp.einsum('ehjdf,ea,gb->jegdhabf', w1r, eye_e, eye_g).reshape(D_CMP * KV_B, m * KV_B)
    w2big = jnp.einsum('efo,ea,gb->egfabo', w2, eye_e, eye_g).reshape(KV_B, KV_B)
    b1big = jnp.broadcast_to(b1[:, None, :], (2, G_B, HD_B)).reshape(1, KV_B)
    b2big = jnp.broadcast_to(b2[:, None, :], (2, G_B, HD_B)).reshape(1, KV_B)
    w1tok = jnp.einsum('ehjdf,gb->ejgdhbf', w1r, eye_g).reshape(2, D_CMP, KV_B // 2, m * KV_B // 2)
    return w1big.astype(BF16), b1big, w2big.astype(BF16), b2big, w1tok.astype(BF16)


def _cmp_to_sel_map(n_cmp_pad, n_cmp, n_sb):
    mm = np.zeros((n_sb, n_cmp_pad), np.float32)
    for n in range(n_cmp):
        for i in (n, n + 1):
            if i // SEL_PER_CMP < n_sb:
                mm[i // SEL_PER_CMP, n] += 1.0
    return mm


def _compress_tail(pp, n_cmp, b1_ref, w2_ref, b2_ref):
    n_ch = pp.shape[0]
    hid = pp[:, :KV_B] + pltpu.roll(pp[:, KV_B:], n_ch - 1, 0) + b1_ref[...]
    hid = hid * _sigmoid(hid)
    kc = jnp.dot(hid.astype(BF16), w2_ref[...], preferred_element_type=F32) + b2_ref[...]
    row = lax.broadcasted_iota(jnp.int32, (n_ch, 1), 0)
    return jnp.where(row < n_cmp, kc, 0.0)


def _nsa_prep_kernel(n_cmp, win, kc_ref, ks_ref, kw_ref, w1_ref, b1_ref, w2_ref, b2_ref,
                     kck_ref, kcvt_ref, ksk_ref, ksvt_ref, kwk_ref, kwvt_ref, ct_ref, st_ref, wt_ref, xk_ref):
    half = KV_B // 2
    t = kc_ref.shape[0]
    n_ch = t // D_CMP
    for e in range(2):
        xk_ref[e] = kc_ref[:, e * half:(e + 1) * half].astype(F32)
    pp = jnp.zeros((n_ch, w1_ref.shape[2]), F32)
    for j in range(D_CMP):
        for e in range(2):
            x = xk_ref[e, pl.ds(j, n_ch, stride=D_CMP), :].astype(BF16)
            pp = pp + jnp.dot(x, w1_ref[j, e * half:(e + 1) * half, :], preferred_element_type=F32)
    kc = _compress_tail(pp, n_cmp, b1_ref, w2_ref, b2_ref)
    kck_ref[...] = kc[:, :half].astype(BF16)
    kcvt_ref[...] = kc[:, half:].T.astype(BF16)
    ct_ref[...] = kc_ref[...].astype(F32).T
    ks_t = ks_ref[...].astype(F32).T
    st_ref[...] = ks_t
    ksk_ref[...] = ks_ref[:, :half].astype(BF16)
    ksvt_ref[...] = ks_t[half:, :].astype(BF16)
    kw_t = kw_ref[...].astype(F32).T
    wt_ref[...] = kw_t[:, t - win:]
    kwk_ref[...] = kw_ref[:, :half].astype(BF16)
    kwvt_ref[...] = kw_t[half:, :].astype(BF16)


def _nsa_prep(proj3, col_kv, cmp_big, win):
    bsz, t, _ = proj3.shape
    n_ch = t // D_CMP
    n_cmp = n_ch - L_CMP // D_CMP + 1
    half = KV_B // 2
    w1big, b1big, w2big, b2big = cmp_big[:4]
    w1pos = w1big.reshape(D_CMP, KV_B, w1big.shape[1])
    cb = col_kv // KV_B
    const = lambda i: (0, 0)
    kspec = lambda n: pl.BlockSpec((None, n, half), lambda i: (i, 0, 0))
    vspec = lambda n: pl.BlockSpec((None, half, n), lambda i: (i, 0, 0))
    fspec = lambda n: pl.BlockSpec((None, KV_B, n), lambda i: (i, 0, 0))
    col = lambda c: pl.BlockSpec((None, t, KV_B), lambda i: (i, 0, cb + c))
    return pl.pallas_call(
        functools.partial(_nsa_prep_kernel, n_cmp, win),
        grid=(bsz,),
        in_specs=[col(0), col(1), col(2),
                  pl.BlockSpec(w1pos.shape, lambda i: (0, 0, 0)), pl.BlockSpec((1, KV_B), const),
                  pl.BlockSpec((KV_B, KV_B), const), pl.BlockSpec((1, KV_B), const)],
        out_specs=[kspec(n_ch), vspec(n_ch), kspec(t), vspec(t), kspec(t), vspec(t), fspec(t), fspec(t), fspec(win)],
        out_shape=[jax.ShapeDtypeStruct((bsz, n_ch, half), BF16), jax.ShapeDtypeStruct((bsz, half, n_ch), BF16),
                   jax.ShapeDtypeStruct((bsz, t, half), BF16), jax.ShapeDtypeStruct((bsz, half, t), BF16),
                   jax.ShapeDtypeStruct((bsz, t, half), BF16), jax.ShapeDtypeStruct((bsz, half, t), BF16),
                   jax.ShapeDtypeStruct((bsz, KV_B, t), F32), jax.ShapeDtypeStruct((bsz, KV_B, t), F32),
                   jax.ShapeDtypeStruct((bsz, KV_B, win), F32)],
        scratch_shapes=[pltpu.VMEM((2, t, half), F32)],
        compiler_params=_cparams(("parallel",), 48),
        name="nsa_prep",
    )(proj3, proj3, proj3, w1pos, b1big, w2big, b2big)


def _nsa_prompt_kernel(tq, tk, n_sb, q_ref, gb_ref, kck_ref, kcvt_ref, ksk_ref, ksvt_ref, kwk_ref, kwvt_ref, mm_ref, o_ref):
    qi = pl.program_id(1)
    q0 = qi * tq
    n_cp = kck_ref.shape[0]
    w4 = HG_B * tq
    q_t = (q_ref[...].astype(F32) * HD_B ** -0.5).T
    g_t = _sigmoid(gb_ref[...].astype(F32)).T
    qpos = q0 + lax.broadcasted_iota(jnp.int32, (1, tq), 1)
    qpos4 = jnp.concatenate([qpos] * HG_B, axis=1)
    zpad = jnp.zeros((HD_B, tq), BF16)

    def update(state, s, pen, vt, g):
        m, l, acc = state
        s = s + jnp.concatenate([pen] * HG_B, axis=1)
        m_new = jnp.maximum(m, jnp.max(s, axis=0, keepdims=True))
        e = jnp.exp(s - m_new)
        a = jnp.exp(m - m_new)
        l = a * l + jnp.sum(e, axis=0, keepdims=True)
        pv = jnp.dot(vt[g * HD_B:(g + 1) * HD_B, :], e.astype(BF16), preferred_element_type=F32)
        return m_new, l, a * acc + pv

    qts, o_cs, sel_ts = [], [], []
    for g in range(G_B):
        cols = []
        for hg in range(HG_B):
            h = g * HG_B + hg
            qh = q_t[h * HD_B:(h + 1) * HD_B, :].astype(BF16)
            cols.append(jnp.concatenate([qh, zpad] if g == 0 else [zpad, qh], axis=0))
        qt = jnp.concatenate(cols, axis=1)
        qts.append(qt)
        s = jnp.dot(kck_ref[...], qt, preferred_element_type=F32)
        c_end = lax.broadcasted_iota(jnp.int32, (n_cp, w4), 0) * D_CMP + (L_CMP - 1)
        mask = c_end <= qpos4
        m = jnp.max(jnp.where(mask, s, NEG_BIG), axis=0, keepdims=True)
        e = jnp.where(mask, jnp.exp(s - m), 0.0)
        d = jnp.sum(e, axis=0, keepdims=True)
        p = e / jnp.where(d > 0, d, 1.0)
        o_cs.append(jnp.dot(kcvt_ref[...], p.astype(BF16), preferred_element_type=F32)[g * HD_B:(g + 1) * HD_B, :])
        imp = p[:, 0:tq]
        for hg in range(1, HG_B):
            imp = imp + p[:, hg * tq:(hg + 1) * tq]
        bs = jnp.dot(mm_ref[...], imp, precision=HIGHEST, preferred_element_type=F32)
        blk = lax.broadcasted_iota(jnp.int32, (n_sb, tq), 0)
        cur = qpos >> 6
        forced = (blk == 0) | (blk == cur) | (blk == cur - 1)
        score = jnp.where(blk <= cur, jnp.where(forced, jnp.inf, bs), -jnp.inf)
        rank = jnp.zeros((n_sb, tq), jnp.int32)
        for i in range(n_sb):
            si = score[i:i + 1, :]
            rank = rank + jnp.where((si > score) | ((si == score) & (blk > i)), 1, 0)
        sel_ts.append(jnp.where((rank < N_SEL) & (blk <= cur), 1.0, 0.0).astype(BF16))

    sel_both = jnp.concatenate(sel_ts, axis=1)
    ej = lax.broadcasted_iota(jnp.int32, (tk, n_sb), 1)
    ek = lax.broadcasted_iota(jnp.int32, (tk, n_sb), 0)
    krow = lax.broadcasted_iota(jnp.int32, (tk, tq), 0)

    def sel_penalties(k0):
        expand = jnp.where(ej == ((k0 + ek) >> 6), 1.0, 0.0).astype(BF16)
        pen = (jnp.dot(expand, sel_both, preferred_element_type=F32) - 1.0) * (-NEG_BIG)
        return [pen[:, g * tq:(g + 1) * tq] for g in range(G_B)]

    def far_tiles(kt, states):
        k0 = pl.multiple_of(kt * tk, tk)
        k_tile, vt_tile = ksk_ref[pl.ds(k0, tk), :], ksvt_ref[:, pl.ds(k0, tk)]
        pens = sel_penalties(k0)
        return tuple(update(states[g], jnp.dot(k_tile, qts[g], preferred_element_type=F32), pens[g], vt_tile, g)
                     for g in range(G_B))

    def near_tiles(kt, states):
        k0 = pl.multiple_of(kt * tk, tk)
        rel = qpos - k0
        causal = krow <= rel
        pen_causal = jnp.where(causal, 0.0, NEG_BIG)
        pen_win = jnp.where(causal & (krow > rel - WINDOW), 0.0, NEG_BIG)
        k_tile, vt_tile = ksk_ref[pl.ds(k0, tk), :], ksvt_ref[:, pl.ds(k0, tk)]
        kw_tile, vwt_tile = kwk_ref[pl.ds(k0, tk), :], kwvt_ref[:, pl.ds(k0, tk)]
        pens = sel_penalties(k0)
        new_sel = tuple(update(states[g], jnp.dot(k_tile, qts[g], preferred_element_type=F32),
                               pens[g] + pen_causal, vt_tile, g) for g in range(G_B))
        new_win = tuple(update(states[G_B + g], jnp.dot(kw_tile, qts[g], preferred_element_type=F32),
                               pen_win, vwt_tile, g) for g in range(G_B))
        return new_sel + new_win

    init = (jnp.full((1, w4), NEG_BIG, F32), jnp.zeros((1, w4), F32), jnp.zeros((HD_B, w4), F32))
    first_near = jnp.maximum((q0 - WINDOW) // tk, 0)
    states = lax.fori_loop(0, first_near, far_tiles, (init,) * G_B)
    states = lax.fori_loop(first_near, (q0 + tq) // tk, near_tiles, states + (init,) * G_B)
    finish = lambda st: st[2] / jnp.where(st[1] > 0, st[1], 1.0)
    outs = []
    for g in range(G_B):
        o_c, o_s, o_w = o_cs[g], finish(states[g]), finish(states[G_B + g])
        for hg in range(HG_B):
            c = (g * HG_B + hg) * 3
            sl = slice(hg * tq, (hg + 1) * tq)
            outs.append(g_t[c:c + 1, :] * o_c[:, sl] + g_t[c + 1:c + 2, :] * o_s[:, sl]
                        + g_t[c + 2:c + 3, :] * o_w[:, sl])
    o_ref[...] = jnp.concatenate(outs, axis=0).T


def _nsa_prompt(proj3, col_qb, col_gb, prep, tq=256, tk=256):
    bsz, t, _ = proj3.shape
    tq, tk = min(tq, t), min(tk, t)
    kck, kcvt, ksk, ksvt, kwk, kwvt = prep
    n_ch = kck.shape[1]
    n_cmp = n_ch - L_CMP // D_CMP + 1
    n_sb = -(-t // L_SLC)
    half = KV_B // 2
    mm = jnp.asarray(_cmp_to_sel_map(n_ch, n_cmp, n_sb))
    per_b = lambda shape: pl.BlockSpec((None,) + shape, lambda i, j: (i, 0, 0))
    return pl.pallas_call(
        functools.partial(_nsa_prompt_kernel, tq, tk, n_sb),
        grid=(bsz, t // tq),
        in_specs=[pl.BlockSpec((None, tq, MIX_B), lambda i, j: (i, j, col_qb // MIX_B)),
                  pl.BlockSpec((None, tq, 128), lambda i, j: (i, j, col_gb // 128)),
                  per_b((n_ch, half)), per_b((half, n_ch)), per_b((t, half)), per_b((half, t)),
                  per_b((t, half)), per_b((half, t)),
                  pl.BlockSpec((n_sb, n_ch), lambda i, j: (0, 0))],
        out_specs=pl.BlockSpec((None, tq, MIX_B), lambda i, j: (i, j, 0)),
        out_shape=jax.ShapeDtypeStruct((bsz, t, MIX_B), F32),
        compiler_params=_cparams(("parallel", "parallel"), 40),
        name="nsa_prompt",
    )(proj3, proj3, kck, kcvt, ksk, ksvt, kwk, kwvt, mm)


def _pages_view(cache):
    n_phys, depth, page = cache.shape[:3]
    return jnp.transpose(cache, (0, 1, 3, 4, 5, 2)).reshape(n_phys, depth, 2, KV_B // 2, page)


def _cmp_pages_kernel(n_pg, page, *refs):
    pages, w1_ref, o_ref, xs_ref = refs[2:2 + n_pg], refs[2 + n_pg], refs[3 + n_pg], refs[4 + n_pg]
    half = KV_B // 2
    ch_pg = page // D_CMP
    n_ch = n_pg * ch_pg
    r = lax.broadcasted_iota(jnp.int32, (page, page), 0)
    pos = lax.broadcasted_iota(jnp.int32, (page, page), 1)
    pick = jnp.where(pos == (r % ch_pg) * D_CMP + r // ch_pg, 1.0, 0.0).astype(BF16)
    for i, pg in enumerate(pages):
        for e in range(2):
            y = lax.dot_general(pick, pg[e].astype(BF16), (((1,), (1,)), ((), ())), preferred_element_type=F32)
            for j in range(D_CMP):
                xs_ref[e, j, i * ch_pg:(i + 1) * ch_pg, :] = y[j * ch_pg:(j + 1) * ch_pg, :]
    for e in range(2):
        acc = jnp.zeros((n_ch, 2 * half), F32)
        for j in range(D_CMP):
            acc = acc + jnp.dot(xs_ref[e, j].astype(BF16), w1_ref[e, j], preferred_element_type=F32)
        o_ref[:, e * half:(e + 1) * half] = acc[:, :half]
        o_ref[:, KV_B + e * half:KV_B + (e + 1) * half] = acc[:, half:]


def _cmp_pages(pages, layer, page_table, w1e):
    n_phys, depth, _, _, page = pages.shape
    bsz, n_pages = page_table.shape
    ch_pg = page // D_CMP
    n_pg = next(c for c in (32, 16, 8, 4, 2, 1) if n_pages % c == 0)

    def page_spec(i):
        return pl.BlockSpec((None, None, 2, KV_B // 2, page),
                            lambda b, s, pt, lyr: (pt[b * n_pages + s * n_pg + i], lyr[0], 0, 0, 0))

    grid_spec = pltpu.PrefetchScalarGridSpec(
        num_scalar_prefetch=2,
        grid=(bsz, n_pages // n_pg),
        in_specs=[page_spec(i) for i in range(n_pg)] + [pl.BlockSpec(w1e.shape, lambda b, s, pt, lyr: (0, 0, 0, 0))],
        out_specs=pl.BlockSpec((None, n_pg * ch_pg, 2 * KV_B), lambda b, s, pt, lyr: (b, s, 0)),
        scratch_shapes=[pltpu.VMEM((2, D_CMP, n_pg * ch_pg, KV_B // 2), F32)])
    return pl.pallas_call(
        functools.partial(_cmp_pages_kernel, n_pg, page),
        grid_spec=grid_spec,
        out_shape=jax.ShapeDtypeStruct((bsz, n_pages * ch_pg, 2 * KV_B), F32),
        compiler_params=_cparams(("parallel", "parallel"), 48),
        name="nsa_cmp_pages",
    )(page_table.reshape(-1), jnp.full((1,), layer, jnp.int32), *([pages] * n_pg), w1e)


def _nsa_score_kernel(past, sd, n_cmp, n_sb, pp_ref, q_ref, cw_ref, nw_ref, b1_ref, w2_ref, b2_ref, mm_ref,
                      oc_ref, ow_ref, idx_ref):
    half = KV_B // 2
    rq = HG_B * sd
    scale = HD_B ** -0.5
    n_ch = pp_ref.shape[0]
    win = cw_ref.shape[2]
    kc = _compress_tail(pp_ref[...], n_cmp, b1_ref, w2_ref, b2_ref)
    kck, kcv = kc[:, :half].astype(BF16), kc[:, half:].astype(BF16)
    qpos = past + lax.broadcasted_iota(jnp.int32, (rq, 1), 0) % sd
    nt = (((1,), (1,)), ((), ()))

    def softmax(parts):
        m = functools.reduce(jnp.maximum, [jnp.max(jnp.where(mk, s, NEG_BIG), axis=-1, keepdims=True) for s, mk in parts])
        es = [jnp.where(mk, jnp.exp(s - m), 0.0) for s, mk in parts]
        d = functools.reduce(jnp.add, [jnp.sum(e, axis=-1, keepdims=True) for e in es])
        return [e / jnp.where(d > 0, d, 1.0) for e in es]

    imps = []
    for g in range(G_B):
        qg = q_ref[g * rq:(g + 1) * rq, :]
        s = lax.dot_general(qg, kck, nt, preferred_element_type=F32) * scale
        c_end = lax.broadcasted_iota(jnp.int32, (rq, n_ch), 1) * D_CMP + (L_CMP - 1)
        p, = softmax([(s, c_end <= qpos)])
        oc_ref[g * rq:(g + 1) * rq, :] = jnp.dot(p.astype(BF16), kcv, preferred_element_type=F32)
        imps.append(functools.reduce(jnp.add, [p[hg * sd:(hg + 1) * sd, :] for hg in range(HG_B)]))
        s1 = jnp.dot(qg, cw_ref[0].astype(BF16), preferred_element_type=F32) * scale
        s2 = lax.dot_general(qg, nw_ref[:, :half].astype(BF16), nt, preferred_element_type=F32) * scale
        wp1 = past - win + lax.broadcasted_iota(jnp.int32, s1.shape, 1)
        j2 = lax.broadcasted_iota(jnp.int32, s2.shape, 1)
        wp2 = past + j2
        p1, p2 = softmax([(s1, (wp1 <= qpos) & (wp1 > qpos - WINDOW) & (wp1 >= 0)),
                          (s2, (wp2 <= qpos) & (wp2 > qpos - WINDOW) & (j2 < sd))])
        ow_ref[g * rq:(g + 1) * rq, :] = (
            lax.dot_general(p1.astype(BF16), cw_ref[1].astype(BF16), nt, preferred_element_type=F32)
            + jnp.dot(p2.astype(BF16), nw_ref[:, half:].astype(BF16), preferred_element_type=F32))
    imp = jnp.concatenate(imps, axis=0)
    rows = G_B * sd
    bs = jnp.dot(imp, mm_ref[...], precision=HIGHEST, preferred_element_type=F32)
    n_lane = bs.shape[1]
    blk = lax.broadcasted_iota(jnp.int32, (rows, n_lane), 1)
    cur = (past + lax.broadcasted_iota(jnp.int32, (rows, 1), 0) % sd) >> 6
    forced = (blk == 0) | (blk == cur) | (blk == cur - 1)
    score = jnp.where(blk <= cur, jnp.where(forced, jnp.inf, bs), -jnp.inf)
    blk_f = blk.astype(F32)
    taken = blk >= n_sb
    lane = lax.broadcasted_iota(jnp.int32, (rows, 128), 1)
    picked = jnp.zeros((rows, 128), F32)
    for it in range(min(N_SEL, n_sb)):
        live = jnp.where(taken, -jnp.inf, score)
        mx = jnp.max(live, axis=-1, keepdims=True)
        ix = jnp.min(jnp.where(~taken & (score == mx), blk_f, float(n_lane)), axis=-1, keepdims=True)
        picked = jnp.where(lane == it, ix, picked)
        taken = taken | (blk_f == ix)
    idx_ref[...] = picked.astype(jnp.int32)


def _nsa_select_kernel(past, sd, n_sb, n_pages, k_sel, *refs):
    pt_ref, ix_ref, lyr_ref = refs[:3]
    pages = refs[3:3 + k_sel]
    new_ref, q_ref, oc_ref, ow_ref, gt_ref, o_ref = refs[3 + k_sel:]
    rq = HG_B * sd
    page = new_ref.shape[-1]
    blk_pg = page // L_SLC
    b, g, i = pl.program_id(0), pl.program_id(1), pl.program_id(2)
    base = ((b * G_B + g) * sd + i) * k_sel
    lane = lax.broadcasted_iota(jnp.int32, (1, k_sel * page), 1)
    in_page = lane % page
    kpos = in_page
    picked = lane < 0
    kts, vts = [], []
    for j in range(k_sel):
        blk = ix_ref[base + j]
        is_new = blk == n_sb - 1
        kts.append(jnp.where(is_new, new_ref[0], pages[j][0]))
        vts.append(jnp.where(is_new, new_ref[1], pages[j][1]))
        mine = lane // page == j
        kpos = kpos + jnp.where(mine, (blk // blk_pg) * page, 0)
        picked = picked | (mine & (in_page // L_SLC == blk % blk_pg))
    kt_all = jnp.concatenate(kts, axis=1).astype(BF16)
    vt_all = jnp.concatenate(vts, axis=1).astype(BF16)
    s = jnp.dot(q_ref[...], kt_all, preferred_element_type=F32) * (HD_B ** -0.5)
    row_q = lax.broadcasted_iota(jnp.int32, (rq, 1), 0) % sd
    mask = picked & (kpos <= past + row_q)
    m = jnp.max(jnp.where(mask, s, NEG_BIG), axis=-1, keepdims=True)
    e = jnp.where(mask, jnp.exp(s - m), 0.0)
    d = jnp.sum(e, axis=-1, keepdims=True)
    p = e / jnp.where(d > 0, d, 1.0)
    o_s = lax.dot_general(p.astype(BF16), vt_all, (((1,), (1,)), ((), ())), preferred_element_type=F32)
    gates = _sigmoid(gt_ref[...])
    y = gates[0] * oc_ref[...] + gates[1] * o_s + gates[2] * ow_ref[...]

    @pl.when(i == 0)
    def _():
        o_ref[...] = jnp.zeros_like(o_ref)

    o_ref[...] += jnp.where(row_q == i, y, 0.0)


def _nsa_sample(proj3, cols, kvs, sd, cmp_big, pages_cmp, pages_slc, win_view, layer, page_table):
    col_qb, col_kv, col_gb = cols
    bsz = proj3.shape[0]
    n_phys, depth, _, _, page = pages_cmp.shape
    n_pages = page_table.shape[1]
    past = n_pages * page
    win = win_view.shape[-1]
    half = KV_B // 2
    rq = HG_B * sd
    n_ch = past // D_CMP
    n_cmp = (past + sd) // D_CMP - L_CMP // D_CMP + 1
    n_sb = -(-(past + sd) // L_SLC)
    k_sel = min(N_SEL, n_sb)
    assert (past + sd) // D_CMP == n_ch and past % L_SLC == 0 and sd <= L_SLC and page % L_SLC == 0
    _, b1big, w2big, b2big, w1tok = cmp_big
    pp = _cmp_pages(pages_cmp, layer, page_table, w1tok)
    q = proj3[:, :sd, col_qb:col_qb + MIX_B].reshape(bsz, sd, G_B, HG_B, HD_B).transpose(0, 2, 3, 1, 4)
    q = jnp.stack([jnp.pad(q[:, g], ((0, 0), (0, 0), (0, 0), (g * HD_B, half - (g + 1) * HD_B))) for g in range(G_B)], 1)
    q = q.reshape(bsz, G_B * rq, half).astype(BF16)
    gt = proj3[:, :sd, col_gb:col_gb + 3 * H_B].astype(F32).reshape(bsz, sd, G_B, HG_B, 3).transpose(0, 4, 2, 3, 1)
    gt = jnp.broadcast_to(gt.reshape(bsz, 3, G_B * rq, 1), (bsz, 3, G_B * rq, half))
    n_lane = -(-n_sb // 128) * 128
    mm = jnp.asarray(np.pad(_cmp_to_sel_map(n_ch, n_cmp, n_sb), ((0, n_lane - n_sb), (0, 0))).T)
    per_b = lambda shape: pl.BlockSpec((None,) + shape, lambda i: (i,) + (0,) * len(shape))
    const = lambda shape: pl.BlockSpec(shape, lambda i: (0,) * len(shape))
    o_c, o_w, idx = pl.pallas_call(
        functools.partial(_nsa_score_kernel, past, sd, n_cmp, n_sb),
        grid=(bsz,),
        in_specs=[per_b((n_ch, 2 * KV_B)), per_b((G_B * rq, half)),
                  pl.BlockSpec((None, None, 2, half, win), lambda i: (layer, i, 0, 0, 0)),
                  pl.BlockSpec((None, PAD_ROWS, KV_B), lambda i: (i, 0, col_kv // KV_B + 2)),
                  const((1, KV_B)), const((KV_B, KV_B)), const((1, KV_B)), const((n_ch, n_lane))],
        out_specs=[per_b((G_B * rq, half)), per_b((G_B * rq, half)), per_b((G_B * sd, 128))],
        out_shape=[jax.ShapeDtypeStruct((bsz, G_B * rq, half), F32), jax.ShapeDtypeStruct((bsz, G_B * rq, half), F32),
                   jax.ShapeDtypeStruct((bsz, G_B * sd, 128), jnp.int32)],
        compiler_params=_cparams(("parallel",), 40),
        name="nsa_score_sample",
    )(pp, q, win_view, proj3, b1big, w2big, b2big, mm)
    blk_pg = page // L_SLC
    n_cached = past // L_SLC
    new_blk = jnp.pad(kvs.reshape(bsz, sd, 2, half).transpose(0, 2, 3, 1), ((0, 0), (0, 0), (0, 0), (0, page - sd)))

    def blk_spec(j):
        def index(b, g, i, pt, ix, lyr):
            blk = jnp.minimum(ix[((b * G_B + g) * sd + i) * k_sel + j], n_cached - 1)
            return pt[b * n_pages + blk // blk_pg], lyr[0], 0, 0, 0
        return pl.BlockSpec((None, None, 2, half, page), index)

    grp = lambda b, g, i, pt, ix, lyr: (b, g, 0)
    grid_spec = pltpu.PrefetchScalarGridSpec(
        num_scalar_prefetch=3,
        grid=(bsz, G_B, sd),
        in_specs=[blk_spec(j) for j in range(k_sel)] + [
            pl.BlockSpec((None, 2, half, page), lambda b, g, i, pt, ix, lyr: (b, 0, 0, 0)),
            pl.BlockSpec((None, rq, half), grp), pl.BlockSpec((None, rq, half), grp), pl.BlockSpec((None, rq, half), grp),
            pl.BlockSpec((None, 3, rq, half), lambda b, g, i, pt, ix, lyr: (b, 0, g, 0))],
        out_specs=pl.BlockSpec((None, rq, half), grp))
    y = pl.pallas_call(
        functools.partial(_nsa_select_kernel, past, sd, n_sb, n_pages, k_sel),
        grid_spec=grid_spec,
        out_shape=jax.ShapeDtypeStruct((bsz, G_B * rq, half), F32),
        compiler_params=_cparams(("parallel", "parallel", "arbitrary"), 40),
        name="nsa_select_sample",
    )(page_table.reshape(-1), idx[:, :, :k_sel].reshape(-1), jnp.full((1,), layer, jnp.int32),
      *([pages_slc] * k_sel), new_blk, q, o_c, o_w, gt)
    y = y.reshape(bsz, G_B, HG_B, sd, G_B, HD_B)
    y = jnp.stack([y[:, g, :, :, g] for g in range(G_B)], axis=1)
    return y.transpose(0, 3, 1, 2, 4).reshape(bsz, sd, MIX_B)


def _prep_w_in(w, d):
    o = np.cumsum([0, MIX_A, MIX_A, MIX_A, MIX_A, MIX_B, KV_B, KV_B, KV_B, 3 * H_B, 2 * C_CONV, 3 * d])
    parts = [w[:, o[0]:o[4]], w[:, o[10]:o[11]], w[:, o[4]:o[5]], w[:, o[9]:o[10]], w[:, o[5]:o[8]], w[:, o[8]:o[9]]]
    n = sum(p.shape[1] for p in parts)
    n_pad = -(-n // 512) * 512
    parts.append(jnp.zeros((w.shape[0], n_pad - n), w.dtype))
    return jnp.concatenate(parts, axis=1).astype(BF16)


def _layer(x3, n_valid, p, s0, conv_prefix, mem_kv, nsa_fn, alpha):
    bsz, t, d = x3.shape
    m = bsz * t
    col_qb = COL_MG + 3 * d
    col_glu = col_qb + MIX_B
    col_kv = col_glu + 2 * C_CONV
    col_gb = col_kv + 3 * KV_B
    proj = _matmul(x3.reshape(m, d), p['w_in'], 1024, 1536, out_dtype=BF16)
    proj3 = proj.reshape(bsz, t, -1)
    ya, s_new = _hgrn(proj3, p['lb'], p['hg_norm'], s0, n_valid)
    yc, conv_state = _conv(proj3, col_glu // C_CONV, conv_prefix, p['conv_w'], p['conv_b'],
                           p['conv_ln_g'], p['conv_ln_b'], n_valid)
    kv_shape = (bsz, n_valid, 2, G_B, HD_B)
    kvc = proj3[:, :n_valid, col_kv:col_kv + KV_B].astype(F32)
    kvs = proj3[:, :n_valid, col_kv + KV_B:col_kv + 2 * KV_B].astype(F32)
    kvw = proj3[:, :n_valid, col_kv + 2 * KV_B:col_kv + 3 * KV_B].astype(F32)
    yb, nsa_extra = nsa_fn(proj3, (col_qb, col_kv, col_gb), kvc, kvs, kvw)
    kvc, kvs = kvc.reshape(kv_shape), kvs.reshape(kv_shape)
    x1 = _merge(ya.reshape(m, MIX_A), yb.reshape(m, MIX_B), yc.reshape(m, C_CONV), proj, x3.reshape(m, d),
                p['w_pa'], p['w_pb'], p['w_pc'], p['w_out'], p['ln_g'][0:1], p['ln_b'][0:1], alpha)
    x2 = _xattn(x1.reshape(bsz, t, d), mem_kv[0], mem_kv[1], p['w_xq'], p['w_xo'], p['ln_g'][1:2], p['ln_b'][1:2], alpha)
    x3n = _mlp(x2.reshape(m, d), p['w_up'], p['w_down'], p['ln_g'][2:3], p['ln_b'][2:3], alpha)
    return x3n.reshape(bsz, t, d), kvc, kvs, nsa_extra, s_new, conv_state


def kernel(x_prompt, x_sample, cache_cmp, cache_slc, cache_win, state_hgrn, state_conv, cache_mem, page_table, mem_prompt, w_in, lb_raw, hg_norm, w_cmp1, b_cmp1, w_cmp2, b_cmp2, conv_w, conv_b, conv_ln_g, conv_ln_b, w_pa, w_pb, w_pc, w_out, ln_g, ln_b, w_xq, w_xkv, w_xo, w_up, w_down):
    bp, t, d = x_prompt.shape
    bd, sd = x_sample.shape[:2]
    depth = w_in.shape[0]
    n_mem = mem_prompt.shape[1]
    win_buf = cache_win.shape[2]
    alpha = (2 * depth) ** 0.25
    lb_cum = jnp.cumsum(jax.nn.softmax(lb_raw.astype(F32), axis=0), axis=0)
    lb_all = lb_cum - lb_cum[0]
    pages_cmp, pages_slc, win_view = _pages_view(cache_cmp), _pages_view(cache_slc), _pages_view(cache_win)
    mem_cache = _mem_tile_order(cache_mem, d)
    xp = x_prompt
    xs = jnp.pad(x_sample, ((0, 0), (0, PAD_ROWS - sd), (0, 0)))
    outs = {k: [] for k in ('cmp_p', 'cmp_s', 'slc_p', 'slc_s', 'win_p', 'win_s', 'hg_p', 'hg_s', 'cv_p', 'cv_s', 'mem_p')}
    for l in range(depth):
        p = {'w_in': _prep_w_in(w_in[l], d), 'lb': lb_all[l], 'hg_norm': hg_norm[l], 'conv_w': conv_w[l],
             'conv_b': conv_b[l].reshape(1, -1), 'conv_ln_g': conv_ln_g[l].reshape(1, -1),
             'conv_ln_b': conv_ln_b[l].reshape(1, -1),
             'w_pa': w_pa[l].astype(BF16), 'w_pb': w_pb[l].astype(BF16), 'w_pc': w_pc[l].astype(BF16),
             'w_out': w_out[l].astype(BF16), 'ln_g': ln_g[l], 'ln_b': ln_b[l],
             'w_xq': w_xq[l].astype(BF16), 'w_xo': w_xo[l].astype(BF16),
             'w_up': w_up[l].astype(BF16), 'w_down': w_down[l].astype(BF16)}
        cmp_w = (w_cmp1[l], b_cmp1[l], w_cmp2[l], b_cmp2[l])

        cmp_big = _cmp_weights(*cmp_w)

        def nsa_prompt(proj3, cols, kvc, kvs, kvw):
            col_qb, col_kv, col_gb = cols
            prep = _nsa_prep(proj3, col_kv, cmp_big, min(win_buf, t))
            o = _nsa_prompt(proj3, col_qb, col_gb, prep[:6])
            ct, st_, wt = prep[6:]
            return o, (ct, st_, jnp.pad(wt, ((0, 0), (0, 0), (max(win_buf - t, 0), 0))))

        def nsa_sample(proj3, cols, kvc, kvs, kvw):
            o = _nsa_sample(proj3, cols, kvs, sd, cmp_big, pages_cmp, pages_slc, win_view, l, page_table)
            o = jnp.pad(o, ((0, 0), (0, PAD_ROWS - sd), (0, 0)))
            win = jnp.concatenate([cache_win[l], kvw.reshape(bd, sd, 2, G_B, HD_B)], axis=1)[:, -win_buf:]
            return o, win

        w_kv = _mem_tile_order(w_xkv[l].reshape(d, 2, NX_H, d // NX_H), d).astype(BF16)
        mem_kv = _matmul(mem_prompt.reshape(bp * n_mem, d), w_kv, 1024, 512).reshape(1, bp, n_mem, 2 * d)
        xp, kc, ks_, wn, sh, cv = _layer(xp, t, p, jnp.zeros((bp, H_A, DK_A, DV_A), F32),
                                         jnp.zeros((bp, CONV_K - 1, C_CONV), F32), (mem_kv, 0), nsa_prompt, alpha)
        outs['cmp_p'].append(wn[0]); outs['slc_p'].append(wn[1]); outs['win_p'].append(wn[2])
        outs['hg_p'].append(sh); outs['cv_p'].append(cv)
        outs['mem_p'].append(_mem_head_order(mem_kv[0], d))
        xs, kc, ks_, wn, sh, cv = _layer(xs, sd, p, state_hgrn[l], state_conv[l], (mem_cache, l), nsa_sample, alpha)
        outs['cmp_s'].append(kc); outs['slc_s'].append(ks_); outs['win_s'].append(wn)
        outs['hg_s'].append(sh); outs['cv_s'].append(cv)
    st = lambda k, ax: jnp.stack(outs[k], axis=ax)

    def rows(k, ax):
        a = st(k, ax)
        a = a.reshape(a.shape[:2] + (2, G_B, HD_B, a.shape[-1]))
        return jnp.transpose(a, (0, 1, 5, 2, 3, 4))

    return (xp, xs[:, :sd],
            rows('cmp_p', 1), st('cmp_s', 1), rows('slc_p', 1), st('slc_s', 1),
            rows('win_p', 0), st('win_s', 0), st('hg_p', 0), st('hg_s', 0),
            st('cv_p', 0), st('cv_s', 0), st('mem_p', 0))
```

```python
import functools

import numpy as np
import jax
import jax.numpy as jnp
from jax import lax
from jax.experimental import pallas as pl
from jax.experimental.pallas import tpu as pltpu

F32 = jnp.float32
BF16 = jnp.bfloat16
HIGHEST = lax.Precision.HIGHEST

H_A, DK_A, DV_A, CHUNK_A = 4, 128, 128, 16
H_B, G_B, HG_B, HD_B = 8, 2, 4, 64
L_CMP, D_CMP, L_SLC, N_SEL, WINDOW = 32, 16, 64, 16, 512
C_CONV, CONV_K = 512, 31
NX_H = 4
LN_EPS, RMS_EPS = 1e-5, 1e-6
PAD_ROWS = 16
LANES = 128

MIX_A = H_A * DK_A
MIX_B = H_B * HD_B
KV_B = 2 * G_B * HD_B
COL_QA, COL_FA, COL_IA, COL_GA = 0, 512, 1024, 1536
COL_MG = 2048
V7X_VMEM_LIMIT = 56 * 2**20


def _cparams(sem, vmem_mb=None):
    return pltpu.CompilerParams(dimension_semantics=sem,
                                vmem_limit_bytes=None if vmem_mb is None else vmem_mb * 2**20)


def _ln(y, g, b):
    mu = jnp.mean(y, axis=-1, keepdims=True)
    d = y - mu
    var = jnp.mean(d * d, axis=-1, keepdims=True)
    return d * lax.rsqrt(var + LN_EPS) * g + b


def _sigmoid(x):
    return 1.0 / (1.0 + jnp.exp(-x))


def _mm_kernel(x_ref, w_ref, o_ref, xb_ref):
    @pl.when(pl.program_id(1) == 0)
    def _():
        xb_ref[...] = x_ref[...].astype(BF16)

    o_ref[...] = jnp.dot(xb_ref[...], w_ref[...], preferred_element_type=F32).astype(o_ref.dtype)


def _matmul(x, w, tm, tn, out_dtype=F32):
    m, k = x.shape
    n = w.shape[1]
    tm, tn = min(tm, m), min(tn, n)
    return pl.pallas_call(
        _mm_kernel,
        grid=(m // tm, n // tn),
        in_specs=[pl.BlockSpec((tm, k), lambda i, j: (i, 0)),
                  pl.BlockSpec((k, tn), lambda i, j: (0, j))],
        out_specs=pl.BlockSpec((tm, tn), lambda i, j: (i, j)),
        out_shape=jax.ShapeDtypeStruct((m, n), out_dtype),
        scratch_shapes=[pltpu.VMEM((tm, k), BF16)],
        compiler_params=_cparams(("parallel", "arbitrary"), 40),
        name="proj_matmul",
    )(x, w)


def _mlp_kernel(alpha, x_ref, wu_ref, wd_ref, g_ref, b_ref, o_ref, xb_ref, acc_ref):
    j = pl.program_id(1)

    @pl.when(j == 0)
    def _():
        xb_ref[...] = x_ref[...].astype(BF16)
        acc_ref[...] = jnp.zeros_like(acc_ref)

    h = jnp.dot(xb_ref[...], wu_ref[...], preferred_element_type=F32)
    h = jnp.square(jnp.maximum(h, 0.0)).astype(BF16)
    acc_ref[...] += jnp.dot(h, wd_ref[...], preferred_element_type=F32)

    @pl.when(j == pl.num_programs(1) - 1)
    def _():
        o_ref[...] = _ln(alpha * x_ref[...] + acc_ref[...], g_ref[...], b_ref[...])


def _mlp(x, w_up, w_down, g, b, alpha, tm=512, tf=2048):
    m, d = x.shape
    ff = w_up.shape[1]
    tm = min(tm, m)
    return pl.pallas_call(
        functools.partial(_mlp_kernel, alpha),
        grid=(m // tm, ff // tf),
        in_specs=[pl.BlockSpec((tm, d), lambda i, j: (i, 0)),
                  pl.BlockSpec((d, tf), lambda i, j: (0, j)),
                  pl.BlockSpec((tf, d), lambda i, j: (j, 0)),
                  pl.BlockSpec((1, d), lambda i, j: (0, 0)),
                  pl.BlockSpec((1, d), lambda i, j: (0, 0))],
        out_specs=pl.BlockSpec((tm, d), lambda i, j: (i, 0)),
        out_shape=jax.ShapeDtypeStruct((m, d), F32),
        scratch_shapes=[pltpu.VMEM((tm, d), BF16), pltpu.VMEM((tm, d), F32)],
        compiler_params=_cparams(("parallel", "arbitrary"), 48),
        name="mlp",
    )(x, w_up, w_down, g, b)


def _merge_kernel(alpha, ya_ref, yb_ref, yc_ref, ma_ref, mb_ref, mc_ref, x_ref,
                  wpa_ref, wpb_ref, wpc_ref, wout_ref, g_ref, b_ref, o_ref):
    def branch(y_ref, m_ref, w_ref):
        return _sigmoid(m_ref[...].astype(F32)) * jnp.dot(y_ref[...].astype(BF16), w_ref[...], preferred_element_type=F32)

    merged = branch(ya_ref, ma_ref, wpa_ref) + branch(yb_ref, mb_ref, wpb_ref) + branch(yc_ref, mc_ref, wpc_ref)
    y = jnp.dot(merged.astype(BF16), wout_ref[...], preferred_element_type=F32)
    o_ref[...] = _ln(alpha * x_ref[...] + y, g_ref[...], b_ref[...])


def _merge(ya, yb, yc, proj, x, wpa, wpb, wpc, wout, g, b, alpha, tm=512):
    m, d = x.shape
    tm = min(tm, m)
    mg0 = COL_MG // d
    row = lambda i: (i, 0)
    const = lambda i: (0, 0)
    return pl.pallas_call(
        functools.partial(_merge_kernel, alpha),
        grid=(m // tm,),
        in_specs=[pl.BlockSpec((tm, MIX_A), row), pl.BlockSpec((tm, MIX_B), row), pl.BlockSpec((tm, C_CONV), row),
                  pl.BlockSpec((tm, d), lambda i: (i, mg0)), pl.BlockSpec((tm, d), lambda i: (i, mg0 + 1)),
                  pl.BlockSpec((tm, d), lambda i: (i, mg0 + 2)),
                  pl.BlockSpec((tm, d), row),
                  pl.BlockSpec((MIX_A, d), const), pl.BlockSpec((MIX_B, d), const), pl.BlockSpec((C_CONV, d), const),
                  pl.BlockSpec((d, d), const), pl.BlockSpec((1, d), const), pl.BlockSpec((1, d), const)],
        out_specs=pl.BlockSpec((tm, d), row),
        out_shape=jax.ShapeDtypeStruct((m, d), F32),
        compiler_params=_cparams(("parallel",), 48),
        name="merge_out",
    )(ya, yb, yc, proj, proj, proj, x, wpa, wpb, wpc, wout, g, b)


def _xattn_kernel(alpha, x_ref, kv_ref, wq_ref, wo_ref, g_ref, b_ref, o_ref):
    x = x_ref[...]
    d = x.shape[-1]
    hd = d // NX_H
    q = jnp.dot(x.astype(BF16), wq_ref[...], preferred_element_type=F32)
    n_dt = hd // LANES

    def head(base, h):
        parts = [kv_ref[:, base + (dt * NX_H + h) * LANES:base + (dt * NX_H + h + 1) * LANES] for dt in range(n_dt)]
        return jnp.concatenate(parts, axis=1).astype(BF16)

    outs = []
    for h in range(NX_H):
        qh = q[:, h * hd:(h + 1) * hd].astype(BF16)
        kh, vh = head(0, h), head(d, h)
        s = lax.dot_general(qh, kh, (((1,), (1,)), ((), ())), preferred_element_type=F32) * (hd ** -0.5)
        e = jnp.exp(s - jnp.max(s, axis=-1, keepdims=True))
        p = e / jnp.sum(e, axis=-1, keepdims=True)
        outs.append(jnp.dot(p.astype(BF16), vh, preferred_element_type=F32))
    o = jnp.concatenate(outs, axis=-1)
    y = jnp.dot(o.astype(BF16), wo_ref[...], preferred_element_type=F32)
    o_ref[...] = _ln(alpha * x + y, g_ref[...], b_ref[...])


def _mem_tile_order(a, d):
    lead = a.shape[:-3]
    a = a.reshape(lead + (2, NX_H, d // NX_H // LANES, LANES))
    return jnp.swapaxes(a, -3, -2).reshape(lead + (2 * d,))


def _mem_head_order(a, d):
    lead = a.shape[:-1]
    a = a.reshape(lead + (2, d // NX_H // LANES, NX_H, LANES))
    return jnp.swapaxes(a, -3, -2).reshape(lead + (2, NX_H, d // NX_H))


def _xattn(x, kv, layer, wq, wo, g, b, alpha, tm=512):
    bsz, t, d = x.shape
    n_mem = kv.shape[2]
    tm = min(tm, t)
    const = lambda i, j: (0, 0)
    return pl.pallas_call(
        functools.partial(_xattn_kernel, alpha),
        grid=(bsz, t // tm),
        in_specs=[pl.BlockSpec((None, tm, d), lambda i, j: (i, j, 0)),
                  pl.BlockSpec((None, None, n_mem, 2 * d), lambda i, j: (layer, i, 0, 0)),
                  pl.BlockSpec((d, d), const), pl.BlockSpec((d, d), const),
                  pl.BlockSpec((1, d), const), pl.BlockSpec((1, d), const)],
        out_specs=pl.BlockSpec((None, tm, d), lambda i, j: (i, j, 0)),
        out_shape=jax.ShapeDtypeStruct((bsz, t, d), F32),
        compiler_params=_cparams(("parallel", "parallel"), 48),
        name="xattn",
    )(x, kv, wq, wo, g, b)


CONV_HALO = 32
CONV_SUB = 32


def _conv_kernel(rt, n_valid_last, a_ref, gt_ref, pre_ref, w_ref, cb_ref, g_ref, b_ref, y_ref, st_ref, ue_ref, sh_ref):
    t = pl.program_id(1)
    off = CONV_HALO - (CONV_K - 1)
    sl = 8

    @pl.when(t == 0)
    def _():
        ue_ref[0:off, :] = jnp.zeros((off, C_CONV), F32)
        ue_ref[off:CONV_HALO, :] = pre_ref[...]

    ue_ref[CONV_HALO:CONV_HALO + rt, :] = a_ref[...].astype(F32) * _sigmoid(gt_ref[...].astype(F32))
    n_sh = rt + CONV_HALO - sl
    for s in range(1, sl):
        sh_ref[s - 1, 0:n_sh, :] = ue_ref[s:s + n_sh, :]
    sub = min(CONV_SUB, rt)
    for r0 in range(0, rt, sub):
        acc = cb_ref[...]
        for j in range(CONV_K):
            a, s = divmod(off + j, sl)
            lo = r0 + a * sl
            win = ue_ref[lo:lo + sub, :] if s == 0 else sh_ref[s - 1, lo:lo + sub, :]
            acc = acc + w_ref[j:j + 1, :] * win
        y = _ln(acc, g_ref[...], b_ref[...])
        y_ref[r0:r0 + sub, :] = y * _sigmoid(y)

    @pl.when(t == pl.num_programs(1) - 1)
    def _():
        st_ref[...] = ue_ref[off + n_valid_last:off + n_valid_last + CONV_K - 1, :]

    ue_ref[0:CONV_HALO, :] = ue_ref[rt:rt + CONV_HALO, :]


def _conv(proj3, col_a, prefix, w, cb, g, b, n_valid, rt=256):
    bsz, t, _ = proj3.shape
    rt = min(rt, t)
    n_valid_last = n_valid - (t - rt)
    vec = lambda i, j: (0, 0)
    return pl.pallas_call(
        functools.partial(_conv_kernel, rt, n_valid_last),
        grid=(bsz, t // rt),
        in_specs=[pl.BlockSpec((None, rt, C_CONV), lambda i, j: (i, j, col_a)),
                  pl.BlockSpec((None, rt, C_CONV), lambda i, j: (i, j, col_a + 1)),
                  pl.BlockSpec((None, CONV_K - 1, C_CONV), lambda i, j: (i, 0, 0)),
                  pl.BlockSpec((CONV_K, C_CONV), vec), pl.BlockSpec((1, C_CONV), vec),
                  pl.BlockSpec((1, C_CONV), vec), pl.BlockSpec((1, C_CONV), vec)],
        out_specs=[pl.BlockSpec((None, rt, C_CONV), lambda i, j: (i, j, 0)),
                   pl.BlockSpec((None, CONV_K - 1, C_CONV), lambda i, j: (i, 0, 0))],
        out_shape=[jax.ShapeDtypeStruct((bsz, t, C_CONV), F32),
                   jax.ShapeDtypeStruct((bsz, CONV_K - 1, C_CONV), F32)],
        scratch_shapes=[pltpu.VMEM((rt + CONV_HALO, C_CONV), F32), pltpu.VMEM((7, rt + CONV_HALO - 8, C_CONV), F32)],
        compiler_params=_cparams(("parallel", "arbitrary"), 32),
        name="conformer_conv",
    )(proj3, proj3, prefix, w, cb, g, b)


def _hgrn_kernel(tt, n_valid, q_ref, f_ref, i_ref, g_ref, la_ref, l1_ref, oml_ref, gn_ref, s0_ref,
                 y_ref, s_ref, st_ref, qs_ref, kk_ref, bb_ref):
    c = CHUNK_A
    t = pl.program_id(1)

    @pl.when(t == 0)
    def _():
        for h in range(H_A):
            st_ref[h] = s0_ref[h].T

    z = f_ref[...].astype(F32)
    cc = l1_ref[...] + (jnp.minimum(z, 0.0) - jnp.log1p(jnp.exp(-jnp.abs(z))))
    a = la_ref[...]
    logf = jnp.maximum(a, cc) + jnp.log1p(jnp.exp(-jnp.abs(a - cc)))
    k = oml_ref[...] * _sigmoid(-z)
    row = lax.broadcasted_iota(jnp.int32, (tt, 1), 0)
    if n_valid < tt:
        logf = jnp.where(row < n_valid, logf, 0.0)
        k = jnp.where(row < n_valid, k, 0.0)
    b = logf
    rc = row & (c - 1)
    sh = 1
    while sh < c:
        b = b + jnp.where(rc >= sh, pltpu.roll(b, sh, 0), 0.0)
        sh *= 2
    q = q_ref[...].astype(F32)
    qs_ref[...] = q * _sigmoid(q)
    kk_ref[...] = k
    bb_ref[...] = b

    rowc = lax.broadcasted_iota(jnp.int32, (c, 1), 0)

    def chunk(ci, carry):
        r0 = pl.multiple_of(ci * c, c)
        for h in range(H_A):
            hs = slice(h * DK_A, (h + 1) * DK_A)
            qc = qs_ref[pl.ds(r0, c), hs]
            kc = kk_ref[pl.ds(r0, c), hs]
            bc = bb_ref[pl.ds(r0, c), hs]
            vc = i_ref[pl.ds(r0, c), hs].astype(F32)
            bl = bc[c - 1:c, :]
            st = st_ref[h]
            qe = (qc * jnp.exp(bc)).astype(BF16)
            o = lax.dot_general(qe, st.astype(BF16), (((1,), (1,)), ((), ())), preferred_element_type=F32)
            for s in range(c):
                e = jnp.exp(jnp.where(rowc >= s, bc - bc[s:s + 1, :], -jnp.inf))
                att = jnp.sum(qc * e * kc[s:s + 1, :], axis=-1, keepdims=True)
                o = o + att * vc[s:s + 1, :]
            y_ref[pl.ds(r0, c), hs] = o
            kd = (kc * jnp.exp(bl - bc)).astype(BF16)
            u = lax.dot_general(vc.astype(BF16), kd, (((0,), (0,)), ((), ())), preferred_element_type=F32)
            st_ref[h] = st * jnp.exp(bl) + u
        return carry

    lax.fori_loop(0, tt // c, chunk, 0, unroll=4 if (tt // c) % 4 == 0 else 1)

    g = g_ref[...].astype(F32)
    gate = g * _sigmoid(g)
    for h in range(H_A):
        hs = slice(h * DV_A, (h + 1) * DV_A)
        o = y_ref[:, hs]
        o = o * lax.rsqrt(jnp.mean(o * o, axis=-1, keepdims=True) + RMS_EPS) * gn_ref[...]
        y_ref[:, hs] = o * gate[:, hs]

    @pl.when(t == pl.num_programs(1) - 1)
    def _():
        for h in range(H_A):
            s_ref[h] = st_ref[h].T


def _hgrn(proj3, lb, gnorm, s0, n_valid, tt=256):
    bsz, t, _ = proj3.shape
    tt = min(tt, t)
    lb = lb.reshape(1, MIX_A).astype(F32)
    la, l1, oml = jnp.log(lb), jnp.log1p(-lb), 1.0 - lb
    vec = lambda i, j: (0, 0)
    col = lambda cb: pl.BlockSpec((None, tt, MIX_A), lambda i, j: (i, j, cb))
    st = pl.BlockSpec((None, H_A, DK_A, DV_A), lambda i, j: (i, 0, 0, 0))
    return pl.pallas_call(
        functools.partial(_hgrn_kernel, tt, n_valid if t == tt else tt),
        grid=(bsz, t // tt),
        in_specs=[col(0), col(1), col(2), col(3),
                  pl.BlockSpec((1, MIX_A), vec), pl.BlockSpec((1, MIX_A), vec), pl.BlockSpec((1, MIX_A), vec),
                  pl.BlockSpec((1, DV_A), vec), st],
        out_specs=[pl.BlockSpec((None, tt, MIX_A), lambda i, j: (i, j, 0)), st],
        out_shape=[jax.ShapeDtypeStruct((bsz, t, MIX_A), F32),
                   jax.ShapeDtypeStruct((bsz, H_A, DK_A, DV_A), F32)],
        scratch_shapes=[pltpu.VMEM((H_A, DV_A, DK_A), F32), pltpu.VMEM((tt, MIX_A), F32),
                        pltpu.VMEM((tt, MIX_A), F32), pltpu.VMEM((tt, MIX_A), F32)],
        compiler_params=_cparams(("parallel", "arbitrary"), 32),
        name="hgrn2",
    )(proj3, proj3, proj3, proj3, la, l1, oml, gnorm.reshape(1, DV_A), s0)


NEG_BIG = -1e30
SEL_PER_CMP = L_SLC // D_CMP


def _cmp_weights(w1, b1, w2, b2):
    m = L_CMP // D_CMP
    eye_e, eye_g = jnp.eye(2, dtype=F32), jnp.eye(G_B, dtype=F32)
    w1r = w1.reshape(2, m, D_CMP, HD_B, HD_B)
    w1big = jnp.einsum('ehjdf,ea,gb->jegdhabf', w1r, eye_e, eye_g).reshape(D_CMP * KV_B, m * KV_B)
    w2big = jnp.einsum('efo,ea,gb->egfabo', w2, eye_e, eye_g).reshape(KV_B, KV_B)
    b1big = jnp.broadcast_to(b1[:, None, :], (2, G_B, HD_B)).reshape(1, KV_B)
    b2big = jnp.broadcast_to(b2[:, None, :], (2, G_B, HD_B)).reshape(1, KV_B)
    w1tok = jnp.einsum('ehjdf,gb->ejgdhbf', w1r, eye_g).reshape(2, D_CMP, KV_B // 2, m * KV_B // 2)
    return w1big.astype(BF16), b1big, w2big.astype(BF16), b2big, w1tok.astype(BF16)


def _cmp_to_sel_map(n_cmp_pad, n_cmp, n_sb):
    mm = np.zeros((n_sb, n_cmp_pad), np.float32)
    for n in range(n_cmp):
        for i in (n, n + 1):
            if i // SEL_PER_CMP < n_sb:
                mm[i // SEL_PER_CMP, n] += 1.0
    return mm


def _compress_tail(pp, n_cmp, b1_ref, w2_ref, b2_ref):
    n_ch = pp.shape[0]
    hid = pp[:, :KV_B] + pltpu.roll(pp[:, KV_B:], n_ch - 1, 0) + b1_ref[...]
    hid = hid * _sigmoid(hid)
    kc = jnp.dot(hid.astype(BF16), w2_ref[...], preferred_element_type=F32) + b2_ref[...]
    row = lax.broadcasted_iota(jnp.int32, (n_ch, 1), 0)
    return jnp.where(row < n_cmp, kc, 0.0)


def _nsa_prep_kernel(n_cmp, win, kc_ref, ks_ref, kw_ref, w1_ref, b1_ref, w2_ref, b2_ref,
                     kck_ref, kcvt_ref, ksk_ref, ksvt_ref, kwk_ref, kwvt_ref, ct_ref, st_ref, wt_ref, xk_ref):
    half = KV_B // 2
    t = kc_ref.shape[0]
    n_ch = t // D_CMP
    for e in range(2):
        xk_ref[e] = kc_ref[:, e * half:(e + 1) * half].astype(F32)
    pp = jnp.zeros((n_ch, w1_ref.shape[2]), F32)
    for j in range(D_CMP):
        for e in range(2):
            x = xk_ref[e, pl.ds(j, n_ch, stride=D_CMP), :].astype(BF16)
            pp = pp + jnp.dot(x, w1_ref[j, e * half:(e + 1) * half, :], preferred_element_type=F32)
    kc = _compress_tail(pp, n_cmp, b1_ref, w2_ref, b2_ref)
    kck_ref[...] = kc[:, :half].astype(BF16)
    kcvt_ref[...] = kc[:, half:].T.astype(BF16)
    ct_ref[...] = kc_ref[...].astype(F32).T
    ks_t = ks_ref[...].astype(F32).T
    st_ref[...] = ks_t
    n_aug = ksk_ref.shape[1] - half
    blk_of_key = lax.broadcasted_iota(jnp.int32, (t, n_aug), 0) // L_SLC
    one_hot = jnp.where(blk_of_key == lax.broadcasted_iota(jnp.int32, (t, n_aug), 1), 1.0, 0.0).astype(BF16)
    ksk_ref[...] = jnp.concatenate([ks_ref[:, :half].astype(BF16), one_hot], axis=1)
    ksvt_ref[...] = ks_t[half:, :].astype(BF16)
    kw_t = kw_ref[...].astype(F32).T
    wt_ref[...] = kw_t[:, t - win:]
    kwk_ref[...] = kw_ref[:, :half].astype(BF16)
    kwvt_ref[...] = kw_t[half:, :].astype(BF16)


def _nsa_prep(proj3, col_kv, cmp_big, win):
    bsz, t, _ = proj3.shape
    n_ch = t // D_CMP
    n_cmp = n_ch - L_CMP // D_CMP + 1
    half = KV_B // 2
    w1big, b1big, w2big, b2big = cmp_big[:4]
    w1pos = w1big.reshape(D_CMP, KV_B, w1big.shape[1])
    cb = col_kv // KV_B
    const = lambda i: (0, 0)
    kspec = lambda n: pl.BlockSpec((None, n, half), lambda i: (i, 0, 0))
    vspec = lambda n: pl.BlockSpec((None, half, n), lambda i: (i, 0, 0))
    fspec = lambda n: pl.BlockSpec((None, KV_B, n), lambda i: (i, 0, 0))
    col = lambda c: pl.BlockSpec((None, t, KV_B), lambda i: (i, 0, cb + c))
    return pl.pallas_call(
        functools.partial(_nsa_prep_kernel, n_cmp, win),
        grid=(bsz,),
        in_specs=[col(0), col(1), col(2),
                  pl.BlockSpec(w1pos.shape, lambda i: (0, 0, 0)), pl.BlockSpec((1, KV_B), const),
                  pl.BlockSpec((KV_B, KV_B), const), pl.BlockSpec((1, KV_B), const)],
        out_specs=[kspec(n_ch), vspec(n_ch), pl.BlockSpec((None, t, KV_B), lambda i: (i, 0, 0)), vspec(t),
                   kspec(t), vspec(t), fspec(t), fspec(t), fspec(win)],
        out_shape=[jax.ShapeDtypeStruct((bsz, n_ch, half), BF16), jax.ShapeDtypeStruct((bsz, half, n_ch), BF16),
                   jax.ShapeDtypeStruct((bsz, t, KV_B), BF16), jax.ShapeDtypeStruct((bsz, half, t), BF16),
                   jax.ShapeDtypeStruct((bsz, t, half), BF16), jax.ShapeDtypeStruct((bsz, half, t), BF16),
                   jax.ShapeDtypeStruct((bsz, KV_B, t), F32), jax.ShapeDtypeStruct((bsz, KV_B, t), F32),
                   jax.ShapeDtypeStruct((bsz, KV_B, win), F32)],
        scratch_shapes=[pltpu.VMEM((2, t, half), F32)],
        compiler_params=_cparams(("parallel",), 48),
        name="nsa_prep",
    )(proj3, proj3, proj3, w1pos, b1big, w2big, b2big)


def _nsa_prompt_kernel(tq, tk, n_sb, q_ref, gb_ref, kck_ref, kcvt_ref, ksk_ref, ksvt_ref, kwk_ref, kwvt_ref, mm_ref, o_ref):
    qi = pl.program_id(1)
    q0 = qi * tq
    n_cp = kck_ref.shape[0]
    w4 = HG_B * tq
    q_t = (q_ref[...].astype(F32) * HD_B ** -0.5).T
    g_t = _sigmoid(gb_ref[...].astype(F32)).T
    qpos = q0 + lax.broadcasted_iota(jnp.int32, (1, tq), 1)
    qpos4 = jnp.concatenate([qpos] * HG_B, axis=1)
    zpad = jnp.zeros((HD_B, tq), BF16)

    def update(state, s, pen, vt, g):
        m, l, acc = state
        if pen is not None:
            s = s + jnp.concatenate([pen] * HG_B, axis=1)
        m_new = jnp.maximum(m, jnp.max(s, axis=0, keepdims=True))
        e = jnp.exp(s - m_new)
        a = jnp.exp(m - m_new)
        l = a * l + jnp.sum(e, axis=0, keepdims=True)
        pv = jnp.dot(vt[g * HD_B:(g + 1) * HD_B, :], e.astype(BF16), preferred_element_type=F32)
        return m_new, l, a * acc + pv

    qts, o_cs, sel_ts = [], [], []
    for g in range(G_B):
        cols = []
        for hg in range(HG_B):
            h = g * HG_B + hg
            qh = q_t[h * HD_B:(h + 1) * HD_B, :].astype(BF16)
            cols.append(jnp.concatenate([qh, zpad] if g == 0 else [zpad, qh], axis=0))
        qt = jnp.concatenate(cols, axis=1)
        qts.append(qt)
        s = jnp.dot(kck_ref[...], qt, preferred_element_type=F32)
        c_end = lax.broadcasted_iota(jnp.int32, (n_cp, w4), 0) * D_CMP + (L_CMP - 1)
        mask = c_end <= qpos4
        m = jnp.max(jnp.where(mask, s, NEG_BIG), axis=0, keepdims=True)
        e = jnp.where(mask, jnp.exp(s - m), 0.0)
        d = jnp.sum(e, axis=0, keepdims=True)
        p = e / jnp.where(d > 0, d, 1.0)
        o_cs.append(jnp.dot(kcvt_ref[...], p.astype(BF16), preferred_element_type=F32)[g * HD_B:(g + 1) * HD_B, :])
        imp = p[:, 0:tq]
        for hg in range(1, HG_B):
            imp = imp + p[:, hg * tq:(hg + 1) * tq]
        bs = jnp.dot(mm_ref[...], imp, precision=HIGHEST, preferred_element_type=F32)
        blk = lax.broadcasted_iota(jnp.int32, (n_sb, tq), 0)
        cur = qpos >> 6
        forced = (blk == 0) | (blk == cur) | (blk == cur - 1)
        score = jnp.where(blk <= cur, jnp.where(forced, jnp.inf, bs), -jnp.inf)
        rank = jnp.zeros((n_sb, tq), jnp.int32)
        for i in range(n_sb):
            si = score[i:i + 1, :]
            rank = rank + jnp.where((si > score) | ((si == score) & (blk > i)), 1, 0)
        sel_ts.append(jnp.where((rank < N_SEL) & (blk <= cur), 1.0, 0.0).astype(BF16))

    n_aug = ksk_ref.shape[1] - 2 * HD_B
    qas = []
    for g in range(G_B):
        pen_rows = ((sel_ts[g].astype(F32) - 1.0) * (-NEG_BIG)).astype(BF16)
        pen_rows = jnp.concatenate([pen_rows, jnp.zeros((n_aug - n_sb, tq), BF16)], axis=0)
        qas.append(jnp.concatenate([qts[g], jnp.concatenate([pen_rows] * HG_B, axis=1)], axis=0))
    krow = lax.broadcasted_iota(jnp.int32, (tk, tq), 0)

    def far_tiles(kt, states):
        k0 = pl.multiple_of(kt * tk, tk)
        k_tile, vt_tile = ksk_ref[pl.ds(k0, tk), :], ksvt_ref[:, pl.ds(k0, tk)]
        return tuple(update(states[g], jnp.dot(k_tile, qas[g], preferred_element_type=F32), None, vt_tile, g)
                     for g in range(G_B))

    def near_tiles(kt, states):
        k0 = pl.multiple_of(kt * tk, tk)
        rel = qpos - k0
        causal = krow <= rel
        pen_causal = jnp.where(causal, 0.0, NEG_BIG)
        pen_win = jnp.where(causal & (krow > rel - WINDOW), 0.0, NEG_BIG)
        k_tile, vt_tile = ksk_ref[pl.ds(k0, tk), :], ksvt_ref[:, pl.ds(k0, tk)]
        kw_tile, vwt_tile = kwk_ref[pl.ds(k0, tk), :], kwvt_ref[:, pl.ds(k0, tk)]
        new_sel = tuple(update(states[g], jnp.dot(k_tile, qas[g], preferred_element_type=F32),
                               pen_causal, vt_tile, g) for g in range(G_B))
        new_win = tuple(update(states[G_B + g], jnp.dot(kw_tile, qts[g], preferred_element_type=F32),
                               pen_win, vwt_tile, g) for g in range(G_B))
        return new_sel + new_win

    init = (jnp.full((1, w4), NEG_BIG, F32), jnp.zeros((1, w4), F32), jnp.zeros((HD_B, w4), F32))
    first_near = jnp.maximum((q0 - WINDOW) // tk, 0)
    states = lax.fori_loop(0, first_near, far_tiles, (init,) * G_B)
    states = lax.fori_loop(first_near, (q0 + tq) // tk, near_tiles, states + (init,) * G_B)
    finish = lambda st: st[2] / jnp.where(st[1] > 0, st[1], 1.0)
    outs = []
    for g in range(G_B):
        o_c, o_s, o_w = o_cs[g], finish(states[g]), finish(states[G_B + g])
        for hg in range(HG_B):
            c = (g * HG_B + hg) * 3
            sl = slice(hg * tq, (hg + 1) * tq)
            outs.append(g_t[c:c + 1, :] * o_c[:, sl] + g_t[c + 1:c + 2, :] * o_s[:, sl]
                        + g_t[c + 2:c + 3, :] * o_w[:, sl])
    o_ref[...] = jnp.concatenate(outs, axis=0).T


def _nsa_prompt(proj3, col_qb, col_gb, prep, tq=256, tk=256):
    bsz, t, _ = proj3.shape
    tq, tk = min(tq, t), min(tk, t)
    kck, kcvt, ksk, ksvt, kwk, kwvt = prep
    n_ch = kck.shape[1]
    n_cmp = n_ch - L_CMP // D_CMP + 1
    n_sb = -(-t // L_SLC)
    half = KV_B // 2
    mm = jnp.asarray(_cmp_to_sel_map(n_ch, n_cmp, n_sb))
    per_b = lambda shape: pl.BlockSpec((None,) + shape, lambda i, j: (i, 0, 0))
    return pl.pallas_call(
        functools.partial(_nsa_prompt_kernel, tq, tk, n_sb),
        grid=(bsz, t // tq),
        in_specs=[pl.BlockSpec((None, tq, MIX_B), lambda i, j: (i, j, col_qb // MIX_B)),
                  pl.BlockSpec((None, tq, 128), lambda i, j: (i, j, col_gb // 128)),
                  per_b((n_ch, half)), per_b((half, n_ch)), per_b((t, KV_B)), per_b((half, t)),
                  per_b((t, half)), per_b((half, t)),
                  pl.BlockSpec((n_sb, n_ch), lambda i, j: (0, 0))],
        out_specs=pl.BlockSpec((None, tq, MIX_B), lambda i, j: (i, j, 0)),
        out_shape=jax.ShapeDtypeStruct((bsz, t, MIX_B), F32),
        compiler_params=_cparams(("parallel", "parallel"), 40),
        name="nsa_prompt",
    )(proj3, proj3, kck, kcvt, ksk, ksvt, kwk, kwvt, mm)


def _pages_view(cache):
    n_phys, depth, page = cache.shape[:3]
    return jnp.transpose(cache, (0, 1, 3, 4, 5, 2)).reshape(n_phys, depth, 2, KV_B // 2, page)


def _cmp_pages_kernel(n_pg, page, *refs):
    pages, w1_ref, o_ref, xs_ref = refs[2:2 + n_pg], refs[2 + n_pg], refs[3 + n_pg], refs[4 + n_pg]
    half = KV_B // 2
    ch_pg = page // D_CMP
    n_ch = n_pg * ch_pg
    r = lax.broadcasted_iota(jnp.int32, (page, page), 0)
    pos = lax.broadcasted_iota(jnp.int32, (page, page), 1)
    pick = jnp.where(pos == (r % ch_pg) * D_CMP + r // ch_pg, 1.0, 0.0).astype(BF16)
    for i, pg in enumerate(pages):
        kv_t = pg[...].reshape(KV_B, page).astype(BF16)
        y = lax.dot_general(pick, kv_t, (((1,), (1,)), ((), ())), preferred_element_type=F32)
        for e in range(2):
            for j in range(D_CMP):
                xs_ref[e, j, i * ch_pg:(i + 1) * ch_pg, :] = y[j * ch_pg:(j + 1) * ch_pg, e * half:(e + 1) * half]
    for e in range(2):
        acc = jnp.zeros((n_ch, 2 * half), F32)
        for j in range(D_CMP):
            acc = acc + jnp.dot(xs_ref[e, j].astype(BF16), w1_ref[e, j], preferred_element_type=F32)
        o_ref[:, e * half:(e + 1) * half] = acc[:, :half]
        o_ref[:, KV_B + e * half:KV_B + (e + 1) * half] = acc[:, half:]


def _cmp_pages(pages, layer, page_table, w1e):
    n_phys, depth, _, _, page = pages.shape
    bsz, n_pages = page_table.shape
    ch_pg = page // D_CMP
    n_pg = next(c for c in (32, 16, 8, 4, 2, 1) if n_pages % c == 0)

    def page_spec(i):
        return pl.BlockSpec((None, None, 2, KV_B // 2, page),
                            lambda b, s, pt, lyr: (pt[b * n_pages + s * n_pg + i], lyr[0], 0, 0, 0))

    grid_spec = pltpu.PrefetchScalarGridSpec(
        num_scalar_prefetch=2,
        grid=(bsz, n_pages // n_pg),
        in_specs=[page_spec(i) for i in range(n_pg)] + [pl.BlockSpec(w1e.shape, lambda b, s, pt, lyr: (0, 0, 0, 0))],
        out_specs=pl.BlockSpec((None, n_pg * ch_pg, 2 * KV_B), lambda b, s, pt, lyr: (b, s, 0)),
        scratch_shapes=[pltpu.VMEM((2, D_CMP, n_pg * ch_pg, KV_B // 2), F32)])
    return pl.pallas_call(
        functools.partial(_cmp_pages_kernel, n_pg, page),
        grid_spec=grid_spec,
        out_shape=jax.ShapeDtypeStruct((bsz, n_pages * ch_pg, 2 * KV_B), F32),
        compiler_params=_cparams(("parallel", "parallel"), 48),
        name="nsa_cmp_pages",
    )(page_table.reshape(-1), jnp.full((1,), layer, jnp.int32), *([pages] * n_pg), w1e)


def _nsa_score_kernel(past, sd, n_cmp, n_sb, pp_ref, q_ref, cw_ref, nw_ref, b1_ref, w2_ref, b2_ref, mm_ref,
                      oc_ref, ow_ref, idx_ref):
    half = KV_B // 2
    rq = HG_B * sd
    scale = HD_B ** -0.5
    n_ch = pp_ref.shape[0]
    win = cw_ref.shape[2]
    kc = _compress_tail(pp_ref[...], n_cmp, b1_ref, w2_ref, b2_ref)
    kck, kcv = kc[:, :half].astype(BF16), kc[:, half:].astype(BF16)
    qpos = past + lax.broadcasted_iota(jnp.int32, (rq, 1), 0) % sd
    nt = (((1,), (1,)), ((), ()))

    def softmax(parts):
        m = functools.reduce(jnp.maximum, [jnp.max(jnp.where(mk, s, NEG_BIG), axis=-1, keepdims=True) for s, mk in parts])
        es = [jnp.where(mk, jnp.exp(s - m), 0.0) for s, mk in parts]
        d = functools.reduce(jnp.add, [jnp.sum(e, axis=-1, keepdims=True) for e in es])
        return [e / jnp.where(d > 0, d, 1.0) for e in es]

    imps = []
    for g in range(G_B):
        qg = q_ref[g * rq:(g + 1) * rq, :]
        s = lax.dot_general(qg, kck, nt, preferred_element_type=F32) * scale
        c_end = lax.broadcasted_iota(jnp.int32, (rq, n_ch), 1) * D_CMP + (L_CMP - 1)
        p, = softmax([(s, c_end <= qpos)])
        oc_ref[g * rq:(g + 1) * rq, :] = jnp.dot(p.astype(BF16), kcv, preferred_element_type=F32)
        imps.append(functools.reduce(jnp.add, [p[hg * sd:(hg + 1) * sd, :] for hg in range(HG_B)]))
        s1 = jnp.dot(qg, cw_ref[0].astype(BF16), preferred_element_type=F32) * scale
        s2 = lax.dot_general(qg, nw_ref[:, :half].astype(BF16), nt, preferred_element_type=F32) * scale
        wp1 = past - win + lax.broadcasted_iota(jnp.int32, s1.shape, 1)
        j2 = lax.broadcasted_iota(jnp.int32, s2.shape, 1)
        wp2 = past + j2
        p1, p2 = softmax([(s1, (wp1 <= qpos) & (wp1 > qpos - WINDOW) & (wp1 >= 0)),
                          (s2, (wp2 <= qpos) & (wp2 > qpos - WINDOW) & (j2 < sd))])
        ow_ref[g * rq:(g + 1) * rq, :] = (
            lax.dot_general(p1.astype(BF16), cw_ref[1].astype(BF16), nt, preferred_element_type=F32)
            + jnp.dot(p2.astype(BF16), nw_ref[:, half:].astype(BF16), preferred_element_type=F32))
    imp = jnp.concatenate(imps, axis=0)
    rows = G_B * sd
    bs = jnp.dot(imp, mm_ref[...], precision=HIGHEST, preferred_element_type=F32)
    n_lane = bs.shape[1]
    blk = lax.broadcasted_iota(jnp.int32, (rows, n_lane), 1)
    cur = (past + lax.broadcasted_iota(jnp.int32, (rows, 1), 0) % sd) >> 6
    forced = (blk == 0) | (blk == cur) | (blk == cur - 1)
    score = jnp.where(blk <= cur, jnp.where(forced, jnp.inf, bs), -jnp.inf)
    blk_f = blk.astype(F32)
    taken = blk >= n_sb
    lane = lax.broadcasted_iota(jnp.int32, (rows, 128), 1)
    picked = jnp.zeros((rows, 128), F32)
    for it in range(min(N_SEL, n_sb)):
        live = jnp.where(taken, -jnp.inf, score)
        mx = jnp.max(live, axis=-1, keepdims=True)
        ix = jnp.min(jnp.where(~taken & (score == mx), blk_f, float(n_lane)), axis=-1, keepdims=True)
        picked = jnp.where(lane == it, ix, picked)
        taken = taken | (blk_f == ix)
    idx_ref[...] = picked.astype(jnp.int32)


def _nsa_select_kernel(past, sd, n_sb, n_pages, k_sel, *refs):
    pt_ref, ix_ref, lyr_ref = refs[:3]
    pages = refs[3:3 + k_sel]
    new_ref, q_ref, oc_ref, ow_ref, gt_ref, o_ref = refs[3 + k_sel:]
    rq = HG_B * sd
    page = new_ref.shape[-1]
    blk_pg = page // L_SLC
    b, g, i = pl.program_id(0), pl.program_id(1), pl.program_id(2)
    base = ((b * G_B + g) * sd + i) * k_sel
    lane = lax.broadcasted_iota(jnp.int32, (1, k_sel * page), 1)
    in_page = lane % page
    kpos = in_page
    picked = lane < 0
    kts, vts = [], []
    for j in range(k_sel):
        blk = ix_ref[base + j]
        is_new = blk == n_sb - 1
        kts.append(jnp.where(is_new, new_ref[0], pages[j][0]))
        vts.append(jnp.where(is_new, new_ref[1], pages[j][1]))
        mine = lane // page == j
        kpos = kpos + jnp.where(mine, (blk // blk_pg) * page, 0)
        picked = picked | (mine & (in_page // L_SLC == blk % blk_pg))
    kt_all = jnp.concatenate(kts, axis=1).astype(BF16)
    vt_all = jnp.concatenate(vts, axis=1).astype(BF16)
    s = jnp.dot(q_ref[...], kt_all, preferred_element_type=F32) * (HD_B ** -0.5)
    row_q = lax.broadcasted_iota(jnp.int32, (rq, 1), 0) % sd
    mask = picked & (kpos <= past + row_q)
    m = jnp.max(jnp.where(mask, s, NEG_BIG), axis=-1, keepdims=True)
    e = jnp.where(mask, jnp.exp(s - m), 0.0)
    d = jnp.sum(e, axis=-1, keepdims=True)
    p = e / jnp.where(d > 0, d, 1.0)
    o_s = lax.dot_general(p.astype(BF16), vt_all, (((1,), (1,)), ((), ())), preferred_element_type=F32)
    gates = _sigmoid(gt_ref[...])
    y = gates[0] * oc_ref[...] + gates[1] * o_s + gates[2] * ow_ref[...]

    @pl.when(i == 0)
    def _():
        o_ref[...] = jnp.zeros_like(o_ref)

    o_ref[...] += jnp.where(row_q == i, y, 0.0)


def _nsa_sample(proj3, cols, kvs, sd, cmp_big, pages_cmp, pages_slc, win_view, layer, page_table):
    col_qb, col_kv, col_gb = cols
    bsz = proj3.shape[0]
    n_phys, depth, _, _, page = pages_cmp.shape
    n_pages = page_table.shape[1]
    past = n_pages * page
    win = win_view.shape[-1]
    half = KV_B // 2
    rq = HG_B * sd
    n_ch = past // D_CMP
    n_cmp = (past + sd) // D_CMP - L_CMP // D_CMP + 1
    n_sb = -(-(past + sd) // L_SLC)
    k_sel = min(N_SEL, n_sb)
    assert (past + sd) // D_CMP == n_ch and past % L_SLC == 0 and sd <= L_SLC and page % L_SLC == 0
    _, b1big, w2big, b2big, w1tok = cmp_big
    pp = _cmp_pages(pages_cmp, layer, page_table, w1tok)
    q = proj3[:, :sd, col_qb:col_qb + MIX_B].reshape(bsz, sd, G_B, HG_B, HD_B).transpose(0, 2, 3, 1, 4)
    q = jnp.stack([jnp.pad(q[:, g], ((0, 0), (0, 0), (0, 0), (g * HD_B, half - (g + 1) * HD_B))) for g in range(G_B)], 1)
    q = q.reshape(bsz, G_B * rq, half).astype(BF16)
    gt = proj3[:, :sd, col_gb:col_gb + 3 * H_B].astype(F32).reshape(bsz, sd, G_B, HG_B, 3).transpose(0, 4, 2, 3, 1)
    gt = jnp.broadcast_to(gt.reshape(bsz, 3, G_B * rq, 1), (bsz, 3, G_B * rq, half))
    n_lane = -(-n_sb // 128) * 128
    mm = jnp.asarray(np.pad(_cmp_to_sel_map(n_ch, n_cmp, n_sb), ((0, n_lane - n_sb), (0, 0))).T)
    per_b = lambda shape: pl.BlockSpec((None,) + shape, lambda i: (i,) + (0,) * len(shape))
    const = lambda shape: pl.BlockSpec(shape, lambda i: (0,) * len(shape))
    o_c, o_w, idx = pl.pallas_call(
        functools.partial(_nsa_score_kernel, past, sd, n_cmp, n_sb),
        grid=(bsz,),
        in_specs=[per_b((n_ch, 2 * KV_B)), per_b((G_B * rq, half)),
                  pl.BlockSpec((None, None, 2, half, win), lambda i: (layer, i, 0, 0, 0)),
                  pl.BlockSpec((None, PAD_ROWS, KV_B), lambda i: (i, 0, col_kv // KV_B + 2)),
                  const((1, KV_B)), const((KV_B, KV_B)), const((1, KV_B)), const((n_ch, n_lane))],
        out_specs=[per_b((G_B * rq, half)), per_b((G_B * rq, half)), per_b((G_B * sd, 128))],
        out_shape=[jax.ShapeDtypeStruct((bsz, G_B * rq, half), F32), jax.ShapeDtypeStruct((bsz, G_B * rq, half), F32),
                   jax.ShapeDtypeStruct((bsz, G_B * sd, 128), jnp.int32)],
        compiler_params=_cparams(("parallel",), 40),
        name="nsa_score_sample",
    )(pp, q, win_view, proj3, b1big, w2big, b2big, mm)
    blk_pg = page // L_SLC
    n_cached = past // L_SLC
    new_blk = jnp.pad(kvs.reshape(bsz, sd, 2, half).transpose(0, 2, 3, 1), ((0, 0), (0, 0), (0, 0), (0, page - sd)))

    def blk_spec(j):
        def index(b, g, i, pt, ix, lyr):
            blk = jnp.minimum(ix[((b * G_B + g) * sd + i) * k_sel + j], n_cached - 1)
            return pt[b * n_pages + blk // blk_pg], lyr[0], 0, 0, 0
        return pl.BlockSpec((None, None, 2, half, page), index)

    grp = lambda b, g, i, pt, ix, lyr: (b, g, 0)
    grid_spec = pltpu.PrefetchScalarGridSpec(
        num_scalar_prefetch=3,
        grid=(bsz, G_B, sd),
        in_specs=[blk_spec(j) for j in range(k_sel)] + [
            pl.BlockSpec((None, 2, half, page), lambda b, g, i, pt, ix, lyr: (b, 0, 0, 0)),
            pl.BlockSpec((None, rq, half), grp), pl.BlockSpec((None, rq, half), grp), pl.BlockSpec((None, rq, half), grp),
            pl.BlockSpec((None, 3, rq, half), lambda b, g, i, pt, ix, lyr: (b, 0, g, 0))],
        out_specs=pl.BlockSpec((None, rq, half), grp))
    y = pl.pallas_call(
        functools.partial(_nsa_select_kernel, past, sd, n_sb, n_pages, k_sel),
        grid_spec=grid_spec,
        out_shape=jax.ShapeDtypeStruct((bsz, G_B * rq, half), F32),
        compiler_params=_cparams(("parallel", "parallel", "arbitrary"), 40),
        name="nsa_select_sample",
    )(page_table.reshape(-1), idx[:, :, :k_sel].reshape(-1), jnp.full((1,), layer, jnp.int32),
      *([pages_slc] * k_sel), new_blk, q, o_c, o_w, gt)
    y = y.reshape(bsz, G_B, HG_B, sd, G_B, HD_B)
    y = jnp.stack([y[:, g, :, :, g] for g in range(G_B)], axis=1)
    return y.transpose(0, 3, 1, 2, 4).reshape(bsz, sd, MIX_B)


def _prep_w_in(w, d):
    o = np.cumsum([0, MIX_A, MIX_A, MIX_A, MIX_A, MIX_B, KV_B, KV_B, KV_B, 3 * H_B, 2 * C_CONV, 3 * d])
    parts = [w[:, o[0]:o[4]], w[:, o[10]:o[11]], w[:, o[4]:o[5]], w[:, o[9]:o[10]], w[:, o[5]:o[8]], w[:, o[8]:o[9]]]
    n = sum(p.shape[1] for p in parts)
    n_pad = -(-n // 512) * 512
    parts.append(jnp.zeros((w.shape[0], n_pad - n), w.dtype))
    return jnp.concatenate(parts, axis=1).astype(BF16)


def _layer(x3, n_valid, p, s0, conv_prefix, mem_kv, nsa_fn, alpha):
    bsz, t, d = x3.shape
    m = bsz * t
    col_qb = COL_MG + 3 * d
    col_glu = col_qb + MIX_B
    col_kv = col_glu + 2 * C_CONV
    col_gb = col_kv + 3 * KV_B
    proj = _matmul(x3.reshape(m, d), p['w_in'], 1024, 1536, out_dtype=BF16)
    proj3 = proj.reshape(bsz, t, -1)
    ya, s_new = _hgrn(proj3, p['lb'], p['hg_norm'], s0, n_valid)
    yc, conv_state = _conv(proj3, col_glu // C_CONV, conv_prefix, p['conv_w'], p['conv_b'],
                           p['conv_ln_g'], p['conv_ln_b'], n_valid)
    kv_shape = (bsz, n_valid, 2, G_B, HD_B)
    kvc = proj3[:, :n_valid, col_kv:col_kv + KV_B].astype(F32)
    kvs = proj3[:, :n_valid, col_kv + KV_B:col_kv + 2 * KV_B].astype(F32)
    kvw = proj3[:, :n_valid, col_kv + 2 * KV_B:col_kv + 3 * KV_B].astype(F32)
    yb, nsa_extra = nsa_fn(proj3, (col_qb, col_kv, col_gb), kvc, kvs, kvw)
    kvc, kvs = kvc.reshape(kv_shape), kvs.reshape(kv_shape)
    x1 = _merge(ya.reshape(m, MIX_A), yb.reshape(m, MIX_B), yc.reshape(m, C_CONV), proj, x3.reshape(m, d),
                p['w_pa'], p['w_pb'], p['w_pc'], p['w_out'], p['ln_g'][0:1], p['ln_b'][0:1], alpha)
    x2 = _xattn(x1.reshape(bsz, t, d), mem_kv[0], mem_kv[1], p['w_xq'], p['w_xo'], p['ln_g'][1:2], p['ln_b'][1:2], alpha)
    x3n = _mlp(x2.reshape(m, d), p['w_up'], p['w_down'], p['ln_g'][2:3], p['ln_b'][2:3], alpha)
    return x3n.reshape(bsz, t, d), kvc, kvs, nsa_extra, s_new, conv_state


def kernel(x_prompt, x_sample, cache_cmp, cache_slc, cache_win, state_hgrn, state_conv, cache_mem, page_table, mem_prompt, w_in, lb_raw, hg_norm, w_cmp1, b_cmp1, w_cmp2, b_cmp2, conv_w, conv_b, conv_ln_g, conv_ln_b, w_pa, w_pb, w_pc, w_out, ln_g, ln_b, w_xq, w_xkv, w_xo, w_up, w_down):
    bp, t, d = x_prompt.shape
    bd, sd = x_sample.shape[:2]
    depth = w_in.shape[0]
    n_mem = mem_prompt.shape[1]
    win_buf = cache_win.shape[2]
    alpha = (2 * depth) ** 0.25
    lb_cum = jnp.cumsum(jax.nn.softmax(lb_raw.astype(F32), axis=0), axis=0)
    lb_all = lb_cum - lb_cum[0]
    pages_cmp, pages_slc, win_view = _pages_view(cache_cmp), _pages_view(cache_slc), _pages_view(cache_win)
    mem_cache = _mem_tile_order(cache_mem, d)
    xp = x_prompt
    xs = jnp.pad(x_sample, ((0, 0), (0, PAD_ROWS - sd), (0, 0)))
    outs = {k: [] for k in ('cmp_p', 'cmp_s', 'slc_p', 'slc_s', 'win_p', 'win_s', 'hg_p', 'hg_s', 'cv_p', 'cv_s', 'mem_p')}
    for l in range(depth):
        p = {'w_in': _prep_w_in(w_in[l], d), 'lb': lb_all[l], 'hg_norm': hg_norm[l], 'conv_w': conv_w[l],
             'conv_b': conv_b[l].reshape(1, -1), 'conv_ln_g': conv_ln_g[l].reshape(1, -1),
             'conv_ln_b': conv_ln_b[l].reshape(1, -1),
             'w_pa': w_pa[l].astype(BF16), 'w_pb': w_pb[l].astype(BF16), 'w_pc': w_pc[l].astype(BF16),
             'w_out': w_out[l].astype(BF16), 'ln_g': ln_g[l], 'ln_b': ln_b[l],
             'w_xq': w_xq[l].astype(BF16), 'w_xo': w_xo[l].astype(BF16),
             'w_up': w_up[l].astype(BF16), 'w_down': w_down[l].astype(BF16)}
        cmp_w = (w_cmp1[l], b_cmp1[l], w_cmp2[l], b_cmp2[l])

        cmp_big = _cmp_weights(*cmp_w)

        def nsa_prompt(proj3, cols, kvc, kvs, kvw):
            col_qb, col_kv, col_gb = cols
            prep = _nsa_prep(proj3, col_kv, cmp_big, min(win_buf, t))
            o = _nsa_prompt(proj3, col_qb, col_gb, prep[:6])
            ct, st_, wt = prep[6:]
            return o, (ct, st_, jnp.pad(wt, ((0, 0), (0, 0), (max(win_buf - t, 0), 0))))

        def nsa_sample(proj3, cols, kvc, kvs, kvw):
            o = _nsa_sample(proj3, cols, kvs, sd, cmp_big, pages_cmp, pages_slc, win_view, l, page_table)
            o = jnp.pad(o, ((0, 0), (0, PAD_ROWS - sd), (0, 0)))
            win = jnp.concatenate([cache_win[l], kvw.reshape(bd, sd, 2, G_B, HD_B)], axis=1)[:, -win_buf:]
            return o, win

        w_kv = _mem_tile_order(w_xkv[l].reshape(d, 2, NX_H, d // NX_H), d).astype(BF16)
        mem_kv = _matmul(mem_prompt.reshape(bp * n_mem, d), w_kv, 1024, 512).reshape(1, bp, n_mem, 2 * d)
        xp, kc, ks_, wn, sh, cv = _layer(xp, t, p, jnp.zeros((bp, H_A, DK_A, DV_A), F32),
                                         jnp.zeros((bp, CONV_K - 1, C_CONV), F32), (mem_kv, 0), nsa_prompt, alpha)
        outs['cmp_p'].append(wn[0]); outs['slc_p'].append(wn[1]); outs['win_p'].append(wn[2])
        outs['hg_p'].append(sh); outs['cv_p'].append(cv)
        outs['mem_p'].append(_mem_head_order(mem_kv[0], d))
        xs, kc, ks_, wn, sh, cv = _layer(xs, sd, p, state_hgrn[l], state_conv[l], (mem_cache, l), nsa_sample, alpha)
        outs['cmp_s'].append(kc); outs['slc_s'].append(ks_); outs['win_s'].append(wn)
        outs['hg_s'].append(sh); outs['cv_s'].append(cv)
    st = lambda k, ax: jnp.stack(outs[k], axis=ax)

    def rows(k, ax):
        a = st(k, ax)
        a = a.reshape(a.shape[:2] + (2, G_B, HD_B, a.shape[-1]))
        return jnp.transpose(a, (0, 1, 5, 2, 3, 4))

    return (xp, xs[:, :sd],
            rows('cmp_p', 1), st('cmp_s', 1), rows('slc_p', 1), st('slc_s', 1),
            rows('win_p', 0), st('win_s', 0), st('hg_p', 0), st('hg_s', 0),
            st('cv_p', 0), st('cv_s', 0), st('mem_p', 0))
```

```python
import functools

import numpy as np
import jax
import jax.numpy as jnp
from jax import lax
from jax.experimental import pallas as pl
from jax.experimental.pallas import tpu as pltpu

F32 = jnp.float32
BF16 = jnp.bfloat16
HIGHEST = lax.Precision.HIGHEST

H_A, DK_A, DV_A, CHUNK_A = 4, 128, 128, 16
H_B, G_B, HG_B, HD_B = 8, 2, 4, 64
L_CMP, D_CMP, L_SLC, N_SEL, WINDOW = 32, 16, 64, 16, 512
C_CONV, CONV_K = 512, 31
NX_H = 4
LN_EPS, RMS_EPS = 1e-5, 1e-6
PAD_ROWS = 16
LANES = 128

MIX_A = H_A * DK_A
MIX_B = H_B * HD_B
KV_B = 2 * G_B * HD_B
COL_QA, COL_FA, COL_IA, COL_GA = 0, 512, 1024, 1536
COL_MG = 2048
V7X_VMEM_LIMIT = 56 * 2**20


def _cparams(sem, vmem_mb=None):
    return pltpu.CompilerParams(dimension_semantics=sem,
                                vmem_limit_bytes=None if vmem_mb is None else vmem_mb * 2**20)


def _ln(y, g, b):
    mu = jnp.mean(y, axis=-1, keepdims=True)
    d = y - mu
    var = jnp.mean(d * d, axis=-1, keepdims=True)
    return d * lax.rsqrt(var + LN_EPS) * g + b


def _sigmoid(x):
    return 1.0 / (1.0 + jnp.exp(-x))


def _mm_kernel(x_ref, w_ref, o_ref, xb_ref):
    @pl.when(pl.program_id(1) == 0)
    def _():
        xb_ref[...] = x_ref[...].astype(BF16)

    o_ref[...] = jnp.dot(xb_ref[...], w_ref[...], preferred_element_type=F32).astype(o_ref.dtype)


def _matmul(x, w, tm, tn, out_dtype=F32):
    m, k = x.shape
    n = w.shape[1]
    tm, tn = min(tm, m), min(tn, n)
    return pl.pallas_call(
        _mm_kernel,
        grid=(m // tm, n // tn),
        in_specs=[pl.BlockSpec((tm, k), lambda i, j: (i, 0)),
                  pl.BlockSpec((k, tn), lambda i, j: (0, j))],
        out_specs=pl.BlockSpec((tm, tn), lambda i, j: (i, j)),
        out_shape=jax.ShapeDtypeStruct((m, n), out_dtype),
        scratch_shapes=[pltpu.VMEM((tm, k), BF16)],
        compiler_params=_cparams(("parallel", "arbitrary"), 40),
        name="proj_matmul",
    )(x, w)


def _mlp_kernel(alpha, x_ref, wu_ref, wd_ref, g_ref, b_ref, o_ref, xb_ref, acc_ref):
    j = pl.program_id(1)

    @pl.when(j == 0)
    def _():
        xb_ref[...] = x_ref[...].astype(BF16)
        acc_ref[...] = jnp.zeros_like(acc_ref)

    h = jnp.dot(xb_ref[...], wu_ref[...], preferred_element_type=F32)
    h = jnp.square(jnp.maximum(h, 0.0)).astype(BF16)
    acc_ref[...] += jnp.dot(h, wd_ref[...], preferred_element_type=F32)

    @pl.when(j == pl.num_programs(1) - 1)
    def _():
        o_ref[...] = _ln(alpha * x_ref[...] + acc_ref[...], g_ref[...], b_ref[...])


def _mlp(x, w_up, w_down, g, b, alpha, tm=512, tf=2048):
    m, d = x.shape
    ff = w_up.shape[1]
    tm = min(tm, m)
    return pl.pallas_call(
        functools.partial(_mlp_kernel, alpha),
        grid=(m // tm, ff // tf),
        in_specs=[pl.BlockSpec((tm, d), lambda i, j: (i, 0)),
                  pl.BlockSpec((d, tf), lambda i, j: (0, j)),
                  pl.BlockSpec((tf, d), lambda i, j: (j, 0)),
                  pl.BlockSpec((1, d), lambda i, j: (0, 0)),
                  pl.BlockSpec((1, d), lambda i, j: (0, 0))],
        out_specs=pl.BlockSpec((tm, d), lambda i, j: (i, 0)),
        out_shape=jax.ShapeDtypeStruct((m, d), F32),
        scratch_shapes=[pltpu.VMEM((tm, d), BF16), pltpu.VMEM((tm, d), F32)],
        compiler_params=_cparams(("parallel", "arbitrary"), 48),
        name="mlp",
    )(x, w_up, w_down, g, b)


def _merge_kernel(alpha, ya_ref, yb_ref, yc_ref, ma_ref, mb_ref, mc_ref, x_ref,
                  wpa_ref, wpb_ref, wpc_ref, wout_ref, g_ref, b_ref, o_ref):
    def branch(y_ref, m_ref, w_ref):
        return _sigmoid(m_ref[...].astype(F32)) * jnp.dot(y_ref[...].astype(BF16), w_ref[...], preferred_element_type=F32)

    merged = branch(ya_ref, ma_ref, wpa_ref) + branch(yb_ref, mb_ref, wpb_ref) + branch(yc_ref, mc_ref, wpc_ref)
    y = jnp.dot(merged.astype(BF16), wout_ref[...], preferred_element_type=F32)
    o_ref[...] = _ln(alpha * x_ref[...] + y, g_ref[...], b_ref[...])


def _merge(ya, yb, yc, proj, x, wpa, wpb, wpc, wout, g, b, alpha, tm=512):
    m, d = x.shape
    tm = min(tm, m)
    mg0 = COL_MG // d
    row = lambda i: (i, 0)
    const = lambda i: (0, 0)
    return pl.pallas_call(
        functools.partial(_merge_kernel, alpha),
        grid=(m // tm,),
        in_specs=[pl.BlockSpec((tm, MIX_A), row), pl.BlockSpec((tm, MIX_B), row), pl.BlockSpec((tm, C_CONV), row),
                  pl.BlockSpec((tm, d), lambda i: (i, mg0)), pl.BlockSpec((tm, d), lambda i: (i, mg0 + 1)),
                  pl.BlockSpec((tm, d), lambda i: (i, mg0 + 2)),
                  pl.BlockSpec((tm, d), row),
                  pl.BlockSpec((MIX_A, d), const), pl.BlockSpec((MIX_B, d), const), pl.BlockSpec((C_CONV, d), const),
                  pl.BlockSpec((d, d), const), pl.BlockSpec((1, d), const), pl.BlockSpec((1, d), const)],
        out_specs=pl.BlockSpec((tm, d), row),
        out_shape=jax.ShapeDtypeStruct((m, d), F32),
        compiler_params=_cparams(("parallel",), 48),
        name="merge_out",
    )(ya, yb, yc, proj, proj, proj, x, wpa, wpb, wpc, wout, g, b)


def _xattn_kernel(alpha, x_ref, kv_ref, wq_ref, wo_ref, g_ref, b_ref, o_ref):
    x = x_ref[...]
    d = x.shape[-1]
    hd = d // NX_H
    q = jnp.dot(x.astype(BF16), wq_ref[...], preferred_element_type=F32)
    n_dt = hd // LANES

    def head(base, h):
        parts = [kv_ref[:, base + (dt * NX_H + h) * LANES:base + (dt * NX_H + h + 1) * LANES] for dt in range(n_dt)]
        return jnp.concatenate(parts, axis=1).astype(BF16)

    outs = []
    for h in range(NX_H):
        qh = q[:, h * hd:(h + 1) * hd].astype(BF16)
        kh, vh = head(0, h), head(d, h)
        s = lax.dot_general(qh, kh, (((1,), (1,)), ((), ())), preferred_element_type=F32) * (hd ** -0.5)
        e = jnp.exp(s - jnp.max(s, axis=-1, keepdims=True))
        p = e / jnp.sum(e, axis=-1, keepdims=True)
        outs.append(jnp.dot(p.astype(BF16), vh, preferred_element_type=F32))
    o = jnp.concatenate(outs, axis=-1)
    y = jnp.dot(o.astype(BF16), wo_ref[...], preferred_element_type=F32)
    o_ref[...] = _ln(alpha * x + y, g_ref[...], b_ref[...])


def _mem_tile_order(a, d):
    lead = a.shape[:-3]
    a = a.reshape(lead + (2, NX_H, d // NX_H // LANES, LANES))
    return jnp.swapaxes(a, -3, -2).reshape(lead + (2 * d,))


def _mem_head_order(a, d):
    lead = a.shape[:-1]
    a = a.reshape(lead + (2, d // NX_H // LANES, NX_H, LANES))
    return jnp.swapaxes(a, -3, -2).reshape(lead + (2, NX_H, d // NX_H))


def _xattn(x, kv, layer, wq, wo, g, b, alpha, tm=512):
    bsz, t, d = x.shape
    n_mem = kv.shape[2]
    tm = min(tm, t)
    const = lambda i, j: (0, 0)
    return pl.pallas_call(
        functools.partial(_xattn_kernel, alpha),
        grid=(bsz, t // tm),
        in_specs=[pl.BlockSpec((None, tm, d), lambda i, j: (i, j, 0)),
                  pl.BlockSpec((None, None, n_mem, 2 * d), lambda i, j: (layer, i, 0, 0)),
                  pl.BlockSpec((d, d), const), pl.BlockSpec((d, d), const),
                  pl.BlockSpec((1, d), const), pl.BlockSpec((1, d), const)],
        out_specs=pl.BlockSpec((None, tm, d), lambda i, j: (i, j, 0)),
        out_shape=jax.ShapeDtypeStruct((bsz, t, d), F32),
        compiler_params=_cparams(("parallel", "parallel"), 48),
        name="xattn",
    )(x, kv, wq, wo, g, b)


CONV_HALO = 32
CONV_SUB = 32


def _conv_kernel(rt, n_valid_last, a_ref, gt_ref, pre_ref, w_ref, cb_ref, g_ref, b_ref, y_ref, st_ref, ue_ref, sh_ref):
    t = pl.program_id(1)
    off = CONV_HALO - (CONV_K - 1)
    sl = 8

    @pl.when(t == 0)
    def _():
        ue_ref[0:off, :] = jnp.zeros((off, C_CONV), F32)
        ue_ref[off:CONV_HALO, :] = pre_ref[...]

    ue_ref[CONV_HALO:CONV_HALO + rt, :] = a_ref[...].astype(F32) * _sigmoid(gt_ref[...].astype(F32))
    n_sh = rt + CONV_HALO - sl
    for s in range(1, sl):
        sh_ref[s - 1, 0:n_sh, :] = ue_ref[s:s + n_sh, :]
    sub = min(CONV_SUB, rt)
    for r0 in range(0, rt, sub):
        acc = cb_ref[...]
        for j in range(CONV_K):
            a, s = divmod(off + j, sl)
            lo = r0 + a * sl
            win = ue_ref[lo:lo + sub, :] if s == 0 else sh_ref[s - 1, lo:lo + sub, :]
            acc = acc + w_ref[j:j + 1, :] * win
        y = _ln(acc, g_ref[...], b_ref[...])
        y_ref[r0:r0 + sub, :] = (y * _sigmoid(y)).astype(y_ref.dtype)

    @pl.when(t == pl.num_programs(1) - 1)
    def _():
        st_ref[...] = ue_ref[off + n_valid_last:off + n_valid_last + CONV_K - 1, :]

    ue_ref[0:CONV_HALO, :] = ue_ref[rt:rt + CONV_HALO, :]


def _conv(proj3, col_a, prefix, w, cb, g, b, n_valid, rt=256):
    bsz, t, _ = proj3.shape
    rt = min(rt, t)
    n_valid_last = n_valid - (t - rt)
    vec = lambda i, j: (0, 0)
    return pl.pallas_call(
        functools.partial(_conv_kernel, rt, n_valid_last),
        grid=(bsz, t // rt),
        in_specs=[pl.BlockSpec((None, rt, C_CONV), lambda i, j: (i, j, col_a)),
                  pl.BlockSpec((None, rt, C_CONV), lambda i, j: (i, j, col_a + 1)),
                  pl.BlockSpec((None, CONV_K - 1, C_CONV), lambda i, j: (i, 0, 0)),
                  pl.BlockSpec((CONV_K, C_CONV), vec), pl.BlockSpec((1, C_CONV), vec),
                  pl.BlockSpec((1, C_CONV), vec), pl.BlockSpec((1, C_CONV), vec)],
        out_specs=[pl.BlockSpec((None, rt, C_CONV), lambda i, j: (i, j, 0)),
                   pl.BlockSpec((None, CONV_K - 1, C_CONV), lambda i, j: (i, 0, 0))],
        out_shape=[jax.ShapeDtypeStruct((bsz, t, C_CONV), BF16),
                   jax.ShapeDtypeStruct((bsz, CONV_K - 1, C_CONV), F32)],
        scratch_shapes=[pltpu.VMEM((rt + CONV_HALO, C_CONV), F32), pltpu.VMEM((7, rt + CONV_HALO - 8, C_CONV), F32)],
        compiler_params=_cparams(("parallel", "arbitrary"), 32),
        name="conformer_conv",
    )(proj3, proj3, prefix, w, cb, g, b)


def _hgrn_kernel(tt, n_valid, q_ref, f_ref, i_ref, g_ref, la_ref, l1_ref, oml_ref, gn_ref, s0_ref,
                 y_ref, s_ref, st_ref, qs_ref, kk_ref, bb_ref, o32_ref):
    c = CHUNK_A
    t = pl.program_id(1)

    @pl.when(t == 0)
    def _():
        for h in range(H_A):
            st_ref[h] = s0_ref[h].T

    z = f_ref[...].astype(F32)
    cc = l1_ref[...] + (jnp.minimum(z, 0.0) - jnp.log1p(jnp.exp(-jnp.abs(z))))
    a = la_ref[...]
    logf = jnp.maximum(a, cc) + jnp.log1p(jnp.exp(-jnp.abs(a - cc)))
    k = oml_ref[...] * _sigmoid(-z)
    row = lax.broadcasted_iota(jnp.int32, (tt, 1), 0)
    if n_valid < tt:
        logf = jnp.where(row < n_valid, logf, 0.0)
        k = jnp.where(row < n_valid, k, 0.0)
    b = logf
    rc = row & (c - 1)
    sh = 1
    while sh < c:
        b = b + jnp.where(rc >= sh, pltpu.roll(b, sh, 0), 0.0)
        sh *= 2
    q = q_ref[...].astype(F32)
    qs_ref[...] = q * _sigmoid(q)
    kk_ref[...] = k
    bb_ref[...] = b

    rowc = lax.broadcasted_iota(jnp.int32, (c, 1), 0)

    def chunk(ci, carry):
        r0 = pl.multiple_of(ci * c, c)
        for h in range(H_A):
            hs = slice(h * DK_A, (h + 1) * DK_A)
            qc = qs_ref[pl.ds(r0, c), hs]
            kc = kk_ref[pl.ds(r0, c), hs]
            bc = bb_ref[pl.ds(r0, c), hs]
            vc = i_ref[pl.ds(r0, c), hs].astype(F32)
            bl = bc[c - 1:c, :]
            st = st_ref[h]
            qe = (qc * jnp.exp(bc)).astype(BF16)
            o = lax.dot_general(qe, st.astype(BF16), (((1,), (1,)), ((), ())), preferred_element_type=F32)
            for s in range(c):
                e = jnp.exp(jnp.where(rowc >= s, bc - bc[s:s + 1, :], -jnp.inf))
                att = jnp.sum(qc * e * kc[s:s + 1, :], axis=-1, keepdims=True)
                o = o + att * vc[s:s + 1, :]
            o32_ref[pl.ds(r0, c), hs] = o
            kd = (kc * jnp.exp(bl - bc)).astype(BF16)
            u = lax.dot_general(vc.astype(BF16), kd, (((0,), (0,)), ((), ())), preferred_element_type=F32)
            st_ref[h] = st * jnp.exp(bl) + u
        return carry

    lax.fori_loop(0, tt // c, chunk, 0, unroll=4 if (tt // c) % 4 == 0 else 1)

    g = g_ref[...].astype(F32)
    gate = g * _sigmoid(g)
    for h in range(H_A):
        hs = slice(h * DV_A, (h + 1) * DV_A)
        o = o32_ref[:, hs]
        o = o * lax.rsqrt(jnp.mean(o * o, axis=-1, keepdims=True) + RMS_EPS) * gn_ref[...]
        y_ref[:, hs] = (o * gate[:, hs]).astype(y_ref.dtype)

    @pl.when(t == pl.num_programs(1) - 1)
    def _():
        for h in range(H_A):
            s_ref[h] = st_ref[h].T


def _hgrn(proj3, lb, gnorm, s0, n_valid, tt=256):
    bsz, t, _ = proj3.shape
    tt = min(tt, t)
    lb = lb.reshape(1, MIX_A).astype(F32)
    la, l1, oml = jnp.log(lb), jnp.log1p(-lb), 1.0 - lb
    vec = lambda i, j: (0, 0)
    col = lambda cb: pl.BlockSpec((None, tt, MIX_A), lambda i, j: (i, j, cb))
    st = pl.BlockSpec((None, H_A, DK_A, DV_A), lambda i, j: (i, 0, 0, 0))
    return pl.pallas_call(
        functools.partial(_hgrn_kernel, tt, n_valid if t == tt else tt),
        grid=(bsz, t // tt),
        in_specs=[col(0), col(1), col(2), col(3),
                  pl.BlockSpec((1, MIX_A), vec), pl.BlockSpec((1, MIX_A), vec), pl.BlockSpec((1, MIX_A), vec),
                  pl.BlockSpec((1, DV_A), vec), st],
        out_specs=[pl.BlockSpec((None, tt, MIX_A), lambda i, j: (i, j, 0)), st],
        out_shape=[jax.ShapeDtypeStruct((bsz, t, MIX_A), BF16),
                   jax.ShapeDtypeStruct((bsz, H_A, DK_A, DV_A), F32)],
        scratch_shapes=[pltpu.VMEM((H_A, DV_A, DK_A), F32), pltpu.VMEM((tt, MIX_A), F32),
                        pltpu.VMEM((tt, MIX_A), F32), pltpu.VMEM((tt, MIX_A), F32), pltpu.VMEM((tt, MIX_A), F32)],
        compiler_params=_cparams(("parallel", "arbitrary"), 32),
        name="hgrn2",
    )(proj3, proj3, proj3, proj3, la, l1, oml, gnorm.reshape(1, DV_A), s0)


NEG_BIG = -1e30
SEL_PER_CMP = L_SLC // D_CMP


def _cmp_weights(w1, b1, w2, b2):
    m = L_CMP // D_CMP
    eye_e, eye_g = jnp.eye(2, dtype=F32), jnp.eye(G_B, dtype=F32)
    w1r = w1.reshape(2, m, D_CMP, HD_B, HD_B)
    w1big = jnp.einsum('ehjdf,ea,gb->jegdhabf', w1r, eye_e, eye_g).reshape(D_CMP * KV_B, m * KV_B)
    w2big = jnp.einsum('efo,ea,gb->egfabo', w2, eye_e, eye_g).reshape(KV_B, KV_B)
    b1big = jnp.broadcast_to(b1[:, None, :], (2, G_B, HD_B)).reshape(1, KV_B)
    b2big = jnp.broadcast_to(b2[:, None, :], (2, G_B, HD_B)).reshape(1, KV_B)
    w1tok = jnp.einsum('ehjdf,gb->ejgdhbf', w1r, eye_g).reshape(2, D_CMP, KV_B // 2, m * KV_B // 2)
    return w1big.astype(BF16), b1big, w2big.astype(BF16), b2big, w1tok.astype(BF16)


def _cmp_to_sel_map(n_cmp_pad, n_cmp, n_sb):
    mm = np.zeros((n_sb, n_cmp_pad), np.float32)
    for n in range(n_cmp):
        for i in (n, n + 1):
            if i // SEL_PER_CMP < n_sb:
                mm[i // SEL_PER_CMP, n] += 1.0
    return mm


def _compress_tail(pp, n_cmp, b1_ref, w2_ref, b2_ref):
    n_ch = pp.shape[0]
    hid = pp[:, :KV_B] + pltpu.roll(pp[:, KV_B:], n_ch - 1, 0) + b1_ref[...]
    hid = hid * _sigmoid(hid)
    kc = jnp.dot(hid.astype(BF16), w2_ref[...], preferred_element_type=F32) + b2_ref[...]
    row = lax.broadcasted_iota(jnp.int32, (n_ch, 1), 0)
    return jnp.where(row < n_cmp, kc, 0.0)


def _nsa_prep_kernel(n_cmp, win, kc_ref, ks_ref, kw_ref, w1_ref, b1_ref, w2_ref, b2_ref,
                     kck_ref, kcvt_ref, ksk_ref, ksvt_ref, kwk_ref, kwvt_ref, ct_ref, st_ref, wt_ref, xk_ref):
    half = KV_B // 2
    t = kc_ref.shape[0]
    n_ch = t // D_CMP
    for e in range(2):
        xk_ref[e] = kc_ref[:, e * half:(e + 1) * half].astype(F32)
    pp = jnp.zeros((n_ch, w1_ref.shape[2]), F32)
    for j in range(D_CMP):
        for e in range(2):
            x = xk_ref[e, pl.ds(j, n_ch, stride=D_CMP), :].astype(BF16)
            pp = pp + jnp.dot(x, w1_ref[j, e * half:(e + 1) * half, :], preferred_element_type=F32)
    kc = _compress_tail(pp, n_cmp, b1_ref, w2_ref, b2_ref)
    kck_ref[...] = kc[:, :half].astype(BF16)
    kcvt_ref[...] = kc[:, half:].T.astype(BF16)
    ct_ref[...] = kc_ref[...].astype(F32).T
    ks_t = ks_ref[...].astype(F32).T
    st_ref[...] = ks_t
    n_aug = ksk_ref.shape[1] - half
    blk_of_key = lax.broadcasted_iota(jnp.int32, (t, n_aug), 0) // L_SLC
    one_hot = jnp.where(blk_of_key == lax.broadcasted_iota(jnp.int32, (t, n_aug), 1), 1.0, 0.0).astype(BF16)
    ksk_ref[...] = jnp.concatenate([ks_ref[:, :half].astype(BF16), one_hot], axis=1)
    ksvt_ref[...] = ks_t[half:, :].astype(BF16)
    kw_t = kw_ref[...].astype(F32).T
    wt_ref[...] = kw_t[:, t - win:]
    kwk_ref[...] = kw_ref[:, :half].astype(BF16)
    kwvt_ref[...] = kw_t[half:, :].astype(BF16)


def _nsa_prep(proj3, col_kv, cmp_big, win):
    bsz, t, _ = proj3.shape
    n_ch = t // D_CMP
    n_cmp = n_ch - L_CMP // D_CMP + 1
    half = KV_B // 2
    w1big, b1big, w2big, b2big = cmp_big[:4]
    w1pos = w1big.reshape(D_CMP, KV_B, w1big.shape[1])
    cb = col_kv // KV_B
    const = lambda i: (0, 0)
    kspec = lambda n: pl.BlockSpec((None, n, half), lambda i: (i, 0, 0))
    vspec = lambda n: pl.BlockSpec((None, half, n), lambda i: (i, 0, 0))
    fspec = lambda n: pl.BlockSpec((None, KV_B, n), lambda i: (i, 0, 0))
    col = lambda c: pl.BlockSpec((None, t, KV_B), lambda i: (i, 0, cb + c))
    return pl.pallas_call(
        functools.partial(_nsa_prep_kernel, n_cmp, win),
        grid=(bsz,),
        in_specs=[col(0), col(1), col(2),
                  pl.BlockSpec(w1pos.shape, lambda i: (0, 0, 0)), pl.BlockSpec((1, KV_B), const),
                  pl.BlockSpec((KV_B, KV_B), const), pl.BlockSpec((1, KV_B), const)],
        out_specs=[kspec(n_ch), vspec(n_ch), pl.BlockSpec((None, t, KV_B), lambda i: (i, 0, 0)), vspec(t),
                   kspec(t), vspec(t), fspec(t), fspec(t), fspec(win)],
        out_shape=[jax.ShapeDtypeStruct((bsz, n_ch, half), BF16), jax.ShapeDtypeStruct((bsz, half, n_ch), BF16),
                   jax.ShapeDtypeStruct((bsz, t, KV_B), BF16), jax.ShapeDtypeStruct((bsz, half, t), BF16),
                   jax.ShapeDtypeStruct((bsz, t, half), BF16), jax.ShapeDtypeStruct((bsz, half, t), BF16),
                   jax.ShapeDtypeStruct((bsz, KV_B, t), F32), jax.ShapeDtypeStruct((bsz, KV_B, t), F32),
                   jax.ShapeDtypeStruct((bsz, KV_B, win), F32)],
        scratch_shapes=[pltpu.VMEM((2, t, half), F32)],
        compiler_params=_cparams(("parallel",), 48),
        name="nsa_prep",
    )(proj3, proj3, proj3, w1pos, b1big, w2big, b2big)


def _nsa_prompt_kernel(tq, tk, n_sb, q_ref, gb_ref, kck_ref, kcvt_ref, ksk_ref, ksvt_ref, kwk_ref, kwvt_ref, mm_ref, o_ref):
    qi = pl.program_id(1)
    q0 = qi * tq
    n_cp = kck_ref.shape[0]
    w4 = HG_B * tq
    q_t = (q_ref[...].astype(F32) * HD_B ** -0.5).T
    g_t = _sigmoid(gb_ref[...].astype(F32)).T
    qpos = q0 + lax.broadcasted_iota(jnp.int32, (1, tq), 1)
    qpos4 = jnp.concatenate([qpos] * HG_B, axis=1)
    zpad = jnp.zeros((HD_B, tq), BF16)

    def update(state, s, pen, vt, g):
        m, l, acc = state
        if pen is not None:
            s = s + jnp.concatenate([pen] * HG_B, axis=1)
        m_new = jnp.maximum(m, jnp.max(s, axis=0, keepdims=True))
        e = jnp.exp(s - m_new)
        a = jnp.exp(m - m_new)
        l = a * l + jnp.sum(e, axis=0, keepdims=True)
        pv = jnp.dot(vt[g * HD_B:(g + 1) * HD_B, :], e.astype(BF16), preferred_element_type=F32)
        return m_new, l, a * acc + pv

    qts, o_cs, sel_ts = [], [], []
    for g in range(G_B):
        cols = []
        for hg in range(HG_B):
            h = g * HG_B + hg
            qh = q_t[h * HD_B:(h + 1) * HD_B, :].astype(BF16)
            cols.append(jnp.concatenate([qh, zpad] if g == 0 else [zpad, qh], axis=0))
        qt = jnp.concatenate(cols, axis=1)
        qts.append(qt)
        s = jnp.dot(kck_ref[...], qt, preferred_element_type=F32)
        c_end = lax.broadcasted_iota(jnp.int32, (n_cp, w4), 0) * D_CMP + (L_CMP - 1)
        mask = c_end <= qpos4
        m = jnp.max(jnp.where(mask, s, NEG_BIG), axis=0, keepdims=True)
        e = jnp.where(mask, jnp.exp(s - m), 0.0)
        d = jnp.sum(e, axis=0, keepdims=True)
        p = e / jnp.where(d > 0, d, 1.0)
        o_cs.append(jnp.dot(kcvt_ref[...], p.astype(BF16), preferred_element_type=F32)[g * HD_B:(g + 1) * HD_B, :])
        imp = p[:, 0:tq]
        for hg in range(1, HG_B):
            imp = imp + p[:, hg * tq:(hg + 1) * tq]
        bs = jnp.dot(mm_ref[...], imp, precision=HIGHEST, preferred_element_type=F32)
        blk = lax.broadcasted_iota(jnp.int32, (n_sb, tq), 0)
        cur = qpos >> 6
        forced = (blk == 0) | (blk == cur) | (blk == cur - 1)
        score = jnp.where(blk <= cur, jnp.where(forced, jnp.inf, bs), -jnp.inf)
        rank = jnp.zeros((n_sb, tq), jnp.int32)
        for i in range(n_sb):
            si = score[i:i + 1, :]
            rank = rank + jnp.where((si > score) | ((si == score) & (blk > i)), 1, 0)
        sel_ts.append(jnp.where((rank < N_SEL) & (blk <= cur), 1.0, 0.0).astype(BF16))

    n_aug = ksk_ref.shape[1] - 2 * HD_B
    qas = []
    for g in range(G_B):
        pen_rows = ((sel_ts[g].astype(F32) - 1.0) * (-NEG_BIG)).astype(BF16)
        pen_rows = jnp.concatenate([pen_rows, jnp.zeros((n_aug - n_sb, tq), BF16)], axis=0)
        qas.append(jnp.concatenate([qts[g], jnp.concatenate([pen_rows] * HG_B, axis=1)], axis=0))
    krow = lax.broadcasted_iota(jnp.int32, (tk, tq), 0)

    def far_tiles(kt, states):
        k0 = pl.multiple_of(kt * tk, tk)
        k_tile, vt_tile = ksk_ref[pl.ds(k0, tk), :], ksvt_ref[:, pl.ds(k0, tk)]
        return tuple(update(states[g], jnp.dot(k_tile, qas[g], preferred_element_type=F32), None, vt_tile, g)
                     for g in range(G_B))

    def near_tiles(kt, states):
        k0 = pl.multiple_of(kt * tk, tk)
        rel = qpos - k0
        causal = krow <= rel
        pen_causal = jnp.where(causal, 0.0, NEG_BIG)
        pen_win = jnp.where(causal & (krow > rel - WINDOW), 0.0, NEG_BIG)
        k_tile, vt_tile = ksk_ref[pl.ds(k0, tk), :], ksvt_ref[:, pl.ds(k0, tk)]
        kw_tile, vwt_tile = kwk_ref[pl.ds(k0, tk), :], kwvt_ref[:, pl.ds(k0, tk)]
        new_sel = tuple(update(states[g], jnp.dot(k_tile, qas[g], preferred_element_type=F32),
                               pen_causal, vt_tile, g) for g in range(G_B))
        new_win = tuple(update(states[G_B + g], jnp.dot(kw_tile, qts[g], preferred_element_type=F32),
                               pen_win, vwt_tile, g) for g in range(G_B))
        return new_sel + new_win

    init = (jnp.full((1, w4), NEG_BIG, F32), jnp.zeros((1, w4), F32), jnp.zeros((HD_B, w4), F32))

    def looped(_):
        first_near = jnp.maximum((q0 - WINDOW) // tk, 0)
        st = lax.fori_loop(0, first_near, far_tiles, (init,) * G_B)
        return lax.fori_loop(first_near, (q0 + tq) // tk, near_tiles, st + (init,) * G_B)

    def straight(_):
        n_w = WINDOW // tk
        lane_q = lax.broadcasted_iota(jnp.int32, (tk, tq), 1)
        pen_oldest = jnp.where(krow > lane_q, 0.0, NEG_BIG)
        pen_diag = jnp.where(krow <= lane_q, 0.0, NEG_BIG)

        def sel(kt, st, pen):
            k0 = pl.multiple_of(kt * tk, tk)
            k_tile, vt_tile = ksk_ref[pl.ds(k0, tk), :], ksvt_ref[:, pl.ds(k0, tk)]
            return tuple(update(st[g], jnp.dot(k_tile, qas[g], preferred_element_type=F32), pen, vt_tile, g)
                         for g in range(G_B))

        def win(kt, st, pen):
            k0 = pl.multiple_of(kt * tk, tk)
            kw_tile, vwt_tile = kwk_ref[pl.ds(k0, tk), :], kwvt_ref[:, pl.ds(k0, tk)]
            return tuple(update(st[g], jnp.dot(kw_tile, qts[g], preferred_element_type=F32), pen, vwt_tile, g)
                         for g in range(G_B))

        n_old = qi
        s_st = lax.fori_loop(0, n_old // 2, lambda i, st: sel(2 * i + 1, sel(2 * i, st, None), None), (init,) * G_B)
        s_st = lax.fori_loop(2 * (n_old // 2), n_old, lambda kt, st: sel(kt, st, None), s_st)
        w_st = win(qi - n_w, (init,) * G_B, pen_oldest)
        for d in range(n_w - 1, 0, -1):
            w_st = win(qi - d, w_st, None)
        return sel(qi, s_st, pen_diag) + win(qi, w_st, pen_diag)

    if tq == tk and WINDOW % tk == 0:
        states = lax.cond(qi >= WINDOW // tk, straight, looped, 0)
    else:
        states = looped(0)
    finish = lambda st: st[2] / jnp.where(st[1] > 0, st[1], 1.0)
    outs = []
    for g in range(G_B):
        o_c, o_s, o_w = o_cs[g], finish(states[g]), finish(states[G_B + g])
        for hg in range(HG_B):
            c = (g * HG_B + hg) * 3
            sl = slice(hg * tq, (hg + 1) * tq)
            outs.append(g_t[c:c + 1, :] * o_c[:, sl] + g_t[c + 1:c + 2, :] * o_s[:, sl]
                        + g_t[c + 2:c + 3, :] * o_w[:, sl])
    o_ref[...] = jnp.concatenate(outs, axis=0).T.astype(o_ref.dtype)


def _nsa_prompt(proj3, col_qb, col_gb, prep, tq=256, tk=256):
    bsz, t, _ = proj3.shape
    tq, tk = min(tq, t), min(tk, t)
    kck, kcvt, ksk, ksvt, kwk, kwvt = prep
    n_ch = kck.shape[1]
    n_cmp = n_ch - L_CMP // D_CMP + 1
    n_sb = -(-t // L_SLC)
    half = KV_B // 2
    mm = jnp.asarray(_cmp_to_sel_map(n_ch, n_cmp, n_sb))
    per_b = lambda shape: pl.BlockSpec((None,) + shape, lambda i, j: (i, 0, 0))
    return pl.pallas_call(
        functools.partial(_nsa_prompt_kernel, tq, tk, n_sb),
        grid=(bsz, t // tq),
        in_specs=[pl.BlockSpec((None, tq, MIX_B), lambda i, j: (i, j, col_qb // MIX_B)),
                  pl.BlockSpec((None, tq, 128), lambda i, j: (i, j, col_gb // 128)),
                  per_b((n_ch, half)), per_b((half, n_ch)), per_b((t, KV_B)), per_b((half, t)),
                  per_b((t, half)), per_b((half, t)),
                  pl.BlockSpec((n_sb, n_ch), lambda i, j: (0, 0))],
        out_specs=pl.BlockSpec((None, tq, MIX_B), lambda i, j: (i, j, 0)),
        out_shape=jax.ShapeDtypeStruct((bsz, t, MIX_B), BF16),
        compiler_params=_cparams(("parallel", "parallel"), 40),
        name="nsa_prompt",
    )(proj3, proj3, kck, kcvt, ksk, ksvt, kwk, kwvt, mm)


def _pages_view(cache):
    n_phys, depth, page = cache.shape[:3]
    return jnp.transpose(cache, (0, 1, 3, 4, 5, 2)).reshape(n_phys, depth, 2, KV_B // 2, page)


def _cmp_pages_kernel(n_pg, page, *refs):
    pages, w1_ref, o_ref, xs_ref = refs[2:2 + n_pg], refs[2 + n_pg], refs[3 + n_pg], refs[4 + n_pg]
    half = KV_B // 2
    ch_pg = page // D_CMP
    n_ch = n_pg * ch_pg
    r = lax.broadcasted_iota(jnp.int32, (page, page), 0)
    pos = lax.broadcasted_iota(jnp.int32, (page, page), 1)
    pick = jnp.where(pos == (r % ch_pg) * D_CMP + r // ch_pg, 1.0, 0.0).astype(BF16)
    for i, pg in enumerate(pages):
        kv_t = pg[...].reshape(KV_B, page).astype(BF16)
        y = lax.dot_general(pick, kv_t, (((1,), (1,)), ((), ())), preferred_element_type=F32)
        for e in range(2):
            for j in range(D_CMP):
                xs_ref[e, j, i * ch_pg:(i + 1) * ch_pg, :] = y[j * ch_pg:(j + 1) * ch_pg, e * half:(e + 1) * half]
    for e in range(2):
        acc = jnp.zeros((n_ch, 2 * half), F32)
        for j in range(D_CMP):
            acc = acc + jnp.dot(xs_ref[e, j].astype(BF16), w1_ref[e, j], preferred_element_type=F32)
        o_ref[:, e * half:(e + 1) * half] = acc[:, :half]
        o_ref[:, KV_B + e * half:KV_B + (e + 1) * half] = acc[:, half:]


def _cmp_pages(pages, layer, page_table, w1e):
    n_phys, depth, _, _, page = pages.shape
    bsz, n_pages = page_table.shape
    ch_pg = page // D_CMP
    n_pg = next(c for c in (32, 16, 8, 4, 2, 1) if n_pages % c == 0)

    def page_spec(i):
        return pl.BlockSpec((None, None, 2, KV_B // 2, page),
                            lambda b, s, pt, lyr: (pt[b * n_pages + s * n_pg + i], lyr[0], 0, 0, 0))

    grid_spec = pltpu.PrefetchScalarGridSpec(
        num_scalar_prefetch=2,
        grid=(bsz, n_pages // n_pg),
        in_specs=[page_spec(i) for i in range(n_pg)] + [pl.BlockSpec(w1e.shape, lambda b, s, pt, lyr: (0, 0, 0, 0))],
        out_specs=pl.BlockSpec((None, n_pg * ch_pg, 2 * KV_B), lambda b, s, pt, lyr: (b, s, 0)),
        scratch_shapes=[pltpu.VMEM((2, D_CMP, n_pg * ch_pg, KV_B // 2), F32)])
    return pl.pallas_call(
        functools.partial(_cmp_pages_kernel, n_pg, page),
        grid_spec=grid_spec,
        out_shape=jax.ShapeDtypeStruct((bsz, n_pages * ch_pg, 2 * KV_B), F32),
        compiler_params=_cparams(("parallel", "parallel"), 48),
        name="nsa_cmp_pages",
    )(page_table.reshape(-1), jnp.full((1,), layer, jnp.int32), *([pages] * n_pg), w1e)


def _nsa_score_kernel(past, sd, n_cmp, n_sb, pp_ref, q_ref, cw_ref, nw_ref, b1_ref, w2_ref, b2_ref, mm_ref,
                      oc_ref, ow_ref, idx_ref):
    half = KV_B // 2
    rq = HG_B * sd
    scale = HD_B ** -0.5
    n_ch = pp_ref.shape[0]
    win = cw_ref.shape[2]
    kc = _compress_tail(pp_ref[...], n_cmp, b1_ref, w2_ref, b2_ref)
    kck, kcv = kc[:, :half].astype(BF16), kc[:, half:].astype(BF16)
    qpos = past + lax.broadcasted_iota(jnp.int32, (rq, 1), 0) % sd
    nt = (((1,), (1,)), ((), ()))

    def softmax(parts):
        m = functools.reduce(jnp.maximum, [jnp.max(jnp.where(mk, s, NEG_BIG), axis=-1, keepdims=True) for s, mk in parts])
        es = [jnp.where(mk, jnp.exp(s - m), 0.0) for s, mk in parts]
        d = functools.reduce(jnp.add, [jnp.sum(e, axis=-1, keepdims=True) for e in es])
        return [e / jnp.where(d > 0, d, 1.0) for e in es]

    imps = []
    for g in range(G_B):
        qg = q_ref[g * rq:(g + 1) * rq, :]
        s = lax.dot_general(qg, kck, nt, preferred_element_type=F32) * scale
        c_end = lax.broadcasted_iota(jnp.int32, (rq, n_ch), 1) * D_CMP + (L_CMP - 1)
        p, = softmax([(s, c_end <= qpos)])
        oc_ref[g * rq:(g + 1) * rq, :] = jnp.dot(p.astype(BF16), kcv, preferred_element_type=F32)
        imps.append(functools.reduce(jnp.add, [p[hg * sd:(hg + 1) * sd, :] for hg in range(HG_B)]))
        s1 = jnp.dot(qg, cw_ref[0].astype(BF16), preferred_element_type=F32) * scale
        s2 = lax.dot_general(qg, nw_ref[:, :half].astype(BF16), nt, preferred_element_type=F32) * scale
        wp1 = past - win + lax.broadcasted_iota(jnp.int32, s1.shape, 1)
        j2 = lax.broadcasted_iota(jnp.int32, s2.shape, 1)
        wp2 = past + j2
        p1, p2 = softmax([(s1, (wp1 <= qpos) & (wp1 > qpos - WINDOW) & (wp1 >= 0)),
                          (s2, (wp2 <= qpos) & (wp2 > qpos - WINDOW) & (j2 < sd))])
        ow_ref[g * rq:(g + 1) * rq, :] = (
            lax.dot_general(p1.astype(BF16), cw_ref[1].astype(BF16), nt, preferred_element_type=F32)
            + jnp.dot(p2.astype(BF16), nw_ref[:, half:].astype(BF16), preferred_element_type=F32))
    imp = jnp.concatenate(imps, axis=0)
    rows = G_B * sd
    bs = jnp.dot(imp, mm_ref[...], precision=HIGHEST, preferred_element_type=F32)
    n_lane = bs.shape[1]
    blk = lax.broadcasted_iota(jnp.int32, (rows, n_lane), 1)
    cur = (past + lax.broadcasted_iota(jnp.int32, (rows, 1), 0) % sd) >> 6
    forced = (blk == 0) | (blk == cur) | (blk == cur - 1)
    score = jnp.where(blk <= cur, jnp.where(forced, jnp.inf, bs), -jnp.inf)
    blk_f = blk.astype(F32)
    taken = blk >= n_sb
    lane = lax.broadcasted_iota(jnp.int32, (rows, 128), 1)
    picked = jnp.zeros((rows, 128), F32)
    for it in range(min(N_SEL, n_sb)):
        live = jnp.where(taken, -jnp.inf, score)
        mx = jnp.max(live, axis=-1, keepdims=True)
        ix = jnp.min(jnp.where(~taken & (score == mx), blk_f, float(n_lane)), axis=-1, keepdims=True)
        picked = jnp.where(lane == it, ix, picked)
        taken = taken | (blk_f == ix)
    idx_ref[...] = picked.astype(jnp.int32)


def _nsa_select_kernel(past, sd, n_sb, n_pages, k_sel, *refs):
    pt_ref, ix_ref, lyr_ref = refs[:3]
    pages = refs[3:3 + k_sel]
    new_ref, q_ref, oc_ref, ow_ref, gt_ref, o_ref = refs[3 + k_sel:]
    rq = HG_B * sd
    page = new_ref.shape[-1]
    blk_pg = page // L_SLC
    b, g, i = pl.program_id(0), pl.program_id(1), pl.program_id(2)
    base = ((b * G_B + g) * sd + i) * k_sel
    lane = lax.broadcasted_iota(jnp.int32, (1, k_sel * page), 1)
    in_page = lane % page
    kpos = in_page
    picked = lane < 0
    kts, vts = [], []
    for j in range(k_sel):
        blk = ix_ref[base + j]
        is_new = blk == n_sb - 1
        kts.append(jnp.where(is_new, new_ref[0], pages[j][0]))
        vts.append(jnp.where(is_new, new_ref[1], pages[j][1]))
        mine = lane // page == j
        kpos = kpos + jnp.where(mine, (blk // blk_pg) * page, 0)
        picked = picked | (mine & (in_page // L_SLC == blk % blk_pg))
    kt_all = jnp.concatenate(kts, axis=1).astype(BF16)
    vt_all = jnp.concatenate(vts, axis=1).astype(BF16)
    s = jnp.dot(q_ref[...], kt_all, preferred_element_type=F32) * (HD_B ** -0.5)
    row_q = lax.broadcasted_iota(jnp.int32, (rq, 1), 0) % sd
    mask = picked & (kpos <= past + row_q)
    m = jnp.max(jnp.where(mask, s, NEG_BIG), axis=-1, keepdims=True)
    e = jnp.where(mask, jnp.exp(s - m), 0.0)
    d = jnp.sum(e, axis=-1, keepdims=True)
    p = e / jnp.where(d > 0, d, 1.0)
    o_s = lax.dot_general(p.astype(BF16), vt_all, (((1,), (1,)), ((), ())), preferred_element_type=F32)
    gates = _sigmoid(gt_ref[...])
    y = gates[0] * oc_ref[...] + gates[1] * o_s + gates[2] * ow_ref[...]

    @pl.when(i == 0)
    def _():
        o_ref[...] = jnp.zeros_like(o_ref)

    o_ref[...] += jnp.where(row_q == i, y, 0.0)


def _nsa_sample(proj3, cols, kvs, sd, cmp_big, pages_cmp, pages_slc, win_view, layer, page_table):
    col_qb, col_kv, col_gb = cols
    bsz = proj3.shape[0]
    n_phys, depth, _, _, page = pages_cmp.shape
    n_pages = page_table.shape[1]
    past = n_pages * page
    win = win_view.shape[-1]
    half = KV_B // 2
    rq = HG_B * sd
    n_ch = past // D_CMP
    n_cmp = (past + sd) // D_CMP - L_CMP // D_CMP + 1
    n_sb = -(-(past + sd) // L_SLC)
    k_sel = min(N_SEL, n_sb)
    assert (past + sd) // D_CMP == n_ch and past % L_SLC == 0 and sd <= L_SLC and page % L_SLC == 0
    _, b1big, w2big, b2big, w1tok = cmp_big
    pp = _cmp_pages(pages_cmp, layer, page_table, w1tok)
    q = proj3[:, :sd, col_qb:col_qb + MIX_B].reshape(bsz, sd, G_B, HG_B, HD_B).transpose(0, 2, 3, 1, 4)
    q = jnp.stack([jnp.pad(q[:, g], ((0, 0), (0, 0), (0, 0), (g * HD_B, half - (g + 1) * HD_B))) for g in range(G_B)], 1)
    q = q.reshape(bsz, G_B * rq, half).astype(BF16)
    gt = proj3[:, :sd, col_gb:col_gb + 3 * H_B].astype(F32).reshape(bsz, sd, G_B, HG_B, 3).transpose(0, 4, 2, 3, 1)
    gt = jnp.broadcast_to(gt.reshape(bsz, 3, G_B * rq, 1), (bsz, 3, G_B * rq, half))
    n_lane = -(-n_sb // 128) * 128
    mm = jnp.asarray(np.pad(_cmp_to_sel_map(n_ch, n_cmp, n_sb), ((0, n_lane - n_sb), (0, 0))).T)
    per_b = lambda shape: pl.BlockSpec((None,) + shape, lambda i: (i,) + (0,) * len(shape))
    const = lambda shape: pl.BlockSpec(shape, lambda i: (0,) * len(shape))
    o_c, o_w, idx = pl.pallas_call(
        functools.partial(_nsa_score_kernel, past, sd, n_cmp, n_sb),
        grid=(bsz,),
        in_specs=[per_b((n_ch, 2 * KV_B)), per_b((G_B * rq, half)),
                  pl.BlockSpec((None, None, 2, half, win), lambda i: (layer, i, 0, 0, 0)),
                  pl.BlockSpec((None, PAD_ROWS, KV_B), lambda i: (i, 0, col_kv // KV_B + 2)),
                  const((1, KV_B)), const((KV_B, KV_B)), const((1, KV_B)), const((n_ch, n_lane))],
        out_specs=[per_b((G_B * rq, half)), per_b((G_B * rq, half)), per_b((G_B * sd, 128))],
        out_shape=[jax.ShapeDtypeStruct((bsz, G_B * rq, half), F32), jax.ShapeDtypeStruct((bsz, G_B * rq, half), F32),
                   jax.ShapeDtypeStruct((bsz, G_B * sd, 128), jnp.int32)],
        compiler_params=_cparams(("parallel",), 40),
        name="nsa_score_sample",
    )(pp, q, win_view, proj3, b1big, w2big, b2big, mm)
    blk_pg = page // L_SLC
    n_cached = past // L_SLC
    new_blk = jnp.pad(kvs.reshape(bsz, sd, 2, half).transpose(0, 2, 3, 1), ((0, 0), (0, 0), (0, 0), (0, page - sd)))

    def blk_spec(j):
        def index(b, g, i, pt, ix, lyr):
            blk = jnp.minimum(ix[((b * G_B + g) * sd + i) * k_sel + j], n_cached - 1)
            return pt[b * n_pages + blk // blk_pg], lyr[0], 0, 0, 0
        return pl.BlockSpec((None, None, 2, half, page), index)

    grp = lambda b, g, i, pt, ix, lyr: (b, g, 0)
    grid_spec = pltpu.PrefetchScalarGridSpec(
        num_scalar_prefetch=3,
        grid=(bsz, G_B, sd),
        in_specs=[blk_spec(j) for j in range(k_sel)] + [
            pl.BlockSpec((None, 2, half, page), lambda b, g, i, pt, ix, lyr: (b, 0, 0, 0)),
            pl.BlockSpec((None, rq, half), grp), pl.BlockSpec((None, rq, half), grp), pl.BlockSpec((None, rq, half), grp),
            pl.BlockSpec((None, 3, rq, half), lambda b, g, i, pt, ix, lyr: (b, 0, g, 0))],
        out_specs=pl.BlockSpec((None, rq, half), grp))
    y = pl.pallas_call(
        functools.partial(_nsa_select_kernel, past, sd, n_sb, n_pages, k_sel),
        grid_spec=grid_spec,
        out_shape=jax.ShapeDtypeStruct((bsz, G_B * rq, half), F32),
        compiler_params=_cparams(("parallel", "parallel", "arbitrary"), 40),
        name="nsa_select_sample",
    )(page_table.reshape(-1), idx[:, :, :k_sel].reshape(-1), jnp.full((1,), layer, jnp.int32),
      *([pages_slc] * k_sel), new_blk, q, o_c, o_w, gt)
    y = y.reshape(bsz, G_B, HG_B, sd, G_B, HD_B)
    y = jnp.stack([y[:, g, :, :, g] for g in range(G_B)], axis=1)
    return y.transpose(0, 3, 1, 2, 4).reshape(bsz, sd, MIX_B)


def _prep_w_in(w, d):
    o = np.cumsum([0, MIX_A, MIX_A, MIX_A, MIX_A, MIX_B, KV_B, KV_B, KV_B, 3 * H_B, 2 * C_CONV, 3 * d])
    parts = [w[:, o[0]:o[4]], w[:, o[10]:o[11]], w[:, o[4]:o[5]], w[:, o[9]:o[10]], w[:, o[5]:o[8]], w[:, o[8]:o[9]]]
    n = sum(p.shape[1] for p in parts)
    n_pad = -(-n // 512) * 512
    parts.append(jnp.zeros((w.shape[0], n_pad - n), w.dtype))
    return jnp.concatenate(parts, axis=1).astype(BF16)


def _layer(x3, n_valid, p, s0, conv_prefix, mem_kv, nsa_fn, alpha):
    bsz, t, d = x3.shape
    m = bsz * t
    col_qb = COL_MG + 3 * d
    col_glu = col_qb + MIX_B
    col_kv = col_glu + 2 * C_CONV
    col_gb = col_kv + 3 * KV_B
    proj = _matmul(x3.reshape(m, d), p['w_in'], 1024, 1536, out_dtype=BF16)
    proj3 = proj.reshape(bsz, t, -1)
    ya, s_new = _hgrn(proj3, p['lb'], p['hg_norm'], s0, n_valid)
    yc, conv_state = _conv(proj3, col_glu // C_CONV, conv_prefix, p['conv_w'], p['conv_b'],
                           p['conv_ln_g'], p['conv_ln_b'], n_valid)
    kv_shape = (bsz, n_valid, 2, G_B, HD_B)
    kvc = proj3[:, :n_valid, col_kv:col_kv + KV_B].astype(F32)
    kvs = proj3[:, :n_valid, col_kv + KV_B:col_kv + 2 * KV_B].astype(F32)
    kvw = proj3[:, :n_valid, col_kv + 2 * KV_B:col_kv + 3 * KV_B].astype(F32)
    yb, nsa_extra = nsa_fn(proj3, (col_qb, col_kv, col_gb), kvc, kvs, kvw)
    kvc, kvs = kvc.reshape(kv_shape), kvs.reshape(kv_shape)
    x1 = _merge(ya.reshape(m, MIX_A), yb.reshape(m, MIX_B), yc.reshape(m, C_CONV), proj, x3.reshape(m, d),
                p['w_pa'], p['w_pb'], p['w_pc'], p['w_out'], p['ln_g'][0:1], p['ln_b'][0:1], alpha)
    x2 = _xattn(x1.reshape(bsz, t, d), mem_kv[0], mem_kv[1], p['w_xq'], p['w_xo'], p['ln_g'][1:2], p['ln_b'][1:2], alpha)
    x3n = _mlp(x2.reshape(m, d), p['w_up'], p['w_down'], p['ln_g'][2:3], p['ln_b'][2:3], alpha)
    return x3n.reshape(bsz, t, d), kvc, kvs, nsa_extra, s_new, conv_state


def kernel(x_prompt, x_sample, cache_cmp, cache_slc, cache_win, state_hgrn, state_conv, cache_mem, page_table, mem_prompt, w_in, lb_raw, hg_norm, w_cmp1, b_cmp1, w_cmp2, b_cmp2, conv_w, conv_b, conv_ln_g, conv_ln_b, w_pa, w_pb, w_pc, w_out, ln_g, ln_b, w_xq, w_xkv, w_xo, w_up, w_down):
    bp, t, d = x_prompt.shape
    bd, sd = x_sample.shape[:2]
    depth = w_in.shape[0]
    n_mem = mem_prompt.shape[1]
    win_buf = cache_win.shape[2]
    alpha = (2 * depth) ** 0.25
    lb_cum = jnp.cumsum(jax.nn.softmax(lb_raw.astype(F32), axis=0), axis=0)
    lb_all = lb_cum - lb_cum[0]
    pages_cmp, pages_slc, win_view = _pages_view(cache_cmp), _pages_view(cache_slc), _pages_view(cache_win)
    mem_cache = _mem_tile_order(cache_mem, d)
    xp = x_prompt
    xs = jnp.pad(x_sample, ((0, 0), (0, PAD_ROWS - sd), (0, 0)))
    outs = {k: [] for k in ('cmp_p', 'cmp_s', 'slc_p', 'slc_s', 'win_p', 'win_s', 'hg_p', 'hg_s', 'cv_p', 'cv_s', 'mem_p')}
    for l in range(depth):
        p = {'w_in': _prep_w_in(w_in[l], d), 'lb': lb_all[l], 'hg_norm': hg_norm[l], 'conv_w': conv_w[l],
             'conv_b': conv_b[l].reshape(1, -1), 'conv_ln_g': conv_ln_g[l].reshape(1, -1),
             'conv_ln_b': conv_ln_b[l].reshape(1, -1),
             'w_pa': w_pa[l].astype(BF16), 'w_pb': w_pb[l].astype(BF16), 'w_pc': w_pc[l].astype(BF16),
             'w_out': w_out[l].astype(BF16), 'ln_g': ln_g[l], 'ln_b': ln_b[l],
             'w_xq': w_xq[l].astype(BF16), 'w_xo': w_xo[l].astype(BF16),
             'w_up': w_up[l].astype(BF16), 'w_down': w_down[l].astype(BF16)}
        cmp_w = (w_cmp1[l], b_cmp1[l], w_cmp2[l], b_cmp2[l])

        cmp_big = _cmp_weights(*cmp_w)

        def nsa_prompt(proj3, cols, kvc, kvs, kvw):
            col_qb, col_kv, col_gb = cols
            prep = _nsa_prep(proj3, col_kv, cmp_big, min(win_buf, t))
            o = _nsa_prompt(proj3, col_qb, col_gb, prep[:6])
            ct, st_, wt = prep[6:]
            return o, (ct, st_, jnp.pad(wt, ((0, 0), (0, 0), (max(win_buf - t, 0), 0))))

        def nsa_sample(proj3, cols, kvc, kvs, kvw):
            o = _nsa_sample(proj3, cols, kvs, sd, cmp_big, pages_cmp, pages_slc, win_view, l, page_table)
            o = jnp.pad(o, ((0, 0), (0, PAD_ROWS - sd), (0, 0))).astype(BF16)
            win = jnp.concatenate([cache_win[l], kvw.reshape(bd, sd, 2, G_B, HD_B)], axis=1)[:, -win_buf:]
            return o, win

        w_kv = _mem_tile_order(w_xkv[l].reshape(d, 2, NX_H, d // NX_H), d).astype(BF16)
        mem_kv = _matmul(mem_prompt.reshape(bp * n_mem, d), w_kv, 1024, 512).reshape(1, bp, n_mem, 2 * d)
        xp, kc, ks_, wn, sh, cv = _layer(xp, t, p, jnp.zeros((bp, H_A, DK_A, DV_A), F32),
                                         jnp.zeros((bp, CONV_K - 1, C_CONV), F32), (mem_kv, 0), nsa_prompt, alpha)
        outs['cmp_p'].append(wn[0]); outs['slc_p'].append(wn[1]); outs['win_p'].append(wn[2])
        outs['hg_p'].append(sh); outs['cv_p'].append(cv)
        outs['mem_p'].append(_mem_head_order(mem_kv[0], d))
        xs, kc, ks_, wn, sh, cv = _layer(xs, sd, p, state_hgrn[l], state_conv[l], (mem_cache, l), nsa_sample, alpha)
        outs['cmp_s'].append(kc); outs['slc_s'].append(ks_); outs['win_s'].append(wn)
        outs['hg_s'].append(sh); outs['cv_s'].append(cv)
    st = lambda k, ax: jnp.stack(outs[k], axis=ax)

    def rows(k, ax):
        a = st(k, ax)
        a = a.reshape(a.shape[:2] + (2, G_B, HD_B, a.shape[-1]))
        return jnp.transpose(a, (0, 1, 5, 2, 3, 4))

    return (xp, xs[:, :sd],
            rows('cmp_p', 1), st('cmp_s', 1), rows('slc_p', 1), st('slc_s', 1),
            rows('win_p', 0), st('win_s', 0), st('hg_p', 0), st('hg_s', 0),
            st('cv_p', 0), st('cv_s', 0), st('mem_p', 0))
```

```python
import functools

import numpy as np
import jax
import jax.numpy as jnp
from jax import lax
from jax.experimental import pallas as pl
from jax.experimental.pallas import tpu as pltpu

F32 = jnp.float32
BF16 = jnp.bfloat16
HIGHEST = lax.Precision.HIGHEST

H_A, DK_A, DV_A, CHUNK_A = 4, 128, 128, 16
H_B, G_B, HG_B, HD_B = 8, 2, 4, 64
L_CMP, D_CMP, L_SLC, N_SEL, WINDOW = 32, 16, 64, 16, 512
C_CONV, CONV_K = 512, 31
NX_H = 4
LN_EPS, RMS_EPS = 1e-5, 1e-6
PAD_ROWS = 16
LANES = 128

MIX_A = H_A * DK_A
MIX_B = H_B * HD_B
KV_B = 2 * G_B * HD_B
COL_QA, COL_FA, COL_IA, COL_GA = 0, 512, 1024, 1536
COL_MG = 2048
V7X_VMEM_LIMIT = 56 * 2**20


def _cparams(sem, vmem_mb=None):
    return pltpu.CompilerParams(dimension_semantics=sem,
                                vmem_limit_bytes=None if vmem_mb is None else vmem_mb * 2**20)


def _ln(y, g, b):
    mu = jnp.mean(y, axis=-1, keepdims=True)
    d = y - mu
    var = jnp.mean(d * d, axis=-1, keepdims=True)
    return d * lax.rsqrt(var + LN_EPS) * g + b


def _sigmoid(x):
    return 1.0 / (1.0 + jnp.exp(-x))


def _mm_kernel(x_ref, w_ref, o_ref, xb_ref):
    @pl.when(pl.program_id(1) == 0)
    def _():
        xb_ref[...] = x_ref[...].astype(BF16)

    o_ref[...] = jnp.dot(xb_ref[...], w_ref[...], preferred_element_type=F32).astype(o_ref.dtype)


def _matmul(x, w, tm, tn, out_dtype=F32):
    m, k = x.shape
    n = w.shape[1]
    tm, tn = min(tm, m), min(tn, n)
    return pl.pallas_call(
        _mm_kernel,
        grid=(m // tm, n // tn),
        in_specs=[pl.BlockSpec((tm, k), lambda i, j: (i, 0)),
                  pl.BlockSpec((k, tn), lambda i, j: (0, j))],
        out_specs=pl.BlockSpec((tm, tn), lambda i, j: (i, j)),
        out_shape=jax.ShapeDtypeStruct((m, n), out_dtype),
        scratch_shapes=[pltpu.VMEM((tm, k), BF16)],
        compiler_params=_cparams(("parallel", "arbitrary"), 40),
        name="proj_matmul",
    )(x, w)


def _mlp_kernel(alpha, x_ref, wu_ref, wd_ref, g_ref, b_ref, o_ref, xb_ref, acc_ref):
    j = pl.program_id(1)

    @pl.when(j == 0)
    def _():
        xb_ref[...] = x_ref[...].astype(BF16)
        acc_ref[...] = jnp.zeros_like(acc_ref)

    h = jnp.dot(xb_ref[...], wu_ref[...], preferred_element_type=F32)
    h = jnp.square(jnp.maximum(h, 0.0)).astype(BF16)
    acc_ref[...] += jnp.dot(h, wd_ref[...], preferred_element_type=F32)

    @pl.when(j == pl.num_programs(1) - 1)
    def _():
        o_ref[...] = _ln(alpha * x_ref[...] + acc_ref[...], g_ref[...], b_ref[...])


def _mlp(x, w_up, w_down, g, b, alpha, tm=512, tf=2048):
    m, d = x.shape
    ff = w_up.shape[1]
    tm = min(tm, m)
    return pl.pallas_call(
        functools.partial(_mlp_kernel, alpha),
        grid=(m // tm, ff // tf),
        in_specs=[pl.BlockSpec((tm, d), lambda i, j: (i, 0)),
                  pl.BlockSpec((d, tf), lambda i, j: (0, j)),
                  pl.BlockSpec((tf, d), lambda i, j: (j, 0)),
                  pl.BlockSpec((1, d), lambda i, j: (0, 0)),
                  pl.BlockSpec((1, d), lambda i, j: (0, 0))],
        out_specs=pl.BlockSpec((tm, d), lambda i, j: (i, 0)),
        out_shape=jax.ShapeDtypeStruct((m, d), F32),
        scratch_shapes=[pltpu.VMEM((tm, d), BF16), pltpu.VMEM((tm, d), F32)],
        compiler_params=_cparams(("parallel", "arbitrary"), 48),
        name="mlp",
    )(x, w_up, w_down, g, b)


def _merge_kernel(alpha, ya_ref, yb_ref, yc_ref, ma_ref, mb_ref, mc_ref, x_ref,
                  wpa_ref, wpb_ref, wpc_ref, wout_ref, g_ref, b_ref, o_ref):
    def branch(y_ref, m_ref, w_ref):
        return _sigmoid(m_ref[...].astype(F32)) * jnp.dot(y_ref[...].astype(BF16), w_ref[...], preferred_element_type=F32)

    merged = branch(ya_ref, ma_ref, wpa_ref) + branch(yb_ref, mb_ref, wpb_ref) + branch(yc_ref, mc_ref, wpc_ref)
    y = jnp.dot(merged.astype(BF16), wout_ref[...], preferred_element_type=F32)
    o_ref[...] = _ln(alpha * x_ref[...] + y, g_ref[...], b_ref[...])


def _merge(ya, yb, yc, proj, x, wpa, wpb, wpc, wout, g, b, alpha, tm=512):
    m, d = x.shape
    tm = min(tm, m)
    mg0 = COL_MG // d
    row = lambda i: (i, 0)
    const = lambda i: (0, 0)
    return pl.pallas_call(
        functools.partial(_merge_kernel, alpha),
        grid=(m // tm,),
        in_specs=[pl.BlockSpec((tm, MIX_A), row), pl.BlockSpec((tm, MIX_B), row), pl.BlockSpec((tm, C_CONV), row),
                  pl.BlockSpec((tm, d), lambda i: (i, mg0)), pl.BlockSpec((tm, d), lambda i: (i, mg0 + 1)),
                  pl.BlockSpec((tm, d), lambda i: (i, mg0 + 2)),
                  pl.BlockSpec((tm, d), row),
                  pl.BlockSpec((MIX_A, d), const), pl.BlockSpec((MIX_B, d), const), pl.BlockSpec((C_CONV, d), const),
                  pl.BlockSpec((d, d), const), pl.BlockSpec((1, d), const), pl.BlockSpec((1, d), const)],
        out_specs=pl.BlockSpec((tm, d), row),
        out_shape=jax.ShapeDtypeStruct((m, d), F32),
        compiler_params=_cparams(("parallel",), 48),
        name="merge_out",
    )(ya, yb, yc, proj, proj, proj, x, wpa, wpb, wpc, wout, g, b)


def _xattn_kernel(alpha, x_ref, kv_ref, wq_ref, wo_ref, g_ref, b_ref, o_ref):
    x = x_ref[...]
    d = x.shape[-1]
    hd = d // NX_H
    q = jnp.dot(x.astype(BF16), wq_ref[...], preferred_element_type=F32)
    n_dt = hd // LANES

    def head(base, h):
        parts = [kv_ref[:, base + (dt * NX_H + h) * LANES:base + (dt * NX_H + h + 1) * LANES] for dt in range(n_dt)]
        return jnp.concatenate(parts, axis=1).astype(BF16)

    outs = []
    for h in range(NX_H):
        qh = q[:, h * hd:(h + 1) * hd].astype(BF16)
        kh, vh = head(0, h), head(d, h)
        s = lax.dot_general(qh, kh, (((1,), (1,)), ((), ())), preferred_element_type=F32) * (hd ** -0.5)
        e = jnp.exp(s - jnp.max(s, axis=-1, keepdims=True))
        p = e / jnp.sum(e, axis=-1, keepdims=True)
        outs.append(jnp.dot(p.astype(BF16), vh, preferred_element_type=F32))
    o = jnp.concatenate(outs, axis=-1)
    y = jnp.dot(o.astype(BF16), wo_ref[...], preferred_element_type=F32)
    o_ref[...] = _ln(alpha * x + y, g_ref[...], b_ref[...])


def _mem_tile_order(a, d):
    lead = a.shape[:-3]
    a = a.reshape(lead + (2, NX_H, d // NX_H // LANES, LANES))
    return jnp.swapaxes(a, -3, -2).reshape(lead + (2 * d,))


def _mem_head_order(a, d):
    lead = a.shape[:-1]
    a = a.reshape(lead + (2, d // NX_H // LANES, NX_H, LANES))
    return jnp.swapaxes(a, -3, -2).reshape(lead + (2, NX_H, d // NX_H))


def _xattn(x, kv, layer, wq, wo, g, b, alpha, tm=512):
    bsz, t, d = x.shape
    n_mem = kv.shape[2]
    tm = min(tm, t)
    const = lambda i, j: (0, 0)
    return pl.pallas_call(
        functools.partial(_xattn_kernel, alpha),
        grid=(bsz, t // tm),
        in_specs=[pl.BlockSpec((None, tm, d), lambda i, j: (i, j, 0)),
                  pl.BlockSpec((None, None, n_mem, 2 * d), lambda i, j: (layer, i, 0, 0)),
                  pl.BlockSpec((d, d), const), pl.BlockSpec((d, d), const),
                  pl.BlockSpec((1, d), const), pl.BlockSpec((1, d), const)],
        out_specs=pl.BlockSpec((None, tm, d), lambda i, j: (i, j, 0)),
        out_shape=jax.ShapeDtypeStruct((bsz, t, d), F32),
        compiler_params=_cparams(("parallel", "parallel"), 48),
        name="xattn",
    )(x, kv, wq, wo, g, b)


CONV_HALO = 32
CONV_SUB = 32


def _conv_kernel(rt, n_valid_last, a_ref, gt_ref, pre_ref, w_ref, cb_ref, g_ref, b_ref, y_ref, st_ref, ue_ref, sh_ref):
    t = pl.program_id(1)
    off = CONV_HALO - (CONV_K - 1)
    sl = 8

    @pl.when(t == 0)
    def _():
        ue_ref[0:off, :] = jnp.zeros((off, C_CONV), F32)
        ue_ref[off:CONV_HALO, :] = pre_ref[...]

    ue_ref[CONV_HALO:CONV_HALO + rt, :] = a_ref[...].astype(F32) * _sigmoid(gt_ref[...].astype(F32))
    n_sh = rt + CONV_HALO - sl
    for s in range(1, sl):
        sh_ref[s - 1, 0:n_sh, :] = ue_ref[s:s + n_sh, :]
    sub = min(CONV_SUB, rt)
    for r0 in range(0, rt, sub):
        acc = cb_ref[...]
        for j in range(CONV_K):
            a, s = divmod(off + j, sl)
            lo = r0 + a * sl
            win = ue_ref[lo:lo + sub, :] if s == 0 else sh_ref[s - 1, lo:lo + sub, :]
            acc = acc + w_ref[j:j + 1, :] * win
        y = _ln(acc, g_ref[...], b_ref[...])
        y_ref[r0:r0 + sub, :] = (y * _sigmoid(y)).astype(y_ref.dtype)

    @pl.when(t == pl.num_programs(1) - 1)
    def _():
        st_ref[...] = ue_ref[off + n_valid_last:off + n_valid_last + CONV_K - 1, :]

    ue_ref[0:CONV_HALO, :] = ue_ref[rt:rt + CONV_HALO, :]


def _conv(proj3, col_a, prefix, w, cb, g, b, n_valid, rt=256):
    bsz, t, _ = proj3.shape
    rt = min(rt, t)
    n_valid_last = n_valid - (t - rt)
    vec = lambda i, j: (0, 0)
    return pl.pallas_call(
        functools.partial(_conv_kernel, rt, n_valid_last),
        grid=(bsz, t // rt),
        in_specs=[pl.BlockSpec((None, rt, C_CONV), lambda i, j: (i, j, col_a)),
                  pl.BlockSpec((None, rt, C_CONV), lambda i, j: (i, j, col_a + 1)),
                  pl.BlockSpec((None, CONV_K - 1, C_CONV), lambda i, j: (i, 0, 0)),
                  pl.BlockSpec((CONV_K, C_CONV), vec), pl.BlockSpec((1, C_CONV), vec),
                  pl.BlockSpec((1, C_CONV), vec), pl.BlockSpec((1, C_CONV), vec)],
        out_specs=[pl.BlockSpec((None, rt, C_CONV), lambda i, j: (i, j, 0)),
                   pl.BlockSpec((None, CONV_K - 1, C_CONV), lambda i, j: (i, 0, 0))],
        out_shape=[jax.ShapeDtypeStruct((bsz, t, C_CONV), BF16),
                   jax.ShapeDtypeStruct((bsz, CONV_K - 1, C_CONV), F32)],
        scratch_shapes=[pltpu.VMEM((rt + CONV_HALO, C_CONV), F32), pltpu.VMEM((7, rt + CONV_HALO - 8, C_CONV), F32)],
        compiler_params=_cparams(("parallel", "arbitrary"), 32),
        name="conformer_conv",
    )(proj3, proj3, prefix, w, cb, g, b)


def _hgrn_kernel(tt, n_valid, q_ref, f_ref, i_ref, g_ref, la_ref, l1_ref, gn_ref, s0_ref,
                 y_ref, s_ref, st_ref, qs_ref, kk_ref, bb_ref, o32_ref):
    c = CHUNK_A
    t = pl.program_id(1)

    @pl.when(t == 0)
    def _():
        for h in range(H_A):
            st_ref[h] = s0_ref[h].T

    z = f_ref[...].astype(F32)
    sp = jnp.log1p(jnp.exp(-jnp.abs(z)))
    cc = l1_ref[...] + (jnp.minimum(z, 0.0) - sp)
    a = la_ref[...]
    logf = jnp.maximum(a, cc) + jnp.log1p(jnp.exp(-jnp.abs(a - cc)))
    logk = l1_ref[...] + (jnp.minimum(-z, 0.0) - sp)
    row = lax.broadcasted_iota(jnp.int32, (tt, 1), 0)
    if n_valid < tt:
        logf = jnp.where(row < n_valid, logf, 0.0)
        logk = jnp.where(row < n_valid, logk, -jnp.inf)
    b = logf
    rc = row & (c - 1)
    sh = 1
    while sh < c:
        b = b + jnp.where(rc >= sh, pltpu.roll(b, sh, 0), 0.0)
        sh *= 2
    q = q_ref[...].astype(F32)
    qs_ref[...] = q * _sigmoid(q)
    kk_ref[...] = logk
    bb_ref[...] = b

    sl = 8
    row8 = lax.broadcasted_iota(jnp.int32, (sl, 1), 0)

    def chunk(ci, carry):
        r0 = pl.multiple_of(ci * c, c)
        for h in range(H_A):
            hs = slice(h * DK_A, (h + 1) * DK_A)
            qc = qs_ref[pl.ds(r0, c), hs]
            lkc = kk_ref[pl.ds(r0, c), hs]
            bc = bb_ref[pl.ds(r0, c), hs]
            vc = i_ref[pl.ds(r0, c), hs].astype(F32)
            bl = bc[c - 1:c, :]
            st = st_ref[h]
            qe = (qc * jnp.exp(bc)).astype(BF16)
            o = lax.dot_general(qe, st.astype(BF16), (((1,), (1,)), ((), ())), preferred_element_type=F32)
            tiles = []
            for r in range(c // sl):
                rows = slice(r * sl, (r + 1) * sl)
                q_r, b_r, o_r = qc[rows], bc[rows], o[rows]
                for s in range((r + 1) * sl):
                    d = b_r - (bc[s:s + 1, :] - lkc[s:s + 1, :])
                    if s >= r * sl:
                        d = jnp.where(row8 >= s - r * sl, d, -jnp.inf)
                    att = jnp.sum(q_r * jnp.exp(d), axis=-1, keepdims=True)
                    o_r = o_r + att * vc[s:s + 1, :]
                tiles.append(o_r)
            o32_ref[pl.ds(r0, c), hs] = jnp.concatenate(tiles, axis=0)
            kd = jnp.exp(lkc + (bl - bc)).astype(BF16)
            u = lax.dot_general(vc.astype(BF16), kd, (((0,), (0,)), ((), ())), preferred_element_type=F32)
            st_ref[h] = st * jnp.exp(bl) + u
        return carry

    lax.fori_loop(0, tt // c, chunk, 0, unroll=8 if (tt // c) % 8 == 0 else 1)

    g = g_ref[...].astype(F32)
    gate = g * _sigmoid(g)
    for h in range(H_A):
        hs = slice(h * DV_A, (h + 1) * DV_A)
        o = o32_ref[:, hs]
        o = o * lax.rsqrt(jnp.mean(o * o, axis=-1, keepdims=True) + RMS_EPS) * gn_ref[...]
        y_ref[:, hs] = (o * gate[:, hs]).astype(y_ref.dtype)

    @pl.when(t == pl.num_programs(1) - 1)
    def _():
        for h in range(H_A):
            s_ref[h] = st_ref[h].T


def _hgrn(proj3, lb, gnorm, s0, n_valid, tt=256):
    bsz, t, _ = proj3.shape
    tt = min(tt, t)
    lb = lb.reshape(1, MIX_A).astype(F32)
    la, l1 = jnp.log(lb), jnp.log1p(-lb)
    vec = lambda i, j: (0, 0)
    col = lambda cb: pl.BlockSpec((None, tt, MIX_A), lambda i, j: (i, j, cb))
    st = pl.BlockSpec((None, H_A, DK_A, DV_A), lambda i, j: (i, 0, 0, 0))
    return pl.pallas_call(
        functools.partial(_hgrn_kernel, tt, n_valid if t == tt else tt),
        grid=(bsz, t // tt),
        in_specs=[col(0), col(1), col(2), col(3),
                  pl.BlockSpec((1, MIX_A), vec), pl.BlockSpec((1, MIX_A), vec),
                  pl.BlockSpec((1, DV_A), vec), st],
        out_specs=[pl.BlockSpec((None, tt, MIX_A), lambda i, j: (i, j, 0)), st],
        out_shape=[jax.ShapeDtypeStruct((bsz, t, MIX_A), BF16),
                   jax.ShapeDtypeStruct((bsz, H_A, DK_A, DV_A), F32)],
        scratch_shapes=[pltpu.VMEM((H_A, DV_A, DK_A), F32), pltpu.VMEM((tt, MIX_A), F32),
                        pltpu.VMEM((tt, MIX_A), F32), pltpu.VMEM((tt, MIX_A), F32), pltpu.VMEM((tt, MIX_A), F32)],
        compiler_params=_cparams(("parallel", "arbitrary"), 32),
        name="hgrn2",
    )(proj3, proj3, proj3, proj3, la, l1, gnorm.reshape(1, DV_A), s0)


NEG_BIG = -1e30
SEL_PER_CMP = L_SLC // D_CMP


def _cmp_weights(w1, b1, w2, b2):
    m = L_CMP // D_CMP
    eye_e, eye_g = jnp.eye(2, dtype=F32), jnp.eye(G_B, dtype=F32)
    w1r = w1.reshape(2, m, D_CMP, HD_B, HD_B)
    w1big = jnp.einsum('ehjdf,ea,gb->jegdhabf', w1r, eye_e, eye_g).reshape(D_CMP * KV_B, m * KV_B)
    w2big = jnp.einsum('efo,ea,gb->egfabo', w2, eye_e, eye_g).reshape(KV_B, KV_B)
    b1big = jnp.broadcast_to(b1[:, None, :], (2, G_B, HD_B)).reshape(1, KV_B)
    b2big = jnp.broadcast_to(b2[:, None, :], (2, G_B, HD_B)).reshape(1, KV_B)
    w1tok = jnp.einsum('ehjdf,gb->ejgdhbf', w1r, eye_g).reshape(2, D_CMP, KV_B // 2, m * KV_B // 2)
    return w1big.astype(BF16), b1big, w2big.astype(BF16), b2big, w1tok.astype(BF16)


def _cmp_to_sel_map(n_cmp_pad, n_cmp, n_sb):
    mm = np.zeros((n_sb, n_cmp_pad), np.float32)
    for n in range(n_cmp):
        for i in (n, n + 1):
            if i // SEL_PER_CMP < n_sb:
                mm[i // SEL_PER_CMP, n] += 1.0
    return mm


def _compress_tail(pp, n_cmp, b1_ref, w2_ref, b2_ref):
    n_ch = pp.shape[0]
    hid = pp[:, :KV_B] + pltpu.roll(pp[:, KV_B:], n_ch - 1, 0) + b1_ref[...]
    hid = hid * _sigmoid(hid)
    kc = jnp.dot(hid.astype(BF16), w2_ref[...], preferred_element_type=F32) + b2_ref[...]
    row = lax.broadcasted_iota(jnp.int32, (n_ch, 1), 0)
    return jnp.where(row < n_cmp, kc, 0.0)


def _nsa_prep_kernel(n_cmp, win, kc_ref, ks_ref, kw_ref, w1_ref, b1_ref, w2_ref, b2_ref,
                     kck_ref, kcvt_ref, ksk_ref, ksvt_ref, kwk_ref, kwvt_ref, ct_ref, st_ref, wt_ref, xk_ref):
    half = KV_B // 2
    t = kc_ref.shape[0]
    n_ch = t // D_CMP
    for e in range(2):
        xk_ref[e] = kc_ref[:, e * half:(e + 1) * half].astype(F32)
    pp = jnp.zeros((n_ch, w1_ref.shape[2]), F32)
    for j in range(D_CMP):
        for e in range(2):
            x = xk_ref[e, pl.ds(j, n_ch, stride=D_CMP), :].astype(BF16)
            pp = pp + jnp.dot(x, w1_ref[j, e * half:(e + 1) * half, :], preferred_element_type=F32)
    kc = _compress_tail(pp, n_cmp, b1_ref, w2_ref, b2_ref)
    kck_ref[...] = kc[:, :half].astype(BF16)
    kcvt_ref[...] = kc[:, half:].T.astype(BF16)
    ct_ref[...] = kc_ref[...].astype(F32).T
    ks_t = ks_ref[...].astype(F32).T
    st_ref[...] = ks_t
    n_aug = ksk_ref.shape[1] - half
    blk_of_key = lax.broadcasted_iota(jnp.int32, (t, n_aug), 0) // L_SLC
    one_hot = jnp.where(blk_of_key == lax.broadcasted_iota(jnp.int32, (t, n_aug), 1), 1.0, 0.0).astype(BF16)
    ksk_ref[...] = jnp.concatenate([ks_ref[:, :half].astype(BF16), one_hot], axis=1)
    ksvt_ref[...] = ks_t[half:, :].astype(BF16)
    kw_t = kw_ref[...].astype(F32).T
    wt_ref[...] = kw_t[:, t - win:]
    kwk_ref[...] = kw_ref[:, :half].astype(BF16)
    kwvt_ref[...] = kw_t[half:, :].astype(BF16)


def _nsa_prep(proj3, col_kv, cmp_big, win):
    bsz, t, _ = proj3.shape
    n_ch = t // D_CMP
    n_cmp = n_ch - L_CMP // D_CMP + 1
    half = KV_B // 2
    w1big, b1big, w2big, b2big = cmp_big[:4]
    w1pos = w1big.reshape(D_CMP, KV_B, w1big.shape[1])
    cb = col_kv // KV_B
    const = lambda i: (0, 0)
    kspec = lambda n: pl.BlockSpec((None, n, half), lambda i: (i, 0, 0))
    vspec = lambda n: pl.BlockSpec((None, half, n), lambda i: (i, 0, 0))
    fspec = lambda n: pl.BlockSpec((None, KV_B, n), lambda i: (i, 0, 0))
    col = lambda c: pl.BlockSpec((None, t, KV_B), lambda i: (i, 0, cb + c))
    return pl.pallas_call(
        functools.partial(_nsa_prep_kernel, n_cmp, win),
        grid=(bsz,),
        in_specs=[col(0), col(1), col(2),
                  pl.BlockSpec(w1pos.shape, lambda i: (0, 0, 0)), pl.BlockSpec((1, KV_B), const),
                  pl.BlockSpec((KV_B, KV_B), const), pl.BlockSpec((1, KV_B), const)],
        out_specs=[kspec(n_ch), vspec(n_ch), pl.BlockSpec((None, t, KV_B), lambda i: (i, 0, 0)), vspec(t),
                   kspec(t), vspec(t), fspec(t), fspec(t), fspec(win)],
        out_shape=[jax.ShapeDtypeStruct((bsz, n_ch, half), BF16), jax.ShapeDtypeStruct((bsz, half, n_ch), BF16),
                   jax.ShapeDtypeStruct((bsz, t, KV_B), BF16), jax.ShapeDtypeStruct((bsz, half, t), BF16),
                   jax.ShapeDtypeStruct((bsz, t, half), BF16), jax.ShapeDtypeStruct((bsz, half, t), BF16),
                   jax.ShapeDtypeStruct((bsz, KV_B, t), F32), jax.ShapeDtypeStruct((bsz, KV_B, t), F32),
                   jax.ShapeDtypeStruct((bsz, KV_B, win), F32)],
        scratch_shapes=[pltpu.VMEM((2, t, half), F32)],
        compiler_params=_cparams(("parallel",), 48),
        name="nsa_prep",
    )(proj3, proj3, proj3, w1pos, b1big, w2big, b2big)


def _nsa_prompt_kernel(tq, tk, n_sb, q_ref, gb_ref, kck_ref, kcvt_ref, ksk_ref, ksvt_ref, kwk_ref, kwvt_ref, mm_ref, o_ref):
    qi = pl.program_id(1)
    q0 = qi * tq
    n_cp = kck_ref.shape[0]
    w4 = HG_B * tq
    q_t = (q_ref[...].astype(F32) * HD_B ** -0.5).T
    g_t = _sigmoid(gb_ref[...].astype(F32)).T
    qpos = q0 + lax.broadcasted_iota(jnp.int32, (1, tq), 1)
    qpos4 = jnp.concatenate([qpos] * HG_B, axis=1)
    zpad = jnp.zeros((HD_B, tq), BF16)

    def update(state, s, pen, vt, g):
        m, l, acc = state
        if pen is not None:
            s = s + jnp.concatenate([pen] * HG_B, axis=1)
        m_new = jnp.maximum(m, jnp.max(s, axis=0, keepdims=True))
        e = jnp.exp(s - m_new)
        a = jnp.exp(m - m_new)
        l = a * l + jnp.sum(e, axis=0, keepdims=True)
        pv = jnp.dot(vt[g * HD_B:(g + 1) * HD_B, :], e.astype(BF16), preferred_element_type=F32)
        return m_new, l, a * acc + pv

    qts, o_cs, sel_ts = [], [], []
    for g in range(G_B):
        cols = []
        for hg in range(HG_B):
            h = g * HG_B + hg
            qh = q_t[h * HD_B:(h + 1) * HD_B, :].astype(BF16)
            cols.append(jnp.concatenate([qh, zpad] if g == 0 else [zpad, qh], axis=0))
        qt = jnp.concatenate(cols, axis=1)
        qts.append(qt)
        s = jnp.dot(kck_ref[...], qt, preferred_element_type=F32)
        c_end = lax.broadcasted_iota(jnp.int32, (n_cp, w4), 0) * D_CMP + (L_CMP - 1)
        mask = c_end <= qpos4
        m = jnp.max(jnp.where(mask, s, NEG_BIG), axis=0, keepdims=True)
        e = jnp.where(mask, jnp.exp(s - m), 0.0)
        d = jnp.sum(e, axis=0, keepdims=True)
        p = e / jnp.where(d > 0, d, 1.0)
        o_cs.append(jnp.dot(kcvt_ref[...], p.astype(BF16), preferred_element_type=F32)[g * HD_B:(g + 1) * HD_B, :])
        imp = p[:, 0:tq]
        for hg in range(1, HG_B):
            imp = imp + p[:, hg * tq:(hg + 1) * tq]
        bs = jnp.dot(mm_ref[...], imp, precision=HIGHEST, preferred_element_type=F32)
        blk = lax.broadcasted_iota(jnp.int32, (n_sb, tq), 0)
        cur = qpos >> 6
        forced = (blk == 0) | (blk == cur) | (blk == cur - 1)
        score = jnp.where(blk <= cur, jnp.where(forced, jnp.inf, bs), -jnp.inf)
        rank = jnp.zeros((n_sb, tq), jnp.int32)
        for i in range(n_sb):
            si = score[i:i + 1, :]
            rank = rank + jnp.where((si > score) | ((si == score) & (blk > i)), 1, 0)
        sel_ts.append(jnp.where((rank < N_SEL) & (blk <= cur), 1.0, 0.0).astype(BF16))

    n_aug = ksk_ref.shape[1] - 2 * HD_B
    qas = []
    for g in range(G_B):
        pen_rows = ((sel_ts[g].astype(F32) - 1.0) * (-NEG_BIG)).astype(BF16)
        pen_rows = jnp.concatenate([pen_rows, jnp.zeros((n_aug - n_sb, tq), BF16)], axis=0)
        qas.append(jnp.concatenate([qts[g], jnp.concatenate([pen_rows] * HG_B, axis=1)], axis=0))
    krow = lax.broadcasted_iota(jnp.int32, (tk, tq), 0)

    def far_tiles(kt, states):
        k0 = pl.multiple_of(kt * tk, tk)
        k_tile, vt_tile = ksk_ref[pl.ds(k0, tk), :], ksvt_ref[:, pl.ds(k0, tk)]
        return tuple(update(states[g], jnp.dot(k_tile, qas[g], preferred_element_type=F32), None, vt_tile, g)
                     for g in range(G_B))

    def near_tiles(kt, states):
        k0 = pl.multiple_of(kt * tk, tk)
        rel = qpos - k0
        causal = krow <= rel
        pen_causal = jnp.where(causal, 0.0, NEG_BIG)
        pen_win = jnp.where(causal & (krow > rel - WINDOW), 0.0, NEG_BIG)
        k_tile, vt_tile = ksk_ref[pl.ds(k0, tk), :], ksvt_ref[:, pl.ds(k0, tk)]
        kw_tile, vwt_tile = kwk_ref[pl.ds(k0, tk), :], kwvt_ref[:, pl.ds(k0, tk)]
        new_sel = tuple(update(states[g], jnp.dot(k_tile, qas[g], preferred_element_type=F32),
                               pen_causal, vt_tile, g) for g in range(G_B))
        new_win = tuple(update(states[G_B + g], jnp.dot(kw_tile, qts[g], preferred_element_type=F32),
                               pen_win, vwt_tile, g) for g in range(G_B))
        return new_sel + new_win

    init = (jnp.full((1, w4), NEG_BIG, F32), jnp.zeros((1, w4), F32), jnp.zeros((HD_B, w4), F32))

    def looped(_):
        first_near = jnp.maximum((q0 - WINDOW) // tk, 0)
        st = lax.fori_loop(0, first_near, far_tiles, (init,) * G_B)
        return lax.fori_loop(first_near, (q0 + tq) // tk, near_tiles, st + (init,) * G_B)

    def straight(_):
        n_w = WINDOW // tk
        lane_q = lax.broadcasted_iota(jnp.int32, (tk, tq), 1)
        pen_oldest = jnp.where(krow > lane_q, 0.0, NEG_BIG)
        pen_diag = jnp.where(krow <= lane_q, 0.0, NEG_BIG)

        def sel(kt, st, pen):
            k0 = pl.multiple_of(kt * tk, tk)
            k_tile, vt_tile = ksk_ref[pl.ds(k0, tk), :], ksvt_ref[:, pl.ds(k0, tk)]
            return tuple(update(st[g], jnp.dot(k_tile, qas[g], preferred_element_type=F32), pen, vt_tile, g)
                         for g in range(G_B))

        def win(kt, st, pen):
            k0 = pl.multiple_of(kt * tk, tk)
            kw_tile, vwt_tile = kwk_ref[pl.ds(k0, tk), :], kwvt_ref[:, pl.ds(k0, tk)]
            return tuple(update(st[g], jnp.dot(kw_tile, qts[g], preferred_element_type=F32), pen, vwt_tile, g)
                         for g in range(G_B))

        n_old = qi
        s_st = lax.fori_loop(0, n_old // 2, lambda i, st: sel(2 * i + 1, sel(2 * i, st, None), None), (init,) * G_B)
        s_st = lax.fori_loop(2 * (n_old // 2), n_old, lambda kt, st: sel(kt, st, None), s_st)
        w_st = win(qi - n_w, (init,) * G_B, pen_oldest)
        for d in range(n_w - 1, 0, -1):
            w_st = win(qi - d, w_st, None)
        return sel(qi, s_st, pen_diag) + win(qi, w_st, pen_diag)

    if tq == tk and WINDOW % tk == 0:
        states = lax.cond(qi >= WINDOW // tk, straight, looped, 0)
    else:
        states = looped(0)
    finish = lambda st: st[2] / jnp.where(st[1] > 0, st[1], 1.0)
    outs = []
    for g in range(G_B):
        o_c, o_s, o_w = o_cs[g], finish(states[g]), finish(states[G_B + g])
        for hg in range(HG_B):
            c = (g * HG_B + hg) * 3
            sl = slice(hg * tq, (hg + 1) * tq)
            outs.append(g_t[c:c + 1, :] * o_c[:, sl] + g_t[c + 1:c + 2, :] * o_s[:, sl]
                        + g_t[c + 2:c + 3, :] * o_w[:, sl])
    o_ref[...] = jnp.concatenate(outs, axis=0).T.astype(o_ref.dtype)


def _nsa_prompt(proj3, col_qb, col_gb, prep, tq=256, tk=256):
    bsz, t, _ = proj3.shape
    tq, tk = min(tq, t), min(tk, t)
    kck, kcvt, ksk, ksvt, kwk, kwvt = prep
    n_ch = kck.shape[1]
    n_cmp = n_ch - L_CMP // D_CMP + 1
    n_sb = -(-t // L_SLC)
    half = KV_B // 2
    mm = jnp.asarray(_cmp_to_sel_map(n_ch, n_cmp, n_sb))
    per_b = lambda shape: pl.BlockSpec((None,) + shape, lambda i, j: (i, 0, 0))
    return pl.pallas_call(
        functools.partial(_nsa_prompt_kernel, tq, tk, n_sb),
        grid=(bsz, t // tq),
        in_specs=[pl.BlockSpec((None, tq, MIX_B), lambda i, j: (i, j, col_qb // MIX_B)),
                  pl.BlockSpec((None, tq, 128), lambda i, j: (i, j, col_gb // 128)),
                  per_b((n_ch, half)), per_b((half, n_ch)), per_b((t, KV_B)), per_b((half, t)),
                  per_b((t, half)), per_b((half, t)),
                  pl.BlockSpec((n_sb, n_ch), lambda i, j: (0, 0))],
        out_specs=pl.BlockSpec((None, tq, MIX_B), lambda i, j: (i, j, 0)),
        out_shape=jax.ShapeDtypeStruct((bsz, t, MIX_B), BF16),
        compiler_params=_cparams(("parallel", "parallel"), 40),
        name="nsa_prompt",
    )(proj3, proj3, kck, kcvt, ksk, ksvt, kwk, kwvt, mm)


def _pages_view(cache):
    n_phys, depth, page = cache.shape[:3]
    return jnp.transpose(cache, (0, 1, 3, 4, 5, 2)).reshape(n_phys, depth, 2, KV_B // 2, page)


def _cmp_pages_kernel(n_pg, page, *refs):
    pages, w1_ref, o_ref, xs_ref = refs[2:2 + n_pg], refs[2 + n_pg], refs[3 + n_pg], refs[4 + n_pg]
    half = KV_B // 2
    ch_pg = page // D_CMP
    n_ch = n_pg * ch_pg
    r = lax.broadcasted_iota(jnp.int32, (page, page), 0)
    pos = lax.broadcasted_iota(jnp.int32, (page, page), 1)
    pick = jnp.where(pos == (r % ch_pg) * D_CMP + r // ch_pg, 1.0, 0.0).astype(BF16)
    for i, pg in enumerate(pages):
        kv_t = pg[...].reshape(KV_B, page).astype(BF16)
        y = lax.dot_general(pick, kv_t, (((1,), (1,)), ((), ())), preferred_element_type=F32)
        for e in range(2):
            for j in range(D_CMP):
                xs_ref[e, j, i * ch_pg:(i + 1) * ch_pg, :] = y[j * ch_pg:(j + 1) * ch_pg, e * half:(e + 1) * half]
    for e in range(2):
        acc = jnp.zeros((n_ch, 2 * half), F32)
        for j in range(D_CMP):
            acc = acc + jnp.dot(xs_ref[e, j].astype(BF16), w1_ref[e, j], preferred_element_type=F32)
        o_ref[:, e * half:(e + 1) * half] = acc[:, :half]
        o_ref[:, KV_B + e * half:KV_B + (e + 1) * half] = acc[:, half:]


def _cmp_pages(pages, layer, page_table, w1e):
    n_phys, depth, _, _, page = pages.shape
    bsz, n_pages = page_table.shape
    ch_pg = page // D_CMP
    n_pg = next(c for c in (32, 16, 8, 4, 2, 1) if n_pages % c == 0)

    def page_spec(i):
        return pl.BlockSpec((None, None, 2, KV_B // 2, page),
                            lambda b, s, pt, lyr: (pt[b * n_pages + s * n_pg + i], lyr[0], 0, 0, 0))

    grid_spec = pltpu.PrefetchScalarGridSpec(
        num_scalar_prefetch=2,
        grid=(bsz, n_pages // n_pg),
        in_specs=[page_spec(i) for i in range(n_pg)] + [pl.BlockSpec(w1e.shape, lambda b, s, pt, lyr: (0, 0, 0, 0))],
        out_specs=pl.BlockSpec((None, n_pg * ch_pg, 2 * KV_B), lambda b, s, pt, lyr: (b, s, 0)),
        scratch_shapes=[pltpu.VMEM((2, D_CMP, n_pg * ch_pg, KV_B // 2), F32)])
    return pl.pallas_call(
        functools.partial(_cmp_pages_kernel, n_pg, page),
        grid_spec=grid_spec,
        out_shape=jax.ShapeDtypeStruct((bsz, n_pages * ch_pg, 2 * KV_B), F32),
        compiler_params=_cparams(("parallel", "parallel"), 48),
        name="nsa_cmp_pages",
    )(page_table.reshape(-1), jnp.full((1,), layer, jnp.int32), *([pages] * n_pg), w1e)


def _nsa_score_kernel(past, sd, n_cmp, n_sb, pp_ref, q_ref, cw_ref, nw_ref, b1_ref, w2_ref, b2_ref, mm_ref,
                      oc_ref, ow_ref, idx_ref):
    half = KV_B // 2
    rq = HG_B * sd
    scale = HD_B ** -0.5
    n_ch = pp_ref.shape[0]
    win = cw_ref.shape[2]
    kc = _compress_tail(pp_ref[...], n_cmp, b1_ref, w2_ref, b2_ref)
    kck, kcv = kc[:, :half].astype(BF16), kc[:, half:].astype(BF16)
    qpos = past + lax.broadcasted_iota(jnp.int32, (rq, 1), 0) % sd
    nt = (((1,), (1,)), ((), ()))

    def softmax(parts):
        m = functools.reduce(jnp.maximum, [jnp.max(jnp.where(mk, s, NEG_BIG), axis=-1, keepdims=True) for s, mk in parts])
        es = [jnp.where(mk, jnp.exp(s - m), 0.0) for s, mk in parts]
        d = functools.reduce(jnp.add, [jnp.sum(e, axis=-1, keepdims=True) for e in es])
        return [e / jnp.where(d > 0, d, 1.0) for e in es]

    imps = []
    for g in range(G_B):
        qg = q_ref[g * rq:(g + 1) * rq, :]
        s = lax.dot_general(qg, kck, nt, preferred_element_type=F32) * scale
        c_end = lax.broadcasted_iota(jnp.int32, (rq, n_ch), 1) * D_CMP + (L_CMP - 1)
        p, = softmax([(s, c_end <= qpos)])
        oc_ref[g * rq:(g + 1) * rq, :] = jnp.dot(p.astype(BF16), kcv, preferred_element_type=F32)
        imps.append(functools.reduce(jnp.add, [p[hg * sd:(hg + 1) * sd, :] for hg in range(HG_B)]))
        s1 = jnp.dot(qg, cw_ref[0].astype(BF16), preferred_element_type=F32) * scale
        s2 = lax.dot_general(qg, nw_ref[:, :half].astype(BF16), nt, preferred_element_type=F32) * scale
        wp1 = past - win + lax.broadcasted_iota(jnp.int32, s1.shape, 1)
        j2 = lax.broadcasted_iota(jnp.int32, s2.shape, 1)
        wp2 = past + j2
        p1, p2 = softmax([(s1, (wp1 <= qpos) & (wp1 > qpos - WINDOW) & (wp1 >= 0)),
                          (s2, (wp2 <= qpos) & (wp2 > qpos - WINDOW) & (j2 < sd))])
        ow_ref[g * rq:(g + 1) * rq, :] = (
            lax.dot_general(p1.astype(BF16), cw_ref[1].astype(BF16), nt, preferred_element_type=F32)
            + jnp.dot(p2.astype(BF16), nw_ref[:, half:].astype(BF16), preferred_element_type=F32))
    imp = jnp.concatenate(imps, axis=0)
    rows = G_B * sd
    bs = jnp.dot(imp, mm_ref[...], precision=HIGHEST, preferred_element_type=F32)
    n_lane = bs.shape[1]
    blk = lax.broadcasted_iota(jnp.int32, (rows, n_lane), 1)
    cur = (past + lax.broadcasted_iota(jnp.int32, (rows, 1), 0) % sd) >> 6
    forced = (blk == 0) | (blk == cur) | (blk == cur - 1)
    score = jnp.where(blk <= cur, jnp.where(forced, jnp.inf, bs), -jnp.inf)
    blk_f = blk.astype(F32)
    taken = blk >= n_sb
    lane = lax.broadcasted_iota(jnp.int32, (rows, 128), 1)
    picked = jnp.zeros((rows, 128), F32)
    for it in range(min(N_SEL, n_sb)):
        live = jnp.where(taken, -jnp.inf, score)
        mx = jnp.max(live, axis=-1, keepdims=True)
        ix = jnp.min(jnp.where(~taken & (score == mx), blk_f, float(n_lane)), axis=-1, keepdims=True)
        picked = jnp.where(lane == it, ix, picked)
        taken = taken | (blk_f == ix)
    idx_ref[...] = picked.astype(jnp.int32)


def _nsa_select_kernel(past, sd, n_sb, n_pages, k_sel, *refs):
    pt_ref, ix_ref, lyr_ref = refs[:3]
    pages = refs[3:3 + k_sel]
    new_ref, q_ref, oc_ref, ow_ref, gt_ref, o_ref = refs[3 + k_sel:]
    rq = HG_B * sd
    page = new_ref.shape[-1]
    blk_pg = page // L_SLC
    b, g, i = pl.program_id(0), pl.program_id(1), pl.program_id(2)
    base = ((b * G_B + g) * sd + i) * k_sel
    lane = lax.broadcasted_iota(jnp.int32, (1, k_sel * page), 1)
    in_page = lane % page
    kpos = in_page
    picked = lane < 0
    kts, vts = [], []
    for j in range(k_sel):
        blk = ix_ref[base + j]
        is_new = blk == n_sb - 1
        kts.append(jnp.where(is_new, new_ref[0], pages[j][0]))
        vts.append(jnp.where(is_new, new_ref[1], pages[j][1]))
        mine = lane // page == j
        kpos = kpos + jnp.where(mine, (blk // blk_pg) * page, 0)
        picked = picked | (mine & (in_page // L_SLC == blk % blk_pg))
    kt_all = jnp.concatenate(kts, axis=1).astype(BF16)
    vt_all = jnp.concatenate(vts, axis=1).astype(BF16)
    s = jnp.dot(q_ref[...], kt_all, preferred_element_type=F32) * (HD_B ** -0.5)
    row_q = lax.broadcasted_iota(jnp.int32, (rq, 1), 0) % sd
    mask = picked & (kpos <= past + row_q)
    m = jnp.max(jnp.where(mask, s, NEG_BIG), axis=-1, keepdims=True)
    e = jnp.where(mask, jnp.exp(s - m), 0.0)
    d = jnp.sum(e, axis=-1, keepdims=True)
    p = e / jnp.where(d > 0, d, 1.0)
    o_s = lax.dot_general(p.astype(BF16), vt_all, (((1,), (1,)), ((), ())), preferred_element_type=F32)
    gates = _sigmoid(gt_ref[...])
    y = gates[0] * oc_ref[...] + gates[1] * o_s + gates[2] * ow_ref[...]

    @pl.when(i == 0)
    def _():
        o_ref[...] = jnp.zeros_like(o_ref)

    o_ref[...] += jnp.where(row_q == i, y, 0.0)


def _nsa_sample(proj3, cols, kvs, sd, cmp_big, pages_cmp, pages_slc, win_view, layer, page_table):
    col_qb, col_kv, col_gb = cols
    bsz = proj3.shape[0]
    n_phys, depth, _, _, page = pages_cmp.shape
    n_pages = page_table.shape[1]
    past = n_pages * page
    win = win_view.shape[-1]
    half = KV_B // 2
    rq = HG_B * sd
    n_ch = past // D_CMP
    n_cmp = (past + sd) // D_CMP - L_CMP // D_CMP + 1
    n_sb = -(-(past + sd) // L_SLC)
    k_sel = min(N_SEL, n_sb)
    assert (past + sd) // D_CMP == n_ch and past % L_SLC == 0 and sd <= L_SLC and page % L_SLC == 0
    _, b1big, w2big, b2big, w1tok = cmp_big
    pp = _cmp_pages(pages_cmp, layer, page_table, w1tok)
    q = proj3[:, :sd, col_qb:col_qb + MIX_B].reshape(bsz, sd, G_B, HG_B, HD_B).transpose(0, 2, 3, 1, 4)
    q = jnp.stack([jnp.pad(q[:, g], ((0, 0), (0, 0), (0, 0), (g * HD_B, half - (g + 1) * HD_B))) for g in range(G_B)], 1)
    q = q.reshape(bsz, G_B * rq, half).astype(BF16)
    gt = proj3[:, :sd, col_gb:col_gb + 3 * H_B].astype(F32).reshape(bsz, sd, G_B, HG_B, 3).transpose(0, 4, 2, 3, 1)
    gt = jnp.broadcast_to(gt.reshape(bsz, 3, G_B * rq, 1), (bsz, 3, G_B * rq, half))
    n_lane = -(-n_sb // 128) * 128
    mm = jnp.asarray(np.pad(_cmp_to_sel_map(n_ch, n_cmp, n_sb), ((0, n_lane - n_sb), (0, 0))).T)
    per_b = lambda shape: pl.BlockSpec((None,) + shape, lambda i: (i,) + (0,) * len(shape))
    const = lambda shape: pl.BlockSpec(shape, lambda i: (0,) * len(shape))
    o_c, o_w, idx = pl.pallas_call(
        functools.partial(_nsa_score_kernel, past, sd, n_cmp, n_sb),
        grid=(bsz,),
        in_specs=[per_b((n_ch, 2 * KV_B)), per_b((G_B * rq, half)),
                  pl.BlockSpec((None, None, 2, half, win), lambda i: (layer, i, 0, 0, 0)),
                  pl.BlockSpec((None, PAD_ROWS, KV_B), lambda i: (i, 0, col_kv // KV_B + 2)),
                  const((1, KV_B)), const((KV_B, KV_B)), const((1, KV_B)), const((n_ch, n_lane))],
        out_specs=[per_b((G_B * rq, half)), per_b((G_B * rq, half)), per_b((G_B * sd, 128))],
        out_shape=[jax.ShapeDtypeStruct((bsz, G_B * rq, half), F32), jax.ShapeDtypeStruct((bsz, G_B * rq, half), F32),
                   jax.ShapeDtypeStruct((bsz, G_B * sd, 128), jnp.int32)],
        compiler_params=_cparams(("parallel",), 40),
        name="nsa_score_sample",
    )(pp, q, win_view, proj3, b1big, w2big, b2big, mm)
    blk_pg = page // L_SLC
    n_cached = past // L_SLC
    new_blk = jnp.pad(kvs.reshape(bsz, sd, 2, half).transpose(0, 2, 3, 1), ((0, 0), (0, 0), (0, 0), (0, page - sd)))

    def blk_spec(j):
        def index(b, g, i, pt, ix, lyr):
            blk = jnp.minimum(ix[((b * G_B + g) * sd + i) * k_sel + j], n_cached - 1)
            return pt[b * n_pages + blk // blk_pg], lyr[0], 0, 0, 0
        return pl.BlockSpec((None, None, 2, half, page), index)

    grp = lambda b, g, i, pt, ix, lyr: (b, g, 0)
    grid_spec = pltpu.PrefetchScalarGridSpec(
        num_scalar_prefetch=3,
        grid=(bsz, G_B, sd),
        in_specs=[blk_spec(j) for j in range(k_sel)] + [
            pl.BlockSpec((None, 2, half, page), lambda b, g, i, pt, ix, lyr: (b, 0, 0, 0)),
            pl.BlockSpec((None, rq, half), grp), pl.BlockSpec((None, rq, half), grp), pl.BlockSpec((None, rq, half), grp),
            pl.BlockSpec((None, 3, rq, half), lambda b, g, i, pt, ix, lyr: (b, 0, g, 0))],
        out_specs=pl.BlockSpec((None, rq, half), grp))
    y = pl.pallas_call(
        functools.partial(_nsa_select_kernel, past, sd, n_sb, n_pages, k_sel),
        grid_spec=grid_spec,
        out_shape=jax.ShapeDtypeStruct((bsz, G_B * rq, half), F32),
        compiler_params=_cparams(("parallel", "parallel", "arbitrary"), 40),
        name="nsa_select_sample",
    )(page_table.reshape(-1), idx[:, :, :k_sel].reshape(-1), jnp.full((1,), layer, jnp.int32),
      *([pages_slc] * k_sel), new_blk, q, o_c, o_w, gt)
    y = y.reshape(bsz, G_B, HG_B, sd, G_B, HD_B)
    y = jnp.stack([y[:, g, :, :, g] for g in range(G_B)], axis=1)
    return y.transpose(0, 3, 1, 2, 4).reshape(bsz, sd, MIX_B)


def _prep_w_in(w, d):
    o = np.cumsum([0, MIX_A, MIX_A, MIX_A, MIX_A, MIX_B, KV_B, KV_B, KV_B, 3 * H_B, 2 * C_CONV, 3 * d])
    parts = [w[:, o[0]:o[4]], w[:, o[10]:o[11]], w[:, o[4]:o[5]], w[:, o[9]:o[10]], w[:, o[5]:o[8]], w[:, o[8]:o[9]]]
    n = sum(p.shape[1] for p in parts)
    n_pad = -(-n // 512) * 512
    parts.append(jnp.zeros((w.shape[0], n_pad - n), w.dtype))
    return jnp.concatenate(parts, axis=1).astype(BF16)


def _layer(x3, n_valid, p, s0, conv_prefix, mem_kv, nsa_fn, alpha):
    bsz, t, d = x3.shape
    m = bsz * t
    col_qb = COL_MG + 3 * d
    col_glu = col_qb + MIX_B
    col_kv = col_glu + 2 * C_CONV
    col_gb = col_kv + 3 * KV_B
    proj = _matmul(x3.reshape(m, d), p['w_in'], 1024, 1536, out_dtype=BF16)
    proj3 = proj.reshape(bsz, t, -1)
    ya, s_new = _hgrn(proj3, p['lb'], p['hg_norm'], s0, n_valid)
    yc, conv_state = _conv(proj3, col_glu // C_CONV, conv_prefix, p['conv_w'], p['conv_b'],
                           p['conv_ln_g'], p['conv_ln_b'], n_valid)
    kv_shape = (bsz, n_valid, 2, G_B, HD_B)
    kvc = proj3[:, :n_valid, col_kv:col_kv + KV_B].astype(F32)
    kvs = proj3[:, :n_valid, col_kv + KV_B:col_kv + 2 * KV_B].astype(F32)
    kvw = proj3[:, :n_valid, col_kv + 2 * KV_B:col_kv + 3 * KV_B].astype(F32)
    yb, nsa_extra = nsa_fn(proj3, (col_qb, col_kv, col_gb), kvc, kvs, kvw)
    kvc, kvs = kvc.reshape(kv_shape), kvs.reshape(kv_shape)
    x1 = _merge(ya.reshape(m, MIX_A), yb.reshape(m, MIX_B), yc.reshape(m, C_CONV), proj, x3.reshape(m, d),
                p['w_pa'], p['w_pb'], p['w_pc'], p['w_out'], p['ln_g'][0:1], p['ln_b'][0:1], alpha)
    x2 = _xattn(x1.reshape(bsz, t, d), mem_kv[0], mem_kv[1], p['w_xq'], p['w_xo'], p['ln_g'][1:2], p['ln_b'][1:2], alpha)
    x3n = _mlp(x2.reshape(m, d), p['w_up'], p['w_down'], p['ln_g'][2:3], p['ln_b'][2:3], alpha)
    return x3n.reshape(bsz, t, d), kvc, kvs, nsa_extra, s_new, conv_state


def kernel(x_prompt, x_sample, cache_cmp, cache_slc, cache_win, state_hgrn, state_conv, cache_mem, page_table, mem_prompt, w_in, lb_raw, hg_norm, w_cmp1, b_cmp1, w_cmp2, b_cmp2, conv_w, conv_b, conv_ln_g, conv_ln_b, w_pa, w_pb, w_pc, w_out, ln_g, ln_b, w_xq, w_xkv, w_xo, w_up, w_down):
    bp, t, d = x_prompt.shape
    bd, sd = x_sample.shape[:2]
    depth = w_in.shape[0]
    n_mem = mem_prompt.shape[1]
    win_buf = cache_win.shape[2]
    alpha = (2 * depth) ** 0.25
    lb_cum = jnp.cumsum(jax.nn.softmax(lb_raw.astype(F32), axis=0), axis=0)
    lb_all = lb_cum - lb_cum[0]
    pages_cmp, pages_slc, win_view = _pages_view(cache_cmp), _pages_view(cache_slc), _pages_view(cache_win)
    mem_cache = _mem_tile_order(cache_mem, d)
    xp = x_prompt
    xs = jnp.pad(x_sample, ((0, 0), (0, PAD_ROWS - sd), (0, 0)))
    outs = {k: [] for k in ('cmp_p', 'cmp_s', 'slc_p', 'slc_s', 'win_p', 'win_s', 'hg_p', 'hg_s', 'cv_p', 'cv_s', 'mem_p')}
    for l in range(depth):
        p = {'w_in': _prep_w_in(w_in[l], d), 'lb': lb_all[l], 'hg_norm': hg_norm[l], 'conv_w': conv_w[l],
             'conv_b': conv_b[l].reshape(1, -1), 'conv_ln_g': conv_ln_g[l].reshape(1, -1),
             'conv_ln_b': conv_ln_b[l].reshape(1, -1),
             'w_pa': w_pa[l].astype(BF16), 'w_pb': w_pb[l].astype(BF16), 'w_pc': w_pc[l].astype(BF16),
             'w_out': w_out[l].astype(BF16), 'ln_g': ln_g[l], 'ln_b': ln_b[l],
             'w_xq': w_xq[l].astype(BF16), 'w_xo': w_xo[l].astype(BF16),
             'w_up': w_up[l].astype(BF16), 'w_down': w_down[l].astype(BF16)}
        cmp_w = (w_cmp1[l], b_cmp1[l], w_cmp2[l], b_cmp2[l])

        cmp_big = _cmp_weights(*cmp_w)

        def nsa_prompt(proj3, cols, kvc, kvs, kvw):
            col_qb, col_kv, col_gb = cols
            prep = _nsa_prep(proj3, col_kv, cmp_big, min(win_buf, t))
            o = _nsa_prompt(proj3, col_qb, col_gb, prep[:6])
            ct, st_, wt = prep[6:]
            return o, (ct, st_, jnp.pad(wt, ((0, 0), (0, 0), (max(win_buf - t, 0), 0))))

        def nsa_sample(proj3, cols, kvc, kvs, kvw):
            o = _nsa_sample(proj3, cols, kvs, sd, cmp_big, pages_cmp, pages_slc, win_view, l, page_table)
            o = jnp.pad(o, ((0, 0), (0, PAD_ROWS - sd), (0, 0))).astype(BF16)
            win = jnp.concatenate([cache_win[l], kvw.reshape(bd, sd, 2, G_B, HD_B)], axis=1)[:, -win_buf:]
            return o, win

        w_kv = _mem_tile_order(w_xkv[l].reshape(d, 2, NX_H, d // NX_H), d).astype(BF16)
        mem_kv = _matmul(mem_prompt.reshape(bp * n_mem, d), w_kv, 1024, 512).reshape(1, bp, n_mem, 2 * d)
        xp, kc, ks_, wn, sh, cv = _layer(xp, t, p, jnp.zeros((bp, H_A, DK_A, DV_A), F32),
                                         jnp.zeros((bp, CONV_K - 1, C_CONV), F32), (mem_kv, 0), nsa_prompt, alpha)
        outs['cmp_p'].append(wn[0]); outs['slc_p'].append(wn[1]); outs['win_p'].append(wn[2])
        outs['hg_p'].append(sh); outs['cv_p'].append(cv)
        outs['mem_p'].append(_mem_head_order(mem_kv[0], d))
        xs, kc, ks_, wn, sh, cv = _layer(xs, sd, p, state_hgrn[l], state_conv[l], (mem_cache, l), nsa_sample, alpha)
        outs['cmp_s'].append(kc); outs['slc_s'].append(ks_); outs['win_s'].append(wn)
        outs['hg_s'].append(sh); outs['cv_s'].append(cv)
    st = lambda k, ax: jnp.stack(outs[k], axis=ax)

    def rows(k, ax):
        a = st(k, ax)
        a = a.reshape(a.shape[:2] + (2, G_B, HD_B, a.shape[-1]))
        return jnp.transpose(a, (0, 1, 5, 2, 3, 4))

    return (xp, xs[:, :sd],
            rows('cmp_p', 1), st('cmp_s', 1), rows('slc_p', 1), st('slc_s', 1),
            rows('win_p', 0), st('win_s', 0), st('hg_p', 0), st('hg_s', 0),
            st('cv_p', 0), st('cv_s', 0), st('mem_p', 0))
```

```python
import functools

import numpy as np
import jax
import jax.numpy as jnp
from jax import lax
from jax.experimental import pallas as pl
from jax.experimental.pallas import tpu as pltpu

F32 = jnp.float32
BF16 = jnp.bfloat16
HIGHEST = lax.Precision.HIGHEST

H_A, DK_A, DV_A, CHUNK_A = 4, 128, 128, 16
H_B, G_B, HG_B, HD_B = 8, 2, 4, 64
L_CMP, D_CMP, L_SLC, N_SEL, WINDOW = 32, 16, 64, 16, 512
C_CONV, CONV_K = 512, 31
NX_H = 4
LN_EPS, RMS_EPS = 1e-5, 1e-6
PAD_ROWS = 16
LANES = 128

MIX_A = H_A * DK_A
MIX_B = H_B * HD_B
KV_B = 2 * G_B * HD_B
COL_MG = 4 * MIX_A


def _cparams(sem, vmem_mb=None):
    return pltpu.CompilerParams(dimension_semantics=sem,
                                vmem_limit_bytes=None if vmem_mb is None else vmem_mb * 2**20)


def _ln(y, g, b):
    mu = jnp.mean(y, axis=-1, keepdims=True)
    d = y - mu
    var = jnp.mean(d * d, axis=-1, keepdims=True)
    return d * lax.rsqrt(var + LN_EPS) * g + b


def _sigmoid(x):
    return 1.0 / (1.0 + jnp.exp(-x))


def _mm_kernel(x_ref, w_ref, o_ref, xb_ref):
    @pl.when(pl.program_id(1) == 0)
    def _():
        xb_ref[...] = x_ref[...].astype(BF16)

    o_ref[...] = jnp.dot(xb_ref[...], w_ref[...], preferred_element_type=F32).astype(o_ref.dtype)


def _matmul(x, w, tm, tn, out_dtype=F32):
    m, k = x.shape
    n = w.shape[1]
    tm, tn = min(tm, m), min(tn, n)
    return pl.pallas_call(
        _mm_kernel,
        grid=(m // tm, n // tn),
        in_specs=[pl.BlockSpec((tm, k), lambda i, j: (i, 0)),
                  pl.BlockSpec((k, tn), lambda i, j: (0, j))],
        out_specs=pl.BlockSpec((tm, tn), lambda i, j: (i, j)),
        out_shape=jax.ShapeDtypeStruct((m, n), out_dtype),
        scratch_shapes=[pltpu.VMEM((tm, k), BF16)],
        compiler_params=_cparams(("parallel", "arbitrary"), 40),
        name="proj_matmul",
    )(x, w)


def _mlp_kernel(alpha, x_ref, wu_ref, wd_ref, g_ref, b_ref, o_ref, xb_ref, acc_ref):
    j = pl.program_id(1)

    @pl.when(j == 0)
    def _():
        xb_ref[...] = x_ref[...].astype(BF16)
        acc_ref[...] = jnp.zeros_like(acc_ref)

    h = jnp.dot(xb_ref[...], wu_ref[...], preferred_element_type=F32)
    h = jnp.square(jnp.maximum(h, 0.0)).astype(BF16)
    acc_ref[...] += jnp.dot(h, wd_ref[...], preferred_element_type=F32)

    @pl.when(j == pl.num_programs(1) - 1)
    def _():
        o_ref[...] = _ln(alpha * x_ref[...] + acc_ref[...], g_ref[...], b_ref[...])


def _mlp(x, w_up, w_down, g, b, alpha, tm=512, tf=2048):
    m, d = x.shape
    ff = w_up.shape[1]
    tm = min(tm, m)
    return pl.pallas_call(
        functools.partial(_mlp_kernel, alpha),
        grid=(m // tm, ff // tf),
        in_specs=[pl.BlockSpec((tm, d), lambda i, j: (i, 0)),
                  pl.BlockSpec((d, tf), lambda i, j: (0, j)),
                  pl.BlockSpec((tf, d), lambda i, j: (j, 0)),
                  pl.BlockSpec((1, d), lambda i, j: (0, 0)),
                  pl.BlockSpec((1, d), lambda i, j: (0, 0))],
        out_specs=pl.BlockSpec((tm, d), lambda i, j: (i, 0)),
        out_shape=jax.ShapeDtypeStruct((m, d), F32),
        scratch_shapes=[pltpu.VMEM((tm, d), BF16), pltpu.VMEM((tm, d), F32)],
        compiler_params=_cparams(("parallel", "arbitrary"), 48),
        name="mlp",
    )(x, w_up, w_down, g, b)


def _merge_kernel(alpha, ya_ref, yb_ref, yc_ref, ma_ref, mb_ref, mc_ref, x_ref,
                  wpa_ref, wpb_ref, wpc_ref, wout_ref, g_ref, b_ref, o_ref):
    def branch(y_ref, m_ref, w_ref):
        return _sigmoid(m_ref[...].astype(F32)) * jnp.dot(y_ref[...].astype(BF16), w_ref[...], preferred_element_type=F32)

    merged = branch(ya_ref, ma_ref, wpa_ref) + branch(yb_ref, mb_ref, wpb_ref) + branch(yc_ref, mc_ref, wpc_ref)
    y = jnp.dot(merged.astype(BF16), wout_ref[...], preferred_element_type=F32)
    o_ref[...] = _ln(alpha * x_ref[...] + y, g_ref[...], b_ref[...])


def _merge(ya, yb, yc, proj, x, wpa, wpb, wpc, wout, g, b, alpha, tm=512):
    m, d = x.shape
    tm = min(tm, m)
    mg0 = COL_MG // d
    row = lambda i: (i, 0)
    const = lambda i: (0, 0)
    return pl.pallas_call(
        functools.partial(_merge_kernel, alpha),
        grid=(m // tm,),
        in_specs=[pl.BlockSpec((tm, MIX_A), row), pl.BlockSpec((tm, MIX_B), row), pl.BlockSpec((tm, C_CONV), row),
                  pl.BlockSpec((tm, d), lambda i: (i, mg0)), pl.BlockSpec((tm, d), lambda i: (i, mg0 + 1)),
                  pl.BlockSpec((tm, d), lambda i: (i, mg0 + 2)),
                  pl.BlockSpec((tm, d), row),
                  pl.BlockSpec((MIX_A, d), const), pl.BlockSpec((MIX_B, d), const), pl.BlockSpec((C_CONV, d), const),
                  pl.BlockSpec((d, d), const), pl.BlockSpec((1, d), const), pl.BlockSpec((1, d), const)],
        out_specs=pl.BlockSpec((tm, d), row),
        out_shape=jax.ShapeDtypeStruct((m, d), F32),
        compiler_params=_cparams(("parallel",), 48),
        name="merge_out",
    )(ya, yb, yc, proj, proj, proj, x, wpa, wpb, wpc, wout, g, b)


def _xattn_kernel(alpha, x_ref, kv_ref, wq_ref, wo_ref, g_ref, b_ref, o_ref):
    x = x_ref[...]
    d = x.shape[-1]
    hd = d // NX_H
    q = jnp.dot(x.astype(BF16), wq_ref[...], preferred_element_type=F32)
    n_dt = hd // LANES

    def head(base, h):
        parts = [kv_ref[:, base + (dt * NX_H + h) * LANES:base + (dt * NX_H + h + 1) * LANES] for dt in range(n_dt)]
        return jnp.concatenate(parts, axis=1).astype(BF16)

    outs = []
    for h in range(NX_H):
        qh = q[:, h * hd:(h + 1) * hd].astype(BF16)
        kh, vh = head(0, h), head(d, h)
        s = lax.dot_general(qh, kh, (((1,), (1,)), ((), ())), preferred_element_type=F32) * (hd ** -0.5)
        e = jnp.exp(s - jnp.max(s, axis=-1, keepdims=True))
        p = e / jnp.sum(e, axis=-1, keepdims=True)
        outs.append(jnp.dot(p.astype(BF16), vh, preferred_element_type=F32))
    o = jnp.concatenate(outs, axis=-1)
    y = jnp.dot(o.astype(BF16), wo_ref[...], preferred_element_type=F32)
    o_ref[...] = _ln(alpha * x + y, g_ref[...], b_ref[...])


def _mem_tile_order(a, d):
    lead = a.shape[:-3]
    a = a.reshape(lead + (2, NX_H, d // NX_H // LANES, LANES))
    return jnp.swapaxes(a, -3, -2).reshape(lead + (2 * d,))


def _mem_head_order(a, d):
    lead = a.shape[:-1]
    a = a.reshape(lead + (2, d // NX_H // LANES, NX_H, LANES))
    return jnp.swapaxes(a, -3, -2).reshape(lead + (2, NX_H, d // NX_H))


def _xattn(x, kv, layer, wq, wo, g, b, alpha, tm=512):
    bsz, t, d = x.shape
    n_mem = kv.shape[2]
    tm = min(tm, t)
    const = lambda i, j: (0, 0)
    return pl.pallas_call(
        functools.partial(_xattn_kernel, alpha),
        grid=(bsz, t // tm),
        in_specs=[pl.BlockSpec((None, tm, d), lambda i, j: (i, j, 0)),
                  pl.BlockSpec((None, None, n_mem, 2 * d), lambda i, j: (layer, i, 0, 0)),
                  pl.BlockSpec((d, d), const), pl.BlockSpec((d, d), const),
                  pl.BlockSpec((1, d), const), pl.BlockSpec((1, d), const)],
        out_specs=pl.BlockSpec((None, tm, d), lambda i, j: (i, j, 0)),
        out_shape=jax.ShapeDtypeStruct((bsz, t, d), F32),
        compiler_params=_cparams(("parallel", "parallel"), 48),
        name="xattn",
    )(x, kv, wq, wo, g, b)


CONV_HALO = 32
CONV_SUB = 32


def _conv_kernel(rt, n_valid_last, a_ref, gt_ref, pre_ref, w_ref, cb_ref, g_ref, b_ref, y_ref, st_ref, ue_ref, sh_ref):
    t = pl.program_id(1)
    off = CONV_HALO - (CONV_K - 1)
    sl = 8

    @pl.when(t == 0)
    def _():
        ue_ref[0:off, :] = jnp.zeros((off, C_CONV), F32)
        ue_ref[off:CONV_HALO, :] = pre_ref[...]

    ue_ref[CONV_HALO:CONV_HALO + rt, :] = a_ref[...].astype(F32) * _sigmoid(gt_ref[...].astype(F32))
    n_sh = rt + CONV_HALO - sl
    for s in range(1, sl):
        sh_ref[s - 1, 0:n_sh, :] = ue_ref[s:s + n_sh, :]
    sub = min(CONV_SUB, rt)
    bias = jnp.broadcast_to(cb_ref[...], (sl, C_CONV))
    for r0 in range(0, rt, sub):
        accs = [bias] * (sub // sl)
        for j in range(CONV_K):
            a, s = divmod(off + j, sl)
            w_j = w_ref[j * sl:(j + 1) * sl, :]
            for rg in range(sub // sl):
                lo = r0 + (a + rg) * sl
                win = ue_ref[lo:lo + sl, :] if s == 0 else sh_ref[s - 1, lo:lo + sl, :]
                accs[rg] = accs[rg] + w_j * win
        acc = jnp.concatenate(accs, axis=0)
        y = _ln(acc, g_ref[...], b_ref[...])
        y_ref[r0:r0 + sub, :] = (y * _sigmoid(y)).astype(y_ref.dtype)

    @pl.when(t == pl.num_programs(1) - 1)
    def _():
        st_ref[...] = ue_ref[off + n_valid_last:off + n_valid_last + CONV_K - 1, :]

    ue_ref[0:CONV_HALO, :] = ue_ref[rt:rt + CONV_HALO, :]


def _conv(proj3, col_a, prefix, w, cb, g, b, n_valid, rt=256):
    bsz, t, _ = proj3.shape
    rt = min(rt, t)
    n_valid_last = n_valid - (t - rt)
    vec = lambda i, j: (0, 0)
    return pl.pallas_call(
        functools.partial(_conv_kernel, rt, n_valid_last),
        grid=(bsz, t // rt),
        in_specs=[pl.BlockSpec((None, rt, C_CONV), lambda i, j: (i, j, col_a)),
                  pl.BlockSpec((None, rt, C_CONV), lambda i, j: (i, j, col_a + 1)),
                  pl.BlockSpec((None, CONV_K - 1, C_CONV), lambda i, j: (i, 0, 0)),
                  pl.BlockSpec((CONV_K * 8, C_CONV), vec), pl.BlockSpec((1, C_CONV), vec),
                  pl.BlockSpec((1, C_CONV), vec), pl.BlockSpec((1, C_CONV), vec)],
        out_specs=[pl.BlockSpec((None, rt, C_CONV), lambda i, j: (i, j, 0)),
                   pl.BlockSpec((None, CONV_K - 1, C_CONV), lambda i, j: (i, 0, 0))],
        out_shape=[jax.ShapeDtypeStruct((bsz, t, C_CONV), BF16),
                   jax.ShapeDtypeStruct((bsz, CONV_K - 1, C_CONV), F32)],
        scratch_shapes=[pltpu.VMEM((rt + CONV_HALO, C_CONV), F32), pltpu.VMEM((7, rt + CONV_HALO - 8, C_CONV), F32)],
        compiler_params=_cparams(("parallel", "arbitrary"), 32),
        name="conformer_conv",
    )(proj3, proj3, prefix, jnp.repeat(w, 8, axis=0), cb, g, b)


def _hgrn_kernel(tt, n_valid, q_ref, f_ref, i_ref, g_ref, la_ref, l1_ref, gn_ref, s0_ref,
                 y_ref, s_ref, st_ref, qs_ref, kk_ref, bb_ref, o32_ref):
    c = CHUNK_A
    t = pl.program_id(1)

    @pl.when(t == 0)
    def _():
        for h in range(H_A):
            st_ref[h] = s0_ref[h].T

    z = f_ref[...].astype(F32)
    sp = jnp.log1p(jnp.exp(-jnp.abs(z)))
    cc = l1_ref[...] + (jnp.minimum(z, 0.0) - sp)
    a = la_ref[...]
    logf = jnp.maximum(a, cc) + jnp.log1p(jnp.exp(-jnp.abs(a - cc)))
    logk = l1_ref[...] + (jnp.minimum(-z, 0.0) - sp)
    row = lax.broadcasted_iota(jnp.int32, (tt, 1), 0)
    if n_valid < tt:
        logf = jnp.where(row < n_valid, logf, 0.0)
        logk = jnp.where(row < n_valid, logk, -jnp.inf)
    b = logf
    rc = row & (c - 1)
    sh = 1
    while sh < c:
        b = b + jnp.where(rc >= sh, pltpu.roll(b, sh, 0), 0.0)
        sh *= 2
    q = q_ref[...].astype(F32)
    qs_ref[...] = q * _sigmoid(q)
    kk_ref[...] = logk
    bb_ref[...] = b

    sl = 8
    row8 = lax.broadcasted_iota(jnp.int32, (sl, 1), 0)

    def chunk(ci, carry):
        r0 = pl.multiple_of(ci * c, c)
        for h in range(H_A):
            hs = slice(h * DK_A, (h + 1) * DK_A)
            qc = qs_ref[pl.ds(r0, c), hs]
            lkc = kk_ref[pl.ds(r0, c), hs]
            bc = bb_ref[pl.ds(r0, c), hs]
            vc = i_ref[pl.ds(r0, c), hs].astype(F32)
            bl = bc[c - 1:c, :]
            st = st_ref[h]
            qe = (qc * jnp.exp(bc)).astype(BF16)
            o = lax.dot_general(qe, st.astype(BF16), (((1,), (1,)), ((), ())), preferred_element_type=F32)
            tiles = []
            for r in range(c // sl):
                rows = slice(r * sl, (r + 1) * sl)
                q_r, b_r, o_r = qc[rows], bc[rows], o[rows]
                for s in range((r + 1) * sl):
                    d = b_r - (bc[s:s + 1, :] - lkc[s:s + 1, :])
                    if s >= r * sl:
                        d = jnp.where(row8 >= s - r * sl, d, -jnp.inf)
                    att = jnp.sum(q_r * jnp.exp(d), axis=-1, keepdims=True)
                    o_r = o_r + att * vc[s:s + 1, :]
                tiles.append(o_r)
            o32_ref[pl.ds(r0, c), hs] = jnp.concatenate(tiles, axis=0)
            kd = jnp.exp(lkc + (bl - bc)).astype(BF16)
            u = lax.dot_general(vc.astype(BF16), kd, (((0,), (0,)), ((), ())), preferred_element_type=F32)
            st_ref[h] = st * jnp.exp(bl) + u
        return carry

    lax.fori_loop(0, tt // c, chunk, 0, unroll=8 if (tt // c) % 8 == 0 else 1)

    g = g_ref[...].astype(F32)
    gate = g * _sigmoid(g)
    for h in range(H_A):
        hs = slice(h * DV_A, (h + 1) * DV_A)
        o = o32_ref[:, hs]
        o = o * lax.rsqrt(jnp.mean(o * o, axis=-1, keepdims=True) + RMS_EPS) * gn_ref[...]
        y_ref[:, hs] = (o * gate[:, hs]).astype(y_ref.dtype)

    @pl.when(t == pl.num_programs(1) - 1)
    def _():
        for h in range(H_A):
            s_ref[h] = st_ref[h].T


def _hgrn(proj3, lb, gnorm, s0, n_valid, tt=256):
    bsz, t, _ = proj3.shape
    tt = min(tt, t)
    lb = lb.reshape(1, MIX_A).astype(F32)
    la, l1 = jnp.log(lb), jnp.log1p(-lb)
    vec = lambda i, j: (0, 0)
    col = lambda cb: pl.BlockSpec((None, tt, MIX_A), lambda i, j: (i, j, cb))
    st = pl.BlockSpec((None, H_A, DK_A, DV_A), lambda i, j: (i, 0, 0, 0))
    return pl.pallas_call(
        functools.partial(_hgrn_kernel, tt, n_valid if t == tt else tt),
        grid=(bsz, t // tt),
        in_specs=[col(0), col(1), col(2), col(3),
                  pl.BlockSpec((1, MIX_A), vec), pl.BlockSpec((1, MIX_A), vec),
                  pl.BlockSpec((1, DV_A), vec), st],
        out_specs=[pl.BlockSpec((None, tt, MIX_A), lambda i, j: (i, j, 0)), st],
        out_shape=[jax.ShapeDtypeStruct((bsz, t, MIX_A), BF16),
                   jax.ShapeDtypeStruct((bsz, H_A, DK_A, DV_A), F32)],
        scratch_shapes=[pltpu.VMEM((H_A, DV_A, DK_A), F32), pltpu.VMEM((tt, MIX_A), F32),
                        pltpu.VMEM((tt, MIX_A), F32), pltpu.VMEM((tt, MIX_A), F32), pltpu.VMEM((tt, MIX_A), F32)],
        compiler_params=_cparams(("parallel", "arbitrary"), 32),
        name="hgrn2",
    )(proj3, proj3, proj3, proj3, la, l1, gnorm.reshape(1, DV_A), s0)


NEG_BIG = -1e30
SEL_PER_CMP = L_SLC // D_CMP


def _cmp_weights(w1, b1, w2, b2):
    m = L_CMP // D_CMP
    eye_e, eye_g = jnp.eye(2, dtype=F32), jnp.eye(G_B, dtype=F32)
    w1r = w1.reshape(2, m, D_CMP, HD_B, HD_B)
    w1big = jnp.einsum('ehjdf,ea,gb->jegdhabf', w1r, eye_e, eye_g).reshape(D_CMP * KV_B, m * KV_B)
    w2big = jnp.einsum('efo,ea,gb->egfabo', w2, eye_e, eye_g).reshape(KV_B, KV_B)
    b1big = jnp.broadcast_to(b1[:, None, :], (2, G_B, HD_B)).reshape(1, KV_B)
    b2big = jnp.broadcast_to(b2[:, None, :], (2, G_B, HD_B)).reshape(1, KV_B)
    w1tok = jnp.einsum('ehjdf,gb->ejgdhbf', w1r, eye_g).reshape(2, D_CMP, KV_B // 2, m * KV_B // 2)
    return w1big.astype(BF16), b1big, w2big.astype(BF16), b2big, w1tok.astype(BF16)


def _cmp_to_sel_map(n_cmp_pad, n_cmp, n_sb):
    mm = np.zeros((n_sb, n_cmp_pad), np.float32)
    for n in range(n_cmp):
        for i in (n, n + 1):
            if i // SEL_PER_CMP < n_sb:
                mm[i // SEL_PER_CMP, n] += 1.0
    return mm


def _compress_tail(pp, n_cmp, b1_ref, w2_ref, b2_ref):
    n_ch = pp.shape[0]
    hid = pp[:, :KV_B] + pltpu.roll(pp[:, KV_B:], n_ch - 1, 0) + b1_ref[...]
    hid = hid * _sigmoid(hid)
    kc = jnp.dot(hid.astype(BF16), w2_ref[...], preferred_element_type=F32) + b2_ref[...]
    row = lax.broadcasted_iota(jnp.int32, (n_ch, 1), 0)
    return jnp.where(row < n_cmp, kc, 0.0)


def _nsa_prep_kernel(n_cmp, win, kc_ref, ks_ref, kw_ref, w1_ref, b1_ref, w2_ref, b2_ref,
                     kck_ref, kcvt_ref, ksk_ref, ksvt_ref, kwk_ref, kwvt_ref, ct_ref, st_ref, wt_ref, xk_ref):
    half = KV_B // 2
    t = kc_ref.shape[0]
    n_ch = t // D_CMP
    for e in range(2):
        xk_ref[e] = kc_ref[:, e * half:(e + 1) * half].astype(F32)
    pp = jnp.zeros((n_ch, w1_ref.shape[2]), F32)
    for j in range(D_CMP):
        for e in range(2):
            x = xk_ref[e, pl.ds(j, n_ch, stride=D_CMP), :].astype(BF16)
            pp = pp + jnp.dot(x, w1_ref[j, e * half:(e + 1) * half, :], preferred_element_type=F32)
    kc = _compress_tail(pp, n_cmp, b1_ref, w2_ref, b2_ref)
    kck_ref[...] = kc[:, :half].astype(BF16)
    kcvt_ref[...] = kc[:, half:].T.astype(BF16)
    ct_ref[...] = kc_ref[...].astype(F32).T
    ks_t = ks_ref[...].astype(F32).T
    st_ref[...] = ks_t
    n_aug = ksk_ref.shape[1] - half
    blk_of_key = lax.broadcasted_iota(jnp.int32, (t, n_aug), 0) // L_SLC
    one_hot = jnp.where(blk_of_key == lax.broadcasted_iota(jnp.int32, (t, n_aug), 1), 1.0, 0.0).astype(BF16)
    ksk_ref[...] = jnp.concatenate([ks_ref[:, :half].astype(BF16), one_hot], axis=1)
    ksvt_ref[...] = ks_t[half:, :].astype(BF16)
    kw_t = kw_ref[...].astype(F32).T
    wt_ref[...] = kw_t[:, t - win:]
    kwk_ref[...] = kw_ref[:, :half].astype(BF16)
    kwvt_ref[...] = kw_t[half:, :].astype(BF16)


def _nsa_prep(proj3, col_kv, cmp_big, win):
    bsz, t, _ = proj3.shape
    n_ch = t // D_CMP
    n_cmp = n_ch - L_CMP // D_CMP + 1
    half = KV_B // 2
    w1big, b1big, w2big, b2big = cmp_big[:4]
    w1pos = w1big.reshape(D_CMP, KV_B, w1big.shape[1])
    cb = col_kv // KV_B
    const = lambda i: (0, 0)
    kspec = lambda n: pl.BlockSpec((None, n, half), lambda i: (i, 0, 0))
    vspec = lambda n: pl.BlockSpec((None, half, n), lambda i: (i, 0, 0))
    fspec = lambda n: pl.BlockSpec((None, KV_B, n), lambda i: (i, 0, 0))
    col = lambda c: pl.BlockSpec((None, t, KV_B), lambda i: (i, 0, cb + c))
    return pl.pallas_call(
        functools.partial(_nsa_prep_kernel, n_cmp, win),
        grid=(bsz,),
        in_specs=[col(0), col(1), col(2),
                  pl.BlockSpec(w1pos.shape, lambda i: (0, 0, 0)), pl.BlockSpec((1, KV_B), const),
                  pl.BlockSpec((KV_B, KV_B), const), pl.BlockSpec((1, KV_B), const)],
        out_specs=[kspec(n_ch), vspec(n_ch), pl.BlockSpec((None, t, KV_B), lambda i: (i, 0, 0)), vspec(t),
                   kspec(t), vspec(t), fspec(t), fspec(t), fspec(win)],
        out_shape=[jax.ShapeDtypeStruct((bsz, n_ch, half), BF16), jax.ShapeDtypeStruct((bsz, half, n_ch), BF16),
                   jax.ShapeDtypeStruct((bsz, t, KV_B), BF16), jax.ShapeDtypeStruct((bsz, half, t), BF16),
                   jax.ShapeDtypeStruct((bsz, t, half), BF16), jax.ShapeDtypeStruct((bsz, half, t), BF16),
                   jax.ShapeDtypeStruct((bsz, KV_B, t), F32), jax.ShapeDtypeStruct((bsz, KV_B, t), F32),
                   jax.ShapeDtypeStruct((bsz, KV_B, win), F32)],
        scratch_shapes=[pltpu.VMEM((2, t, half), F32)],
        compiler_params=_cparams(("parallel",), 48),
        name="nsa_prep",
    )(proj3, proj3, proj3, w1pos, b1big, w2big, b2big)


def _nsa_prompt_kernel(tq, tk, n_sb, q_ref, gb_ref, kck_ref, kcvt_ref, ksk_ref, ksvt_ref, kwk_ref, kwvt_ref, mm_ref, o_ref):
    qi = pl.program_id(1)
    q0 = qi * tq
    n_cp = kck_ref.shape[0]
    w4 = HG_B * tq
    q_t = (q_ref[...].astype(F32) * HD_B ** -0.5).T
    g_t = _sigmoid(gb_ref[...].astype(F32)).T
    qpos = q0 + lax.broadcasted_iota(jnp.int32, (1, tq), 1)
    qpos4 = jnp.concatenate([qpos] * HG_B, axis=1)
    zpad = jnp.zeros((HD_B, tq), BF16)

    def update(state, s, pen, vt, g):
        m, l, acc = state
        if pen is not None:
            s = s + jnp.concatenate([pen] * HG_B, axis=1)
        m_new = jnp.maximum(m, jnp.max(s, axis=0, keepdims=True))
        e = jnp.exp(s - m_new)
        a = jnp.exp(m - m_new)
        l = a * l + jnp.sum(e, axis=0, keepdims=True)
        pv = jnp.dot(vt[g * HD_B:(g + 1) * HD_B, :], e.astype(BF16), preferred_element_type=F32)
        return m_new, l, a * acc + pv

    qts, o_cs, sel_ts = [], [], []
    for g in range(G_B):
        cols = []
        for hg in range(HG_B):
            h = g * HG_B + hg
            qh = q_t[h * HD_B:(h + 1) * HD_B, :].astype(BF16)
            cols.append(jnp.concatenate([qh, zpad] if g == 0 else [zpad, qh], axis=0))
        qt = jnp.concatenate(cols, axis=1)
        qts.append(qt)
        s = jnp.dot(kck_ref[...], qt, preferred_element_type=F32)
        c_end = lax.broadcasted_iota(jnp.int32, (n_cp, w4), 0) * D_CMP + (L_CMP - 1)
        mask = c_end <= qpos4
        m = jnp.max(jnp.where(mask, s, NEG_BIG), axis=0, keepdims=True)
        e = jnp.where(mask, jnp.exp(s - m), 0.0)
        d = jnp.sum(e, axis=0, keepdims=True)
        p = e / jnp.where(d > 0, d, 1.0)
        o_cs.append(jnp.dot(kcvt_ref[...], p.astype(BF16), preferred_element_type=F32)[g * HD_B:(g + 1) * HD_B, :])
        imp = p[:, 0:tq]
        for hg in range(1, HG_B):
            imp = imp + p[:, hg * tq:(hg + 1) * tq]
        bs = jnp.dot(mm_ref[...], imp, precision=HIGHEST, preferred_element_type=F32)
        blk = lax.broadcasted_iota(jnp.int32, (n_sb, tq), 0)
        cur = qpos >> 6
        forced = (blk == 0) | (blk == cur) | (blk == cur - 1)
        score = jnp.where(blk <= cur, jnp.where(forced, jnp.inf, bs), -jnp.inf)
        rank = jnp.zeros((n_sb, tq), jnp.int32)
        for i in range(n_sb):
            si = score[i:i + 1, :]
            rank = rank + jnp.where((si > score) | ((si == score) & (blk > i)), 1, 0)
        sel_ts.append(jnp.where((rank < N_SEL) & (blk <= cur), 1.0, 0.0).astype(BF16))

    n_aug = ksk_ref.shape[1] - 2 * HD_B
    qas = []
    for g in range(G_B):
        pen_rows = ((sel_ts[g].astype(F32) - 1.0) * (-NEG_BIG)).astype(BF16)
        pen_rows = jnp.concatenate([pen_rows, jnp.zeros((n_aug - n_sb, tq), BF16)], axis=0)
        qas.append(jnp.concatenate([qts[g], jnp.concatenate([pen_rows] * HG_B, axis=1)], axis=0))
    krow = lax.broadcasted_iota(jnp.int32, (tk, tq), 0)

    def far_tiles(kt, states):
        k0 = pl.multiple_of(kt * tk, tk)
        k_tile, vt_tile = ksk_ref[pl.ds(k0, tk), :], ksvt_ref[:, pl.ds(k0, tk)]
        return tuple(update(states[g], jnp.dot(k_tile, qas[g], preferred_element_type=F32), None, vt_tile, g)
                     for g in range(G_B))

    def near_tiles(kt, states):
        k0 = pl.multiple_of(kt * tk, tk)
        rel = qpos - k0
        causal = krow <= rel
        pen_causal = jnp.where(causal, 0.0, NEG_BIG)
        pen_win = jnp.where(causal & (krow > rel - WINDOW), 0.0, NEG_BIG)
        k_tile, vt_tile = ksk_ref[pl.ds(k0, tk), :], ksvt_ref[:, pl.ds(k0, tk)]
        kw_tile, vwt_tile = kwk_ref[pl.ds(k0, tk), :], kwvt_ref[:, pl.ds(k0, tk)]
        new_sel = tuple(update(states[g], jnp.dot(k_tile, qas[g], preferred_element_type=F32),
                               pen_causal, vt_tile, g) for g in range(G_B))
        new_win = tuple(update(states[G_B + g], jnp.dot(kw_tile, qts[g], preferred_element_type=F32),
                               pen_win, vwt_tile, g) for g in range(G_B))
        return new_sel + new_win

    init = (jnp.full((1, w4), NEG_BIG, F32), jnp.zeros((1, w4), F32), jnp.zeros((HD_B, w4), F32))

    def looped(_):
        first_near = jnp.maximum((q0 - WINDOW) // tk, 0)
        st = lax.fori_loop(0, first_near, far_tiles, (init,) * G_B)
        return lax.fori_loop(first_near, (q0 + tq) // tk, near_tiles, st + (init,) * G_B)

    def straight(_):
        n_w = WINDOW // tk
        lane_q = lax.broadcasted_iota(jnp.int32, (tk, tq), 1)
        pen_oldest = jnp.where(krow > lane_q, 0.0, NEG_BIG)
        pen_diag = jnp.where(krow <= lane_q, 0.0, NEG_BIG)

        def sel(kt, st, pen):
            k0 = pl.multiple_of(kt * tk, tk)
            k_tile, vt_tile = ksk_ref[pl.ds(k0, tk), :], ksvt_ref[:, pl.ds(k0, tk)]
            return tuple(update(st[g], jnp.dot(k_tile, qas[g], preferred_element_type=F32), pen, vt_tile, g)
                         for g in range(G_B))

        def win(kt, st, pen):
            k0 = pl.multiple_of(kt * tk, tk)
            kw_tile, vwt_tile = kwk_ref[pl.ds(k0, tk), :], kwvt_ref[:, pl.ds(k0, tk)]
            return tuple(update(st[g], jnp.dot(kw_tile, qts[g], preferred_element_type=F32), pen, vwt_tile, g)
                         for g in range(G_B))

        n_old = qi
        s_st = lax.fori_loop(0, n_old // 2, lambda i, st: sel(2 * i + 1, sel(2 * i, st, None), None), (init,) * G_B)
        s_st = lax.fori_loop(2 * (n_old // 2), n_old, lambda kt, st: sel(kt, st, None), s_st)
        w_st = win(qi - n_w, (init,) * G_B, pen_oldest)
        for d in range(n_w - 1, 0, -1):
            w_st = win(qi - d, w_st, None)
        return sel(qi, s_st, pen_diag) + win(qi, w_st, pen_diag)

    if tq == tk and WINDOW % tk == 0:
        states = lax.cond(qi >= WINDOW // tk, straight, looped, 0)
    else:
        states = looped(0)
    finish = lambda st: st[2] / jnp.where(st[1] > 0, st[1], 1.0)
    outs = []
    for g in range(G_B):
        o_c, o_s, o_w = o_cs[g], finish(states[g]), finish(states[G_B + g])
        for hg in range(HG_B):
            c = (g * HG_B + hg) * 3
            sl = slice(hg * tq, (hg + 1) * tq)
            outs.append(g_t[c:c + 1, :] * o_c[:, sl] + g_t[c + 1:c + 2, :] * o_s[:, sl]
                        + g_t[c + 2:c + 3, :] * o_w[:, sl])
    o_ref[...] = jnp.concatenate(outs, axis=0).T.astype(o_ref.dtype)


def _nsa_prompt(proj3, col_qb, col_gb, prep, tq=256, tk=256):
    bsz, t, _ = proj3.shape
    tq, tk = min(tq, t), min(tk, t)
    kck, kcvt, ksk, ksvt, kwk, kwvt = prep
    n_ch = kck.shape[1]
    n_cmp = n_ch - L_CMP // D_CMP + 1
    n_sb = -(-t // L_SLC)
    half = KV_B // 2
    mm = jnp.asarray(_cmp_to_sel_map(n_ch, n_cmp, n_sb))
    per_b = lambda shape: pl.BlockSpec((None,) + shape, lambda i, j: (i, 0, 0))
    return pl.pallas_call(
        functools.partial(_nsa_prompt_kernel, tq, tk, n_sb),
        grid=(bsz, t // tq),
        in_specs=[pl.BlockSpec((None, tq, MIX_B), lambda i, j: (i, j, col_qb // MIX_B)),
                  pl.BlockSpec((None, tq, 128), lambda i, j: (i, j, col_gb // 128)),
                  per_b((n_ch, half)), per_b((half, n_ch)), per_b((t, KV_B)), per_b((half, t)),
                  per_b((t, half)), per_b((half, t)),
                  pl.BlockSpec((n_sb, n_ch), lambda i, j: (0, 0))],
        out_specs=pl.BlockSpec((None, tq, MIX_B), lambda i, j: (i, j, 0)),
        out_shape=jax.ShapeDtypeStruct((bsz, t, MIX_B), BF16),
        compiler_params=_cparams(("parallel", "parallel"), 40),
        name="nsa_prompt",
    )(proj3, proj3, kck, kcvt, ksk, ksvt, kwk, kwvt, mm)


def _pages_view(cache):
    n_phys, depth, page = cache.shape[:3]
    return jnp.transpose(cache, (0, 1, 3, 4, 5, 2)).reshape(n_phys, depth, 2, KV_B // 2, page)


def _cmp_pages_kernel(n_pg, page, *refs):
    pages, w1_ref, o_ref, xs_ref = refs[2:2 + n_pg], refs[2 + n_pg], refs[3 + n_pg], refs[4 + n_pg]
    half = KV_B // 2
    ch_pg = page // D_CMP
    n_ch = n_pg * ch_pg
    r = lax.broadcasted_iota(jnp.int32, (page, page), 0)
    pos = lax.broadcasted_iota(jnp.int32, (page, page), 1)
    pick = jnp.where(pos == (r % ch_pg) * D_CMP + r // ch_pg, 1.0, 0.0).astype(BF16)
    for i, pg in enumerate(pages):
        kv_t = pg[...].reshape(KV_B, page).astype(BF16)
        y = lax.dot_general(pick, kv_t, (((1,), (1,)), ((), ())), preferred_element_type=F32)
        for e in range(2):
            for j in range(D_CMP):
                xs_ref[e, j, i * ch_pg:(i + 1) * ch_pg, :] = y[j * ch_pg:(j + 1) * ch_pg, e * half:(e + 1) * half]
    for e in range(2):
        acc = jnp.zeros((n_ch, 2 * half), F32)
        for j in range(D_CMP):
            acc = acc + jnp.dot(xs_ref[e, j].astype(BF16), w1_ref[e, j], preferred_element_type=F32)
        o_ref[:, e * half:(e + 1) * half] = acc[:, :half]
        o_ref[:, KV_B + e * half:KV_B + (e + 1) * half] = acc[:, half:]


def _cmp_pages(pages, layer, page_table, w1e):
    n_phys, depth, _, _, page = pages.shape
    bsz, n_pages = page_table.shape
    ch_pg = page // D_CMP
    n_pg = next(c for c in (32, 16, 8, 4, 2, 1) if n_pages % c == 0)

    def page_spec(i):
        return pl.BlockSpec((None, None, 2, KV_B // 2, page),
                            lambda b, s, pt, lyr: (pt[b * n_pages + s * n_pg + i], lyr[0], 0, 0, 0))

    grid_spec = pltpu.PrefetchScalarGridSpec(
        num_scalar_prefetch=2,
        grid=(bsz, n_pages // n_pg),
        in_specs=[page_spec(i) for i in range(n_pg)] + [pl.BlockSpec(w1e.shape, lambda b, s, pt, lyr: (0, 0, 0, 0))],
        out_specs=pl.BlockSpec((None, n_pg * ch_pg, 2 * KV_B), lambda b, s, pt, lyr: (b, s, 0)),
        scratch_shapes=[pltpu.VMEM((2, D_CMP, n_pg * ch_pg, KV_B // 2), F32)])
    return pl.pallas_call(
        functools.partial(_cmp_pages_kernel, n_pg, page),
        grid_spec=grid_spec,
        out_shape=jax.ShapeDtypeStruct((bsz, n_pages * ch_pg, 2 * KV_B), F32),
        compiler_params=_cparams(("parallel", "parallel"), 48),
        name="nsa_cmp_pages",
    )(page_table.reshape(-1), jnp.full((1,), layer, jnp.int32), *([pages] * n_pg), w1e)


def _nsa_score_kernel(past, sd, n_cmp, n_sb, pp_ref, q_ref, cw_ref, nw_ref, b1_ref, w2_ref, b2_ref, mm_ref,
                      oc_ref, ow_ref, idx_ref):
    half = KV_B // 2
    rq = HG_B * sd
    scale = HD_B ** -0.5
    n_ch = pp_ref.shape[0]
    win = cw_ref.shape[2]
    kc = _compress_tail(pp_ref[...], n_cmp, b1_ref, w2_ref, b2_ref)
    kck, kcv = kc[:, :half].astype(BF16), kc[:, half:].astype(BF16)
    qpos = past + lax.broadcasted_iota(jnp.int32, (rq, 1), 0) % sd
    nt = (((1,), (1,)), ((), ()))

    def softmax(parts):
        m = functools.reduce(jnp.maximum, [jnp.max(jnp.where(mk, s, NEG_BIG), axis=-1, keepdims=True) for s, mk in parts])
        es = [jnp.where(mk, jnp.exp(s - m), 0.0) for s, mk in parts]
        d = functools.reduce(jnp.add, [jnp.sum(e, axis=-1, keepdims=True) for e in es])
        return [e / jnp.where(d > 0, d, 1.0) for e in es]

    imps = []
    for g in range(G_B):
        qg = q_ref[g * rq:(g + 1) * rq, :]
        s = lax.dot_general(qg, kck, nt, preferred_element_type=F32) * scale
        c_end = lax.broadcasted_iota(jnp.int32, (rq, n_ch), 1) * D_CMP + (L_CMP - 1)
        p, = softmax([(s, c_end <= qpos)])
        oc_ref[g * rq:(g + 1) * rq, :] = jnp.dot(p.astype(BF16), kcv, preferred_element_type=F32)
        imps.append(functools.reduce(jnp.add, [p[hg * sd:(hg + 1) * sd, :] for hg in range(HG_B)]))
        s1 = jnp.dot(qg, cw_ref[0].astype(BF16), preferred_element_type=F32) * scale
        s2 = lax.dot_general(qg, nw_ref[:, :half].astype(BF16), nt, preferred_element_type=F32) * scale
        wp1 = past - win + lax.broadcasted_iota(jnp.int32, s1.shape, 1)
        j2 = lax.broadcasted_iota(jnp.int32, s2.shape, 1)
        wp2 = past + j2
        p1, p2 = softmax([(s1, (wp1 <= qpos) & (wp1 > qpos - WINDOW) & (wp1 >= 0)),
                          (s2, (wp2 <= qpos) & (wp2 > qpos - WINDOW) & (j2 < sd))])
        ow_ref[g * rq:(g + 1) * rq, :] = (
            lax.dot_general(p1.astype(BF16), cw_ref[1].astype(BF16), nt, preferred_element_type=F32)
            + jnp.dot(p2.astype(BF16), nw_ref[:, half:].astype(BF16), preferred_element_type=F32))
    imp = jnp.concatenate(imps, axis=0)
    rows = G_B * sd
    bs = jnp.dot(imp, mm_ref[...], precision=HIGHEST, preferred_element_type=F32)
    n_lane = bs.shape[1]
    blk = lax.broadcasted_iota(jnp.int32, (rows, n_lane), 1)
    cur = (past + lax.broadcasted_iota(jnp.int32, (rows, 1), 0) % sd) >> 6
    forced = (blk == 0) | (blk == cur) | (blk == cur - 1)
    score = jnp.where(blk <= cur, jnp.where(forced, jnp.inf, bs), -jnp.inf)
    blk_f = blk.astype(F32)
    taken = blk >= n_sb
    lane = lax.broadcasted_iota(jnp.int32, (rows, 128), 1)
    picked = jnp.zeros((rows, 128), F32)
    for it in range(min(N_SEL, n_sb)):
        live = jnp.where(taken, -jnp.inf, score)
        mx = jnp.max(live, axis=-1, keepdims=True)
        ix = jnp.min(jnp.where(~taken & (score == mx), blk_f, float(n_lane)), axis=-1, keepdims=True)
        picked = jnp.where(lane == it, ix, picked)
        taken = taken | (blk_f == ix)
    idx_ref[...] = picked.astype(jnp.int32)


def _nsa_select_kernel(past, sd, n_sb, n_pages, k_sel, *refs):
    pt_ref, ix_ref, lyr_ref = refs[:3]
    pages = refs[3:3 + k_sel]
    new_ref, q_ref, oc_ref, ow_ref, gt_ref, o_ref = refs[3 + k_sel:]
    rq = HG_B * sd
    page = new_ref.shape[-1]
    blk_pg = page // L_SLC
    b, g, i = pl.program_id(0), pl.program_id(1), pl.program_id(2)
    base = ((b * G_B + g) * sd + i) * k_sel
    lane = lax.broadcasted_iota(jnp.int32, (1, k_sel * page), 1)
    in_page = lane % page
    kpos = in_page
    picked = lane < 0
    kts, vts = [], []
    for j in range(k_sel):
        blk = ix_ref[base + j]
        is_new = blk == n_sb - 1
        kts.append(jnp.where(is_new, new_ref[0], pages[j][0]))
        vts.append(jnp.where(is_new, new_ref[1], pages[j][1]))
        mine = lane // page == j
        kpos = kpos + jnp.where(mine, (blk // blk_pg) * page, 0)
        picked = picked | (mine & (in_page // L_SLC == blk % blk_pg))
    kt_all = jnp.concatenate(kts, axis=1).astype(BF16)
    vt_all = jnp.concatenate(vts, axis=1).astype(BF16)
    s = jnp.dot(q_ref[...], kt_all, preferred_element_type=F32) * (HD_B ** -0.5)
    row_q = lax.broadcasted_iota(jnp.int32, (rq, 1), 0) % sd
    mask = picked & (kpos <= past + row_q)
    m = jnp.max(jnp.where(mask, s, NEG_BIG), axis=-1, keepdims=True)
    e = jnp.where(mask, jnp.exp(s - m), 0.0)
    d = jnp.sum(e, axis=-1, keepdims=True)
    p = e / jnp.where(d > 0, d, 1.0)
    o_s = lax.dot_general(p.astype(BF16), vt_all, (((1,), (1,)), ((), ())), preferred_element_type=F32)
    gates = _sigmoid(gt_ref[...])
    y = gates[0] * oc_ref[...] + gates[1] * o_s + gates[2] * ow_ref[...]

    @pl.when(i == 0)
    def _():
        o_ref[...] = jnp.zeros_like(o_ref)

    o_ref[...] += jnp.where(row_q == i, y, 0.0)


def _nsa_sample(proj3, cols, kvs, sd, cmp_big, pages_cmp, pages_slc, win_view, layer, page_table):
    col_qb, col_kv, col_gb = cols
    bsz = proj3.shape[0]
    n_phys, depth, _, _, page = pages_cmp.shape
    n_pages = page_table.shape[1]
    past = n_pages * page
    win = win_view.shape[-1]
    half = KV_B // 2
    rq = HG_B * sd
    n_ch = past // D_CMP
    n_cmp = (past + sd) // D_CMP - L_CMP // D_CMP + 1
    n_sb = -(-(past + sd) // L_SLC)
    k_sel = min(N_SEL, n_sb)
    assert (past + sd) // D_CMP == n_ch and past % L_SLC == 0 and sd <= L_SLC and page % L_SLC == 0
    _, b1big, w2big, b2big, w1tok = cmp_big
    pp = _cmp_pages(pages_cmp, layer, page_table, w1tok)
    q = proj3[:, :sd, col_qb:col_qb + MIX_B].reshape(bsz, sd, G_B, HG_B, HD_B).transpose(0, 2, 3, 1, 4)
    q = jnp.stack([jnp.pad(q[:, g], ((0, 0), (0, 0), (0, 0), (g * HD_B, half - (g + 1) * HD_B))) for g in range(G_B)], 1)
    q = q.reshape(bsz, G_B * rq, half).astype(BF16)
    gt = proj3[:, :sd, col_gb:col_gb + 3 * H_B].astype(F32).reshape(bsz, sd, G_B, HG_B, 3).transpose(0, 4, 2, 3, 1)
    gt = jnp.broadcast_to(gt.reshape(bsz, 3, G_B * rq, 1), (bsz, 3, G_B * rq, half))
    n_lane = -(-n_sb // 128) * 128
    mm = jnp.asarray(np.pad(_cmp_to_sel_map(n_ch, n_cmp, n_sb), ((0, n_lane - n_sb), (0, 0))).T)
    per_b = lambda shape: pl.BlockSpec((None,) + shape, lambda i: (i,) + (0,) * len(shape))
    const = lambda shape: pl.BlockSpec(shape, lambda i: (0,) * len(shape))
    o_c, o_w, idx = pl.pallas_call(
        functools.partial(_nsa_score_kernel, past, sd, n_cmp, n_sb),
        grid=(bsz,),
        in_specs=[per_b((n_ch, 2 * KV_B)), per_b((G_B * rq, half)),
                  pl.BlockSpec((None, None, 2, half, win), lambda i: (layer, i, 0, 0, 0)),
                  pl.BlockSpec((None, PAD_ROWS, KV_B), lambda i: (i, 0, col_kv // KV_B + 2)),
                  const((1, KV_B)), const((KV_B, KV_B)), const((1, KV_B)), const((n_ch, n_lane))],
        out_specs=[per_b((G_B * rq, half)), per_b((G_B * rq, half)), per_b((G_B * sd, 128))],
        out_shape=[jax.ShapeDtypeStruct((bsz, G_B * rq, half), F32), jax.ShapeDtypeStruct((bsz, G_B * rq, half), F32),
                   jax.ShapeDtypeStruct((bsz, G_B * sd, 128), jnp.int32)],
        compiler_params=_cparams(("parallel",), 40),
        name="nsa_score_sample",
    )(pp, q, win_view, proj3, b1big, w2big, b2big, mm)
    blk_pg = page // L_SLC
    n_cached = past // L_SLC
    new_blk = jnp.pad(kvs.reshape(bsz, sd, 2, half).transpose(0, 2, 3, 1), ((0, 0), (0, 0), (0, 0), (0, page - sd)))

    def blk_spec(j):
        def index(b, g, i, pt, ix, lyr):
            blk = jnp.minimum(ix[((b * G_B + g) * sd + i) * k_sel + j], n_cached - 1)
            return pt[b * n_pages + blk // blk_pg], lyr[0], 0, 0, 0
        return pl.BlockSpec((None, None, 2, half, page), index)

    grp = lambda b, g, i, pt, ix, lyr: (b, g, 0)
    grid_spec = pltpu.PrefetchScalarGridSpec(
        num_scalar_prefetch=3,
        grid=(bsz, G_B, sd),
        in_specs=[blk_spec(j) for j in range(k_sel)] + [
            pl.BlockSpec((None, 2, half, page), lambda b, g, i, pt, ix, lyr: (b, 0, 0, 0)),
            pl.BlockSpec((None, rq, half), grp), pl.BlockSpec((None, rq, half), grp), pl.BlockSpec((None, rq, half), grp),
            pl.BlockSpec((None, 3, rq, half), lambda b, g, i, pt, ix, lyr: (b, 0, g, 0))],
        out_specs=pl.BlockSpec((None, rq, half), grp))
    y = pl.pallas_call(
        functools.partial(_nsa_select_kernel, past, sd, n_sb, n_pages, k_sel),
        grid_spec=grid_spec,
        out_shape=jax.ShapeDtypeStruct((bsz, G_B * rq, half), F32),
        compiler_params=_cparams(("parallel", "parallel", "arbitrary"), 40),
        name="nsa_select_sample",
    )(page_table.reshape(-1), idx[:, :, :k_sel].reshape(-1), jnp.full((1,), layer, jnp.int32),
      *([pages_slc] * k_sel), new_blk, q, o_c, o_w, gt)
    y = y.reshape(bsz, G_B, HG_B, sd, G_B, HD_B)
    y = jnp.stack([y[:, g, :, :, g] for g in range(G_B)], axis=1)
    return y.transpose(0, 3, 1, 2, 4).reshape(bsz, sd, MIX_B)


def _prep_w_in(w, d):
    o = np.cumsum([0, MIX_A, MIX_A, MIX_A, MIX_A, MIX_B, KV_B, KV_B, KV_B, 3 * H_B, 2 * C_CONV, 3 * d])
    parts = [w[:, o[0]:o[4]], w[:, o[10]:o[11]], w[:, o[4]:o[5]], w[:, o[9]:o[10]], w[:, o[5]:o[8]], w[:, o[8]:o[9]]]
    n = sum(p.shape[1] for p in parts)
    n_pad = -(-n // 512) * 512
    parts.append(jnp.zeros((w.shape[0], n_pad - n), w.dtype))
    return jnp.concatenate(parts, axis=1).astype(BF16)


def _layer(x3, n_valid, p, s0, conv_prefix, mem_kv, nsa_fn, alpha):
    bsz, t, d = x3.shape
    m = bsz * t
    col_qb = COL_MG + 3 * d
    col_glu = col_qb + MIX_B
    col_kv = col_glu + 2 * C_CONV
    col_gb = col_kv + 3 * KV_B
    proj = _matmul(x3.reshape(m, d), p['w_in'], 1024, 1536, out_dtype=BF16)
    proj3 = proj.reshape(bsz, t, -1)
    ya, s_new = _hgrn(proj3, p['lb'], p['hg_norm'], s0, n_valid)
    yc, conv_state = _conv(proj3, col_glu // C_CONV, conv_prefix, p['conv_w'], p['conv_b'],
                           p['conv_ln_g'], p['conv_ln_b'], n_valid)
    kv_shape = (bsz, n_valid, 2, G_B, HD_B)
    kvc = proj3[:, :n_valid, col_kv:col_kv + KV_B].astype(F32)
    kvs = proj3[:, :n_valid, col_kv + KV_B:col_kv + 2 * KV_B].astype(F32)
    kvw = proj3[:, :n_valid, col_kv + 2 * KV_B:col_kv + 3 * KV_B].astype(F32)
    yb, nsa_extra = nsa_fn(proj3, (col_qb, col_kv, col_gb), kvc, kvs, kvw)
    kvc, kvs = kvc.reshape(kv_shape), kvs.reshape(kv_shape)
    x1 = _merge(ya.reshape(m, MIX_A), yb.reshape(m, MIX_B), yc.reshape(m, C_CONV), proj, x3.reshape(m, d),
                p['w_pa'], p['w_pb'], p['w_pc'], p['w_out'], p['ln_g'][0:1], p['ln_b'][0:1], alpha)
    x2 = _xattn(x1.reshape(bsz, t, d), mem_kv[0], mem_kv[1], p['w_xq'], p['w_xo'], p['ln_g'][1:2], p['ln_b'][1:2], alpha)
    x3n = _mlp(x2.reshape(m, d), p['w_up'], p['w_down'], p['ln_g'][2:3], p['ln_b'][2:3], alpha)
    return x3n.reshape(bsz, t, d), kvc, kvs, nsa_extra, s_new, conv_state


def kernel(x_prompt, x_sample, cache_cmp, cache_slc, cache_win, state_hgrn, state_conv, cache_mem, page_table, mem_prompt, w_in, lb_raw, hg_norm, w_cmp1, b_cmp1, w_cmp2, b_cmp2, conv_w, conv_b, conv_ln_g, conv_ln_b, w_pa, w_pb, w_pc, w_out, ln_g, ln_b, w_xq, w_xkv, w_xo, w_up, w_down):
    bp, t, d = x_prompt.shape
    bd, sd = x_sample.shape[:2]
    depth = w_in.shape[0]
    n_mem = mem_prompt.shape[1]
    win_buf = cache_win.shape[2]
    alpha = (2 * depth) ** 0.25
    lb_cum = jnp.cumsum(jax.nn.softmax(lb_raw.astype(F32), axis=0), axis=0)
    lb_all = lb_cum - lb_cum[0]
    pages_cmp, pages_slc, win_view = _pages_view(cache_cmp), _pages_view(cache_slc), _pages_view(cache_win)
    mem_cache = _mem_tile_order(cache_mem, d)
    xp = x_prompt
    xs = jnp.pad(x_sample, ((0, 0), (0, PAD_ROWS - sd), (0, 0)))
    outs = {k: [] for k in ('cmp_p', 'cmp_s', 'slc_p', 'slc_s', 'win_p', 'win_s', 'hg_p', 'hg_s', 'cv_p', 'cv_s', 'mem_p')}
    for l in range(depth):
        p = {'w_in': _prep_w_in(w_in[l], d), 'lb': lb_all[l], 'hg_norm': hg_norm[l], 'conv_w': conv_w[l],
             'conv_b': conv_b[l].reshape(1, -1), 'conv_ln_g': conv_ln_g[l].reshape(1, -1),
             'conv_ln_b': conv_ln_b[l].reshape(1, -1),
             'w_pa': w_pa[l].astype(BF16), 'w_pb': w_pb[l].astype(BF16), 'w_pc': w_pc[l].astype(BF16),
             'w_out': w_out[l].astype(BF16), 'ln_g': ln_g[l], 'ln_b': ln_b[l],
             'w_xq': w_xq[l].astype(BF16), 'w_xo': w_xo[l].astype(BF16),
             'w_up': w_up[l].astype(BF16), 'w_down': w_down[l].astype(BF16)}
        cmp_w = (w_cmp1[l], b_cmp1[l], w_cmp2[l], b_cmp2[l])

        cmp_big = _cmp_weights(*cmp_w)

        def nsa_prompt(proj3, cols, kvc, kvs, kvw):
            col_qb, col_kv, col_gb = cols
            prep = _nsa_prep(proj3, col_kv, cmp_big, min(win_buf, t))
            o = _nsa_prompt(proj3, col_qb, col_gb, prep[:6])
            ct, st_, wt = prep[6:]
            return o, (ct, st_, jnp.pad(wt, ((0, 0), (0, 0), (max(win_buf - t, 0), 0))))

        def nsa_sample(proj3, cols, kvc, kvs, kvw):
            o = _nsa_sample(proj3, cols, kvs, sd, cmp_big, pages_cmp, pages_slc, win_view, l, page_table)
            o = jnp.pad(o, ((0, 0), (0, PAD_ROWS - sd), (0, 0))).astype(BF16)
            win = jnp.concatenate([cache_win[l], kvw.reshape(bd, sd, 2, G_B, HD_B)], axis=1)[:, -win_buf:]
            return o, win

        w_kv = _mem_tile_order(w_xkv[l].reshape(d, 2, NX_H, d // NX_H), d).astype(BF16)
        mem_kv = _matmul(mem_prompt.reshape(bp * n_mem, d), w_kv, 1024, 512).reshape(1, bp, n_mem, 2 * d)
        xp, kc, ks_, wn, sh, cv = _layer(xp, t, p, jnp.zeros((bp, H_A, DK_A, DV_A), F32),
                                         jnp.zeros((bp, CONV_K - 1, C_CONV), F32), (mem_kv, 0), nsa_prompt, alpha)
        outs['cmp_p'].append(wn[0]); outs['slc_p'].append(wn[1]); outs['win_p'].append(wn[2])
        outs['hg_p'].append(sh); outs['cv_p'].append(cv)
        outs['mem_p'].append(_mem_head_order(mem_kv[0], d))
        xs, kc, ks_, wn, sh, cv = _layer(xs, sd, p, state_hgrn[l], state_conv[l], (mem_cache, l), nsa_sample, alpha)
        outs['cmp_s'].append(kc); outs['slc_s'].append(ks_); outs['win_s'].append(wn)
        outs['hg_s'].append(sh); outs['cv_s'].append(cv)
    st = lambda k, ax: jnp.stack(outs[k], axis=ax)

    def rows(k, ax):
        a = st(k, ax)
        a = a.reshape(a.shape[:2] + (2, G_B, HD_B, a.shape[-1]))
        return jnp.transpose(a, (0, 1, 5, 2, 3, 4))

    return (xp, xs[:, :sd],
            rows('cmp_p', 1), st('cmp_s', 1), rows('slc_p', 1), st('slc_s', 1),
            rows('win_p', 0), st('win_s', 0), st('hg_p', 0), st('hg_s', 0),
            st('cv_p', 0), st('cv_s', 0), st('mem_p', 0))
```

```python
import functools

import numpy as np
import jax
import jax.numpy as jnp
from jax import lax
from jax.experimental import pallas as pl
from jax.experimental.pallas import tpu as pltpu

F32 = jnp.float32
BF16 = jnp.bfloat16
HIGHEST = lax.Precision.HIGHEST

H_A, DK_A, DV_A, CHUNK_A = 4, 128, 128, 16
H_B, G_B, HG_B, HD_B = 8, 2, 4, 64
L_CMP, D_CMP, L_SLC, N_SEL, WINDOW = 32, 16, 64, 16, 512
C_CONV, CONV_K = 512, 31
NX_H = 4
LN_EPS, RMS_EPS = 1e-5, 1e-6
PAD_ROWS = 16
LANES = 128

MIX_A = H_A * DK_A
MIX_B = H_B * HD_B
KV_B = 2 * G_B * HD_B
COL_MG = 4 * MIX_A


def _cparams(sem, vmem_mb=None):
    return pltpu.CompilerParams(dimension_semantics=sem,
                                vmem_limit_bytes=None if vmem_mb is None else vmem_mb * 2**20)


def _ln(y, g, b):
    mu = jnp.mean(y, axis=-1, keepdims=True)
    d = y - mu
    var = jnp.mean(d * d, axis=-1, keepdims=True)
    return d * lax.rsqrt(var + LN_EPS) * g + b


def _sigmoid(x):
    return 1.0 / (1.0 + jnp.exp(-x))


def _mm_kernel(x_ref, w_ref, o_ref, xb_ref):
    @pl.when(pl.program_id(1) == 0)
    def _():
        xb_ref[...] = x_ref[...].astype(BF16)

    o_ref[...] = jnp.dot(xb_ref[...], w_ref[...], preferred_element_type=F32).astype(o_ref.dtype)


def _matmul(x, w, tm, tn, out_dtype=F32):
    m, k = x.shape
    n = w.shape[1]
    tm, tn = min(tm, m), min(tn, n)
    return pl.pallas_call(
        _mm_kernel,
        grid=(m // tm, n // tn),
        in_specs=[pl.BlockSpec((tm, k), lambda i, j: (i, 0)),
                  pl.BlockSpec((k, tn), lambda i, j: (0, j))],
        out_specs=pl.BlockSpec((tm, tn), lambda i, j: (i, j)),
        out_shape=jax.ShapeDtypeStruct((m, n), out_dtype),
        scratch_shapes=[pltpu.VMEM((tm, k), BF16)],
        compiler_params=_cparams(("parallel", "arbitrary"), 40),
        name="proj_matmul",
    )(x, w)


def _mlp_kernel(alpha, x_ref, wu_ref, wd_ref, g_ref, b_ref, o_ref, xb_ref, acc_ref):
    j = pl.program_id(1)

    @pl.when(j == 0)
    def _():
        xb_ref[...] = x_ref[...].astype(BF16)
        acc_ref[...] = jnp.zeros_like(acc_ref)

    h = jnp.dot(xb_ref[...], wu_ref[...], preferred_element_type=F32)
    h = jnp.square(jnp.maximum(h, 0.0)).astype(BF16)
    acc_ref[...] += jnp.dot(h, wd_ref[...], preferred_element_type=F32)

    @pl.when(j == pl.num_programs(1) - 1)
    def _():
        o_ref[...] = _ln(alpha * x_ref[...] + acc_ref[...], g_ref[...], b_ref[...])


def _mlp(x, w_up, w_down, g, b, alpha, tm=512, tf=2048):
    m, d = x.shape
    ff = w_up.shape[1]
    tm = min(tm, m)
    return pl.pallas_call(
        functools.partial(_mlp_kernel, alpha),
        grid=(m // tm, ff // tf),
        in_specs=[pl.BlockSpec((tm, d), lambda i, j: (i, 0)),
                  pl.BlockSpec((d, tf), lambda i, j: (0, j)),
                  pl.BlockSpec((tf, d), lambda i, j: (j, 0)),
                  pl.BlockSpec((1, d), lambda i, j: (0, 0)),
                  pl.BlockSpec((1, d), lambda i, j: (0, 0))],
        out_specs=pl.BlockSpec((tm, d), lambda i, j: (i, 0)),
        out_shape=jax.ShapeDtypeStruct((m, d), F32),
        scratch_shapes=[pltpu.VMEM((tm, d), BF16), pltpu.VMEM((tm, d), F32)],
        compiler_params=_cparams(("parallel", "arbitrary"), 48),
        name="mlp",
    )(x, w_up, w_down, g, b)


def _merge_kernel(alpha, ya_ref, yb_ref, yc_ref, ma_ref, mb_ref, mc_ref, x_ref,
                  wpa_ref, wpb_ref, wpc_ref, wout_ref, g_ref, b_ref, o_ref):
    def branch(y_ref, m_ref, w_ref):
        return _sigmoid(m_ref[...].astype(F32)) * jnp.dot(y_ref[...].astype(BF16), w_ref[...], preferred_element_type=F32)

    merged = branch(ya_ref, ma_ref, wpa_ref) + branch(yb_ref, mb_ref, wpb_ref) + branch(yc_ref, mc_ref, wpc_ref)
    y = jnp.dot(merged.astype(BF16), wout_ref[...], preferred_element_type=F32)
    o_ref[...] = _ln(alpha * x_ref[...] + y, g_ref[...], b_ref[...])


def _merge(ya, yb, yc, proj, x, wpa, wpb, wpc, wout, g, b, alpha, tm=512):
    m, d = x.shape
    tm = min(tm, m)
    mg0 = COL_MG // d
    row = lambda i: (i, 0)
    const = lambda i: (0, 0)
    return pl.pallas_call(
        functools.partial(_merge_kernel, alpha),
        grid=(m // tm,),
        in_specs=[pl.BlockSpec((tm, MIX_A), row), pl.BlockSpec((tm, MIX_B), row), pl.BlockSpec((tm, C_CONV), row),
                  pl.BlockSpec((tm, d), lambda i: (i, mg0)), pl.BlockSpec((tm, d), lambda i: (i, mg0 + 1)),
                  pl.BlockSpec((tm, d), lambda i: (i, mg0 + 2)),
                  pl.BlockSpec((tm, d), row),
                  pl.BlockSpec((MIX_A, d), const), pl.BlockSpec((MIX_B, d), const), pl.BlockSpec((C_CONV, d), const),
                  pl.BlockSpec((d, d), const), pl.BlockSpec((1, d), const), pl.BlockSpec((1, d), const)],
        out_specs=pl.BlockSpec((tm, d), row),
        out_shape=jax.ShapeDtypeStruct((m, d), F32),
        compiler_params=_cparams(("parallel",), 48),
        name="merge_out",
    )(ya, yb, yc, proj, proj, proj, x, wpa, wpb, wpc, wout, g, b)


def _xattn_kernel(alpha, x_ref, kv_ref, wq_ref, wo_ref, g_ref, b_ref, o_ref):
    x = x_ref[...]
    d = x.shape[-1]
    hd = d // NX_H
    q = jnp.dot(x.astype(BF16), wq_ref[...], preferred_element_type=F32)
    n_dt = hd // LANES

    def head(base, h):
        parts = [kv_ref[:, base + (dt * NX_H + h) * LANES:base + (dt * NX_H + h + 1) * LANES] for dt in range(n_dt)]
        return jnp.concatenate(parts, axis=1).astype(BF16)

    outs = []
    for h in range(NX_H):
        qh = q[:, h * hd:(h + 1) * hd].astype(BF16)
        kh, vh = head(0, h), head(d, h)
        s = lax.dot_general(qh, kh, (((1,), (1,)), ((), ())), preferred_element_type=F32) * (hd ** -0.5)
        e = jnp.exp(s - jnp.max(s, axis=-1, keepdims=True))
        p = e / jnp.sum(e, axis=-1, keepdims=True)
        outs.append(jnp.dot(p.astype(BF16), vh, preferred_element_type=F32))
    o = jnp.concatenate(outs, axis=-1)
    y = jnp.dot(o.astype(BF16), wo_ref[...], preferred_element_type=F32)
    o_ref[...] = _ln(alpha * x + y, g_ref[...], b_ref[...])


def _mem_tile_order(a, d):
    lead = a.shape[:-3]
    a = a.reshape(lead + (2, NX_H, d // NX_H // LANES, LANES))
    return jnp.swapaxes(a, -3, -2).reshape(lead + (2 * d,))


def _mem_head_order(a, d):
    lead = a.shape[:-1]
    a = a.reshape(lead + (2, d // NX_H // LANES, NX_H, LANES))
    return jnp.swapaxes(a, -3, -2).reshape(lead + (2, NX_H, d // NX_H))


def _xattn(x, kv, layer, wq, wo, g, b, alpha, tm=512):
    bsz, t, d = x.shape
    n_mem = kv.shape[2]
    tm = min(tm, t)
    const = lambda i, j: (0, 0)
    return pl.pallas_call(
        functools.partial(_xattn_kernel, alpha),
        grid=(bsz, t // tm),
        in_specs=[pl.BlockSpec((None, tm, d), lambda i, j: (i, j, 0)),
                  pl.BlockSpec((None, None, n_mem, 2 * d), lambda i, j: (layer, i, 0, 0)),
                  pl.BlockSpec((d, d), const), pl.BlockSpec((d, d), const),
                  pl.BlockSpec((1, d), const), pl.BlockSpec((1, d), const)],
        out_specs=pl.BlockSpec((None, tm, d), lambda i, j: (i, j, 0)),
        out_shape=jax.ShapeDtypeStruct((bsz, t, d), F32),
        compiler_params=_cparams(("parallel", "parallel"), 48),
        name="xattn",
    )(x, kv, wq, wo, g, b)


CONV_HALO = 32
CONV_SUB = 32


def _conv_kernel(rt, n_valid_last, a_ref, gt_ref, pre_ref, w_ref, cb_ref, g_ref, b_ref, y_ref, st_ref, ue_ref, sh_ref):
    t = pl.program_id(1)
    off = CONV_HALO - (CONV_K - 1)
    sl = 8

    @pl.when(t == 0)
    def _():
        ue_ref[0:off, :] = jnp.zeros((off, C_CONV), F32)
        ue_ref[off:CONV_HALO, :] = pre_ref[...]

    ue_ref[CONV_HALO:CONV_HALO + rt, :] = a_ref[...].astype(F32) * _sigmoid(gt_ref[...].astype(F32))
    n_sh = rt + CONV_HALO - sl
    for s in range(1, sl):
        sh_ref[s - 1, 0:n_sh, :] = ue_ref[s:s + n_sh, :]
    sub = min(CONV_SUB, rt)
    bias = jnp.broadcast_to(cb_ref[...], (sl, C_CONV))
    for r0 in range(0, rt, sub):
        accs = [bias] * (sub // sl)
        for j in range(CONV_K):
            a, s = divmod(off + j, sl)
            w_j = w_ref[j * sl:(j + 1) * sl, :]
            for rg in range(sub // sl):
                lo = r0 + (a + rg) * sl
                win = ue_ref[lo:lo + sl, :] if s == 0 else sh_ref[s - 1, lo:lo + sl, :]
                accs[rg] = accs[rg] + w_j * win
        acc = jnp.concatenate(accs, axis=0)
        y = _ln(acc, g_ref[...], b_ref[...])
        y_ref[r0:r0 + sub, :] = (y * _sigmoid(y)).astype(y_ref.dtype)

    @pl.when(t == pl.num_programs(1) - 1)
    def _():
        st_ref[...] = ue_ref[off + n_valid_last:off + n_valid_last + CONV_K - 1, :]

    ue_ref[0:CONV_HALO, :] = ue_ref[rt:rt + CONV_HALO, :]


def _conv(proj3, col_a, prefix, w, cb, g, b, n_valid, rt=256):
    bsz, t, _ = proj3.shape
    rt = min(rt, t)
    n_valid_last = n_valid - (t - rt)
    vec = lambda i, j: (0, 0)
    return pl.pallas_call(
        functools.partial(_conv_kernel, rt, n_valid_last),
        grid=(bsz, t // rt),
        in_specs=[pl.BlockSpec((None, rt, C_CONV), lambda i, j: (i, j, col_a)),
                  pl.BlockSpec((None, rt, C_CONV), lambda i, j: (i, j, col_a + 1)),
                  pl.BlockSpec((None, CONV_K - 1, C_CONV), lambda i, j: (i, 0, 0)),
                  pl.BlockSpec((CONV_K * 8, C_CONV), vec), pl.BlockSpec((1, C_CONV), vec),
                  pl.BlockSpec((1, C_CONV), vec), pl.BlockSpec((1, C_CONV), vec)],
        out_specs=[pl.BlockSpec((None, rt, C_CONV), lambda i, j: (i, j, 0)),
                   pl.BlockSpec((None, CONV_K - 1, C_CONV), lambda i, j: (i, 0, 0))],
        out_shape=[jax.ShapeDtypeStruct((bsz, t, C_CONV), BF16),
                   jax.ShapeDtypeStruct((bsz, CONV_K - 1, C_CONV), F32)],
        scratch_shapes=[pltpu.VMEM((rt + CONV_HALO, C_CONV), F32), pltpu.VMEM((7, rt + CONV_HALO - 8, C_CONV), F32)],
        compiler_params=_cparams(("parallel", "arbitrary"), 32),
        name="conformer_conv",
    )(proj3, proj3, prefix, jnp.repeat(w, 8, axis=0), cb, g, b)


def _hgrn_kernel(tt, n_valid, q_ref, f_ref, i_ref, g_ref, la_ref, l1_ref, gn_ref, s0_ref,
                 y_ref, s_ref, st_ref, qs_ref, kk_ref, bb_ref, o32_ref):
    c = CHUNK_A
    t = pl.program_id(1)

    @pl.when(t == 0)
    def _():
        for h in range(H_A):
            st_ref[h] = s0_ref[h].T

    z = f_ref[...].astype(F32)
    sp = jnp.log(1.0 + jnp.exp(-jnp.abs(z)))
    cc = l1_ref[...] + (jnp.minimum(z, 0.0) - sp)
    a = la_ref[...]
    logf = jnp.maximum(a, cc) + jnp.log(1.0 + jnp.exp(-jnp.abs(a - cc)))
    logk = l1_ref[...] + (jnp.minimum(-z, 0.0) - sp)
    row = lax.broadcasted_iota(jnp.int32, (tt, 1), 0)
    if n_valid < tt:
        logf = jnp.where(row < n_valid, logf, 0.0)
        logk = jnp.where(row < n_valid, logk, -jnp.inf)
    b = logf
    rc = row & (c - 1)
    sh = 1
    while sh < c:
        b = b + jnp.where(rc >= sh, pltpu.roll(b, sh, 0), 0.0)
        sh *= 2
    q = q_ref[...].astype(F32)
    qs_ref[...] = q * _sigmoid(q)
    kk_ref[...] = logk
    bb_ref[...] = b

    sl = 8
    row8 = lax.broadcasted_iota(jnp.int32, (sl, 1), 0)

    def chunk(ci, carry):
        r0 = pl.multiple_of(ci * c, c)
        for h in range(H_A):
            hs = slice(h * DK_A, (h + 1) * DK_A)
            qc = qs_ref[pl.ds(r0, c), hs]
            lkc = kk_ref[pl.ds(r0, c), hs]
            bc = bb_ref[pl.ds(r0, c), hs]
            vc = i_ref[pl.ds(r0, c), hs].astype(F32)
            bl = bc[c - 1:c, :]
            st = st_ref[h]
            qe = (qc * jnp.exp(bc)).astype(BF16)
            o = lax.dot_general(qe, st.astype(BF16), (((1,), (1,)), ((), ())), preferred_element_type=F32)
            tiles = []
            for r in range(c // sl):
                rows = slice(r * sl, (r + 1) * sl)
                q_r, b_r, o_r = qc[rows], bc[rows], o[rows]
                for s in range((r + 1) * sl):
                    d = b_r - (bc[s:s + 1, :] - lkc[s:s + 1, :])
                    if s >= r * sl:
                        d = jnp.where(row8 >= s - r * sl, d, -jnp.inf)
                    att = jnp.sum(q_r * jnp.exp(d), axis=-1, keepdims=True)
                    o_r = o_r + att * vc[s:s + 1, :]
                tiles.append(o_r)
            o32_ref[pl.ds(r0, c), hs] = jnp.concatenate(tiles, axis=0)
            kd = jnp.exp(lkc + (bl - bc)).astype(BF16)
            u = lax.dot_general(vc.astype(BF16), kd, (((0,), (0,)), ((), ())), preferred_element_type=F32)
            st_ref[h] = st * jnp.exp(bl) + u
        return carry

    lax.fori_loop(0, tt // c, chunk, 0, unroll=8 if (tt // c) % 8 == 0 else 1)

    g = g_ref[...].astype(F32)
    gate = g * _sigmoid(g)
    for h in range(H_A):
        hs = slice(h * DV_A, (h + 1) * DV_A)
        o = o32_ref[:, hs]
        o = o * lax.rsqrt(jnp.mean(o * o, axis=-1, keepdims=True) + RMS_EPS) * gn_ref[...]
        y_ref[:, hs] = (o * gate[:, hs]).astype(y_ref.dtype)

    @pl.when(t == pl.num_programs(1) - 1)
    def _():
        for h in range(H_A):
            s_ref[h] = st_ref[h].T


def _hgrn(proj3, lb, gnorm, s0, n_valid, tt=256):
    bsz, t, _ = proj3.shape
    tt = min(tt, t)
    lb = lb.reshape(1, MIX_A).astype(F32)
    la, l1 = jnp.log(lb), jnp.log1p(-lb)
    vec = lambda i, j: (0, 0)
    col = lambda cb: pl.BlockSpec((None, tt, MIX_A), lambda i, j: (i, j, cb))
    st = pl.BlockSpec((None, H_A, DK_A, DV_A), lambda i, j: (i, 0, 0, 0))
    return pl.pallas_call(
        functools.partial(_hgrn_kernel, tt, n_valid if t == tt else tt),
        grid=(bsz, t // tt),
        in_specs=[col(0), col(1), col(2), col(3),
                  pl.BlockSpec((1, MIX_A), vec), pl.BlockSpec((1, MIX_A), vec),
                  pl.BlockSpec((1, DV_A), vec), st],
        out_specs=[pl.BlockSpec((None, tt, MIX_A), lambda i, j: (i, j, 0)), st],
        out_shape=[jax.ShapeDtypeStruct((bsz, t, MIX_A), BF16),
                   jax.ShapeDtypeStruct((bsz, H_A, DK_A, DV_A), F32)],
        scratch_shapes=[pltpu.VMEM((H_A, DV_A, DK_A), F32), pltpu.VMEM((tt, MIX_A), F32),
                        pltpu.VMEM((tt, MIX_A), F32), pltpu.VMEM((tt, MIX_A), F32), pltpu.VMEM((tt, MIX_A), F32)],
        compiler_params=_cparams(("parallel", "arbitrary"), 32),
        name="hgrn2",
    )(proj3, proj3, proj3, proj3, la, l1, gnorm.reshape(1, DV_A), s0)


NEG_BIG = -1e30
SEL_PER_CMP = L_SLC // D_CMP


def _cmp_weights(w1, b1, w2, b2):
    m = L_CMP // D_CMP
    eye_e, eye_g = jnp.eye(2, dtype=F32), jnp.eye(G_B, dtype=F32)
    w1r = w1.reshape(2, m, D_CMP, HD_B, HD_B)
    w1big = jnp.einsum('ehjdf,ea,gb->jegdhabf', w1r, eye_e, eye_g).reshape(D_CMP * KV_B, m * KV_B)
    w2big = jnp.einsum('efo,ea,gb->egfabo', w2, eye_e, eye_g).reshape(KV_B, KV_B)
    b1big = jnp.broadcast_to(b1[:, None, :], (2, G_B, HD_B)).reshape(1, KV_B)
    b2big = jnp.broadcast_to(b2[:, None, :], (2, G_B, HD_B)).reshape(1, KV_B)
    w1tok = jnp.einsum('ehjdf,gb->ejgdhbf', w1r, eye_g).reshape(2, D_CMP, KV_B // 2, m * KV_B // 2)
    return w1big.astype(BF16), b1big, w2big.astype(BF16), b2big, w1tok.astype(BF16)


def _cmp_to_sel_map(n_cmp_pad, n_cmp, n_sb):
    mm = np.zeros((n_sb, n_cmp_pad), np.float32)
    for n in range(n_cmp):
        for i in (n, n + 1):
            if i // SEL_PER_CMP < n_sb:
                mm[i // SEL_PER_CMP, n] += 1.0
    return mm


def _compress_tail(pp, n_cmp, b1_ref, w2_ref, b2_ref):
    n_ch = pp.shape[0]
    hid = pp[:, :KV_B] + pltpu.roll(pp[:, KV_B:], n_ch - 1, 0) + b1_ref[...]
    hid = hid * _sigmoid(hid)
    kc = jnp.dot(hid.astype(BF16), w2_ref[...], preferred_element_type=F32) + b2_ref[...]
    row = lax.broadcasted_iota(jnp.int32, (n_ch, 1), 0)
    return jnp.where(row < n_cmp, kc, 0.0)


def _nsa_prep_kernel(n_cmp, win, kc_ref, ks_ref, kw_ref, w1_ref, b1_ref, w2_ref, b2_ref,
                     kck_ref, kcvt_ref, ksk_ref, ksvt_ref, kwk_ref, kwvt_ref, ct_ref, st_ref, wt_ref, xk_ref):
    half = KV_B // 2
    t = kc_ref.shape[0]
    n_ch = t // D_CMP
    for e in range(2):
        xk_ref[e] = kc_ref[:, e * half:(e + 1) * half].astype(F32)
    pp = jnp.zeros((n_ch, w1_ref.shape[2]), F32)
    for j in range(D_CMP):
        for e in range(2):
            x = xk_ref[e, pl.ds(j, n_ch, stride=D_CMP), :].astype(BF16)
            pp = pp + jnp.dot(x, w1_ref[j, e * half:(e + 1) * half, :], preferred_element_type=F32)
    kc = _compress_tail(pp, n_cmp, b1_ref, w2_ref, b2_ref)
    kck_ref[...] = kc[:, :half].astype(BF16)
    kcvt_ref[...] = kc[:, half:].T.astype(BF16)
    ct_ref[...] = kc_ref[...].astype(F32).T
    ks_t = ks_ref[...].astype(F32).T
    st_ref[...] = ks_t
    n_aug = ksk_ref.shape[1] - half
    blk_of_key = lax.broadcasted_iota(jnp.int32, (t, n_aug), 0) // L_SLC
    one_hot = jnp.where(blk_of_key == lax.broadcasted_iota(jnp.int32, (t, n_aug), 1), 1.0, 0.0).astype(BF16)
    ksk_ref[...] = jnp.concatenate([ks_ref[:, :half].astype(BF16), one_hot], axis=1)
    ksvt_ref[...] = ks_t[half:, :].astype(BF16)
    kw_t = kw_ref[...].astype(F32).T
    wt_ref[...] = kw_t[:, t - win:]
    kwk_ref[...] = kw_ref[:, :half].astype(BF16)
    kwvt_ref[...] = kw_t[half:, :].astype(BF16)


def _nsa_prep(proj3, col_kv, cmp_big, win):
    bsz, t, _ = proj3.shape
    n_ch = t // D_CMP
    n_cmp = n_ch - L_CMP // D_CMP + 1
    half = KV_B // 2
    w1big, b1big, w2big, b2big = cmp_big[:4]
    w1pos = w1big.reshape(D_CMP, KV_B, w1big.shape[1])
    cb = col_kv // KV_B
    const = lambda i: (0, 0)
    kspec = lambda n: pl.BlockSpec((None, n, half), lambda i: (i, 0, 0))
    vspec = lambda n: pl.BlockSpec((None, half, n), lambda i: (i, 0, 0))
    fspec = lambda n: pl.BlockSpec((None, KV_B, n), lambda i: (i, 0, 0))
    col = lambda c: pl.BlockSpec((None, t, KV_B), lambda i: (i, 0, cb + c))
    return pl.pallas_call(
        functools.partial(_nsa_prep_kernel, n_cmp, win),
        grid=(bsz,),
        in_specs=[col(0), col(1), col(2),
                  pl.BlockSpec(w1pos.shape, lambda i: (0, 0, 0)), pl.BlockSpec((1, KV_B), const),
                  pl.BlockSpec((KV_B, KV_B), const), pl.BlockSpec((1, KV_B), const)],
        out_specs=[kspec(n_ch), vspec(n_ch), pl.BlockSpec((None, t, KV_B), lambda i: (i, 0, 0)), vspec(t),
                   kspec(t), vspec(t), fspec(t), fspec(t), fspec(win)],
        out_shape=[jax.ShapeDtypeStruct((bsz, n_ch, half), BF16), jax.ShapeDtypeStruct((bsz, half, n_ch), BF16),
                   jax.ShapeDtypeStruct((bsz, t, KV_B), BF16), jax.ShapeDtypeStruct((bsz, half, t), BF16),
                   jax.ShapeDtypeStruct((bsz, t, half), BF16), jax.ShapeDtypeStruct((bsz, half, t), BF16),
                   jax.ShapeDtypeStruct((bsz, KV_B, t), F32), jax.ShapeDtypeStruct((bsz, KV_B, t), F32),
                   jax.ShapeDtypeStruct((bsz, KV_B, win), F32)],
        scratch_shapes=[pltpu.VMEM((2, t, half), F32)],
        compiler_params=_cparams(("parallel",), 48),
        name="nsa_prep",
    )(proj3, proj3, proj3, w1pos, b1big, w2big, b2big)


def _nsa_prompt_kernel(tq, tk, n_sb, q_ref, gb_ref, kck_ref, kcvt_ref, ksk_ref, ksvt_ref, kwk_ref, kwvt_ref, mm_ref, o_ref):
    qi = pl.program_id(1)
    q0 = qi * tq
    n_cp = kck_ref.shape[0]
    w4 = HG_B * tq
    q_t = (q_ref[...].astype(F32) * HD_B ** -0.5).T
    g_t = _sigmoid(gb_ref[...].astype(F32)).T
    qpos = q0 + lax.broadcasted_iota(jnp.int32, (1, tq), 1)
    qpos4 = jnp.concatenate([qpos] * HG_B, axis=1)
    zpad = jnp.zeros((HD_B, tq), BF16)

    def update(state, s, pen, vt, g):
        m, l, acc = state
        if pen is not None:
            s = s + jnp.concatenate([pen] * HG_B, axis=1)
        m_new = jnp.maximum(m, jnp.max(s, axis=0, keepdims=True))
        e = jnp.exp(s - m_new)
        a = jnp.exp(m - m_new)
        l = a * l + jnp.sum(e, axis=0, keepdims=True)
        pv = jnp.dot(vt[g * HD_B:(g + 1) * HD_B, :], e.astype(BF16), preferred_element_type=F32)
        return m_new, l, a * acc + pv

    qts, o_cs, sel_ts = [], [], []
    for g in range(G_B):
        cols = []
        for hg in range(HG_B):
            h = g * HG_B + hg
            qh = q_t[h * HD_B:(h + 1) * HD_B, :].astype(BF16)
            cols.append(jnp.concatenate([qh, zpad] if g == 0 else [zpad, qh], axis=0))
        qt = jnp.concatenate(cols, axis=1)
        qts.append(qt)
        s = jnp.dot(kck_ref[...], qt, preferred_element_type=F32)
        c_end = lax.broadcasted_iota(jnp.int32, (n_cp, w4), 0) * D_CMP + (L_CMP - 1)
        mask = c_end <= qpos4
        m = jnp.max(jnp.where(mask, s, NEG_BIG), axis=0, keepdims=True)
        e = jnp.where(mask, jnp.exp(s - m), 0.0)
        d = jnp.sum(e, axis=0, keepdims=True)
        p = e / jnp.where(d > 0, d, 1.0)
        o_cs.append(jnp.dot(kcvt_ref[...], p.astype(BF16), preferred_element_type=F32)[g * HD_B:(g + 1) * HD_B, :])
        imp = p[:, 0:tq]
        for hg in range(1, HG_B):
            imp = imp + p[:, hg * tq:(hg + 1) * tq]
        bs = jnp.dot(mm_ref[...], imp, precision=HIGHEST, preferred_element_type=F32)
        blk = lax.broadcasted_iota(jnp.int32, (n_sb, tq), 0)
        cur = qpos >> 6
        forced = (blk == 0) | (blk == cur) | (blk == cur - 1)
        score = jnp.where(blk <= cur, jnp.where(forced, jnp.inf, bs), -jnp.inf)
        rank = jnp.zeros((n_sb, tq), jnp.int32)
        for i in range(n_sb):
            si = score[i:i + 1, :]
            rank = rank + jnp.where((si > score) | ((si == score) & (blk > i)), 1, 0)
        sel_ts.append(jnp.where((rank < N_SEL) & (blk <= cur), 1.0, 0.0).astype(BF16))

    n_aug = ksk_ref.shape[1] - 2 * HD_B
    qas = []
    for g in range(G_B):
        pen_rows = ((sel_ts[g].astype(F32) - 1.0) * (-NEG_BIG)).astype(BF16)
        pen_rows = jnp.concatenate([pen_rows, jnp.zeros((n_aug - n_sb, tq), BF16)], axis=0)
        qas.append(jnp.concatenate([qts[g], jnp.concatenate([pen_rows] * HG_B, axis=1)], axis=0))
    krow = lax.broadcasted_iota(jnp.int32, (tk, tq), 0)

    def far_tiles(kt, states):
        k0 = pl.multiple_of(kt * tk, tk)
        k_tile, vt_tile = ksk_ref[pl.ds(k0, tk), :], ksvt_ref[:, pl.ds(k0, tk)]
        return tuple(update(states[g], jnp.dot(k_tile, qas[g], preferred_element_type=F32), None, vt_tile, g)
                     for g in range(G_B))

    def near_tiles(kt, states):
        k0 = pl.multiple_of(kt * tk, tk)
        rel = qpos - k0
        causal = krow <= rel
        pen_causal = jnp.where(causal, 0.0, NEG_BIG)
        pen_win = jnp.where(causal & (krow > rel - WINDOW), 0.0, NEG_BIG)
        k_tile, vt_tile = ksk_ref[pl.ds(k0, tk), :], ksvt_ref[:, pl.ds(k0, tk)]
        kw_tile, vwt_tile = kwk_ref[pl.ds(k0, tk), :], kwvt_ref[:, pl.ds(k0, tk)]
        new_sel = tuple(update(states[g], jnp.dot(k_tile, qas[g], preferred_element_type=F32),
                               pen_causal, vt_tile, g) for g in range(G_B))
        new_win = tuple(update(states[G_B + g], jnp.dot(kw_tile, qts[g], preferred_element_type=F32),
                               pen_win, vwt_tile, g) for g in range(G_B))
        return new_sel + new_win

    init = (jnp.full((1, w4), NEG_BIG, F32), jnp.zeros((1, w4), F32), jnp.zeros((HD_B, w4), F32))

    def looped(_):
        first_near = jnp.maximum((q0 - WINDOW) // tk, 0)
        st = lax.fori_loop(0, first_near, far_tiles, (init,) * G_B)
        return lax.fori_loop(first_near, (q0 + tq) // tk, near_tiles, st + (init,) * G_B)

    def straight(_):
        n_w = WINDOW // tk
        lane_q = lax.broadcasted_iota(jnp.int32, (tk, tq), 1)
        pen_oldest = jnp.where(krow > lane_q, 0.0, NEG_BIG)
        pen_diag = jnp.where(krow <= lane_q, 0.0, NEG_BIG)

        def sel(kt, st, pen):
            k0 = pl.multiple_of(kt * tk, tk)
            k_tile, vt_tile = ksk_ref[pl.ds(k0, tk), :], ksvt_ref[:, pl.ds(k0, tk)]
            return tuple(update(st[g], jnp.dot(k_tile, qas[g], preferred_element_type=F32), pen, vt_tile, g)
                         for g in range(G_B))

        def win(kt, st, pen):
            k0 = pl.multiple_of(kt * tk, tk)
            kw_tile, vwt_tile = kwk_ref[pl.ds(k0, tk), :], kwvt_ref[:, pl.ds(k0, tk)]
            return tuple(update(st[g], jnp.dot(kw_tile, qts[g], preferred_element_type=F32), pen, vwt_tile, g)
                         for g in range(G_B))

        n_old = qi
        s_st = lax.fori_loop(0, n_old // 2, lambda i, st: sel(2 * i + 1, sel(2 * i, st, None), None), (init,) * G_B)
        s_st = lax.fori_loop(2 * (n_old // 2), n_old, lambda kt, st: sel(kt, st, None), s_st)
        w_st = win(qi - n_w, (init,) * G_B, pen_oldest)
        for d in range(n_w - 1, 0, -1):
            w_st = win(qi - d, w_st, None)
        return sel(qi, s_st, pen_diag) + win(qi, w_st, pen_diag)

    if tq == tk and WINDOW % tk == 0:
        states = lax.cond(qi >= WINDOW // tk, straight, looped, 0)
    else:
        states = looped(0)
    finish = lambda st: st[2] / jnp.where(st[1] > 0, st[1], 1.0)
    outs = []
    for g in range(G_B):
        o_c, o_s, o_w = o_cs[g], finish(states[g]), finish(states[G_B + g])
        for hg in range(HG_B):
            c = (g * HG_B + hg) * 3
            sl = slice(hg * tq, (hg + 1) * tq)
            outs.append(g_t[c:c + 1, :] * o_c[:, sl] + g_t[c + 1:c + 2, :] * o_s[:, sl]
                        + g_t[c + 2:c + 3, :] * o_w[:, sl])
    o_ref[...] = jnp.concatenate(outs, axis=0).T.astype(o_ref.dtype)


def _nsa_prompt(proj3, col_qb, col_gb, prep, tq=256, tk=256):
    bsz, t, _ = proj3.shape
    tq, tk = min(tq, t), min(tk, t)
    kck, kcvt, ksk, ksvt, kwk, kwvt = prep
    n_ch = kck.shape[1]
    n_cmp = n_ch - L_CMP // D_CMP + 1
    n_sb = -(-t // L_SLC)
    half = KV_B // 2
    mm = jnp.asarray(_cmp_to_sel_map(n_ch, n_cmp, n_sb))
    per_b = lambda shape: pl.BlockSpec((None,) + shape, lambda i, j: (i, 0, 0))
    return pl.pallas_call(
        functools.partial(_nsa_prompt_kernel, tq, tk, n_sb),
        grid=(bsz, t // tq),
        in_specs=[pl.BlockSpec((None, tq, MIX_B), lambda i, j: (i, j, col_qb // MIX_B)),
                  pl.BlockSpec((None, tq, 128), lambda i, j: (i, j, col_gb // 128)),
                  per_b((n_ch, half)), per_b((half, n_ch)), per_b((t, KV_B)), per_b((half, t)),
                  per_b((t, half)), per_b((half, t)),
                  pl.BlockSpec((n_sb, n_ch), lambda i, j: (0, 0))],
        out_specs=pl.BlockSpec((None, tq, MIX_B), lambda i, j: (i, j, 0)),
        out_shape=jax.ShapeDtypeStruct((bsz, t, MIX_B), BF16),
        compiler_params=_cparams(("parallel", "parallel"), 40),
        name="nsa_prompt",
    )(proj3, proj3, kck, kcvt, ksk, ksvt, kwk, kwvt, mm)


def _pages_view(cache):
    n_phys, depth, page = cache.shape[:3]
    return jnp.transpose(cache, (0, 1, 3, 4, 5, 2)).reshape(n_phys, depth, 2, KV_B // 2, page)


def _cmp_pages_kernel(n_pg, page, *refs):
    pages, w1_ref, o_ref, xs_ref = refs[2:2 + n_pg], refs[2 + n_pg], refs[3 + n_pg], refs[4 + n_pg]
    half = KV_B // 2
    ch_pg = page // D_CMP
    n_ch = n_pg * ch_pg
    r = lax.broadcasted_iota(jnp.int32, (page, page), 0)
    pos = lax.broadcasted_iota(jnp.int32, (page, page), 1)
    pick = jnp.where(pos == (r % ch_pg) * D_CMP + r // ch_pg, 1.0, 0.0).astype(BF16)
    for i, pg in enumerate(pages):
        kv_t = pg[...].reshape(KV_B, page).astype(BF16)
        y = lax.dot_general(pick, kv_t, (((1,), (1,)), ((), ())), preferred_element_type=F32)
        for e in range(2):
            for j in range(D_CMP):
                xs_ref[e, j, i * ch_pg:(i + 1) * ch_pg, :] = y[j * ch_pg:(j + 1) * ch_pg, e * half:(e + 1) * half]
    for e in range(2):
        acc = jnp.zeros((n_ch, 2 * half), F32)
        for j in range(D_CMP):
            acc = acc + jnp.dot(xs_ref[e, j].astype(BF16), w1_ref[e, j], preferred_element_type=F32)
        o_ref[:, e * half:(e + 1) * half] = acc[:, :half]
        o_ref[:, KV_B + e * half:KV_B + (e + 1) * half] = acc[:, half:]


def _cmp_pages(pages, layer, page_table, w1e):
    n_phys, depth, _, _, page = pages.shape
    bsz, n_pages = page_table.shape
    ch_pg = page // D_CMP
    n_pg = next(c for c in (32, 16, 8, 4, 2, 1) if n_pages % c == 0)

    def page_spec(i):
        return pl.BlockSpec((None, None, 2, KV_B // 2, page),
                            lambda b, s, pt, lyr: (pt[b * n_pages + s * n_pg + i], lyr[0], 0, 0, 0))

    grid_spec = pltpu.PrefetchScalarGridSpec(
        num_scalar_prefetch=2,
        grid=(bsz, n_pages // n_pg),
        in_specs=[page_spec(i) for i in range(n_pg)] + [pl.BlockSpec(w1e.shape, lambda b, s, pt, lyr: (0, 0, 0, 0))],
        out_specs=pl.BlockSpec((None, n_pg * ch_pg, 2 * KV_B), lambda b, s, pt, lyr: (b, s, 0)),
        scratch_shapes=[pltpu.VMEM((2, D_CMP, n_pg * ch_pg, KV_B // 2), F32)])
    return pl.pallas_call(
        functools.partial(_cmp_pages_kernel, n_pg, page),
        grid_spec=grid_spec,
        out_shape=jax.ShapeDtypeStruct((bsz, n_pages * ch_pg, 2 * KV_B), F32),
        compiler_params=_cparams(("parallel", "parallel"), 48),
        name="nsa_cmp_pages",
    )(page_table.reshape(-1), jnp.full((1,), layer, jnp.int32), *([pages] * n_pg), w1e)


def _nsa_score_kernel(past, sd, n_cmp, n_sb, pp_ref, q_ref, cw_ref, nw_ref, b1_ref, w2_ref, b2_ref, mm_ref,
                      oc_ref, ow_ref, idx_ref):
    half = KV_B // 2
    rq = HG_B * sd
    scale = HD_B ** -0.5
    n_ch = pp_ref.shape[0]
    win = cw_ref.shape[2]
    kc = _compress_tail(pp_ref[...], n_cmp, b1_ref, w2_ref, b2_ref)
    kck, kcv = kc[:, :half].astype(BF16), kc[:, half:].astype(BF16)
    qpos = past + lax.broadcasted_iota(jnp.int32, (rq, 1), 0) % sd
    nt = (((1,), (1,)), ((), ()))

    def softmax(parts):
        m = functools.reduce(jnp.maximum, [jnp.max(jnp.where(mk, s, NEG_BIG), axis=-1, keepdims=True) for s, mk in parts])
        es = [jnp.where(mk, jnp.exp(s - m), 0.0) for s, mk in parts]
        d = functools.reduce(jnp.add, [jnp.sum(e, axis=-1, keepdims=True) for e in es])
        return [e / jnp.where(d > 0, d, 1.0) for e in es]

    imps = []
    for g in range(G_B):
        qg = q_ref[g * rq:(g + 1) * rq, :]
        s = lax.dot_general(qg, kck, nt, preferred_element_type=F32) * scale
        c_end = lax.broadcasted_iota(jnp.int32, (rq, n_ch), 1) * D_CMP + (L_CMP - 1)
        p, = softmax([(s, c_end <= qpos)])
        oc_ref[g * rq:(g + 1) * rq, :] = jnp.dot(p.astype(BF16), kcv, preferred_element_type=F32)
        imps.append(functools.reduce(jnp.add, [p[hg * sd:(hg + 1) * sd, :] for hg in range(HG_B)]))
        s1 = jnp.dot(qg, cw_ref[0].astype(BF16), preferred_element_type=F32) * scale
        s2 = lax.dot_general(qg, nw_ref[:, :half].astype(BF16), nt, preferred_element_type=F32) * scale
        wp1 = past - win + lax.broadcasted_iota(jnp.int32, s1.shape, 1)
        j2 = lax.broadcasted_iota(jnp.int32, s2.shape, 1)
        wp2 = past + j2
        p1, p2 = softmax([(s1, (wp1 <= qpos) & (wp1 > qpos - WINDOW) & (wp1 >= 0)),
                          (s2, (wp2 <= qpos) & (wp2 > qpos - WINDOW) & (j2 < sd))])
        ow_ref[g * rq:(g + 1) * rq, :] = (
            lax.dot_general(p1.astype(BF16), cw_ref[1].astype(BF16), nt, preferred_element_type=F32)
            + jnp.dot(p2.astype(BF16), nw_ref[:, half:].astype(BF16), preferred_element_type=F32))
    imp = jnp.concatenate(imps, axis=0)
    rows = G_B * sd
    bs = jnp.dot(imp, mm_ref[...], precision=HIGHEST, preferred_element_type=F32)
    n_lane = bs.shape[1]
    blk = lax.broadcasted_iota(jnp.int32, (rows, n_lane), 1)
    cur = (past + lax.broadcasted_iota(jnp.int32, (rows, 1), 0) % sd) >> 6
    forced = (blk == 0) | (blk == cur) | (blk == cur - 1)
    score = jnp.where(blk <= cur, jnp.where(forced, jnp.inf, bs), -jnp.inf)
    blk_f = blk.astype(F32)
    taken = blk >= n_sb
    lane = lax.broadcasted_iota(jnp.int32, (rows, 128), 1)
    picked = jnp.zeros((rows, 128), F32)
    for it in range(min(N_SEL, n_sb)):
        live = jnp.where(taken, -jnp.inf, score)
        mx = jnp.max(live, axis=-1, keepdims=True)
        ix = jnp.min(jnp.where(~taken & (score == mx), blk_f, float(n_lane)), axis=-1, keepdims=True)
        picked = jnp.where(lane == it, ix, picked)
        taken = taken | (blk_f == ix)
    idx_ref[...] = picked.astype(jnp.int32)


def _nsa_select_kernel(past, sd, n_sb, n_pages, k_sel, *refs):
    pt_ref, ix_ref, lyr_ref = refs[:3]
    pages = refs[3:3 + k_sel]
    new_ref, q_ref, oc_ref, ow_ref, gt_ref, o_ref = refs[3 + k_sel:]
    rq = HG_B * sd
    page = new_ref.shape[-1]
    blk_pg = page // L_SLC
    b, g, i = pl.program_id(0), pl.program_id(1), pl.program_id(2)
    base = ((b * G_B + g) * sd + i) * k_sel
    lane = lax.broadcasted_iota(jnp.int32, (1, k_sel * page), 1)
    in_page = lane % page
    kpos = in_page
    picked = lane < 0
    kts, vts = [], []
    for j in range(k_sel):
        blk = ix_ref[base + j]
        is_new = blk == n_sb - 1
        kts.append(jnp.where(is_new, new_ref[0], pages[j][0]))
        vts.append(jnp.where(is_new, new_ref[1], pages[j][1]))
        mine = lane // page == j
        kpos = kpos + jnp.where(mine, (blk // blk_pg) * page, 0)
        picked = picked | (mine & (in_page // L_SLC == blk % blk_pg))
    kt_all = jnp.concatenate(kts, axis=1).astype(BF16)
    vt_all = jnp.concatenate(vts, axis=1).astype(BF16)
    s = jnp.dot(q_ref[...], kt_all, preferred_element_type=F32) * (HD_B ** -0.5)
    row_q = lax.broadcasted_iota(jnp.int32, (rq, 1), 0) % sd
    mask = picked & (kpos <= past + row_q)
    m = jnp.max(jnp.where(mask, s, NEG_BIG), axis=-1, keepdims=True)
    e = jnp.where(mask, jnp.exp(s - m), 0.0)
    d = jnp.sum(e, axis=-1, keepdims=True)
    p = e / jnp.where(d > 0, d, 1.0)
    o_s = lax.dot_general(p.astype(BF16), vt_all, (((1,), (1,)), ((), ())), preferred_element_type=F32)
    gates = _sigmoid(gt_ref[...])
    y = gates[0] * oc_ref[...] + gates[1] * o_s + gates[2] * ow_ref[...]

    @pl.when(i == 0)
    def _():
        o_ref[...] = jnp.zeros_like(o_ref)

    o_ref[...] += jnp.where(row_q == i, y, 0.0)


def _nsa_sample(proj3, cols, kvs, sd, cmp_big, pages_cmp, pages_slc, win_view, layer, page_table):
    col_qb, col_kv, col_gb = cols
    bsz = proj3.shape[0]
    n_phys, depth, _, _, page = pages_cmp.shape
    n_pages = page_table.shape[1]
    past = n_pages * page
    win = win_view.shape[-1]
    half = KV_B // 2
    rq = HG_B * sd
    n_ch = past // D_CMP
    n_cmp = (past + sd) // D_CMP - L_CMP // D_CMP + 1
    n_sb = -(-(past + sd) // L_SLC)
    k_sel = min(N_SEL, n_sb)
    assert (past + sd) // D_CMP == n_ch and past % L_SLC == 0 and sd <= L_SLC and page % L_SLC == 0
    _, b1big, w2big, b2big, w1tok = cmp_big
    pp = _cmp_pages(pages_cmp, layer, page_table, w1tok)
    q = proj3[:, :sd, col_qb:col_qb + MIX_B].reshape(bsz, sd, G_B, HG_B, HD_B).transpose(0, 2, 3, 1, 4)
    q = jnp.stack([jnp.pad(q[:, g], ((0, 0), (0, 0), (0, 0), (g * HD_B, half - (g + 1) * HD_B))) for g in range(G_B)], 1)
    q = q.reshape(bsz, G_B * rq, half).astype(BF16)
    gt = proj3[:, :sd, col_gb:col_gb + 3 * H_B].astype(F32).reshape(bsz, sd, G_B, HG_B, 3).transpose(0, 4, 2, 3, 1)
    gt = jnp.broadcast_to(gt.reshape(bsz, 3, G_B * rq, 1), (bsz, 3, G_B * rq, half))
    n_lane = -(-n_sb // 128) * 128
    mm = jnp.asarray(np.pad(_cmp_to_sel_map(n_ch, n_cmp, n_sb), ((0, n_lane - n_sb), (0, 0))).T)
    per_b = lambda shape: pl.BlockSpec((None,) + shape, lambda i: (i,) + (0,) * len(shape))
    const = lambda shape: pl.BlockSpec(shape, lambda i: (0,) * len(shape))
    o_c, o_w, idx = pl.pallas_call(
        functools.partial(_nsa_score_kernel, past, sd, n_cmp, n_sb),
        grid=(bsz,),
        in_specs=[per_b((n_ch, 2 * KV_B)), per_b((G_B * rq, half)),
                  pl.BlockSpec((None, None, 2, half, win), lambda i: (layer, i, 0, 0, 0)),
                  pl.BlockSpec((None, PAD_ROWS, KV_B), lambda i: (i, 0, col_kv // KV_B + 2)),
                  const((1, KV_B)), const((KV_B, KV_B)), const((1, KV_B)), const((n_ch, n_lane))],
        out_specs=[per_b((G_B * rq, half)), per_b((G_B * rq, half)), per_b((G_B * sd, 128))],
        out_shape=[jax.ShapeDtypeStruct((bsz, G_B * rq, half), F32), jax.ShapeDtypeStruct((bsz, G_B * rq, half), F32),
                   jax.ShapeDtypeStruct((bsz, G_B * sd, 128), jnp.int32)],
        compiler_params=_cparams(("parallel",), 40),
        name="nsa_score_sample",
    )(pp, q, win_view, proj3, b1big, w2big, b2big, mm)
    blk_pg = page // L_SLC
    n_cached = past // L_SLC
    new_blk = jnp.pad(kvs.reshape(bsz, sd, 2, half).transpose(0, 2, 3, 1), ((0, 0), (0, 0), (0, 0), (0, page - sd)))

    def blk_spec(j):
        def index(b, g, i, pt, ix, lyr):
            blk = jnp.minimum(ix[((b * G_B + g) * sd + i) * k_sel + j], n_cached - 1)
            return pt[b * n_pages + blk // blk_pg], lyr[0], 0, 0, 0
        return pl.BlockSpec((None, None, 2, half, page), index)

    grp = lambda b, g, i, pt, ix, lyr: (b, g, 0)
    grid_spec = pltpu.PrefetchScalarGridSpec(
        num_scalar_prefetch=3,
        grid=(bsz, G_B, sd),
        in_specs=[blk_spec(j) for j in range(k_sel)] + [
            pl.BlockSpec((None, 2, half, page), lambda b, g, i, pt, ix, lyr: (b, 0, 0, 0)),
            pl.BlockSpec((None, rq, half), grp), pl.BlockSpec((None, rq, half), grp), pl.BlockSpec((None, rq, half), grp),
            pl.BlockSpec((None, 3, rq, half), lambda b, g, i, pt, ix, lyr: (b, 0, g, 0))],
        out_specs=pl.BlockSpec((None, rq, half), grp))
    y = pl.pallas_call(
        functools.partial(_nsa_select_kernel, past, sd, n_sb, n_pages, k_sel),
        grid_spec=grid_spec,
        out_shape=jax.ShapeDtypeStruct((bsz, G_B * rq, half), F32),
        compiler_params=_cparams(("parallel", "parallel", "arbitrary"), 40),
        name="nsa_select_sample",
    )(page_table.reshape(-1), idx[:, :, :k_sel].reshape(-1), jnp.full((1,), layer, jnp.int32),
      *([pages_slc] * k_sel), new_blk, q, o_c, o_w, gt)
    y = y.reshape(bsz, G_B, HG_B, sd, G_B, HD_B)
    y = jnp.stack([y[:, g, :, :, g] for g in range(G_B)], axis=1)
    return y.transpose(0, 3, 1, 2, 4).reshape(bsz, sd, MIX_B)


def _prep_w_in(w, d):
    o = np.cumsum([0, MIX_A, MIX_A, MIX_A, MIX_A, MIX_B, KV_B, KV_B, KV_B, 3 * H_B, 2 * C_CONV, 3 * d])
    parts = [w[:, o[0]:o[4]], w[:, o[10]:o[11]], w[:, o[4]:o[5]], w[:, o[9]:o[10]], w[:, o[5]:o[8]], w[:, o[8]:o[9]]]
    n = sum(p.shape[1] for p in parts)
    n_pad = -(-n // 512) * 512
    parts.append(jnp.zeros((w.shape[0], n_pad - n), w.dtype))
    return jnp.concatenate(parts, axis=1).astype(BF16)


def _layer(x3, n_valid, p, s0, conv_prefix, mem_kv, nsa_fn, alpha):
    bsz, t, d = x3.shape
    m = bsz * t
    col_qb = COL_MG + 3 * d
    col_glu = col_qb + MIX_B
    col_kv = col_glu + 2 * C_CONV
    col_gb = col_kv + 3 * KV_B
    proj = _matmul(x3.reshape(m, d), p['w_in'], 1024, 1536, out_dtype=BF16)
    proj3 = proj.reshape(bsz, t, -1)
    ya, s_new = _hgrn(proj3, p['lb'], p['hg_norm'], s0, n_valid)
    yc, conv_state = _conv(proj3, col_glu // C_CONV, conv_prefix, p['conv_w'], p['conv_b'],
                           p['conv_ln_g'], p['conv_ln_b'], n_valid)
    kv_shape = (bsz, n_valid, 2, G_B, HD_B)
    kvc = proj3[:, :n_valid, col_kv:col_kv + KV_B].astype(F32)
    kvs = proj3[:, :n_valid, col_kv + KV_B:col_kv + 2 * KV_B].astype(F32)
    kvw = proj3[:, :n_valid, col_kv + 2 * KV_B:col_kv + 3 * KV_B].astype(F32)
    yb, nsa_extra = nsa_fn(proj3, (col_qb, col_kv, col_gb), kvc, kvs, kvw)
    kvc, kvs = kvc.reshape(kv_shape), kvs.reshape(kv_shape)
    x1 = _merge(ya.reshape(m, MIX_A), yb.reshape(m, MIX_B), yc.reshape(m, C_CONV), proj, x3.reshape(m, d),
                p['w_pa'], p['w_pb'], p['w_pc'], p['w_out'], p['ln_g'][0:1], p['ln_b'][0:1], alpha)
    x2 = _xattn(x1.reshape(bsz, t, d), mem_kv[0], mem_kv[1], p['w_xq'], p['w_xo'], p['ln_g'][1:2], p['ln_b'][1:2], alpha)
    x3n = _mlp(x2.reshape(m, d), p['w_up'], p['w_down'], p['ln_g'][2:3], p['ln_b'][2:3], alpha)
    return x3n.reshape(bsz, t, d), kvc, kvs, nsa_extra, s_new, conv_state


def kernel(x_prompt, x_sample, cache_cmp, cache_slc, cache_win, state_hgrn, state_conv, cache_mem, page_table, mem_prompt, w_in, lb_raw, hg_norm, w_cmp1, b_cmp1, w_cmp2, b_cmp2, conv_w, conv_b, conv_ln_g, conv_ln_b, w_pa, w_pb, w_pc, w_out, ln_g, ln_b, w_xq, w_xkv, w_xo, w_up, w_down):
    bp, t, d = x_prompt.shape
    bd, sd = x_sample.shape[:2]
    depth = w_in.shape[0]
    n_mem = mem_prompt.shape[1]
    win_buf = cache_win.shape[2]
    alpha = (2 * depth) ** 0.25
    lb_cum = jnp.cumsum(jax.nn.softmax(lb_raw.astype(F32), axis=0), axis=0)
    lb_all = lb_cum - lb_cum[0]
    pages_cmp, pages_slc, win_view = _pages_view(cache_cmp), _pages_view(cache_slc), _pages_view(cache_win)
    mem_cache = _mem_tile_order(cache_mem, d)
    xp = x_prompt
    xs = jnp.pad(x_sample, ((0, 0), (0, PAD_ROWS - sd), (0, 0)))
    outs = {k: [] for k in ('cmp_p', 'cmp_s', 'slc_p', 'slc_s', 'win_p', 'win_s', 'hg_p', 'hg_s', 'cv_p', 'cv_s', 'mem_p')}
    for l in range(depth):
        p = {'w_in': _prep_w_in(w_in[l], d), 'lb': lb_all[l], 'hg_norm': hg_norm[l], 'conv_w': conv_w[l],
             'conv_b': conv_b[l].reshape(1, -1), 'conv_ln_g': conv_ln_g[l].reshape(1, -1),
             'conv_ln_b': conv_ln_b[l].reshape(1, -1),
             'w_pa': w_pa[l].astype(BF16), 'w_pb': w_pb[l].astype(BF16), 'w_pc': w_pc[l].astype(BF16),
             'w_out': w_out[l].astype(BF16), 'ln_g': ln_g[l], 'ln_b': ln_b[l],
             'w_xq': w_xq[l].astype(BF16), 'w_xo': w_xo[l].astype(BF16),
             'w_up': w_up[l].astype(BF16), 'w_down': w_down[l].astype(BF16)}
        cmp_w = (w_cmp1[l], b_cmp1[l], w_cmp2[l], b_cmp2[l])

        cmp_big = _cmp_weights(*cmp_w)

        def nsa_prompt(proj3, cols, kvc, kvs, kvw):
            col_qb, col_kv, col_gb = cols
            prep = _nsa_prep(proj3, col_kv, cmp_big, min(win_buf, t))
            o = _nsa_prompt(proj3, col_qb, col_gb, prep[:6])
            ct, st_, wt = prep[6:]
            return o, (ct, st_, jnp.pad(wt, ((0, 0), (0, 0), (max(win_buf - t, 0), 0))))

        def nsa_sample(proj3, cols, kvc, kvs, kvw):
            o = _nsa_sample(proj3, cols, kvs, sd, cmp_big, pages_cmp, pages_slc, win_view, l, page_table)
            o = jnp.pad(o, ((0, 0), (0, PAD_ROWS - sd), (0, 0))).astype(BF16)
            win = jnp.concatenate([cache_win[l], kvw.reshape(bd, sd, 2, G_B, HD_B)], axis=1)[:, -win_buf:]
            return o, win

        w_kv = _mem_tile_order(w_xkv[l].reshape(d, 2, NX_H, d // NX_H), d).astype(BF16)
        mem_kv = _matmul(mem_prompt.reshape(bp * n_mem, d), w_kv, 1024, 512).reshape(1, bp, n_mem, 2 * d)
        xp, kc, ks_, wn, sh, cv = _layer(xp, t, p, jnp.zeros((bp, H_A, DK_A, DV_A), F32),
                                         jnp.zeros((bp, CONV_K - 1, C_CONV), F32), (mem_kv, 0), nsa_prompt, alpha)
        outs['cmp_p'].append(wn[0]); outs['slc_p'].append(wn[1]); outs['win_p'].append(wn[2])
        outs['hg_p'].append(sh); outs['cv_p'].append(cv)
        outs['mem_p'].append(_mem_head_order(mem_kv[0], d))
        xs, kc, ks_, wn, sh, cv = _layer(xs, sd, p, state_hgrn[l], state_conv[l], (mem_cache, l), nsa_sample, alpha)
        outs['cmp_s'].append(kc); outs['slc_s'].append(ks_); outs['win_s'].append(wn)
        outs['hg_s'].append(sh); outs['cv_s'].append(cv)
    st = lambda k, ax: jnp.stack(outs[k], axis=ax)

    def rows(k, ax):
        a = st(k, ax)
        a = a.reshape(a.shape[:2] + (2, G_B, HD_B, a.shape[-1]))
        return jnp.transpose(a, (0, 1, 5, 2, 3, 4))

    return (xp, xs[:, :sd],
            rows('cmp_p', 1), st('cmp_s', 1), rows('slc_p', 1), st('slc_s', 1),
            rows('win_p', 0), st('win_s', 0), st('hg_p', 0), st('hg_s', 0),
            st('cv_p', 0), st('cv_s', 0), st('mem_p', 0))
```

```python
import functools

import numpy as np
import jax
import jax.numpy as jnp
from jax import lax
from jax.experimental import pallas as pl
from jax.experimental.pallas import tpu as pltpu

F32 = jnp.float32
BF16 = jnp.bfloat16
HIGHEST = lax.Precision.HIGHEST

H_A, DK_A, DV_A, CHUNK_A = 4, 128, 128, 16
H_B, G_B, HG_B, HD_B = 8, 2, 4, 64
L_CMP, D_CMP, L_SLC, N_SEL, WINDOW = 32, 16, 64, 16, 512
C_CONV, CONV_K = 512, 31
NX_H = 4
LN_EPS, RMS_EPS = 1e-5, 1e-6
PAD_ROWS = 16
LANES = 128

MIX_A = H_A * DK_A
MIX_B = H_B * HD_B
KV_B = 2 * G_B * HD_B
COL_MG = 4 * MIX_A


def _cparams(sem, vmem_mb=None):
    return pltpu.CompilerParams(dimension_semantics=sem,
                                vmem_limit_bytes=None if vmem_mb is None else vmem_mb * 2**20)


def _ln(y, g, b):
    mu = jnp.mean(y, axis=-1, keepdims=True)
    d = y - mu
    var = jnp.mean(d * d, axis=-1, keepdims=True)
    return d * lax.rsqrt(var + LN_EPS) * g + b


def _sigmoid(x):
    return 1.0 / (1.0 + jnp.exp(-x))


def _mm_kernel(x_ref, w_ref, o_ref, xb_ref):
    @pl.when(pl.program_id(1) == 0)
    def _():
        xb_ref[...] = x_ref[...].astype(BF16)

    o_ref[...] = jnp.dot(xb_ref[...], w_ref[...], preferred_element_type=F32).astype(o_ref.dtype)


def _matmul(x, w, tm, tn, out_dtype=F32):
    m, k = x.shape
    n = w.shape[1]
    tm, tn = min(tm, m), min(tn, n)
    return pl.pallas_call(
        _mm_kernel,
        grid=(m // tm, n // tn),
        in_specs=[pl.BlockSpec((tm, k), lambda i, j: (i, 0)),
                  pl.BlockSpec((k, tn), lambda i, j: (0, j))],
        out_specs=pl.BlockSpec((tm, tn), lambda i, j: (i, j)),
        out_shape=jax.ShapeDtypeStruct((m, n), out_dtype),
        scratch_shapes=[pltpu.VMEM((tm, k), BF16)],
        compiler_params=_cparams(("parallel", "arbitrary"), 40),
        name="proj_matmul",
    )(x, w)


def _mlp_kernel(alpha, x_ref, wu_ref, wd_ref, g_ref, b_ref, o_ref, xb_ref, acc_ref):
    j = pl.program_id(1)

    @pl.when(j == 0)
    def _():
        xb_ref[...] = x_ref[...].astype(BF16)
        acc_ref[...] = jnp.zeros_like(acc_ref)

    h = jnp.dot(xb_ref[...], wu_ref[...], preferred_element_type=F32)
    h = jnp.square(jnp.maximum(h, 0.0)).astype(BF16)
    acc_ref[...] += jnp.dot(h, wd_ref[...], preferred_element_type=F32)

    @pl.when(j == pl.num_programs(1) - 1)
    def _():
        o_ref[...] = _ln(alpha * x_ref[...] + acc_ref[...], g_ref[...], b_ref[...])


def _mlp(x, w_up, w_down, g, b, alpha, tm=512, tf=2048):
    m, d = x.shape
    ff = w_up.shape[1]
    tm = min(tm, m)
    return pl.pallas_call(
        functools.partial(_mlp_kernel, alpha),
        grid=(m // tm, ff // tf),
        in_specs=[pl.BlockSpec((tm, d), lambda i, j: (i, 0)),
                  pl.BlockSpec((d, tf), lambda i, j: (0, j)),
                  pl.BlockSpec((tf, d), lambda i, j: (j, 0)),
                  pl.BlockSpec((1, d), lambda i, j: (0, 0)),
                  pl.BlockSpec((1, d), lambda i, j: (0, 0))],
        out_specs=pl.BlockSpec((tm, d), lambda i, j: (i, 0)),
        out_shape=jax.ShapeDtypeStruct((m, d), F32),
        scratch_shapes=[pltpu.VMEM((tm, d), BF16), pltpu.VMEM((tm, d), F32)],
        compiler_params=_cparams(("parallel", "arbitrary"), 48),
        name="mlp",
    )(x, w_up, w_down, g, b)


def _merge_kernel(alpha, ya_ref, yb_ref, yc_ref, ma_ref, mb_ref, mc_ref, x_ref,
                  wpa_ref, wpb_ref, wpc_ref, wout_ref, g_ref, b_ref, o_ref):
    def branch(y_ref, m_ref, w_ref):
        return _sigmoid(m_ref[...].astype(F32)) * jnp.dot(y_ref[...].astype(BF16), w_ref[...], preferred_element_type=F32)

    merged = branch(ya_ref, ma_ref, wpa_ref) + branch(yb_ref, mb_ref, wpb_ref) + branch(yc_ref, mc_ref, wpc_ref)
    y = jnp.dot(merged.astype(BF16), wout_ref[...], preferred_element_type=F32)
    o_ref[...] = _ln(alpha * x_ref[...] + y, g_ref[...], b_ref[...])


def _merge(ya, yb, yc, proj, x, wpa, wpb, wpc, wout, g, b, alpha, tm=512):
    m, d = x.shape
    tm = min(tm, m)
    mg0 = COL_MG // d
    row = lambda i: (i, 0)
    const = lambda i: (0, 0)
    return pl.pallas_call(
        functools.partial(_merge_kernel, alpha),
        grid=(m // tm,),
        in_specs=[pl.BlockSpec((tm, MIX_A), row), pl.BlockSpec((tm, MIX_B), row), pl.BlockSpec((tm, C_CONV), row),
                  pl.BlockSpec((tm, d), lambda i: (i, mg0)), pl.BlockSpec((tm, d), lambda i: (i, mg0 + 1)),
                  pl.BlockSpec((tm, d), lambda i: (i, mg0 + 2)),
                  pl.BlockSpec((tm, d), row),
                  pl.BlockSpec((MIX_A, d), const), pl.BlockSpec((MIX_B, d), const), pl.BlockSpec((C_CONV, d), const),
                  pl.BlockSpec((d, d), const), pl.BlockSpec((1, d), const), pl.BlockSpec((1, d), const)],
        out_specs=pl.BlockSpec((tm, d), row),
        out_shape=jax.ShapeDtypeStruct((m, d), F32),
        compiler_params=_cparams(("parallel",), 48),
        name="merge_out",
    )(ya, yb, yc, proj, proj, proj, x, wpa, wpb, wpc, wout, g, b)


def _xattn_kernel(alpha, x_ref, kv_ref, wq_ref, wo_ref, g_ref, b_ref, o_ref):
    x = x_ref[...]
    d = x.shape[-1]
    hd = d // NX_H
    q = jnp.dot(x.astype(BF16), wq_ref[...], preferred_element_type=F32)
    n_dt = hd // LANES

    def head(base, h):
        parts = [kv_ref[:, base + (dt * NX_H + h) * LANES:base + (dt * NX_H + h + 1) * LANES] for dt in range(n_dt)]
        return jnp.concatenate(parts, axis=1).astype(BF16)

    outs = []
    for h in range(NX_H):
        qh = q[:, h * hd:(h + 1) * hd].astype(BF16)
        kh, vh = head(0, h), head(d, h)
        s = lax.dot_general(qh, kh, (((1,), (1,)), ((), ())), preferred_element_type=F32) * (hd ** -0.5)
        e = jnp.exp(s - jnp.max(s, axis=-1, keepdims=True))
        p = e / jnp.sum(e, axis=-1, keepdims=True)
        outs.append(jnp.dot(p.astype(BF16), vh, preferred_element_type=F32))
    o = jnp.concatenate(outs, axis=-1)
    y = jnp.dot(o.astype(BF16), wo_ref[...], preferred_element_type=F32)
    o_ref[...] = _ln(alpha * x + y, g_ref[...], b_ref[...])


def _mem_tile_order(a, d):
    lead = a.shape[:-3]
    a = a.reshape(lead + (2, NX_H, d // NX_H // LANES, LANES))
    return jnp.swapaxes(a, -3, -2).reshape(lead + (2 * d,))


def _mem_head_order(a, d):
    lead = a.shape[:-1]
    a = a.reshape(lead + (2, d // NX_H // LANES, NX_H, LANES))
    return jnp.swapaxes(a, -3, -2).reshape(lead + (2, NX_H, d // NX_H))


def _xattn(x, kv, layer, wq, wo, g, b, alpha, tm=512):
    bsz, t, d = x.shape
    n_mem = kv.shape[2]
    tm = min(tm, t)
    const = lambda i, j: (0, 0)
    return pl.pallas_call(
        functools.partial(_xattn_kernel, alpha),
        grid=(bsz, t // tm),
        in_specs=[pl.BlockSpec((None, tm, d), lambda i, j: (i, j, 0)),
                  pl.BlockSpec((None, None, n_mem, 2 * d), lambda i, j: (layer, i, 0, 0)),
                  pl.BlockSpec((d, d), const), pl.BlockSpec((d, d), const),
                  pl.BlockSpec((1, d), const), pl.BlockSpec((1, d), const)],
        out_specs=pl.BlockSpec((None, tm, d), lambda i, j: (i, j, 0)),
        out_shape=jax.ShapeDtypeStruct((bsz, t, d), F32),
        compiler_params=_cparams(("parallel", "parallel"), 48),
        name="xattn",
    )(x, kv, wq, wo, g, b)


CONV_HALO = 32
CONV_SUB = 32


def _conv_kernel(rt, n_valid_last, a_ref, gt_ref, pre_ref, w_ref, cb_ref, g_ref, b_ref, y_ref, st_ref, ue_ref, sh_ref):
    t = pl.program_id(1)
    off = CONV_HALO - (CONV_K - 1)
    sl = 8

    @pl.when(t == 0)
    def _():
        ue_ref[0:off, :] = jnp.zeros((off, C_CONV), F32)
        ue_ref[off:CONV_HALO, :] = pre_ref[...]

    ue_ref[CONV_HALO:CONV_HALO + rt, :] = a_ref[...].astype(F32) * _sigmoid(gt_ref[...].astype(F32))
    n_sh = rt + CONV_HALO - sl
    for s in range(1, sl):
        sh_ref[s - 1, 0:n_sh, :] = ue_ref[s:s + n_sh, :]
    sub = min(CONV_SUB, rt)
    bias = jnp.broadcast_to(cb_ref[...], (sl, C_CONV))
    for r0 in range(0, rt, sub):
        accs = [bias] * (sub // sl)
        for j in range(CONV_K):
            a, s = divmod(off + j, sl)
            w_j = w_ref[j * sl:(j + 1) * sl, :]
            for rg in range(sub // sl):
                lo = r0 + (a + rg) * sl
                win = ue_ref[lo:lo + sl, :] if s == 0 else sh_ref[s - 1, lo:lo + sl, :]
                accs[rg] = accs[rg] + w_j * win
        acc = jnp.concatenate(accs, axis=0)
        y = _ln(acc, g_ref[...], b_ref[...])
        y_ref[r0:r0 + sub, :] = (y * _sigmoid(y)).astype(y_ref.dtype)

    @pl.when(t == pl.num_programs(1) - 1)
    def _():
        st_ref[...] = ue_ref[off + n_valid_last:off + n_valid_last + CONV_K - 1, :]

    ue_ref[0:CONV_HALO, :] = ue_ref[rt:rt + CONV_HALO, :]


def _conv(proj3, col_a, prefix, w, cb, g, b, n_valid, rt=256):
    bsz, t, _ = proj3.shape
    rt = min(rt, t)
    n_valid_last = n_valid - (t - rt)
    vec = lambda i, j: (0, 0)
    return pl.pallas_call(
        functools.partial(_conv_kernel, rt, n_valid_last),
        grid=(bsz, t // rt),
        in_specs=[pl.BlockSpec((None, rt, C_CONV), lambda i, j: (i, j, col_a)),
                  pl.BlockSpec((None, rt, C_CONV), lambda i, j: (i, j, col_a + 1)),
                  pl.BlockSpec((None, CONV_K - 1, C_CONV), lambda i, j: (i, 0, 0)),
                  pl.BlockSpec((CONV_K * 8, C_CONV), vec), pl.BlockSpec((1, C_CONV), vec),
                  pl.BlockSpec((1, C_CONV), vec), pl.BlockSpec((1, C_CONV), vec)],
        out_specs=[pl.BlockSpec((None, rt, C_CONV), lambda i, j: (i, j, 0)),
                   pl.BlockSpec((None, CONV_K - 1, C_CONV), lambda i, j: (i, 0, 0))],
        out_shape=[jax.ShapeDtypeStruct((bsz, t, C_CONV), BF16),
                   jax.ShapeDtypeStruct((bsz, CONV_K - 1, C_CONV), F32)],
        scratch_shapes=[pltpu.VMEM((rt + CONV_HALO, C_CONV), F32), pltpu.VMEM((7, rt + CONV_HALO - 8, C_CONV), F32)],
        compiler_params=_cparams(("parallel", "arbitrary"), 32),
        name="conformer_conv",
    )(proj3, proj3, prefix, jnp.repeat(w, 8, axis=0), cb, g, b)


def _hgrn_kernel(tt, n_valid, q_ref, f_ref, i_ref, g_ref, la_ref, l1_ref, gn_ref, s0_ref,
                 y_ref, s_ref, st_ref, qs_ref, kk_ref, bb_ref, o32_ref):
    c = CHUNK_A
    t = pl.program_id(1)

    @pl.when(t == 0)
    def _():
        for h in range(H_A):
            st_ref[h] = s0_ref[h].T

    z = f_ref[...].astype(F32)
    sp = jnp.log(1.0 + jnp.exp(-jnp.abs(z)))
    cc = l1_ref[...] + (jnp.minimum(z, 0.0) - sp)
    a = la_ref[...]
    logf = jnp.maximum(a, cc) + jnp.log(1.0 + jnp.exp(-jnp.abs(a - cc)))
    logk = l1_ref[...] + (jnp.minimum(-z, 0.0) - sp)
    row = lax.broadcasted_iota(jnp.int32, (tt, 1), 0)
    if n_valid < tt:
        logf = jnp.where(row < n_valid, logf, 0.0)
        logk = jnp.where(row < n_valid, logk, -jnp.inf)
    b = logf
    rc = row & (c - 1)
    sh = 1
    while sh < c:
        b = b + jnp.where(rc >= sh, pltpu.roll(b, sh, 0), 0.0)
        sh *= 2
    q = q_ref[...].astype(F32)
    qs_ref[...] = q * _sigmoid(q)
    kk_ref[...] = logk
    bb_ref[...] = b

    sl = 8
    row8 = lax.broadcasted_iota(jnp.int32, (sl, 1), 0)

    def chunk(ci, carry):
        r0 = pl.multiple_of(ci * c, c)
        for h in range(H_A):
            hs = slice(h * DK_A, (h + 1) * DK_A)
            qc = qs_ref[pl.ds(r0, c), hs]
            lkc = kk_ref[pl.ds(r0, c), hs]
            bc = bb_ref[pl.ds(r0, c), hs]
            vc = i_ref[pl.ds(r0, c), hs].astype(F32)
            bl = bc[c - 1:c, :]
            st = st_ref[h]
            qe = (qc * jnp.exp(bc)).astype(BF16)
            o = lax.dot_general(qe, st.astype(BF16), (((1,), (1,)), ((), ())), preferred_element_type=F32)
            tiles = []
            for r in range(c // sl):
                rows = slice(r * sl, (r + 1) * sl)
                q_r, b_r, o_r = qc[rows], bc[rows], o[rows]
                for s in range((r + 1) * sl):
                    d = b_r - (bc[s:s + 1, :] - lkc[s:s + 1, :])
                    if s >= r * sl:
                        d = jnp.where(row8 >= s - r * sl, d, -jnp.inf)
                    att = jnp.sum(q_r * jnp.exp(d), axis=-1, keepdims=True)
                    o_r = o_r + att * vc[s:s + 1, :]
                tiles.append(o_r)
            o32_ref[pl.ds(r0, c), hs] = jnp.concatenate(tiles, axis=0)
            kd = jnp.exp(lkc + (bl - bc)).astype(BF16)
            u = lax.dot_general(vc.astype(BF16), kd, (((0,), (0,)), ((), ())), preferred_element_type=F32)
            st_ref[h] = st * jnp.exp(bl) + u
        return carry

    lax.fori_loop(0, tt // c, chunk, 0, unroll=8 if (tt // c) % 8 == 0 else 1)

    g = g_ref[...].astype(F32)
    gate = g * _sigmoid(g)
    for h in range(H_A):
        hs = slice(h * DV_A, (h + 1) * DV_A)
        o = o32_ref[:, hs]
        o = o * lax.rsqrt(jnp.mean(o * o, axis=-1, keepdims=True) + RMS_EPS) * gn_ref[...]
        y_ref[:, hs] = (o * gate[:, hs]).astype(y_ref.dtype)

    @pl.when(t == pl.num_programs(1) - 1)
    def _():
        for h in range(H_A):
            s_ref[h] = st_ref[h].T


def _hgrn(proj3, lb, gnorm, s0, n_valid, tt=512):
    bsz, t, _ = proj3.shape
    tt = min(tt, t)
    lb = lb.reshape(1, MIX_A).astype(F32)
    la, l1 = jnp.log(lb), jnp.log1p(-lb)
    vec = lambda i, j: (0, 0)
    col = lambda cb: pl.BlockSpec((None, tt, MIX_A), lambda i, j: (i, j, cb))
    st = pl.BlockSpec((None, H_A, DK_A, DV_A), lambda i, j: (i, 0, 0, 0))
    return pl.pallas_call(
        functools.partial(_hgrn_kernel, tt, n_valid if t == tt else tt),
        grid=(bsz, t // tt),
        in_specs=[col(0), col(1), col(2), col(3),
                  pl.BlockSpec((1, MIX_A), vec), pl.BlockSpec((1, MIX_A), vec),
                  pl.BlockSpec((1, DV_A), vec), st],
        out_specs=[pl.BlockSpec((None, tt, MIX_A), lambda i, j: (i, j, 0)), st],
        out_shape=[jax.ShapeDtypeStruct((bsz, t, MIX_A), BF16),
                   jax.ShapeDtypeStruct((bsz, H_A, DK_A, DV_A), F32)],
        scratch_shapes=[pltpu.VMEM((H_A, DV_A, DK_A), F32), pltpu.VMEM((tt, MIX_A), F32),
                        pltpu.VMEM((tt, MIX_A), F32), pltpu.VMEM((tt, MIX_A), F32), pltpu.VMEM((tt, MIX_A), F32)],
        compiler_params=_cparams(("parallel", "arbitrary"), 32),
        name="hgrn2",
    )(proj3, proj3, proj3, proj3, la, l1, gnorm.reshape(1, DV_A), s0)


NEG_BIG = -1e30
SEL_PER_CMP = L_SLC // D_CMP


def _cmp_weights(w1, b1, w2, b2):
    m = L_CMP // D_CMP
    eye_e, eye_g = jnp.eye(2, dtype=F32), jnp.eye(G_B, dtype=F32)
    w1r = w1.reshape(2, m, D_CMP, HD_B, HD_B)
    w1big = jnp.einsum('ehjdf,ea,gb->jegdhabf', w1r, eye_e, eye_g).reshape(D_CMP * KV_B, m * KV_B)
    w2big = jnp.einsum('efo,ea,gb->egfabo', w2, eye_e, eye_g).reshape(KV_B, KV_B)
    b1big = jnp.broadcast_to(b1[:, None, :], (2, G_B, HD_B)).reshape(1, KV_B)
    b2big = jnp.broadcast_to(b2[:, None, :], (2, G_B, HD_B)).reshape(1, KV_B)
    w1tok = jnp.einsum('ehjdf,gb->ejgdhbf', w1r, eye_g).reshape(2, D_CMP, KV_B // 2, m * KV_B // 2)
    return w1big.astype(BF16), b1big, w2big.astype(BF16), b2big, w1tok.astype(BF16)


def _cmp_to_sel_map(n_cmp_pad, n_cmp, n_sb):
    mm = np.zeros((n_sb, n_cmp_pad), np.float32)
    for n in range(n_cmp):
        for i in (n, n + 1):
            if i // SEL_PER_CMP < n_sb:
                mm[i // SEL_PER_CMP, n] += 1.0
    return mm


def _compress_tail(pp, n_cmp, b1_ref, w2_ref, b2_ref):
    n_ch = pp.shape[0]
    hid = pp[:, :KV_B] + pltpu.roll(pp[:, KV_B:], n_ch - 1, 0) + b1_ref[...]
    hid = hid * _sigmoid(hid)
    kc = jnp.dot(hid.astype(BF16), w2_ref[...], preferred_element_type=F32) + b2_ref[...]
    row = lax.broadcasted_iota(jnp.int32, (n_ch, 1), 0)
    return jnp.where(row < n_cmp, kc, 0.0)


def _nsa_prep_kernel(n_cmp, win, kc_ref, ks_ref, kw_ref, w1_ref, b1_ref, w2_ref, b2_ref,
                     kck_ref, kcvt_ref, ksk_ref, ksvt_ref, kwk_ref, kwvt_ref, ct_ref, st_ref, wt_ref, xk_ref):
    half = KV_B // 2
    t = kc_ref.shape[0]
    n_ch = t // D_CMP
    for e in range(2):
        xk_ref[e] = kc_ref[:, e * half:(e + 1) * half].astype(F32)
    pp = jnp.zeros((n_ch, w1_ref.shape[2]), F32)
    for j in range(D_CMP):
        for e in range(2):
            x = xk_ref[e, pl.ds(j, n_ch, stride=D_CMP), :].astype(BF16)
            pp = pp + jnp.dot(x, w1_ref[j, e * half:(e + 1) * half, :], preferred_element_type=F32)
    kc = _compress_tail(pp, n_cmp, b1_ref, w2_ref, b2_ref)
    kck_ref[...] = kc[:, :half].astype(BF16)
    kcvt_ref[...] = kc[:, half:].T.astype(BF16)
    ct_ref[...] = kc_ref[...].astype(F32).T
    ks_t = ks_ref[...].astype(F32).T
    st_ref[...] = ks_t
    n_aug = ksk_ref.shape[1] - half
    blk_of_key = lax.broadcasted_iota(jnp.int32, (t, n_aug), 0) // L_SLC
    one_hot = jnp.where(blk_of_key == lax.broadcasted_iota(jnp.int32, (t, n_aug), 1), 1.0, 0.0).astype(BF16)
    ksk_ref[...] = jnp.concatenate([ks_ref[:, :half].astype(BF16), one_hot], axis=1)
    ksvt_ref[...] = ks_t[half:, :].astype(BF16)
    kw_t = kw_ref[...].astype(F32).T
    wt_ref[...] = kw_t[:, t - win:]
    kwk_ref[...] = kw_ref[:, :half].astype(BF16)
    kwvt_ref[...] = kw_t[half:, :].astype(BF16)


def _nsa_prep(proj3, col_kv, cmp_big, win):
    bsz, t, _ = proj3.shape
    n_ch = t // D_CMP
    n_cmp = n_ch - L_CMP // D_CMP + 1
    half = KV_B // 2
    w1big, b1big, w2big, b2big = cmp_big[:4]
    w1pos = w1big.reshape(D_CMP, KV_B, w1big.shape[1])
    cb = col_kv // KV_B
    const = lambda i: (0, 0)
    kspec = lambda n: pl.BlockSpec((None, n, half), lambda i: (i, 0, 0))
    vspec = lambda n: pl.BlockSpec((None, half, n), lambda i: (i, 0, 0))
    fspec = lambda n: pl.BlockSpec((None, KV_B, n), lambda i: (i, 0, 0))
    col = lambda c: pl.BlockSpec((None, t, KV_B), lambda i: (i, 0, cb + c))
    return pl.pallas_call(
        functools.partial(_nsa_prep_kernel, n_cmp, win),
        grid=(bsz,),
        in_specs=[col(0), col(1), col(2),
                  pl.BlockSpec(w1pos.shape, lambda i: (0, 0, 0)), pl.BlockSpec((1, KV_B), const),
                  pl.BlockSpec((KV_B, KV_B), const), pl.BlockSpec((1, KV_B), const)],
        out_specs=[kspec(n_ch), vspec(n_ch), pl.BlockSpec((None, t, KV_B), lambda i: (i, 0, 0)), vspec(t),
                   kspec(t), vspec(t), fspec(t), fspec(t), fspec(win)],
        out_shape=[jax.ShapeDtypeStruct((bsz, n_ch, half), BF16), jax.ShapeDtypeStruct((bsz, half, n_ch), BF16),
                   jax.ShapeDtypeStruct((bsz, t, KV_B), BF16), jax.ShapeDtypeStruct((bsz, half, t), BF16),
                   jax.ShapeDtypeStruct((bsz, t, half), BF16), jax.ShapeDtypeStruct((bsz, half, t), BF16),
                   jax.ShapeDtypeStruct((bsz, KV_B, t), F32), jax.ShapeDtypeStruct((bsz, KV_B, t), F32),
                   jax.ShapeDtypeStruct((bsz, KV_B, win), F32)],
        scratch_shapes=[pltpu.VMEM((2, t, half), F32)],
        compiler_params=_cparams(("parallel",), 48),
        name="nsa_prep",
    )(proj3, proj3, proj3, w1pos, b1big, w2big, b2big)


def _nsa_prompt_kernel(tq, tk, n_sb, q_ref, gb_ref, kck_ref, kcvt_ref, ksk_ref, ksvt_ref, kwk_ref, kwvt_ref, mm_ref, o_ref):
    qi = pl.program_id(1)
    q0 = qi * tq
    n_cp = kck_ref.shape[0]
    w4 = HG_B * tq
    q_t = (q_ref[...].astype(F32) * HD_B ** -0.5).T
    g_t = _sigmoid(gb_ref[...].astype(F32)).T
    qpos = q0 + lax.broadcasted_iota(jnp.int32, (1, tq), 1)
    qpos4 = jnp.concatenate([qpos] * HG_B, axis=1)
    zpad = jnp.zeros((HD_B, tq), BF16)

    def update(state, s, pen, vt, g):
        m, l, acc = state
        if pen is not None:
            s = s + jnp.concatenate([pen] * HG_B, axis=1)
        m_new = jnp.maximum(m, jnp.max(s, axis=0, keepdims=True))
        e = jnp.exp(s - m_new)
        a = jnp.exp(m - m_new)
        l = a * l + jnp.sum(e, axis=0, keepdims=True)
        pv = jnp.dot(vt[g * HD_B:(g + 1) * HD_B, :], e.astype(BF16), preferred_element_type=F32)
        return m_new, l, a * acc + pv

    qts, o_cs, sel_ts = [], [], []
    for g in range(G_B):
        cols = []
        for hg in range(HG_B):
            h = g * HG_B + hg
            qh = q_t[h * HD_B:(h + 1) * HD_B, :].astype(BF16)
            cols.append(jnp.concatenate([qh, zpad] if g == 0 else [zpad, qh], axis=0))
        qt = jnp.concatenate(cols, axis=1)
        qts.append(qt)
        s = jnp.dot(kck_ref[...], qt, preferred_element_type=F32)
        c_end = lax.broadcasted_iota(jnp.int32, (n_cp, w4), 0) * D_CMP + (L_CMP - 1)
        mask = c_end <= qpos4
        m = jnp.max(jnp.where(mask, s, NEG_BIG), axis=0, keepdims=True)
        e = jnp.where(mask, jnp.exp(s - m), 0.0)
        d = jnp.sum(e, axis=0, keepdims=True)
        p = e / jnp.where(d > 0, d, 1.0)
        o_cs.append(jnp.dot(kcvt_ref[...], p.astype(BF16), preferred_element_type=F32)[g * HD_B:(g + 1) * HD_B, :])
        imp = p[:, 0:tq]
        for hg in range(1, HG_B):
            imp = imp + p[:, hg * tq:(hg + 1) * tq]
        bs = jnp.dot(mm_ref[...], imp, precision=HIGHEST, preferred_element_type=F32)
        blk = lax.broadcasted_iota(jnp.int32, (n_sb, tq), 0)
        cur = qpos >> 6
        forced = (blk == 0) | (blk == cur) | (blk == cur - 1)
        score = jnp.where(blk <= cur, jnp.where(forced, jnp.inf, bs), -jnp.inf)
        rank = jnp.zeros((n_sb, tq), jnp.int32)
        for i in range(n_sb):
            si = score[i:i + 1, :]
            rank = rank + jnp.where((si > score) | ((si == score) & (blk > i)), 1, 0)
        sel_ts.append(jnp.where((rank < N_SEL) & (blk <= cur), 1.0, 0.0).astype(BF16))

    n_aug = ksk_ref.shape[1] - 2 * HD_B
    qas = []
    for g in range(G_B):
        pen_rows = ((sel_ts[g].astype(F32) - 1.0) * (-NEG_BIG)).astype(BF16)
        pen_rows = jnp.concatenate([pen_rows, jnp.zeros((n_aug - n_sb, tq), BF16)], axis=0)
        qas.append(jnp.concatenate([qts[g], jnp.concatenate([pen_rows] * HG_B, axis=1)], axis=0))
    krow = lax.broadcasted_iota(jnp.int32, (tk, tq), 0)

    def far_tiles(kt, states):
        k0 = pl.multiple_of(kt * tk, tk)
        k_tile, vt_tile = ksk_ref[pl.ds(k0, tk), :], ksvt_ref[:, pl.ds(k0, tk)]
        return tuple(update(states[g], jnp.dot(k_tile, qas[g], preferred_element_type=F32), None, vt_tile, g)
                     for g in range(G_B))

    def near_tiles(kt, states):
        k0 = pl.multiple_of(kt * tk, tk)
        rel = qpos - k0
        causal = krow <= rel
        pen_causal = jnp.where(causal, 0.0, NEG_BIG)
        pen_win = jnp.where(causal & (krow > rel - WINDOW), 0.0, NEG_BIG)
        k_tile, vt_tile = ksk_ref[pl.ds(k0, tk), :], ksvt_ref[:, pl.ds(k0, tk)]
        kw_tile, vwt_tile = kwk_ref[pl.ds(k0, tk), :], kwvt_ref[:, pl.ds(k0, tk)]
        new_sel = tuple(update(states[g], jnp.dot(k_tile, qas[g], preferred_element_type=F32),
                               pen_causal, vt_tile, g) for g in range(G_B))
        new_win = tuple(update(states[G_B + g], jnp.dot(kw_tile, qts[g], preferred_element_type=F32),
                               pen_win, vwt_tile, g) for g in range(G_B))
        return new_sel + new_win

    init = (jnp.full((1, w4), NEG_BIG, F32), jnp.zeros((1, w4), F32), jnp.zeros((HD_B, w4), F32))

    def looped(_):
        first_near = jnp.maximum((q0 - WINDOW) // tk, 0)
        st = lax.fori_loop(0, first_near, far_tiles, (init,) * G_B)
        return lax.fori_loop(first_near, (q0 + tq) // tk, near_tiles, st + (init,) * G_B)

    def straight(_):
        n_w = WINDOW // tk
        lane_q = lax.broadcasted_iota(jnp.int32, (tk, tq), 1)
        pen_oldest = jnp.where(krow > lane_q, 0.0, NEG_BIG)
        pen_diag = jnp.where(krow <= lane_q, 0.0, NEG_BIG)

        def sel(kt, st, pen):
            k0 = pl.multiple_of(kt * tk, tk)
            k_tile, vt_tile = ksk_ref[pl.ds(k0, tk), :], ksvt_ref[:, pl.ds(k0, tk)]
            return tuple(update(st[g], jnp.dot(k_tile, qas[g], preferred_element_type=F32), pen, vt_tile, g)
                         for g in range(G_B))

        def win(kt, st, pen):
            k0 = pl.multiple_of(kt * tk, tk)
            kw_tile, vwt_tile = kwk_ref[pl.ds(k0, tk), :], kwvt_ref[:, pl.ds(k0, tk)]
            return tuple(update(st[g], jnp.dot(kw_tile, qts[g], preferred_element_type=F32), pen, vwt_tile, g)
                         for g in range(G_B))

        n_old = qi
        s_st = lax.fori_loop(0, n_old // 2, lambda i, st: sel(2 * i + 1, sel(2 * i, st, None), None), (init,) * G_B)
        s_st = lax.fori_loop(2 * (n_old // 2), n_old, lambda kt, st: sel(kt, st, None), s_st)
        w_st = win(qi - n_w, (init,) * G_B, pen_oldest)
        for d in range(n_w - 1, 0, -1):
            w_st = win(qi - d, w_st, None)
        return sel(qi, s_st, pen_diag) + win(qi, w_st, pen_diag)

    if tq == tk and WINDOW % tk == 0:
        states = lax.cond(qi >= WINDOW // tk, straight, looped, 0)
    else:
        states = looped(0)
    finish = lambda st: st[2] / jnp.where(st[1] > 0, st[1], 1.0)
    outs = []
    for g in range(G_B):
        o_c, o_s, o_w = o_cs[g], finish(states[g]), finish(states[G_B + g])
        for hg in range(HG_B):
            c = (g * HG_B + hg) * 3
            sl = slice(hg * tq, (hg + 1) * tq)
            outs.append(g_t[c:c + 1, :] * o_c[:, sl] + g_t[c + 1:c + 2, :] * o_s[:, sl]
                        + g_t[c + 2:c + 3, :] * o_w[:, sl])
    o_ref[...] = jnp.concatenate(outs, axis=0).T.astype(o_ref.dtype)


def _nsa_prompt(proj3, col_qb, col_gb, prep, tq=256, tk=256):
    bsz, t, _ = proj3.shape
    tq, tk = min(tq, t), min(tk, t)
    kck, kcvt, ksk, ksvt, kwk, kwvt = prep
    n_ch = kck.shape[1]
    n_cmp = n_ch - L_CMP // D_CMP + 1
    n_sb = -(-t // L_SLC)
    half = KV_B // 2
    mm = jnp.asarray(_cmp_to_sel_map(n_ch, n_cmp, n_sb))
    per_b = lambda shape: pl.BlockSpec((None,) + shape, lambda i, j: (i, 0, 0))
    return pl.pallas_call(
        functools.partial(_nsa_prompt_kernel, tq, tk, n_sb),
        grid=(bsz, t // tq),
        in_specs=[pl.BlockSpec((None, tq, MIX_B), lambda i, j: (i, j, col_qb // MIX_B)),
                  pl.BlockSpec((None, tq, 128), lambda i, j: (i, j, col_gb // 128)),
                  per_b((n_ch, half)), per_b((half, n_ch)), per_b((t, KV_B)), per_b((half, t)),
                  per_b((t, half)), per_b((half, t)),
                  pl.BlockSpec((n_sb, n_ch), lambda i, j: (0, 0))],
        out_specs=pl.BlockSpec((None, tq, MIX_B), lambda i, j: (i, j, 0)),
        out_shape=jax.ShapeDtypeStruct((bsz, t, MIX_B), BF16),
        compiler_params=_cparams(("parallel", "parallel"), 40),
        name="nsa_prompt",
    )(proj3, proj3, kck, kcvt, ksk, ksvt, kwk, kwvt, mm)


def _pages_view(cache):
    n_phys, depth, page = cache.shape[:3]
    return jnp.transpose(cache, (0, 1, 3, 4, 5, 2)).reshape(n_phys, depth, 2, KV_B // 2, page)


def _cmp_pages_kernel(n_pg, page, *refs):
    pages, w1_ref, o_ref, xs_ref = refs[2:2 + n_pg], refs[2 + n_pg], refs[3 + n_pg], refs[4 + n_pg]
    half = KV_B // 2
    ch_pg = page // D_CMP
    n_ch = n_pg * ch_pg
    r = lax.broadcasted_iota(jnp.int32, (page, page), 0)
    pos = lax.broadcasted_iota(jnp.int32, (page, page), 1)
    pick = jnp.where(pos == (r % ch_pg) * D_CMP + r // ch_pg, 1.0, 0.0).astype(BF16)
    for i, pg in enumerate(pages):
        kv_t = pg[...].reshape(KV_B, page).astype(BF16)
        y = lax.dot_general(pick, kv_t, (((1,), (1,)), ((), ())), preferred_element_type=F32)
        for e in range(2):
            for j in range(D_CMP):
                xs_ref[e, j, i * ch_pg:(i + 1) * ch_pg, :] = y[j * ch_pg:(j + 1) * ch_pg, e * half:(e + 1) * half]
    for e in range(2):
        acc = jnp.zeros((n_ch, 2 * half), F32)
        for j in range(D_CMP):
            acc = acc + jnp.dot(xs_ref[e, j].astype(BF16), w1_ref[e, j], preferred_element_type=F32)
        o_ref[:, e * half:(e + 1) * half] = acc[:, :half]
        o_ref[:, KV_B + e * half:KV_B + (e + 1) * half] = acc[:, half:]


def _cmp_pages(pages, layer, page_table, w1e):
    n_phys, depth, _, _, page = pages.shape
    bsz, n_pages = page_table.shape
    ch_pg = page // D_CMP
    n_pg = next(c for c in (64, 32, 16, 8, 4, 2, 1) if n_pages % c == 0)

    def page_spec(i):
        return pl.BlockSpec((None, None, 2, KV_B // 2, page),
                            lambda b, s, pt, lyr: (pt[b * n_pages + s * n_pg + i], lyr[0], 0, 0, 0))

    grid_spec = pltpu.PrefetchScalarGridSpec(
        num_scalar_prefetch=2,
        grid=(bsz, n_pages // n_pg),
        in_specs=[page_spec(i) for i in range(n_pg)] + [pl.BlockSpec(w1e.shape, lambda b, s, pt, lyr: (0, 0, 0, 0))],
        out_specs=pl.BlockSpec((None, n_pg * ch_pg, 2 * KV_B), lambda b, s, pt, lyr: (b, s, 0)),
        scratch_shapes=[pltpu.VMEM((2, D_CMP, n_pg * ch_pg, KV_B // 2), F32)])
    return pl.pallas_call(
        functools.partial(_cmp_pages_kernel, n_pg, page),
        grid_spec=grid_spec,
        out_shape=jax.ShapeDtypeStruct((bsz, n_pages * ch_pg, 2 * KV_B), F32),
        compiler_params=_cparams(("parallel", "parallel"), 48),
        name="nsa_cmp_pages",
    )(page_table.reshape(-1), jnp.full((1,), layer, jnp.int32), *([pages] * n_pg), w1e)


def _nsa_score_kernel(past, sd, n_cmp, n_sb, pp_ref, q_ref, cw_ref, nw_ref, b1_ref, w2_ref, b2_ref, mm_ref,
                      oc_ref, ow_ref, idx_ref):
    half = KV_B // 2
    rq = HG_B * sd
    scale = HD_B ** -0.5
    n_ch = pp_ref.shape[0]
    win = cw_ref.shape[2]
    kc = _compress_tail(pp_ref[...], n_cmp, b1_ref, w2_ref, b2_ref)
    kck, kcv = kc[:, :half].astype(BF16), kc[:, half:].astype(BF16)
    qpos = past + lax.broadcasted_iota(jnp.int32, (rq, 1), 0) % sd
    nt = (((1,), (1,)), ((), ()))

    def softmax(parts):
        m = functools.reduce(jnp.maximum, [jnp.max(jnp.where(mk, s, NEG_BIG), axis=-1, keepdims=True) for s, mk in parts])
        es = [jnp.where(mk, jnp.exp(s - m), 0.0) for s, mk in parts]
        d = functools.reduce(jnp.add, [jnp.sum(e, axis=-1, keepdims=True) for e in es])
        return [e / jnp.where(d > 0, d, 1.0) for e in es]

    imps = []
    for g in range(G_B):
        qg = q_ref[g * rq:(g + 1) * rq, :]
        s = lax.dot_general(qg, kck, nt, preferred_element_type=F32) * scale
        c_end = lax.broadcasted_iota(jnp.int32, (rq, n_ch), 1) * D_CMP + (L_CMP - 1)
        p, = softmax([(s, c_end <= qpos)])
        oc_ref[g * rq:(g + 1) * rq, :] = jnp.dot(p.astype(BF16), kcv, preferred_element_type=F32)
        imps.append(functools.reduce(jnp.add, [p[hg * sd:(hg + 1) * sd, :] for hg in range(HG_B)]))
        s1 = jnp.dot(qg, cw_ref[0].astype(BF16), preferred_element_type=F32) * scale
        s2 = lax.dot_general(qg, nw_ref[:, :half].astype(BF16), nt, preferred_element_type=F32) * scale
        wp1 = past - win + lax.broadcasted_iota(jnp.int32, s1.shape, 1)
        j2 = lax.broadcasted_iota(jnp.int32, s2.shape, 1)
        wp2 = past + j2
        p1, p2 = softmax([(s1, (wp1 <= qpos) & (wp1 > qpos - WINDOW) & (wp1 >= 0)),
                          (s2, (wp2 <= qpos) & (wp2 > qpos - WINDOW) & (j2 < sd))])
        ow_ref[g * rq:(g + 1) * rq, :] = (
            lax.dot_general(p1.astype(BF16), cw_ref[1].astype(BF16), nt, preferred_element_type=F32)
            + jnp.dot(p2.astype(BF16), nw_ref[:, half:].astype(BF16), preferred_element_type=F32))
    imp = jnp.concatenate(imps, axis=0)
    rows = G_B * sd
    bs = jnp.dot(imp, mm_ref[...], precision=HIGHEST, preferred_element_type=F32)
    n_lane = bs.shape[1]
    blk = lax.broadcasted_iota(jnp.int32, (rows, n_lane), 1)
    cur = (past + lax.broadcasted_iota(jnp.int32, (rows, 1), 0) % sd) >> 6
    forced = (blk == 0) | (blk == cur) | (blk == cur - 1)
    score = jnp.where(blk <= cur, jnp.where(forced, jnp.inf, bs), -jnp.inf)
    blk_f = blk.astype(F32)
    taken = blk >= n_sb
    lane = lax.broadcasted_iota(jnp.int32, (rows, 128), 1)
    picked = jnp.zeros((rows, 128), F32)
    for it in range(min(N_SEL, n_sb)):
        live = jnp.where(taken, -jnp.inf, score)
        mx = jnp.max(live, axis=-1, keepdims=True)
        ix = jnp.min(jnp.where(~taken & (score == mx), blk_f, float(n_lane)), axis=-1, keepdims=True)
        picked = jnp.where(lane == it, ix, picked)
        taken = taken | (blk_f == ix)
    idx_ref[...] = picked.astype(jnp.int32)


def _nsa_select_kernel(past, sd, n_sb, n_pages, k_sel, *refs):
    pt_ref, ix_ref, lyr_ref = refs[:3]
    pages = refs[3:3 + k_sel]
    new_ref, q_ref, oc_ref, ow_ref, gt_ref, o_ref = refs[3 + k_sel:]
    rq = HG_B * sd
    page = new_ref.shape[-1]
    blk_pg = page // L_SLC
    b, g, i = pl.program_id(0), pl.program_id(1), pl.program_id(2)
    base = ((b * G_B + g) * sd + i) * k_sel
    lane = lax.broadcasted_iota(jnp.int32, (1, k_sel * page), 1)
    in_page = lane % page
    kpos = in_page
    picked = lane < 0
    kts, vts = [], []
    for j in range(k_sel):
        blk = ix_ref[base + j]
        is_new = blk == n_sb - 1
        kts.append(jnp.where(is_new, new_ref[0], pages[j][0]))
        vts.append(jnp.where(is_new, new_ref[1], pages[j][1]))
        mine = lane // page == j
        kpos = kpos + jnp.where(mine, (blk // blk_pg) * page, 0)
        picked = picked | (mine & (in_page // L_SLC == blk % blk_pg))
    kt_all = jnp.concatenate(kts, axis=1).astype(BF16)
    vt_all = jnp.concatenate(vts, axis=1).astype(BF16)
    s = jnp.dot(q_ref[...], kt_all, preferred_element_type=F32) * (HD_B ** -0.5)
    row_q = lax.broadcasted_iota(jnp.int32, (rq, 1), 0) % sd
    mask = picked & (kpos <= past + row_q)
    m = jnp.max(jnp.where(mask, s, NEG_BIG), axis=-1, keepdims=True)
    e = jnp.where(mask, jnp.exp(s - m), 0.0)
    d = jnp.sum(e, axis=-1, keepdims=True)
    p = e / jnp.where(d > 0, d, 1.0)
    o_s = lax.dot_general(p.astype(BF16), vt_all, (((1,), (1,)), ((), ())), preferred_element_type=F32)
    gates = _sigmoid(gt_ref[...])
    y = gates[0] * oc_ref[...] + gates[1] * o_s + gates[2] * ow_ref[...]

    @pl.when(i == 0)
    def _():
        o_ref[...] = jnp.zeros_like(o_ref)

    o_ref[...] += jnp.where(row_q == i, y, 0.0)


def _nsa_sample(proj3, cols, kvs, sd, cmp_big, pages_cmp, pages_slc, win_view, layer, page_table):
    col_qb, col_kv, col_gb = cols
    bsz = proj3.shape[0]
    n_phys, depth, _, _, page = pages_cmp.shape
    n_pages = page_table.shape[1]
    past = n_pages * page
    win = win_view.shape[-1]
    half = KV_B // 2
    rq = HG_B * sd
    n_ch = past // D_CMP
    n_cmp = (past + sd) // D_CMP - L_CMP // D_CMP + 1
    n_sb = -(-(past + sd) // L_SLC)
    k_sel = min(N_SEL, n_sb)
    assert (past + sd) // D_CMP == n_ch and past % L_SLC == 0 and sd <= L_SLC and page % L_SLC == 0
    _, b1big, w2big, b2big, w1tok = cmp_big
    pp = _cmp_pages(pages_cmp, layer, page_table, w1tok)
    q = proj3[:, :sd, col_qb:col_qb + MIX_B].reshape(bsz, sd, G_B, HG_B, HD_B).transpose(0, 2, 3, 1, 4)
    q = jnp.stack([jnp.pad(q[:, g], ((0, 0), (0, 0), (0, 0), (g * HD_B, half - (g + 1) * HD_B))) for g in range(G_B)], 1)
    q = q.reshape(bsz, G_B * rq, half).astype(BF16)
    gt = proj3[:, :sd, col_gb:col_gb + 3 * H_B].astype(F32).reshape(bsz, sd, G_B, HG_B, 3).transpose(0, 4, 2, 3, 1)
    gt = jnp.broadcast_to(gt.reshape(bsz, 3, G_B * rq, 1), (bsz, 3, G_B * rq, half))
    n_lane = -(-n_sb // 128) * 128
    mm = jnp.asarray(np.pad(_cmp_to_sel_map(n_ch, n_cmp, n_sb), ((0, n_lane - n_sb), (0, 0))).T)
    per_b = lambda shape: pl.BlockSpec((None,) + shape, lambda i: (i,) + (0,) * len(shape))
    const = lambda shape: pl.BlockSpec(shape, lambda i: (0,) * len(shape))
    o_c, o_w, idx = pl.pallas_call(
        functools.partial(_nsa_score_kernel, past, sd, n_cmp, n_sb),
        grid=(bsz,),
        in_specs=[per_b((n_ch, 2 * KV_B)), per_b((G_B * rq, half)),
                  pl.BlockSpec((None, None, 2, half, win), lambda i: (layer, i, 0, 0, 0)),
                  pl.BlockSpec((None, PAD_ROWS, KV_B), lambda i: (i, 0, col_kv // KV_B + 2)),
                  const((1, KV_B)), const((KV_B, KV_B)), const((1, KV_B)), const((n_ch, n_lane))],
        out_specs=[per_b((G_B * rq, half)), per_b((G_B * rq, half)), per_b((G_B * sd, 128))],
        out_shape=[jax.ShapeDtypeStruct((bsz, G_B * rq, half), F32), jax.ShapeDtypeStruct((bsz, G_B * rq, half), F32),
                   jax.ShapeDtypeStruct((bsz, G_B * sd, 128), jnp.int32)],
        compiler_params=_cparams(("parallel",), 40),
        name="nsa_score_sample",
    )(pp, q, win_view, proj3, b1big, w2big, b2big, mm)
    blk_pg = page // L_SLC
    n_cached = past // L_SLC
    new_blk = jnp.pad(kvs.reshape(bsz, sd, 2, half).transpose(0, 2, 3, 1), ((0, 0), (0, 0), (0, 0), (0, page - sd)))

    def blk_spec(j):
        def index(b, g, i, pt, ix, lyr):
            blk = jnp.minimum(ix[((b * G_B + g) * sd + i) * k_sel + j], n_cached - 1)
            return pt[b * n_pages + blk // blk_pg], lyr[0], 0, 0, 0
        return pl.BlockSpec((None, None, 2, half, page), index)

    grp = lambda b, g, i, pt, ix, lyr: (b, g, 0)
    grid_spec = pltpu.PrefetchScalarGridSpec(
        num_scalar_prefetch=3,
        grid=(bsz, G_B, sd),
        in_specs=[blk_spec(j) for j in range(k_sel)] + [
            pl.BlockSpec((None, 2, half, page), lambda b, g, i, pt, ix, lyr: (b, 0, 0, 0)),
            pl.BlockSpec((None, rq, half), grp), pl.BlockSpec((None, rq, half), grp), pl.BlockSpec((None, rq, half), grp),
            pl.BlockSpec((None, 3, rq, half), lambda b, g, i, pt, ix, lyr: (b, 0, g, 0))],
        out_specs=pl.BlockSpec((None, rq, half), grp))
    y = pl.pallas_call(
        functools.partial(_nsa_select_kernel, past, sd, n_sb, n_pages, k_sel),
        grid_spec=grid_spec,
        out_shape=jax.ShapeDtypeStruct((bsz, G_B * rq, half), F32),
        compiler_params=_cparams(("parallel", "parallel", "arbitrary"), 40),
        name="nsa_select_sample",
    )(page_table.reshape(-1), idx[:, :, :k_sel].reshape(-1), jnp.full((1,), layer, jnp.int32),
      *([pages_slc] * k_sel), new_blk, q, o_c, o_w, gt)
    y = y.reshape(bsz, G_B, HG_B, sd, G_B, HD_B)
    y = jnp.stack([y[:, g, :, :, g] for g in range(G_B)], axis=1)
    return y.transpose(0, 3, 1, 2, 4).reshape(bsz, sd, MIX_B)


def _prep_w_in(w, d):
    o = np.cumsum([0, MIX_A, MIX_A, MIX_A, MIX_A, MIX_B, KV_B, KV_B, KV_B, 3 * H_B, 2 * C_CONV, 3 * d])
    parts = [w[:, o[0]:o[4]], w[:, o[10]:o[11]], w[:, o[4]:o[5]], w[:, o[9]:o[10]], w[:, o[5]:o[8]], w[:, o[8]:o[9]]]
    n = sum(p.shape[1] for p in parts)
    n_pad = -(-n // 512) * 512
    parts.append(jnp.zeros((w.shape[0], n_pad - n), w.dtype))
    return jnp.concatenate(parts, axis=1).astype(BF16)


def _layer(x3, n_valid, p, s0, conv_prefix, mem_kv, nsa_fn, alpha):
    bsz, t, d = x3.shape
    m = bsz * t
    col_qb = COL_MG + 3 * d
    col_glu = col_qb + MIX_B
    col_kv = col_glu + 2 * C_CONV
    col_gb = col_kv + 3 * KV_B
    proj = _matmul(x3.reshape(m, d), p['w_in'], 1024, 1536, out_dtype=BF16)
    proj3 = proj.reshape(bsz, t, -1)
    ya, s_new = _hgrn(proj3, p['lb'], p['hg_norm'], s0, n_valid)
    yc, conv_state = _conv(proj3, col_glu // C_CONV, conv_prefix, p['conv_w'], p['conv_b'],
                           p['conv_ln_g'], p['conv_ln_b'], n_valid)
    kv_shape = (bsz, n_valid, 2, G_B, HD_B)
    kvc = proj3[:, :n_valid, col_kv:col_kv + KV_B].astype(F32)
    kvs = proj3[:, :n_valid, col_kv + KV_B:col_kv + 2 * KV_B].astype(F32)
    kvw = proj3[:, :n_valid, col_kv + 2 * KV_B:col_kv + 3 * KV_B].astype(F32)
    yb, nsa_extra = nsa_fn(proj3, (col_qb, col_kv, col_gb), kvc, kvs, kvw)
    kvc, kvs = kvc.reshape(kv_shape), kvs.reshape(kv_shape)
    x1 = _merge(ya.reshape(m, MIX_A), yb.reshape(m, MIX_B), yc.reshape(m, C_CONV), proj, x3.reshape(m, d),
                p['w_pa'], p['w_pb'], p['w_pc'], p['w_out'], p['ln_g'][0:1], p['ln_b'][0:1], alpha)
    x2 = _xattn(x1.reshape(bsz, t, d), mem_kv[0], mem_kv[1], p['w_xq'], p['w_xo'], p['ln_g'][1:2], p['ln_b'][1:2], alpha)
    x3n = _mlp(x2.reshape(m, d), p['w_up'], p['w_down'], p['ln_g'][2:3], p['ln_b'][2:3], alpha)
    return x3n.reshape(bsz, t, d), kvc, kvs, nsa_extra, s_new, conv_state


def kernel(x_prompt, x_sample, cache_cmp, cache_slc, cache_win, state_hgrn, state_conv, cache_mem, page_table, mem_prompt, w_in, lb_raw, hg_norm, w_cmp1, b_cmp1, w_cmp2, b_cmp2, conv_w, conv_b, conv_ln_g, conv_ln_b, w_pa, w_pb, w_pc, w_out, ln_g, ln_b, w_xq, w_xkv, w_xo, w_up, w_down):
    bp, t, d = x_prompt.shape
    bd, sd = x_sample.shape[:2]
    depth = w_in.shape[0]
    n_mem = mem_prompt.shape[1]
    win_buf = cache_win.shape[2]
    alpha = (2 * depth) ** 0.25
    lb_cum = jnp.cumsum(jax.nn.softmax(lb_raw.astype(F32), axis=0), axis=0)
    lb_all = lb_cum - lb_cum[0]
    pages_cmp, pages_slc, win_view = _pages_view(cache_cmp), _pages_view(cache_slc), _pages_view(cache_win)
    mem_cache = _mem_tile_order(cache_mem, d)
    xp = x_prompt
    xs = jnp.pad(x_sample, ((0, 0), (0, PAD_ROWS - sd), (0, 0)))
    outs = {k: [] for k in ('cmp_p', 'cmp_s', 'slc_p', 'slc_s', 'win_p', 'win_s', 'hg_p', 'hg_s', 'cv_p', 'cv_s', 'mem_p')}
    for l in range(depth):
        p = {'w_in': _prep_w_in(w_in[l], d), 'lb': lb_all[l], 'hg_norm': hg_norm[l], 'conv_w': conv_w[l],
             'conv_b': conv_b[l].reshape(1, -1), 'conv_ln_g': conv_ln_g[l].reshape(1, -1),
             'conv_ln_b': conv_ln_b[l].reshape(1, -1),
             'w_pa': w_pa[l].astype(BF16), 'w_pb': w_pb[l].astype(BF16), 'w_pc': w_pc[l].astype(BF16),
             'w_out': w_out[l].astype(BF16), 'ln_g': ln_g[l], 'ln_b': ln_b[l],
             'w_xq': w_xq[l].astype(BF16), 'w_xo': w_xo[l].astype(BF16),
             'w_up': w_up[l].astype(BF16), 'w_down': w_down[l].astype(BF16)}
        cmp_w = (w_cmp1[l], b_cmp1[l], w_cmp2[l], b_cmp2[l])

        cmp_big = _cmp_weights(*cmp_w)

        def nsa_prompt(proj3, cols, kvc, kvs, kvw):
            col_qb, col_kv, col_gb = cols
            prep = _nsa_prep(proj3, col_kv, cmp_big, min(win_buf, t))
            o = _nsa_prompt(proj3, col_qb, col_gb, prep[:6])
            ct, st_, wt = prep[6:]
            return o, (ct, st_, jnp.pad(wt, ((0, 0), (0, 0), (max(win_buf - t, 0), 0))))

        def nsa_sample(proj3, cols, kvc, kvs, kvw):
            o = _nsa_sample(proj3, cols, kvs, sd, cmp_big, pages_cmp, pages_slc, win_view, l, page_table)
            o = jnp.pad(o, ((0, 0), (0, PAD_ROWS - sd), (0, 0))).astype(BF16)
            win = jnp.concatenate([cache_win[l], kvw.reshape(bd, sd, 2, G_B, HD_B)], axis=1)[:, -win_buf:]
            return o, win

        w_kv = _mem_tile_order(w_xkv[l].reshape(d, 2, NX_H, d // NX_H), d).astype(BF16)
        mem_kv = _matmul(mem_prompt.reshape(bp * n_mem, d), w_kv, 1024, 512).reshape(1, bp, n_mem, 2 * d)
        xp, kc, ks_, wn, sh, cv = _layer(xp, t, p, jnp.zeros((bp, H_A, DK_A, DV_A), F32),
                                         jnp.zeros((bp, CONV_K - 1, C_CONV), F32), (mem_kv, 0), nsa_prompt, alpha)
        outs['cmp_p'].append(wn[0]); outs['slc_p'].append(wn[1]); outs['win_p'].append(wn[2])
        outs['hg_p'].append(sh); outs['cv_p'].append(cv)
        outs['mem_p'].append(_mem_head_order(mem_kv[0], d))
        xs, kc, ks_, wn, sh, cv = _layer(xs, sd, p, state_hgrn[l], state_conv[l], (mem_cache, l), nsa_sample, alpha)
        outs['cmp_s'].append(kc); outs['slc_s'].append(ks_); outs['win_s'].append(wn)
        outs['hg_s'].append(sh); outs['cv_s'].append(cv)
    st = lambda k, ax: jnp.stack(outs[k], axis=ax)

    def rows(k, ax):
        a = st(k, ax)
        a = a.reshape(a.shape[:2] + (2, G_B, HD_B, a.shape[-1]))
        return jnp.transpose(a, (0, 1, 5, 2, 3, 4))

    return (xp, xs[:, :sd],
            rows('cmp_p', 1), st('cmp_s', 1), rows('slc_p', 1), st('slc_s', 1),
            rows('win_p', 0), st('win_s', 0), st('hg_p', 0), st('hg_s', 0),
            st('cv_p', 0), st('cv_s', 0), st('mem_p', 0))
```
